```python
import jax, jax.numpy as jnp
from jax import lax
import numpy as np

D_MODEL = 1024
BATCH = 2
SEQ = 16384
DEPTH = 1

CHUNK = 64
Q_BLOCK = 128
EPS = 1e-6

GLA_HEADS = 4
GLA_DK = D_MODEL // 2 // GLA_HEADS
GLA_DV = D_MODEL // GLA_HEADS
GLA_GATE_RANK = 16
GLA_GATE_TEMP = 16.0

DSA_HEADS = 8
DSA_HEAD_DIM = 128
IDX_HEADS = 8
IDX_DIM = 64
TOPK_MAX = 256

N_GROUPS = 4
EXPERTS_PER_GROUP = 8
N_EXPERTS = N_GROUPS * EXPERTS_PER_GROUP
TOP_K_EXPERTS = 2
EXPERT_HIDDEN = D_MODEL // 2

GLA_QK_W = GLA_HEADS * GLA_DK
GLA_V_W = GLA_HEADS * GLA_DV
DSA_W = DSA_HEADS * DSA_HEAD_DIM
IN_SPLITS = (GLA_QK_W, GLA_QK_W, GLA_V_W, GLA_GATE_RANK, GLA_V_W,
             DSA_W, DSA_W, DSA_W, IDX_HEADS * IDX_DIM, IDX_DIM, IDX_HEADS,
             D_MODEL, D_MODEL)
IN_WIDTH = sum(IN_SPLITS)

kernel_name = "hybrid_gla_dsa_hiermoe_block"


def rms_norm(x, g):
    xf = x.astype(jnp.float32)
    y = xf * lax.rsqrt(jnp.mean(xf * xf, axis=-1, keepdims=True) + EPS)
    return y * g.astype(jnp.float32)


def layer_norm(x, g, b):
    xf = x.astype(jnp.float32)
    mu = jnp.mean(xf, axis=-1, keepdims=True)
    var = jnp.mean(jnp.square(xf - mu), axis=-1, keepdims=True)
    return (xf - mu) * lax.rsqrt(var + EPS) * g.astype(jnp.float32) + b.astype(jnp.float32)


def split_cols(z):
    offsets = [int(o) for o in np.cumsum(IN_SPLITS)[:-1]]
    return jnp.split(z, offsets, axis=-1)


def gla_branch(q, k, v, a_lr, r, w_a2, b_a, norm_g):
    B, S, _ = q.shape
    nc = S // CHUNK
    f32 = jnp.float32
    q = q.astype(f32).reshape(B, nc, CHUNK, GLA_HEADS, GLA_DK) * (GLA_DK ** -0.5)
    k = k.astype(f32).reshape(B, nc, CHUNK, GLA_HEADS, GLA_DK)
    v = v.astype(f32).reshape(B, nc, CHUNK, GLA_HEADS, GLA_DV)
    log_a = jax.nn.log_sigmoid((a_lr @ w_a2 + b_a).astype(f32)) / GLA_GATE_TEMP
    log_a = log_a.reshape(B, nc, CHUNK, GLA_HEADS, GLA_DK)
    cum = jnp.cumsum(log_a, axis=2)
    cum_end = cum[:, :, -1:]
    k_dec = k * jnp.exp(cum_end - cum)
    chunk_decay = jnp.exp(cum_end[:, :, 0])
    xs = (jnp.moveaxis(q, 1, 0), jnp.moveaxis(k_dec, 1, 0),
          jnp.moveaxis(v, 1, 0), jnp.moveaxis(chunk_decay, 1, 0))

    def step(state, inp):
        qc, kc, vc, dc = inp
        state = dc[..., None] * state + jnp.einsum('bchk,bchv->bhkv', kc, vc)
        out = jnp.einsum('bchk,bhkv->bchv', qc, state)
        return state, out

    s0 = jnp.zeros((B, GLA_HEADS, GLA_DK, GLA_DV), f32)
    _, o = lax.scan(step, s0, xs)
    o = jnp.moveaxis(o, 0, 1).reshape(B, S, GLA_HEADS, GLA_DV)
    o = rms_norm(o, norm_g)
    o = o * jax.nn.silu(r.astype(f32).reshape(B, S, GLA_HEADS, GLA_DV))
    return o.reshape(B, S, GLA_V_W)


def dsa_branch(q, k, v, qi, ki, wi, q_norm_g, k_norm_g, ln_g, ln_b):
    B, S, _ = q.shape
    f32 = jnp.float32
    topk = min(TOPK_MAX, S // 4)
    nb = S // Q_BLOCK
    q = rms_norm(q.reshape(B, S, DSA_HEADS, DSA_HEAD_DIM), q_norm_g)
    k = rms_norm(k.reshape(B, S, DSA_HEADS, DSA_HEAD_DIM), k_norm_g)
    v = v.astype(f32).reshape(B, S, DSA_HEADS, DSA_HEAD_DIM)
    qi = qi.astype(f32).reshape(B, S, IDX_HEADS, IDX_DIM)
    ki = layer_norm(ki, ln_g, ln_b)
    wi = wi.astype(f32) * (IDX_HEADS ** -0.5)
    key_chunk = jnp.arange(S) // CHUNK
    gather = jax.vmap(lambda arr, ids: arr[ids])

    def block(i):
        start = i * Q_BLOCK
        qb = lax.dynamic_slice_in_dim(q, start, Q_BLOCK, axis=1)
        qib = lax.dynamic_slice_in_dim(qi, start, Q_BLOCK, axis=1)
        wib = lax.dynamic_slice_in_dim(wi, start, Q_BLOCK, axis=1)
        q_chunk = (start + jnp.arange(Q_BLOCK)) // CHUNK
        admissible = key_chunk[None, :] <= q_chunk[:, None]
        idx_logits = jnp.einsum('bqhd,bsd->bqhs', qib, ki) * (IDX_DIM ** -0.5)
        score = jnp.einsum('bqh,bqhs->bqs', wib, jax.nn.relu(idx_logits))
        score = jnp.where(admissible[None], score, -jnp.inf)
        top_val, top_idx = lax.top_k(score, topk)
        valid = jnp.isfinite(top_val)
        k_sel = gather(k, top_idx)
        v_sel = gather(v, top_idx)
        logits = jnp.einsum('bqhd,bqkhd->bqhk', qb, k_sel) * (DSA_HEAD_DIM ** -0.5)
        logits = jnp.where(valid[:, :, None, :], logits, -jnp.inf)
        p = jax.nn.softmax(logits, axis=-1)
        return jnp.einsum('bqhk,bqkhd->bqhd', p, v_sel)

    o = lax.map(block, jnp.arange(nb))
    return jnp.moveaxis(o, 0, 1).reshape(B, S, DSA_W)


def hier_moe(h, w_rg, b_rg, w_re, b_re, w_gate, w_up, w_down):
    B, S, D = h.shape
    f32 = jnp.float32
    t = h.reshape(B * S, D)
    g_prob = jax.nn.softmax((t @ w_rg + b_rg).astype(f32), axis=-1)
    g_idx = jnp.argmax(g_prob, axis=-1)
    g_w = jnp.take_along_axis(g_prob, g_idx[:, None], axis=1)
    e_logits = (t @ w_re + b_re).astype(f32).reshape(B * S, N_GROUPS, EXPERTS_PER_GROUP)
    e_logits = jnp.take_along_axis(e_logits, g_idx[:, None, None], axis=1)[:, 0]
    e_prob = jax.nn.softmax(e_logits, axis=-1)
    top_p, top_e = lax.top_k(e_prob, TOP_K_EXPERTS)
    top_w = g_w * top_p / jnp.sum(top_p, axis=-1, keepdims=True)
    expert_ids = g_idx[:, None] * EXPERTS_PER_GROUP + top_e
    gates = jnp.einsum('tk,tke->te', top_w,
                       jax.nn.one_hot(expert_ids, N_EXPERTS, dtype=f32))
    y = jnp.zeros((B * S, D), f32)
    for e in range(N_EXPERTS):
        hid = jax.nn.silu(t @ w_gate[e]) * (t @ w_up[e])
        y = y + gates[:, e:e + 1] * (hid @ w_down[e]).astype(f32)
    return y.reshape(B, S, D)


def setup_inputs(seed: int = 0) -> dict:
    key = jax.random.key(seed)
    ks = jax.random.split(key, 24)
    f32 = jnp.float32
    L, D, F = DEPTH, D_MODEL, EXPERT_HIDDEN

    def nrm(k, shape, scale):
        return jax.random.normal(k, shape, f32) * scale

    return {
        "x": nrm(ks[0], (BATCH, SEQ, D), 1.0),
        "norm1_g": 1.0 + nrm(ks[1], (L, D), 0.05),
        "w_in": nrm(ks[2], (L, D, IN_WIDTH), D ** -0.5),
        "gla_w_a2": nrm(ks[3], (L, GLA_GATE_RANK, GLA_QK_W), GLA_GATE_RANK ** -0.5),
        "gla_b_a": nrm(ks[4], (L, GLA_QK_W), 0.02),
        "gla_norm_g": 1.0 + nrm(ks[5], (L, GLA_DV), 0.05),
        "dsa_q_norm_g": 1.0 + nrm(ks[6], (L, DSA_HEAD_DIM), 0.05),
        "dsa_k_norm_g": 1.0 + nrm(ks[7], (L, DSA_HEAD_DIM), 0.05),
        "idx_k_ln_g": 1.0 + nrm(ks[8], (L, IDX_DIM), 0.05),
        "idx_k_ln_b": nrm(ks[9], (L, IDX_DIM), 0.02),
        "w_branch_gla": nrm(ks[10], (L, GLA_V_W, D), GLA_V_W ** -0.5),
        "w_branch_dsa": nrm(ks[11], (L, DSA_W, D), DSA_W ** -0.5),
        "w_out": nrm(ks[12], (L, D, D), D ** -0.5),
        "norm2_g": 1.0 + nrm(ks[13], (L, D), 0.05),
        "w_router_group": nrm(ks[14], (L, D, N_GROUPS), D ** -0.5),
        "b_router_group": nrm(ks[15], (L, N_GROUPS), 0.01),
        "w_router_expert": nrm(ks[16], (L, D, N_EXPERTS), D ** -0.5),
        "b_router_expert": nrm(ks[17], (L, N_EXPERTS), 0.01),
        "w_exp_gate": nrm(ks[18], (L, N_EXPERTS, D, F), D ** -0.5),
        "w_exp_up": nrm(ks[19], (L, N_EXPERTS, D, F), D ** -0.5),
        "w_exp_down": nrm(ks[20], (L, N_EXPERTS, F, D), F ** -0.5),
    }


def reference(x, norm1_g, w_in, gla_w_a2, gla_b_a, gla_norm_g, dsa_q_norm_g, dsa_k_norm_g,
              idx_k_ln_g, idx_k_ln_b, w_branch_gla, w_branch_dsa, w_out, norm2_g,
              w_router_group, b_router_group, w_router_expert, b_router_expert,
              w_exp_gate, w_exp_up, w_exp_down):
    dt = x.dtype
    for l in range(DEPTH):
        h = rms_norm(x, norm1_g[l]).astype(dt)
        z = h @ w_in[l]
        (gq, gk, gv, ga, gr, dq, dk, dv, iq, ik, iw, gate_a, gate_b) = split_cols(z)
        o_gla = gla_branch(gq, gk, gv, ga, gr, gla_w_a2[l], gla_b_a[l], gla_norm_g[l]).astype(dt)
        o_dsa = dsa_branch(dq, dk, dv, iq, ik, iw, dsa_q_norm_g[l], dsa_k_norm_g[l],
                           idx_k_ln_g[l], idx_k_ln_b[l]).astype(dt)
        mix = (jax.nn.sigmoid(gate_a) * (o_gla @ w_branch_gla[l])
               + jax.nn.sigmoid(gate_b) * (o_dsa @ w_branch_dsa[l]))
        x = x + mix @ w_out[l]
        h2 = rms_norm(x, norm2_g[l]).astype(dt)
        x = x + hier_moe(h2, w_router_group[l], b_router_group[l], w_router_expert[l],
                         b_router_expert[l], w_exp_gate[l], w_exp_up[l], w_exp_down[l]).astype(dt)
    return x
```

```python
import functools
import math

import jax
import jax.numpy as jnp
from jax import lax
from jax.experimental import pallas as pl
from jax.experimental.pallas import tpu as pltpu

F32 = jnp.float32
BF16 = jnp.bfloat16
I32 = jnp.int32
MXU_DTYPE = BF16

CHUNK = 64
EPS = 1e-6
GLA_HEADS, GLA_DK, GLA_DV = 4, 128, 256
GLA_GATE_RANK = 16
GLA_GATE_TEMP = 16.0
DSA_HEADS, DSA_HEAD_DIM = 8, 128
IDX_HEADS, IDX_DIM = 8, 64
TOPK_MAX = 256
N_GROUPS, EXPERTS_PER_GROUP = 4, 8
N_EXPERTS = N_GROUPS * EXPERTS_PER_GROUP
LANES = 128

INT_MIN = -(2 ** 31)
INT_MAX = 2 ** 31 - 1

VMEM_LIMIT = 56 * 1024 * 1024

_NT = (((1,), (1,)), ((), ()))
_TN = (((0,), (0,)), ((), ()))


def _params(sem):
    return pltpu.CompilerParams(dimension_semantics=sem, vmem_limit_bytes=VMEM_LIMIT)


def _mm(a, b):
    return jnp.dot(a.astype(MXU_DTYPE), b.astype(MXU_DTYPE), preferred_element_type=F32)


def _mm_nt(a, b):
    return lax.dot_general(a.astype(MXU_DTYPE), b.astype(MXU_DTYPE), _NT, preferred_element_type=F32)


def _mm_tn(a, b):
    return lax.dot_general(a.astype(MXU_DTYPE), b.astype(MXU_DTYPE), _TN, preferred_element_type=F32)


def _split(a):
    hi = a.astype(BF16)
    lo = (a - hi.astype(F32)).astype(BF16)
    return hi, lo


def _dot3(a, b):
    ah, al = _split(a)
    bh, bl = _split(b)
    d = lambda u, v: jnp.dot(u, v, preferred_element_type=F32)
    return d(ah, bh) + (d(ah, bl) + d(al, bh))


def _rmsnorm_kernel(x_ref, g_ref, o_ref):
    x = x_ref[...]
    ms = jnp.mean(x * x, axis=-1, keepdims=True)
    o_ref[...] = (x * lax.rsqrt(ms + EPS) * g_ref[...]).astype(o_ref.dtype)


def _rmsnorm(x2, g, tm):
    T, D = x2.shape
    return pl.pallas_call(
        _rmsnorm_kernel,
        out_shape=jax.ShapeDtypeStruct((T, D), MXU_DTYPE),
        grid=(T // tm,),
        in_specs=[pl.BlockSpec((tm, D), lambda i: (i, 0)), pl.BlockSpec((1, D), lambda i: (0, 0))],
        out_specs=pl.BlockSpec((tm, D), lambda i: (i, 0)),
        compiler_params=_params(("parallel",)),
        name="rmsnorm",
    )(x2, g.reshape(1, D))


def _matmul_kernel(h_ref, w_ref, o_ref):
    o_ref[...] = jnp.dot(h_ref[...], w_ref[...], preferred_element_type=F32).astype(o_ref.dtype)


def _matmul(h, w, tm, tn, out_dtype):
    T, D = h.shape
    N = w.shape[1]
    return pl.pallas_call(
        _matmul_kernel,
        out_shape=jax.ShapeDtypeStruct((T, N), out_dtype),
        grid=(T // tm, N // tn),
        in_specs=[pl.BlockSpec((tm, D), lambda i, j: (i, 0)), pl.BlockSpec((D, tn), lambda i, j: (0, j))],
        out_specs=pl.BlockSpec((tm, tn), lambda i, j: (i, j)),
        compiler_params=_params(("parallel", "arbitrary")),
        name="proj_main",
    )(h, w)


def _qk_kernel(h_ref, w_ref, g_ref, o_ref):
    z = jnp.dot(h_ref[...], w_ref[...], preferred_element_type=F32)
    for hh in range(DSA_HEADS):
        sl = slice(hh * DSA_HEAD_DIM, (hh + 1) * DSA_HEAD_DIM)
        zh = z[:, sl]
        ms = jnp.mean(zh * zh, axis=-1, keepdims=True)
        o_ref[:, sl] = (zh * lax.rsqrt(ms + EPS) * g_ref[:, sl]).astype(o_ref.dtype)


def _qk_proj(h, w, g, tm):
    T, D = h.shape
    W = DSA_HEADS * DSA_HEAD_DIM
    return pl.pallas_call(
        _qk_kernel,
        out_shape=jax.ShapeDtypeStruct((T, 2 * W), MXU_DTYPE),
        grid=(T // tm, 2),
        in_specs=[
            pl.BlockSpec((tm, D), lambda i, j: (i, 0)),
            pl.BlockSpec((D, W), lambda i, j: (0, j)),
            pl.BlockSpec((1, W), lambda i, j: (0, j)),
        ],
        out_specs=pl.BlockSpec((tm, W), lambda i, j: (i, j)),
        compiler_params=_params(("parallel", "arbitrary")),
        name="proj_qk",
    )(h, w, g)


def _vt_kernel(h_ref, wt_ref, o_ref):
    o_ref[...] = lax.dot_general(wt_ref[...], h_ref[...], _NT, preferred_element_type=F32).astype(o_ref.dtype)


def _vt_proj(h, wt, B, S, tm):
    T, D = h.shape
    W = wt.shape[0]
    nt = S // tm
    return pl.pallas_call(
        _vt_kernel,
        out_shape=jax.ShapeDtypeStruct((B, W, S), MXU_DTYPE),
        grid=(T // tm,),
        in_specs=[pl.BlockSpec((tm, D), lambda i: (i, 0)), pl.BlockSpec((W, D), lambda i: (0, 0))],
        out_specs=pl.BlockSpec((None, W, tm), lambda i: (i // nt, 0, i % nt)),
        compiler_params=_params(("parallel",)),
        name="proj_vt",
    )(h, wt)


IDX_Q_W = IDX_HEADS * IDX_DIM


def _idx_kernel(h_ref, w_ref, wiwt_ref, lng_ref, lnb_ref, iq_ref, ki_ref, ga_ref, wit_ref):
    h = h_ref[...]
    z = jnp.dot(h, w_ref[...], preferred_element_type=F32)
    iq_ref[...] = z[:, :IDX_Q_W].astype(iq_ref.dtype)
    ik = z[:, IDX_Q_W:IDX_Q_W + IDX_DIM]
    mu = jnp.mean(ik, axis=-1, keepdims=True)
    var = jnp.mean(jnp.square(ik - mu), axis=-1, keepdims=True)
    ki = (ik - mu) * lax.rsqrt(var + EPS) * lng_ref[...] + lnb_ref[...]
    ki_ref[...] = ki.astype(ki_ref.dtype)
    ga_ref[...] = z[:, IDX_Q_W + LANES:IDX_Q_W + LANES + GLA_GATE_RANK]
    wt = lax.dot_general(wiwt_ref[...], h, _NT, preferred_element_type=F32)
    wit_ref[...] = wt[:IDX_HEADS, :] * (IDX_HEADS ** -0.5 * IDX_DIM ** -0.5)


def _idx_proj(h, w, wiwt, lng, lnb, tm):
    T, D = h.shape
    N = w.shape[1]
    return pl.pallas_call(
        _idx_kernel,
        out_shape=(
            jax.ShapeDtypeStruct((T, IDX_Q_W), MXU_DTYPE),
            jax.ShapeDtypeStruct((T, IDX_DIM), MXU_DTYPE),
            jax.ShapeDtypeStruct((T, GLA_GATE_RANK), F32),
            jax.ShapeDtypeStruct((IDX_HEADS, T), F32),
        ),
        grid=(T // tm,),
        in_specs=[
            pl.BlockSpec((tm, D), lambda i: (i, 0)),
            pl.BlockSpec((D, N), lambda i: (0, 0)),
            pl.BlockSpec(wiwt.shape, lambda i: (0, 0)),
            pl.BlockSpec((1, IDX_DIM), lambda i: (0, 0)),
            pl.BlockSpec((1, IDX_DIM), lambda i: (0, 0)),
        ],
        out_specs=(
            pl.BlockSpec((tm, IDX_Q_W), lambda i: (i, 0)),
            pl.BlockSpec((tm, IDX_DIM), lambda i: (i, 0)),
            pl.BlockSpec((tm, GLA_GATE_RANK), lambda i: (i, 0)),
            pl.BlockSpec((IDX_HEADS, tm), lambda i: (0, i)),
        ),
        compiler_params=_params(("parallel",)),
        name="proj_idx",
    )(h, w, wiwt, lng, lnb)


def _gla_kernel(q_ref, k_ref, v_ref, r_ref, ga_ref, wa2_ref, ba_ref, ng_ref, o_ref, st_ref, la_ref, *, n_chunks):
    @pl.when(pl.program_id(1) == 0)
    def _():
        st_ref[...] = jnp.zeros_like(st_ref)

    x = _dot3(ga_ref[...], wa2_ref[...]) + ba_ref[...]
    la_ref[...] = (jnp.minimum(x, 0.0) - jnp.log(1.0 + jnp.exp(-jnp.abs(x)))) * (1.0 / GLA_GATE_TEMP)

    row = lax.broadcasted_iota(I32, (CHUNK, CHUNK), 0)
    col = lax.broadcasted_iota(I32, (CHUNK, CHUNK), 1)
    tril = jnp.where(row >= col, 1.0, 0.0).astype(BF16)
    ng = ng_ref[...]

    def chunk(c, carry):
        rows = pl.ds(pl.multiple_of(c * CHUNK, CHUNK), CHUNK)
        for hh in range(GLA_HEADS):
            ksl = slice(hh * GLA_DK, (hh + 1) * GLA_DK)
            vsl = slice(hh * GLA_DV, (hh + 1) * GLA_DV)
            la_hi, la_lo = _split(la_ref[rows, ksl])
            cum = jnp.dot(tril, la_hi, preferred_element_type=F32) + jnp.dot(tril, la_lo, preferred_element_type=F32)
            cum_end = cum[CHUNK - 1:CHUNK, :]
            k_dec = k_ref[rows, ksl].astype(F32) * jnp.exp(cum_end - cum)
            st = st_ref[hh] * jnp.exp(cum_end) + _mm_tn(v_ref[rows, vsl], k_dec)
            st_ref[hh] = st
            o = _mm_nt(q_ref[rows, ksl], st) * (GLA_DK ** -0.5)
            ms = jnp.mean(o * o, axis=-1, keepdims=True)
            r = r_ref[rows, vsl].astype(F32)
            y = o * lax.rsqrt(ms + EPS) * ng * (r / (1.0 + jnp.exp(-r)))
            o_ref[rows, vsl] = y.astype(o_ref.dtype)
        return carry

    lax.fori_loop(0, n_chunks, chunk, 0)


def _gla(zm, ga, wa2, ba, ng, B, S, tb):
    T = B * S
    nb = S // tb
    qk_w = GLA_HEADS * GLA_DK
    v_w = GLA_HEADS * GLA_DV
    tok = lambda b, i: b * nb + i
    return pl.pallas_call(
        functools.partial(_gla_kernel, n_chunks=tb // CHUNK),
        out_shape=jax.ShapeDtypeStruct((T, v_w), MXU_DTYPE),
        grid=(B, nb),
        in_specs=[
            pl.BlockSpec((tb, qk_w), lambda b, i: (tok(b, i), 0)),
            pl.BlockSpec((tb, qk_w), lambda b, i: (tok(b, i), 1)),
            pl.BlockSpec((tb, v_w), lambda b, i: (tok(b, i), 1)),
            pl.BlockSpec((tb, v_w), lambda b, i: (tok(b, i), 2)),
            pl.BlockSpec((tb, GLA_GATE_RANK), lambda b, i: (tok(b, i), 0)),
            pl.BlockSpec((GLA_GATE_RANK, qk_w), lambda b, i: (0, 0)),
            pl.BlockSpec((1, qk_w), lambda b, i: (0, 0)),
            pl.BlockSpec((1, GLA_DV), lambda b, i: (0, 0)),
        ],
        out_specs=pl.BlockSpec((tb, v_w), lambda b, i: (tok(b, i), 0)),
        scratch_shapes=[pltpu.VMEM((GLA_HEADS, GLA_DV, GLA_DK), F32), pltpu.VMEM((tb, qk_w), F32)],
        compiler_params=_params(("parallel", "arbitrary")),
        name="gla",
    )(zm, zm, zm, zm, ga, wa2, ba, ng)


def _last_kb(qb, QB, KB):
    return ((qb + 1) * QB - 1) // KB


def _select_kernel(iq_ref, wit_ref, ki_ref, mask_ref, key_ref, *, QB, KB, S, topk):
    qb = pl.program_id(1)
    kb = pl.program_id(2)
    last = _last_kb(qb, QB, KB)
    row_iota = lax.broadcasted_iota(I32, (KB, QB), 0)

    def rows_of(i):
        return pl.ds(pl.multiple_of(i * KB, KB), KB)

    @pl.when(kb <= last)
    def _():
        ki = ki_ref[...]
        sc = jnp.zeros((KB, QB), F32)
        for hh in range(IDX_HEADS):
            lg = _mm_nt(ki, iq_ref[:, hh * IDX_DIM:(hh + 1) * IDX_DIM])
            sc = sc + jnp.maximum(lg, 0.0) * wit_ref[hh:hh + 1, :]
        s_idx = kb * KB + row_iota
        t_idx = qb * QB + lax.broadcasted_iota(I32, (KB, QB), 1)
        adm = lax.shift_right_logical(s_idx, 6) <= lax.shift_right_logical(t_idx, 6)
        bits = lax.bitcast_convert_type(sc, I32)
        bits = jnp.where(bits == INT_MIN, 0, bits)
        key = bits ^ (lax.shift_right_arithmetic(bits, 31) & INT_MAX)
        key_ref[rows_of(kb), :] = jnp.where(adm, key, INT_MIN)

    @pl.when(kb == last)
    def _():
        nblk = last + 1

        def colsum(m):
            part = m[0:8]
            for r in range(1, KB // 8):
                part = part + m[r * 8:(r + 1) * 8]
            return part

        def count(pred):
            def body(i, acc):
                return acc + colsum(pred(key_ref[rows_of(i), :], i))
            acc = lax.fori_loop(0, nblk, body, jnp.zeros((8, QB), I32))
            return jnp.sum(acc, axis=0, keepdims=True)

        def count_ge(thr):
            return count(lambda tile, i: jnp.where(tile >= thr, 1, 0))

        def bisect(_, c):
            lo, hi = c
            mid = (lo & hi) + lax.shift_right_arithmetic(lo ^ hi, 1) + ((lo ^ hi) & 1)
            ok = count_ge(mid) >= topk
            return jnp.where(ok, mid, lo), jnp.where(ok, hi, mid - 1)

        lo0 = jnp.full((1, QB), INT_MIN + 1, I32)
        hi0 = jnp.full((1, QB), INT_MAX, I32)
        vstar, _unused = lax.fori_loop(0, 32, bisect, (lo0, hi0))
        c_ge = count_ge(vstar)
        need = topk - count_ge(vstar + 1)

        def count_tie_lt(j):
            return count(lambda tile, i: jnp.where(tile == vstar, jnp.where(i * KB + row_iota < j, 1, 0), 0))

        def tie_cut():
            def step(_, c):
                lo, hi = c
                mid = lax.shift_right_arithmetic(lo + hi, 1)
                ok = count_tie_lt(mid) >= need
                return jnp.where(ok, lo, mid + 1), jnp.where(ok, mid, hi)
            n_steps = max(1, math.ceil(math.log2(S + 1)))
            lo, _hi = lax.fori_loop(0, n_steps, step, (jnp.zeros((1, QB), I32), jnp.full((1, QB), S, I32)))
            return lo

        any_excess = jnp.max(jnp.where(c_ge > topk, 1, 0)) > 0
        jcut = lax.cond(any_excess, tie_cut, lambda: jnp.full((1, QB), S, I32))

        def write(i, carry):
            tile = key_ref[rows_of(i), :]
            tie = jnp.where(i * KB + row_iota < jcut, 1, 0)
            sel = jnp.where(tile > vstar, 1, jnp.where(tile == vstar, tie, 0))
            mask_ref[rows_of(i), :] = sel.astype(mask_ref.dtype)
            return carry

        lax.fori_loop(0, nblk, write, 0)

        def clear(i, carry):
            mask_ref[rows_of(i), :] = jnp.zeros((KB, QB), mask_ref.dtype)
            return carry

        lax.fori_loop(nblk, S // KB, clear, 0)


def _dsa_select(iq, wit, ki, B, S, QB, KB):
    nq, nk = S // QB, S // KB
    topk = min(TOPK_MAX, S // 4)
    return pl.pallas_call(
        functools.partial(_select_kernel, QB=QB, KB=KB, S=S, topk=topk),
        out_shape=jax.ShapeDtypeStruct((B, S, S), jnp.int8),
        grid=(B, nq, nk),
        in_specs=[
            pl.BlockSpec((QB, IDX_Q_W), lambda b, q, k: (b * nq + q, 0)),
            pl.BlockSpec((IDX_HEADS, QB), lambda b, q, k: (0, b * nq + q)),
            pl.BlockSpec((KB, IDX_DIM), lambda b, q, k: (b * nk + jnp.minimum(k, _last_kb(q, QB, KB)), 0)),
        ],
        out_specs=pl.BlockSpec((None, S, QB), lambda b, q, k: (b, 0, q)),
        scratch_shapes=[pltpu.VMEM((S, QB), I32)],
        compiler_params=_params(("parallel", "arbitrary", "arbitrary")),
        name="dsa_select",
    )(iq, wit, ki)


def _attn_kernel(q_ref, k_ref, vt_ref, m_ref, gq_ref, gk_ref, o_ref, acc_ref, l_ref, *, QB, KB):
    qb = pl.program_id(1)
    kb = pl.program_id(2)
    last = _last_kb(qb, QB, KB)

    @pl.when(kb == 0)
    def _():
        acc_ref[...] = jnp.zeros_like(acc_ref)
        l_ref[...] = jnp.zeros_like(l_ref)

    @pl.when(kb <= last)
    def _():
        bound = (DSA_HEAD_DIM ** 0.5) * jnp.max(jnp.abs(gq_ref[...]), keepdims=True) * jnp.max(
            jnp.abs(gk_ref[...]), keepdims=True)
        mf = m_ref[...].astype(F32)
        for hh in range(DSA_HEADS):
            sl = slice(hh * DSA_HEAD_DIM, (hh + 1) * DSA_HEAD_DIM)
            lg = _mm_nt(k_ref[:, sl], q_ref[:, sl])
            p = jnp.exp(lg - bound) * mf
            part = p[0:8]
            for r in range(1, KB // 8):
                part = part + p[r * 8:(r + 1) * 8]
            l_ref[hh] += part
            acc_ref[hh] += _mm(vt_ref[sl, :], p)

    @pl.when(kb == last)
    def _():
        for hh in range(DSA_HEADS):
            sl = slice(hh * DSA_HEAD_DIM, (hh + 1) * DSA_HEAD_DIM)
            l = jnp.sum(l_ref[hh], axis=0, keepdims=True)
            o_ref[:, sl] = (acc_ref[hh] / l).T.astype(o_ref.dtype)


def _dsa_attn(qk, vt, mask, gq, gk, B, S, QB, KB):
    T = B * S
    nq, nk = S // QB, S // KB
    W = DSA_HEADS * DSA_HEAD_DIM
    kcl = lambda q, k: jnp.minimum(k, _last_kb(q, QB, KB))
    return pl.pallas_call(
        functools.partial(_attn_kernel, QB=QB, KB=KB),
        out_shape=jax.ShapeDtypeStruct((T, W), MXU_DTYPE),
        grid=(B, nq, nk),
        in_specs=[
            pl.BlockSpec((QB, W), lambda b, q, k: (b * nq + q, 0)),
            pl.BlockSpec((KB, W), lambda b, q, k: (b * nk + kcl(q, k), 1)),
            pl.BlockSpec((None, W, KB), lambda b, q, k: (b, 0, kcl(q, k))),
            pl.BlockSpec((None, KB, QB), lambda b, q, k: (b, kcl(q, k), q)),
            pl.BlockSpec((1, DSA_HEAD_DIM), lambda b, q, k: (0, 0)),
            pl.BlockSpec((1, DSA_HEAD_DIM), lambda b, q, k: (0, 0)),
        ],
        out_specs=pl.BlockSpec((QB, W), lambda b, q, k: (b * nq + q, 0)),
        scratch_shapes=[pltpu.VMEM((DSA_HEADS, DSA_HEAD_DIM, QB), F32), pltpu.VMEM((DSA_HEADS, 8, QB), F32)],
        compiler_params=_params(("parallel", "arbitrary", "arbitrary")),
        name="dsa_attn",
    )(qk, qk, vt, mask, gq, gk)


def _post_kernel(og_ref, od_ref, ga_ref, gb_ref, x_ref, pa_ref, pb_ref, wo_ref, g2_ref, wr_ref, br_ref,
                 x1_ref, h2_ref, gates_ref):
    sig = lambda v: 1.0 / (1.0 + jnp.exp(-v))
    a = jnp.dot(og_ref[...], pa_ref[...], preferred_element_type=F32)
    b = jnp.dot(od_ref[...], pb_ref[...], preferred_element_type=F32)
    mix = sig(ga_ref[...].astype(F32)) * a + sig(gb_ref[...].astype(F32)) * b
    x1 = x_ref[...] + _mm(mix, wo_ref[...])
    x1_ref[...] = x1
    ms = jnp.mean(x1 * x1, axis=-1, keepdims=True)
    h2 = x1 * lax.rsqrt(ms + EPS) * g2_ref[...]
    h2_ref[...] = h2.astype(h2_ref.dtype)

    logits = _dot3(h2, wr_ref[...]) + br_ref[...]
    tm = logits.shape[0]
    lane = lax.broadcasted_iota(I32, (tm, LANES), 1).astype(F32)
    neg = -jnp.inf
    far = float(2 * LANES)
    rmax = lambda v: jnp.max(v, axis=1, keepdims=True)
    rmin = lambda v: jnp.min(v, axis=1, keepdims=True)
    gl = jnp.where(lane >= N_EXPERTS, jnp.where(lane < N_EXPERTS + N_GROUPS, logits, neg), neg)
    gmax = rmax(gl)
    g_w = 1.0 / jnp.sum(jnp.exp(gl - gmax), axis=1, keepdims=True)
    g_idx = rmin(jnp.where(gl == gmax, lane, far)) - N_EXPERTS
    e_lo = g_idx * EXPERTS_PER_GROUP
    el = jnp.where(lane >= e_lo, jnp.where(lane < e_lo + EXPERTS_PER_GROUP, logits, neg), neg)
    m1 = rmax(el)
    e1 = rmin(jnp.where(el == m1, lane, far))
    el2 = jnp.where(lane == e1, neg, el)
    m2 = rmax(el2)
    e2 = rmin(jnp.where(el2 == m2, lane, far))
    p2 = jnp.exp(m2 - m1)
    w1 = g_w / (1.0 + p2)
    gates_ref[...] = jnp.where(lane == e1, w1, jnp.where(lane == e2, w1 * p2, 0.0))


def _post(og, od, zm, x2, pa, pb, wo, g2, wr, br, tm):
    T, D = x2.shape
    row = lambda i: (i, 0)
    full = lambda i: (0, 0)
    return pl.pallas_call(
        _post_kernel,
        out_shape=(
            jax.ShapeDtypeStruct((T, D), F32),
            jax.ShapeDtypeStruct((T, D), MXU_DTYPE),
            jax.ShapeDtypeStruct((T, LANES), F32),
        ),
        grid=(T // tm,),
        in_specs=[
            pl.BlockSpec((tm, D), row),
            pl.BlockSpec((tm, D), row),
            pl.BlockSpec((tm, D), lambda i: (i, 3)),
            pl.BlockSpec((tm, D), lambda i: (i, 4)),
            pl.BlockSpec((tm, D), row),
            pl.BlockSpec((D, D), full),
            pl.BlockSpec((D, D), full),
            pl.BlockSpec((D, D), full),
            pl.BlockSpec((1, D), full),
            pl.BlockSpec((D, LANES), full),
            pl.BlockSpec((1, LANES), full),
        ],
        out_specs=(pl.BlockSpec((tm, D), row), pl.BlockSpec((tm, D), row), pl.BlockSpec((tm, LANES), row)),
        compiler_params=_params(("parallel",)),
        name="merge_router",
    )(og, od, zm, zm, x2, pa, pb, wo, g2, wr, br)


def _moe_kernel(h_ref, gates_ref, x1_ref, wg_ref, wu_ref, wd_ref, o_ref):
    e = pl.program_id(1)
    h = h_ref[...]
    lane = lax.broadcasted_iota(I32, gates_ref.shape, 1)
    g_e = jnp.sum(jnp.where(lane == e, gates_ref[...], 0.0), axis=1, keepdims=True)
    a = jnp.dot(h, wg_ref[...], preferred_element_type=F32)
    u = jnp.dot(h, wu_ref[...], preferred_element_type=F32)
    hid = a / (1.0 + jnp.exp(-a)) * u
    y = g_e * _mm(hid, wd_ref[...])

    @pl.when(e == 0)
    def _():
        o_ref[...] = x1_ref[...] + y

    @pl.when(e > 0)
    def _():
        o_ref[...] += y


def _moe(h2, gates, x1, wg, wu, wd, tm):
    T, D = x1.shape
    E, _, F = wg.shape
    row = lambda i, e: (i, 0)
    return pl.pallas_call(
        _moe_kernel,
        out_shape=jax.ShapeDtypeStruct((T, D), F32),
        grid=(T // tm, E),
        in_specs=[
            pl.BlockSpec((tm, D), row),
            pl.BlockSpec((tm, LANES), row),
            pl.BlockSpec((tm, D), row),
            pl.BlockSpec((None, D, F), lambda i, e: (e, 0, 0)),
            pl.BlockSpec((None, D, F), lambda i, e: (e, 0, 0)),
            pl.BlockSpec((None, F, D), lambda i, e: (e, 0, 0)),
        ],
        out_specs=pl.BlockSpec((tm, D), row),
        compiler_params=_params(("parallel", "arbitrary")),
        name="experts",
    )(h2, gates, x1, wg, wu, wd)


def _pad_cols(w, n):
    return jnp.pad(w, ((0, 0), (0, n - w.shape[1])))


def _layer(x2, B, S, p):
    T, D = x2.shape
    assert D == 2 * GLA_HEADS * GLA_DK == GLA_HEADS * GLA_DV == DSA_HEADS * DSA_HEAD_DIM
    assert S % 512 == 0
    cd = MXU_DTYPE
    qk_w, v_w, dsa_w = GLA_HEADS * GLA_DK, GLA_HEADS * GLA_DV, DSA_HEADS * DSA_HEAD_DIM
    splits = (qk_w, qk_w, v_w, GLA_GATE_RANK, v_w, dsa_w, dsa_w, dsa_w, IDX_Q_W, IDX_DIM, IDX_HEADS, D, D)
    offs = [0]
    for s in splits:
        offs.append(offs[-1] + s)
    w_in = p["w_in"]
    (w_gq, w_gk, w_gv, w_ga, w_gr, w_dq, w_dk, w_dv, w_iq, w_ik, w_iw, w_ta, w_tb) = [
        w_in[:, offs[i]:offs[i + 1]] for i in range(len(splits))]

    w_main = jnp.concatenate([w_gq, w_gk, w_gv, w_gr, w_ta, w_tb], axis=1).astype(cd)
    w_qk = jnp.concatenate([w_dq, w_dk], axis=1).astype(cd)
    g_qk = jnp.concatenate([jnp.tile(p["dsa_q_norm_g"] * (DSA_HEAD_DIM ** -0.5), DSA_HEADS),
                            jnp.tile(p["dsa_k_norm_g"], DSA_HEADS)]).reshape(1, 2 * dsa_w).astype(F32)
    w_vt = w_dv.T.astype(cd)
    w_idx = jnp.concatenate([w_iq, _pad_cols(w_ik, LANES), _pad_cols(w_ga, LANES)], axis=1).astype(cd)
    w_iwt = jnp.pad(w_iw.T, ((0, 16 - IDX_HEADS), (0, 0))).astype(cd)

    tm = min(1024, T)
    h = _rmsnorm(x2, p["norm1_g"], tm)
    zm = _matmul(h, w_main, tm, 512, cd)
    qk = _qk_proj(h, w_qk, g_qk, min(512, T))
    vt = _vt_proj(h, w_vt, B, S, min(512, S))
    iq, ki, ga, wit = _idx_proj(h, w_idx, w_iwt, p["idx_k_ln_g"].reshape(1, -1), p["idx_k_ln_b"].reshape(1, -1),
                                min(512, T))

    o_gla = _gla(zm, ga, p["gla_w_a2"], p["gla_b_a"].reshape(1, -1), p["gla_norm_g"].reshape(1, -1), B, S, 512)

    QB, KB = 256, 512
    mask = _dsa_select(iq, wit, ki, B, S, QB, KB)
    o_dsa = _dsa_attn(qk, vt, mask, p["dsa_q_norm_g"].reshape(1, -1) * (DSA_HEAD_DIM ** -0.5),
                      p["dsa_k_norm_g"].reshape(1, -1), B, S, QB, KB)

    w_r = _pad_cols(jnp.concatenate([p["w_router_expert"], p["w_router_group"]], axis=1), LANES)
    b_r = _pad_cols(jnp.concatenate([p["b_router_expert"], p["b_router_group"]]).reshape(1, -1), LANES)
    x1, h2, gates = _post(o_gla, o_dsa, zm, x2, p["w_branch_gla"].astype(cd), p["w_branch_dsa"].astype(cd),
                          p["w_out"].astype(cd), p["norm2_g"].reshape(1, -1), w_r, b_r, min(512, T))
    return _moe(h2, gates, x1, p["w_exp_gate"].astype(cd), p["w_exp_up"].astype(cd), p["w_exp_down"].astype(cd),
                min(1024, T))


def kernel(x, norm1_g, w_in, gla_w_a2, gla_b_a, gla_norm_g, dsa_q_norm_g, dsa_k_norm_g, idx_k_ln_g, idx_k_ln_b,
           w_branch_gla, w_branch_dsa, w_out, norm2_g, w_router_group, b_router_group, w_router_expert,
           b_router_expert, w_exp_gate, w_exp_up, w_exp_down):
    B, S, D = x.shape
    stacked = dict(norm1_g=norm1_g, w_in=w_in, gla_w_a2=gla_w_a2, gla_b_a=gla_b_a, gla_norm_g=gla_norm_g,
                   dsa_q_norm_g=dsa_q_norm_g, dsa_k_norm_g=dsa_k_norm_g, idx_k_ln_g=idx_k_ln_g,
                   idx_k_ln_b=idx_k_ln_b, w_branch_gla=w_branch_gla, w_branch_dsa=w_branch_dsa, w_out=w_out,
                   norm2_g=norm2_g, w_router_group=w_router_group, b_router_group=b_router_group,
                   w_router_expert=w_router_expert, b_router_expert=b_router_expert, w_exp_gate=w_exp_gate,
                   w_exp_up=w_exp_up, w_exp_down=w_exp_down)
    x2 = x.reshape(B * S, D).astype(F32)
    for l in range(w_in.shape[0]):
        x2 = _layer(x2, B, S, {k: v[l] for k, v in stacked.items()})
    return x2.reshape(B, S, D).astype(x.dtype)
```

```python
import functools
import math

import jax
import jax.numpy as jnp
from jax import lax
from jax.experimental import pallas as pl
from jax.experimental.pallas import tpu as pltpu

F32 = jnp.float32
BF16 = jnp.bfloat16
I32 = jnp.int32
MXU_DTYPE = BF16

CHUNK = 64
EPS = 1e-6
GLA_HEADS, GLA_DK, GLA_DV = 4, 128, 256
GLA_GATE_RANK = 16
GLA_GATE_TEMP = 16.0
DSA_HEADS, DSA_HEAD_DIM = 8, 128
IDX_HEADS, IDX_DIM = 8, 64
TOPK_MAX = 256
N_GROUPS, EXPERTS_PER_GROUP = 4, 8
N_EXPERTS = N_GROUPS * EXPERTS_PER_GROUP
LANES = 128

INT_MIN = -(2 ** 31)
INT_MAX = 2 ** 31 - 1

VMEM_LIMIT = 56 * 1024 * 1024

_NT = (((1,), (1,)), ((), ()))
_TN = (((0,), (0,)), ((), ()))


def _params(sem):
    return pltpu.CompilerParams(dimension_semantics=sem, vmem_limit_bytes=VMEM_LIMIT)


def _mm(a, b):
    return jnp.dot(a.astype(MXU_DTYPE), b.astype(MXU_DTYPE), preferred_element_type=F32)


def _mm_nt(a, b):
    return lax.dot_general(a.astype(MXU_DTYPE), b.astype(MXU_DTYPE), _NT, preferred_element_type=F32)


def _mm_tn(a, b):
    return lax.dot_general(a.astype(MXU_DTYPE), b.astype(MXU_DTYPE), _TN, preferred_element_type=F32)


def _split(a):
    hi = a.astype(BF16)
    lo = (a - hi.astype(F32)).astype(BF16)
    return hi, lo


def _dot3(a, b):
    ah, al = _split(a)
    bh, bl = _split(b)
    d = lambda u, v: jnp.dot(u, v, preferred_element_type=F32)
    return d(ah, bh) + (d(ah, bl) + d(al, bh))


def _rmsnorm_kernel(x_ref, g_ref, o_ref):
    x = x_ref[...]
    ms = jnp.mean(x * x, axis=-1, keepdims=True)
    o_ref[...] = (x * lax.rsqrt(ms + EPS) * g_ref[...]).astype(o_ref.dtype)


def _rmsnorm(x2, g, tm):
    T, D = x2.shape
    return pl.pallas_call(
        _rmsnorm_kernel,
        out_shape=jax.ShapeDtypeStruct((T, D), MXU_DTYPE),
        grid=(T // tm,),
        in_specs=[pl.BlockSpec((tm, D), lambda i: (i, 0)), pl.BlockSpec((1, D), lambda i: (0, 0))],
        out_specs=pl.BlockSpec((tm, D), lambda i: (i, 0)),
        compiler_params=_params(("parallel",)),
        name="rmsnorm",
    )(x2, g.reshape(1, D))


def _matmul_kernel(h_ref, w_ref, o_ref):
    o_ref[...] = jnp.dot(h_ref[...], w_ref[...], preferred_element_type=F32).astype(o_ref.dtype)


def _matmul(h, w, tm, tn, out_dtype):
    T, D = h.shape
    N = w.shape[1]
    return pl.pallas_call(
        _matmul_kernel,
        out_shape=jax.ShapeDtypeStruct((T, N), out_dtype),
        grid=(T // tm, N // tn),
        in_specs=[pl.BlockSpec((tm, D), lambda i, j: (i, 0)), pl.BlockSpec((D, tn), lambda i, j: (0, j))],
        out_specs=pl.BlockSpec((tm, tn), lambda i, j: (i, j)),
        compiler_params=_params(("parallel", "arbitrary")),
        name="proj_main",
    )(h, w)


def _qk_kernel(h_ref, w_ref, g_ref, o_ref):
    z = jnp.dot(h_ref[...], w_ref[...], preferred_element_type=F32)
    for hh in range(DSA_HEADS):
        sl = slice(hh * DSA_HEAD_DIM, (hh + 1) * DSA_HEAD_DIM)
        zh = z[:, sl]
        ms = jnp.mean(zh * zh, axis=-1, keepdims=True)
        o_ref[:, sl] = (zh * lax.rsqrt(ms + EPS) * g_ref[:, sl]).astype(o_ref.dtype)


def _qk_proj(h, w, g, tm):
    T, D = h.shape
    W = DSA_HEADS * DSA_HEAD_DIM
    return pl.pallas_call(
        _qk_kernel,
        out_shape=jax.ShapeDtypeStruct((T, 2 * W), MXU_DTYPE),
        grid=(T // tm, 2),
        in_specs=[
            pl.BlockSpec((tm, D), lambda i, j: (i, 0)),
            pl.BlockSpec((D, W), lambda i, j: (0, j)),
            pl.BlockSpec((1, W), lambda i, j: (0, j)),
        ],
        out_specs=pl.BlockSpec((tm, W), lambda i, j: (i, j)),
        compiler_params=_params(("parallel", "arbitrary")),
        name="proj_qk",
    )(h, w, g)


VT_ONES = 16
VT_ROWS = DSA_HEAD_DIM + VT_ONES


def _vt_kernel(h_ref, wt_ref, o_ref):
    vt = lax.dot_general(wt_ref[...], h_ref[...], _NT, preferred_element_type=F32).astype(o_ref.dtype)
    ones = jnp.ones((VT_ONES, vt.shape[1]), o_ref.dtype)
    for hh in range(DSA_HEADS):
        o_ref[hh, :DSA_HEAD_DIM, :] = vt[hh * DSA_HEAD_DIM:(hh + 1) * DSA_HEAD_DIM]
        o_ref[hh, DSA_HEAD_DIM:, :] = ones


def _vt_proj(h, wt, B, S, tm):
    T, D = h.shape
    W = wt.shape[0]
    nt = S // tm
    return pl.pallas_call(
        _vt_kernel,
        out_shape=jax.ShapeDtypeStruct((B, DSA_HEADS, VT_ROWS, S), MXU_DTYPE),
        grid=(T // tm,),
        in_specs=[pl.BlockSpec((tm, D), lambda i: (i, 0)), pl.BlockSpec((W, D), lambda i: (0, 0))],
        out_specs=pl.BlockSpec((None, DSA_HEADS, VT_ROWS, tm), lambda i: (i // nt, 0, 0, i % nt)),
        compiler_params=_params(("parallel",)),
        name="proj_vt",
    )(h, wt)


IDX_Q_W = IDX_HEADS * IDX_DIM


def _idx_kernel(h_ref, w_ref, wiwt_ref, lng_ref, lnb_ref, iq_ref, ki_ref, ga_ref, wit_ref):
    h = h_ref[...]
    z = jnp.dot(h, w_ref[...], preferred_element_type=F32)
    iq_ref[...] = z[:, :IDX_Q_W].astype(iq_ref.dtype)
    ik = z[:, IDX_Q_W:IDX_Q_W + IDX_DIM]
    mu = jnp.mean(ik, axis=-1, keepdims=True)
    var = jnp.mean(jnp.square(ik - mu), axis=-1, keepdims=True)
    ki = (ik - mu) * lax.rsqrt(var + EPS) * lng_ref[...] + lnb_ref[...]
    ki_ref[...] = ki.astype(ki_ref.dtype)
    ga_ref[...] = z[:, IDX_Q_W + LANES:IDX_Q_W + LANES + GLA_GATE_RANK]
    wt = lax.dot_general(wiwt_ref[...], h, _NT, preferred_element_type=F32)
    wit_ref[...] = wt[:IDX_HEADS, :] * (IDX_HEADS ** -0.5 * IDX_DIM ** -0.5)


def _idx_proj(h, w, wiwt, lng, lnb, tm):
    T, D = h.shape
    N = w.shape[1]
    return pl.pallas_call(
        _idx_kernel,
        out_shape=(
            jax.ShapeDtypeStruct((T, IDX_Q_W), MXU_DTYPE),
            jax.ShapeDtypeStruct((T, IDX_DIM), MXU_DTYPE),
            jax.ShapeDtypeStruct((T, GLA_GATE_RANK), F32),
            jax.ShapeDtypeStruct((IDX_HEADS, T), F32),
        ),
        grid=(T // tm,),
        in_specs=[
            pl.BlockSpec((tm, D), lambda i: (i, 0)),
            pl.BlockSpec((D, N), lambda i: (0, 0)),
            pl.BlockSpec(wiwt.shape, lambda i: (0, 0)),
            pl.BlockSpec((1, IDX_DIM), lambda i: (0, 0)),
            pl.BlockSpec((1, IDX_DIM), lambda i: (0, 0)),
        ],
        out_specs=(
            pl.BlockSpec((tm, IDX_Q_W), lambda i: (i, 0)),
            pl.BlockSpec((tm, IDX_DIM), lambda i: (i, 0)),
            pl.BlockSpec((tm, GLA_GATE_RANK), lambda i: (i, 0)),
            pl.BlockSpec((IDX_HEADS, tm), lambda i: (0, i)),
        ),
        compiler_params=_params(("parallel",)),
        name="proj_idx",
    )(h, w, wiwt, lng, lnb)


def _gla_kernel(q_ref, k_ref, v_ref, r_ref, ga_ref, wa2_ref, ba_ref, ng_ref, o_ref, st_ref, la_ref, *, n_chunks):
    @pl.when(pl.program_id(1) == 0)
    def _():
        st_ref[...] = jnp.zeros_like(st_ref)

    x = _dot3(ga_ref[...], wa2_ref[...]) + ba_ref[...]
    la_ref[...] = (jnp.minimum(x, 0.0) - jnp.log(1.0 + jnp.exp(-jnp.abs(x)))) * (1.0 / GLA_GATE_TEMP)

    row = lax.broadcasted_iota(I32, (CHUNK, CHUNK), 0)
    col = lax.broadcasted_iota(I32, (CHUNK, CHUNK), 1)
    tril = jnp.where(row >= col, 1.0, 0.0).astype(BF16)
    ng = ng_ref[...]

    def chunk(c, carry):
        rows = pl.ds(pl.multiple_of(c * CHUNK, CHUNK), CHUNK)
        for hh in range(GLA_HEADS):
            ksl = slice(hh * GLA_DK, (hh + 1) * GLA_DK)
            vsl = slice(hh * GLA_DV, (hh + 1) * GLA_DV)
            la_hi, la_lo = _split(la_ref[rows, ksl])
            cum = jnp.dot(tril, la_hi, preferred_element_type=F32) + jnp.dot(tril, la_lo, preferred_element_type=F32)
            cum_end = cum[CHUNK - 1:CHUNK, :]
            k_dec = k_ref[rows, ksl].astype(F32) * jnp.exp(cum_end - cum)
            st = st_ref[hh] * jnp.exp(cum_end) + _mm_tn(v_ref[rows, vsl], k_dec)
            st_ref[hh] = st
            o = _mm_nt(q_ref[rows, ksl], st) * (GLA_DK ** -0.5)
            ms = jnp.mean(o * o, axis=-1, keepdims=True)
            r = r_ref[rows, vsl].astype(F32)
            y = o * lax.rsqrt(ms + EPS) * ng * (r / (1.0 + jnp.exp(-r)))
            o_ref[rows, vsl] = y.astype(o_ref.dtype)
        return carry

    lax.fori_loop(0, n_chunks, chunk, 0)


def _gla(zm, ga, wa2, ba, ng, B, S, tb):
    T = B * S
    nb = S // tb
    qk_w = GLA_HEADS * GLA_DK
    v_w = GLA_HEADS * GLA_DV
    tok = lambda b, i: b * nb + i
    return pl.pallas_call(
        functools.partial(_gla_kernel, n_chunks=tb // CHUNK),
        out_shape=jax.ShapeDtypeStruct((T, v_w), MXU_DTYPE),
        grid=(B, nb),
        in_specs=[
            pl.BlockSpec((tb, qk_w), lambda b, i: (tok(b, i), 0)),
            pl.BlockSpec((tb, qk_w), lambda b, i: (tok(b, i), 1)),
            pl.BlockSpec((tb, v_w), lambda b, i: (tok(b, i), 1)),
            pl.BlockSpec((tb, v_w), lambda b, i: (tok(b, i), 2)),
            pl.BlockSpec((tb, GLA_GATE_RANK), lambda b, i: (tok(b, i), 0)),
            pl.BlockSpec((GLA_GATE_RANK, qk_w), lambda b, i: (0, 0)),
            pl.BlockSpec((1, qk_w), lambda b, i: (0, 0)),
            pl.BlockSpec((1, GLA_DV), lambda b, i: (0, 0)),
        ],
        out_specs=pl.BlockSpec((tb, v_w), lambda b, i: (tok(b, i), 0)),
        scratch_shapes=[pltpu.VMEM((GLA_HEADS, GLA_DV, GLA_DK), F32), pltpu.VMEM((tb, qk_w), F32)],
        compiler_params=_params(("parallel", "arbitrary")),
        name="gla",
    )(zm, zm, zm, zm, ga, wa2, ba, ng)


def _last_kb(qb, QB, KB):
    return ((qb + 1) * QB - 1) // KB


def _select_kernel(iq_ref, wit_ref, ki_ref, mask_ref, key_ref, gm_ref, *, QB, KB, S, topk):
    qb = pl.program_id(1)
    kb = pl.program_id(2)
    last = _last_kb(qb, QB, KB)
    row_iota = lax.broadcasted_iota(I32, (KB, QB), 0)

    def rows_of(i):
        return pl.ds(pl.multiple_of(i * KB, KB), KB)

    @pl.when(kb == 0)
    def _():
        gm_ref[...] = jnp.full(gm_ref.shape, INT_MIN, I32)

    @pl.when(kb <= last)
    def _():
        ki = ki_ref[...]
        sc = jnp.zeros((KB, QB), F32)
        for hh in range(IDX_HEADS):
            lg = _mm_nt(ki, iq_ref[:, hh * IDX_DIM:(hh + 1) * IDX_DIM])
            sc = sc + jnp.maximum(lg, 0.0) * wit_ref[hh:hh + 1, :]
        s_idx = kb * KB + row_iota
        t_idx = qb * QB + lax.broadcasted_iota(I32, (KB, QB), 1)
        adm = lax.shift_right_logical(s_idx, 6) <= lax.shift_right_logical(t_idx, 6)
        bits = lax.bitcast_convert_type(sc, I32)
        bits = jnp.where(bits == INT_MIN, 0, bits)
        key = bits ^ (lax.shift_right_arithmetic(bits, 31) & INT_MAX)
        key = jnp.where(adm, key, INT_MIN)
        key_ref[rows_of(kb), :] = key
        gm = gm_ref[...]
        for r in range(KB // topk):
            gm = jnp.maximum(gm, key[r * topk:(r + 1) * topk])
        gm_ref[...] = gm

    @pl.when(kb == last)
    def _():
        nblk = last + 1
        slab = 32

        def colsum(m):
            part = m[0:slab]
            for r in range(1, KB // slab):
                part = part + m[r * slab:(r + 1) * slab]
            return part

        def count(pred):
            def body(i, acc):
                return acc + colsum(pred(key_ref[rows_of(i), :], i))
            acc = lax.fori_loop(0, nblk, body, jnp.zeros((slab, QB), I32))
            return jnp.sum(acc, axis=0, keepdims=True)

        def count_ge(thr):
            return count(lambda tile, i: jnp.where(tile >= thr, 1, 0))

        gm = gm_ref[...]
        lo0 = jnp.maximum(jnp.min(gm, axis=0, keepdims=True), INT_MIN + 1)
        hi0 = jnp.max(gm, axis=0, keepdims=True) + 1
        state0 = (lo0, count_ge(lo0), hi0, jnp.zeros((1, QB), I32))

        def unfinished(st):
            lo, c_lo, hi, _c_hi = st
            return jnp.max(jnp.where(c_lo > topk, jnp.where(hi != lo + 1, 1, 0), 0)) > 0

        def bisect(st):
            lo, c_lo, hi, c_hi = st
            mid = (lo & hi) + lax.shift_right_arithmetic(lo ^ hi, 1)
            c = count_ge(mid)
            ok = c >= topk
            return jnp.where(ok, mid, lo), jnp.where(ok, c, c_lo), jnp.where(ok, hi, mid), jnp.where(ok, c_hi, c)

        vstar, c_ge, _hi, c_gt = lax.while_loop(unfinished, bisect, state0)
        excess = c_ge > topk
        need = jnp.where(excess, topk - c_gt, S + 1)

        def count_tie_lt(j):
            return count(lambda tile, i: jnp.where(tile == vstar, jnp.where(i * KB + row_iota < j, 1, 0), 0))

        def tie_cut():
            def step(_, c):
                lo, hi = c
                mid = lax.shift_right_arithmetic(lo + hi, 1)
                ok = count_tie_lt(mid) >= need
                return jnp.where(ok, lo, mid + 1), jnp.where(ok, mid, hi)
            n_steps = max(1, math.ceil(math.log2(S + 1)))
            lo, _hi = lax.fori_loop(0, n_steps, step, (jnp.zeros((1, QB), I32), jnp.full((1, QB), S, I32)))
            return lo

        any_excess = jnp.max(jnp.where(excess, 1, 0)) > 0
        jcut = lax.cond(any_excess, tie_cut, lambda: jnp.full((1, QB), S, I32))

        def write(i, carry):
            tile = key_ref[rows_of(i), :]
            tie = jnp.where(i * KB + row_iota < jcut, 1, 0)
            sel = jnp.where(tile > vstar, 1, jnp.where(tile == vstar, tie, 0))
            mask_ref[rows_of(i), :] = sel.astype(mask_ref.dtype)
            return carry

        lax.fori_loop(0, nblk, write, 0)

        def clear(i, carry):
            mask_ref[rows_of(i), :] = jnp.zeros((KB, QB), mask_ref.dtype)
            return carry

        lax.fori_loop(nblk, S // KB, clear, 0)


def _dsa_select(iq, wit, ki, B, S, QB, KB):
    nq, nk = S // QB, S // KB
    topk = min(TOPK_MAX, S // 4)
    assert KB % topk == 0
    return pl.pallas_call(
        functools.partial(_select_kernel, QB=QB, KB=KB, S=S, topk=topk),
        out_shape=jax.ShapeDtypeStruct((B, S, S), jnp.int8),
        grid=(B, nq, nk),
        in_specs=[
            pl.BlockSpec((QB, IDX_Q_W), lambda b, q, k: (b * nq + q, 0)),
            pl.BlockSpec((IDX_HEADS, QB), lambda b, q, k: (0, b * nq + q)),
            pl.BlockSpec((KB, IDX_DIM), lambda b, q, k: (b * nk + jnp.minimum(k, _last_kb(q, QB, KB)), 0)),
        ],
        out_specs=pl.BlockSpec((None, S, QB), lambda b, q, k: (b, 0, q)),
        scratch_shapes=[pltpu.VMEM((S, QB), I32), pltpu.VMEM((topk, QB), I32)],
        compiler_params=_params(("parallel", "arbitrary", "arbitrary")),
        name="dsa_select",
    )(iq, wit, ki)


def _attn_kernel(q_ref, k_ref, vt_ref, m_ref, o_ref, acc_ref, *, QB, KB):
    qb = pl.program_id(1)
    kb = pl.program_id(2)
    last = _last_kb(qb, QB, KB)

    @pl.when(kb == 0)
    def _():
        acc_ref[...] = jnp.zeros_like(acc_ref)

    @pl.when(kb <= last)
    def _():
        mb = m_ref[...].astype(MXU_DTYPE)
        head = lambda hh: slice(hh * DSA_HEAD_DIM, (hh + 1) * DSA_HEAD_DIM)
        logits = lambda hh: _mm_nt(k_ref[:, head(hh)], q_ref[:, head(hh)])
        lg = logits(0)
        for hh in range(DSA_HEADS):
            lg_next = logits(hh + 1) if hh + 1 < DSA_HEADS else None
            p = jnp.exp2(lg).astype(MXU_DTYPE) * mb
            acc_ref[hh] += jnp.dot(vt_ref[hh], p, preferred_element_type=F32)
            lg = lg_next

    @pl.when(kb == last)
    def _():
        for hh in range(DSA_HEADS):
            acc = acc_ref[hh]
            o = acc[:DSA_HEAD_DIM] / acc[DSA_HEAD_DIM:DSA_HEAD_DIM + 1]
            o_ref[:, hh * DSA_HEAD_DIM:(hh + 1) * DSA_HEAD_DIM] = o.T.astype(o_ref.dtype)


def _dsa_attn(qk, vt, mask, B, S, QB, KB):
    T = B * S
    nq, nk = S // QB, S // KB
    W = DSA_HEADS * DSA_HEAD_DIM
    kcl = lambda q, k: jnp.minimum(k, _last_kb(q, QB, KB))
    return pl.pallas_call(
        functools.partial(_attn_kernel, QB=QB, KB=KB),
        out_shape=jax.ShapeDtypeStruct((T, W), MXU_DTYPE),
        grid=(B, nq, nk),
        in_specs=[
            pl.BlockSpec((QB, W), lambda b, q, k: (b * nq + q, 0)),
            pl.BlockSpec((KB, W), lambda b, q, k: (b * nk + kcl(q, k), 1)),
            pl.BlockSpec((None, DSA_HEADS, VT_ROWS, KB), lambda b, q, k: (b, 0, 0, kcl(q, k))),
            pl.BlockSpec((None, KB, QB), lambda b, q, k: (b, kcl(q, k), q)),
        ],
        out_specs=pl.BlockSpec((QB, W), lambda b, q, k: (b * nq + q, 0)),
        scratch_shapes=[pltpu.VMEM((DSA_HEADS, VT_ROWS, QB), F32)],
        compiler_params=_params(("parallel", "arbitrary", "arbitrary")),
        name="dsa_attn",
    )(qk, qk, vt, mask)


def _post_kernel(og_ref, od_ref, ga_ref, gb_ref, x_ref, pa_ref, pb_ref, wo_ref, g2_ref, wr_ref, br_ref,
                 x1_ref, h2_ref, gates_ref):
    sig = lambda v: 1.0 / (1.0 + jnp.exp(-v))
    a = jnp.dot(og_ref[...], pa_ref[...], preferred_element_type=F32)
    b = jnp.dot(od_ref[...], pb_ref[...], preferred_element_type=F32)
    mix = sig(ga_ref[...].astype(F32)) * a + sig(gb_ref[...].astype(F32)) * b
    x1 = x_ref[...] + _mm(mix, wo_ref[...])
    x1_ref[...] = x1
    ms = jnp.mean(x1 * x1, axis=-1, keepdims=True)
    h2 = x1 * lax.rsqrt(ms + EPS) * g2_ref[...]
    h2_ref[...] = h2.astype(h2_ref.dtype)

    logits = _dot3(h2, wr_ref[...]) + br_ref[...]
    tm = logits.shape[0]
    lane = lax.broadcasted_iota(I32, (tm, LANES), 1).astype(F32)
    neg = -jnp.inf
    far = float(2 * LANES)
    rmax = lambda v: jnp.max(v, axis=1, keepdims=True)
    rmin = lambda v: jnp.min(v, axis=1, keepdims=True)
    gl = jnp.where(lane >= N_EXPERTS, jnp.where(lane < N_EXPERTS + N_GROUPS, logits, neg), neg)
    gmax = rmax(gl)
    g_w = 1.0 / jnp.sum(jnp.exp(gl - gmax), axis=1, keepdims=True)
    g_idx = rmin(jnp.where(gl == gmax, lane, far)) - N_EXPERTS
    e_lo = g_idx * EXPERTS_PER_GROUP
    el = jnp.where(lane >= e_lo, jnp.where(lane < e_lo + EXPERTS_PER_GROUP, logits, neg), neg)
    m1 = rmax(el)
    e1 = rmin(jnp.where(el == m1, lane, far))
    el2 = jnp.where(lane == e1, neg, el)
    m2 = rmax(el2)
    e2 = rmin(jnp.where(el2 == m2, lane, far))
    p2 = jnp.exp(m2 - m1)
    w1 = g_w / (1.0 + p2)
    gates_ref[...] = jnp.where(lane == e1, w1, jnp.where(lane == e2, w1 * p2, 0.0))


def _post(og, od, zm, x2, pa, pb, wo, g2, wr, br, tm):
    T, D = x2.shape
    row = lambda i: (i, 0)
    full = lambda i: (0, 0)
    return pl.pallas_call(
        _post_kernel,
        out_shape=(
            jax.ShapeDtypeStruct((T, D), F32),
            jax.ShapeDtypeStruct((T, D), MXU_DTYPE),
            jax.ShapeDtypeStruct((T, LANES), F32),
        ),
        grid=(T // tm,),
        in_specs=[
            pl.BlockSpec((tm, D), row),
            pl.BlockSpec((tm, D), row),
            pl.BlockSpec((tm, D), lambda i: (i, 3)),
            pl.BlockSpec((tm, D), lambda i: (i, 4)),
            pl.BlockSpec((tm, D), row),
            pl.BlockSpec((D, D), full),
            pl.BlockSpec((D, D), full),
            pl.BlockSpec((D, D), full),
            pl.BlockSpec((1, D), full),
            pl.BlockSpec((D, LANES), full),
            pl.BlockSpec((1, LANES), full),
        ],
        out_specs=(pl.BlockSpec((tm, D), row), pl.BlockSpec((tm, D), row), pl.BlockSpec((tm, LANES), row)),
        compiler_params=_params(("parallel",)),
        name="merge_router",
    )(og, od, zm, zm, x2, pa, pb, wo, g2, wr, br)


def _moe_kernel(h_ref, gates_ref, x1_ref, wg_ref, wu_ref, wd_ref, o_ref):
    e = pl.program_id(1)
    h = h_ref[...]
    lane = lax.broadcasted_iota(I32, gates_ref.shape, 1)
    g_e = jnp.sum(jnp.where(lane == e, gates_ref[...], 0.0), axis=1, keepdims=True)
    a = jnp.dot(h, wg_ref[...], preferred_element_type=F32)
    u = jnp.dot(h, wu_ref[...], preferred_element_type=F32)
    hid = a / (1.0 + jnp.exp(-a)) * u
    y = g_e * _mm(hid, wd_ref[...])

    @pl.when(e == 0)
    def _():
        o_ref[...] = x1_ref[...] + y

    @pl.when(e > 0)
    def _():
        o_ref[...] += y


def _moe(h2, gates, x1, wg, wu, wd, tm):
    T, D = x1.shape
    E, _, F = wg.shape
    row = lambda i, e: (i, 0)
    return pl.pallas_call(
        _moe_kernel,
        out_shape=jax.ShapeDtypeStruct((T, D), F32),
        grid=(T // tm, E),
        in_specs=[
            pl.BlockSpec((tm, D), row),
            pl.BlockSpec((tm, LANES), row),
            pl.BlockSpec((tm, D), row),
            pl.BlockSpec((None, D, F), lambda i, e: (e, 0, 0)),
            pl.BlockSpec((None, D, F), lambda i, e: (e, 0, 0)),
            pl.BlockSpec((None, F, D), lambda i, e: (e, 0, 0)),
        ],
        out_specs=pl.BlockSpec((tm, D), row),
        compiler_params=_params(("parallel", "arbitrary")),
        name="experts",
    )(h2, gates, x1, wg, wu, wd)


def _pad_cols(w, n):
    return jnp.pad(w, ((0, 0), (0, n - w.shape[1])))


def _layer(x2, B, S, p):
    T, D = x2.shape
    assert D == 2 * GLA_HEADS * GLA_DK == GLA_HEADS * GLA_DV == DSA_HEADS * DSA_HEAD_DIM
    assert S % 512 == 0
    cd = MXU_DTYPE
    qk_w, v_w, dsa_w = GLA_HEADS * GLA_DK, GLA_HEADS * GLA_DV, DSA_HEADS * DSA_HEAD_DIM
    splits = (qk_w, qk_w, v_w, GLA_GATE_RANK, v_w, dsa_w, dsa_w, dsa_w, IDX_Q_W, IDX_DIM, IDX_HEADS, D, D)
    offs = [0]
    for s in splits:
        offs.append(offs[-1] + s)
    w_in = p["w_in"]
    (w_gq, w_gk, w_gv, w_ga, w_gr, w_dq, w_dk, w_dv, w_iq, w_ik, w_iw, w_ta, w_tb) = [
        w_in[:, offs[i]:offs[i + 1]] for i in range(len(splits))]

    w_main = jnp.concatenate([w_gq, w_gk, w_gv, w_gr, w_ta, w_tb], axis=1).astype(cd)
    w_qk = jnp.concatenate([w_dq, w_dk], axis=1).astype(cd)
    g_qk = jnp.concatenate([jnp.tile(p["dsa_q_norm_g"] * (DSA_HEAD_DIM ** -0.5 * math.log2(math.e)), DSA_HEADS),
                            jnp.tile(p["dsa_k_norm_g"], DSA_HEADS)]).reshape(1, 2 * dsa_w).astype(F32)
    w_vt = w_dv.T.astype(cd)
    w_idx = jnp.concatenate([w_iq, _pad_cols(w_ik, LANES), _pad_cols(w_ga, LANES)], axis=1).astype(cd)
    w_iwt = jnp.pad(w_iw.T, ((0, 16 - IDX_HEADS), (0, 0))).astype(cd)

    tm = min(1024, T)
    h = _rmsnorm(x2, p["norm1_g"], tm)
    zm = _matmul(h, w_main, tm, 512, cd)
    qk = _qk_proj(h, w_qk, g_qk, min(512, T))
    vt = _vt_proj(h, w_vt, B, S, min(512, S))
    iq, ki, ga, wit = _idx_proj(h, w_idx, w_iwt, p["idx_k_ln_g"].reshape(1, -1), p["idx_k_ln_b"].reshape(1, -1),
                                min(512, T))

    o_gla = _gla(zm, ga, p["gla_w_a2"], p["gla_b_a"].reshape(1, -1), p["gla_norm_g"].reshape(1, -1), B, S, 512)

    QB, KB = 256, 512
    mask = _dsa_select(iq, wit, ki, B, S, QB, KB)
    o_dsa = _dsa_attn(qk, vt, mask, B, S, QB, KB)

    w_r = _pad_cols(jnp.concatenate([p["w_router_expert"], p["w_router_group"]], axis=1), LANES)
    b_r = _pad_cols(jnp.concatenate([p["b_router_expert"], p["b_router_group"]]).reshape(1, -1), LANES)
    x1, h2, gates = _post(o_gla, o_dsa, zm, x2, p["w_branch_gla"].astype(cd), p["w_branch_dsa"].astype(cd),
                          p["w_out"].astype(cd), p["norm2_g"].reshape(1, -1), w_r, b_r, min(512, T))
    return _moe(h2, gates, x1, p["w_exp_gate"].astype(cd), p["w_exp_up"].astype(cd), p["w_exp_down"].astype(cd),
                min(1024, T))


def kernel(x, norm1_g, w_in, gla_w_a2, gla_b_a, gla_norm_g, dsa_q_norm_g, dsa_k_norm_g, idx_k_ln_g, idx_k_ln_b,
           w_branch_gla, w_branch_dsa, w_out, norm2_g, w_router_group, b_router_group, w_router_expert,
           b_router_expert, w_exp_gate, w_exp_up, w_exp_down):
    B, S, D = x.shape
    stacked = dict(norm1_g=norm1_g, w_in=w_in, gla_w_a2=gla_w_a2, gla_b_a=gla_b_a, gla_norm_g=gla_norm_g,
                   dsa_q_norm_g=dsa_q_norm_g, dsa_k_norm_g=dsa_k_norm_g, idx_k_ln_g=idx_k_ln_g,
                   idx_k_ln_b=idx_k_ln_b, w_branch_gla=w_branch_gla, w_branch_dsa=w_branch_dsa, w_out=w_out,
                   norm2_g=norm2_g, w_router_group=w_router_group, b_router_group=b_router_group,
                   w_router_expert=w_router_expert, b_router_expert=b_router_expert, w_exp_gate=w_exp_gate,
                   w_exp_up=w_exp_up, w_exp_down=w_exp_down)
    x2 = x.reshape(B * S, D).astype(F32)
    for l in range(w_in.shape[0]):
        x2 = _layer(x2, B, S, {k: v[l] for k, v in stacked.items()})
    return x2.reshape(B, S, D).astype(x.dtype)
```

```python
import functools
import math

import jax
import jax.numpy as jnp
from jax import lax
from jax.experimental import pallas as pl
from jax.experimental.pallas import tpu as pltpu

F32 = jnp.float32
BF16 = jnp.bfloat16
I32 = jnp.int32
MXU_DTYPE = BF16

CHUNK = 64
EPS = 1e-6
GLA_HEADS, GLA_DK, GLA_DV = 4, 128, 256
GLA_GATE_RANK = 16
GLA_GATE_TEMP = 16.0
DSA_HEADS, DSA_HEAD_DIM = 8, 128
IDX_HEADS, IDX_DIM = 8, 64
TOPK_MAX = 256
N_GROUPS, EXPERTS_PER_GROUP = 4, 8
N_EXPERTS = N_GROUPS * EXPERTS_PER_GROUP
LANES = 128

INT_MIN = -(2 ** 31)
INT_MAX = 2 ** 31 - 1

VMEM_LIMIT = 56 * 1024 * 1024

_NT = (((1,), (1,)), ((), ()))
_TN = (((0,), (0,)), ((), ()))


def _params(sem):
    return pltpu.CompilerParams(dimension_semantics=sem, vmem_limit_bytes=VMEM_LIMIT)


def _mm(a, b):
    return jnp.dot(a.astype(MXU_DTYPE), b.astype(MXU_DTYPE), preferred_element_type=F32)


def _mm_nt(a, b):
    return lax.dot_general(a.astype(MXU_DTYPE), b.astype(MXU_DTYPE), _NT, preferred_element_type=F32)


def _mm_tn(a, b):
    return lax.dot_general(a.astype(MXU_DTYPE), b.astype(MXU_DTYPE), _TN, preferred_element_type=F32)


def _split(a):
    hi = a.astype(BF16)
    lo = (a - hi.astype(F32)).astype(BF16)
    return hi, lo


def _dot3(a, b):
    ah, al = _split(a)
    bh, bl = _split(b)
    d = lambda u, v: jnp.dot(u, v, preferred_element_type=F32)
    return d(ah, bh) + (d(ah, bl) + d(al, bh))


def _rmsnorm_kernel(x_ref, g_ref, o_ref):
    x = x_ref[...]
    ms = jnp.mean(x * x, axis=-1, keepdims=True)
    o_ref[...] = (x * lax.rsqrt(ms + EPS) * g_ref[...]).astype(o_ref.dtype)


def _rmsnorm(x2, g, tm):
    T, D = x2.shape
    return pl.pallas_call(
        _rmsnorm_kernel,
        out_shape=jax.ShapeDtypeStruct((T, D), MXU_DTYPE),
        grid=(T // tm,),
        in_specs=[pl.BlockSpec((tm, D), lambda i: (i, 0)), pl.BlockSpec((1, D), lambda i: (0, 0))],
        out_specs=pl.BlockSpec((tm, D), lambda i: (i, 0)),
        compiler_params=_params(("parallel",)),
        name="rmsnorm",
    )(x2, g.reshape(1, D))


def _matmul_kernel(h_ref, w_ref, o_ref):
    o_ref[...] = jnp.dot(h_ref[...], w_ref[...], preferred_element_type=F32).astype(o_ref.dtype)


def _matmul(h, w, tm, tn, out_dtype):
    T, D = h.shape
    N = w.shape[1]
    return pl.pallas_call(
        _matmul_kernel,
        out_shape=jax.ShapeDtypeStruct((T, N), out_dtype),
        grid=(T // tm, N // tn),
        in_specs=[pl.BlockSpec((tm, D), lambda i, j: (i, 0)), pl.BlockSpec((D, tn), lambda i, j: (0, j))],
        out_specs=pl.BlockSpec((tm, tn), lambda i, j: (i, j)),
        compiler_params=_params(("parallel", "arbitrary")),
        name="proj_main",
    )(h, w)


def _qk_kernel(h_ref, w_ref, g_ref, o_ref):
    z = jnp.dot(h_ref[...], w_ref[...], preferred_element_type=F32)
    for hh in range(DSA_HEADS):
        sl = slice(hh * DSA_HEAD_DIM, (hh + 1) * DSA_HEAD_DIM)
        zh = z[:, sl]
        ms = jnp.mean(zh * zh, axis=-1, keepdims=True)
        o_ref[:, sl] = (zh * lax.rsqrt(ms + EPS) * g_ref[:, sl]).astype(o_ref.dtype)


def _qk_proj(h, w, g, tm):
    T, D = h.shape
    W = DSA_HEADS * DSA_HEAD_DIM
    return pl.pallas_call(
        _qk_kernel,
        out_shape=jax.ShapeDtypeStruct((T, 2 * W), MXU_DTYPE),
        grid=(T // tm, 2),
        in_specs=[
            pl.BlockSpec((tm, D), lambda i, j: (i, 0)),
            pl.BlockSpec((D, W), lambda i, j: (0, j)),
            pl.BlockSpec((1, W), lambda i, j: (0, j)),
        ],
        out_specs=pl.BlockSpec((tm, W), lambda i, j: (i, j)),
        compiler_params=_params(("parallel", "arbitrary")),
        name="proj_qk",
    )(h, w, g)


VT_ONES = 16
VT_ROWS = DSA_HEAD_DIM + VT_ONES


def _vt_kernel(h_ref, wt_ref, o_ref):
    vt = lax.dot_general(wt_ref[...], h_ref[...], _NT, preferred_element_type=F32).astype(o_ref.dtype)
    ones = jnp.ones((VT_ONES, vt.shape[1]), o_ref.dtype)
    for hh in range(DSA_HEADS):
        o_ref[hh, :DSA_HEAD_DIM, :] = vt[hh * DSA_HEAD_DIM:(hh + 1) * DSA_HEAD_DIM]
        o_ref[hh, DSA_HEAD_DIM:, :] = ones


def _vt_proj(h, wt, B, S, tm):
    T, D = h.shape
    W = wt.shape[0]
    nt = S // tm
    return pl.pallas_call(
        _vt_kernel,
        out_shape=jax.ShapeDtypeStruct((B, DSA_HEADS, VT_ROWS, S), MXU_DTYPE),
        grid=(T // tm,),
        in_specs=[pl.BlockSpec((tm, D), lambda i: (i, 0)), pl.BlockSpec((W, D), lambda i: (0, 0))],
        out_specs=pl.BlockSpec((None, DSA_HEADS, VT_ROWS, tm), lambda i: (i // nt, 0, 0, i % nt)),
        compiler_params=_params(("parallel",)),
        name="proj_vt",
    )(h, wt)


IDX_Q_W = IDX_HEADS * IDX_DIM


def _idx_kernel(h_ref, w_ref, wiwt_ref, lng_ref, lnb_ref, iq_ref, ki_ref, ga_ref, wit_ref):
    h = h_ref[...]
    z = jnp.dot(h, w_ref[...], preferred_element_type=F32)
    iq_ref[...] = z[:, :IDX_Q_W].astype(iq_ref.dtype)
    ik = z[:, IDX_Q_W:IDX_Q_W + IDX_DIM]
    mu = jnp.mean(ik, axis=-1, keepdims=True)
    var = jnp.mean(jnp.square(ik - mu), axis=-1, keepdims=True)
    ki = (ik - mu) * lax.rsqrt(var + EPS) * lng_ref[...] + lnb_ref[...]
    ki_ref[...] = ki.astype(ki_ref.dtype)
    ga_ref[...] = z[:, IDX_Q_W + LANES:IDX_Q_W + LANES + GLA_GATE_RANK]
    wt = lax.dot_general(wiwt_ref[...], h, _NT, preferred_element_type=F32)
    wit_ref[...] = wt[:IDX_HEADS, :] * (IDX_HEADS ** -0.5 * IDX_DIM ** -0.5)


def _idx_proj(h, w, wiwt, lng, lnb, tm):
    T, D = h.shape
    N = w.shape[1]
    return pl.pallas_call(
        _idx_kernel,
        out_shape=(
            jax.ShapeDtypeStruct((T, IDX_Q_W), MXU_DTYPE),
            jax.ShapeDtypeStruct((T, IDX_DIM), MXU_DTYPE),
            jax.ShapeDtypeStruct((T, GLA_GATE_RANK), F32),
            jax.ShapeDtypeStruct((IDX_HEADS, T), F32),
        ),
        grid=(T // tm,),
        in_specs=[
            pl.BlockSpec((tm, D), lambda i: (i, 0)),
            pl.BlockSpec((D, N), lambda i: (0, 0)),
            pl.BlockSpec(wiwt.shape, lambda i: (0, 0)),
            pl.BlockSpec((1, IDX_DIM), lambda i: (0, 0)),
            pl.BlockSpec((1, IDX_DIM), lambda i: (0, 0)),
        ],
        out_specs=(
            pl.BlockSpec((tm, IDX_Q_W), lambda i: (i, 0)),
            pl.BlockSpec((tm, IDX_DIM), lambda i: (i, 0)),
            pl.BlockSpec((tm, GLA_GATE_RANK), lambda i: (i, 0)),
            pl.BlockSpec((IDX_HEADS, tm), lambda i: (0, i)),
        ),
        compiler_params=_params(("parallel",)),
        name="proj_idx",
    )(h, w, wiwt, lng, lnb)


def _gla_kernel(q_ref, k_ref, v_ref, r_ref, ga_ref, wa2_ref, ba_ref, ng_ref, o_ref, st_ref, la_ref, *, n_chunks):
    @pl.when(pl.program_id(1) == 0)
    def _():
        st_ref[...] = jnp.zeros_like(st_ref)

    x = _dot3(ga_ref[...], wa2_ref[...]) + ba_ref[...]
    la_ref[...] = (jnp.minimum(x, 0.0) - jnp.log(1.0 + jnp.exp(-jnp.abs(x)))) * (1.0 / GLA_GATE_TEMP)

    row = lax.broadcasted_iota(I32, (CHUNK, CHUNK), 0)
    col = lax.broadcasted_iota(I32, (CHUNK, CHUNK), 1)
    tril = jnp.where(row >= col, 1.0, 0.0).astype(BF16)
    ng = ng_ref[...]

    def chunk(c, carry):
        rows = pl.ds(pl.multiple_of(c * CHUNK, CHUNK), CHUNK)
        for hh in range(GLA_HEADS):
            ksl = slice(hh * GLA_DK, (hh + 1) * GLA_DK)
            vsl = slice(hh * GLA_DV, (hh + 1) * GLA_DV)
            la_hi, la_lo = _split(la_ref[rows, ksl])
            cum = jnp.dot(tril, la_hi, preferred_element_type=F32) + jnp.dot(tril, la_lo, preferred_element_type=F32)
            cum_end = cum[CHUNK - 1:CHUNK, :]
            k_dec = k_ref[rows, ksl].astype(F32) * jnp.exp(cum_end - cum)
            st = st_ref[hh] * jnp.exp(cum_end) + _mm_tn(v_ref[rows, vsl], k_dec)
            st_ref[hh] = st
            o = _mm_nt(q_ref[rows, ksl], st) * (GLA_DK ** -0.5)
            ms = jnp.mean(o * o, axis=-1, keepdims=True)
            r = r_ref[rows, vsl].astype(F32)
            y = o * lax.rsqrt(ms + EPS) * ng * (r / (1.0 + jnp.exp(-r)))
            o_ref[rows, vsl] = y.astype(o_ref.dtype)
        return carry

    lax.fori_loop(0, n_chunks, chunk, 0)


def _gla(zm, ga, wa2, ba, ng, B, S, tb):
    T = B * S
    nb = S // tb
    qk_w = GLA_HEADS * GLA_DK
    v_w = GLA_HEADS * GLA_DV
    tok = lambda b, i: b * nb + i
    return pl.pallas_call(
        functools.partial(_gla_kernel, n_chunks=tb // CHUNK),
        out_shape=jax.ShapeDtypeStruct((T, v_w), MXU_DTYPE),
        grid=(B, nb),
        in_specs=[
            pl.BlockSpec((tb, qk_w), lambda b, i: (tok(b, i), 0)),
            pl.BlockSpec((tb, qk_w), lambda b, i: (tok(b, i), 1)),
            pl.BlockSpec((tb, v_w), lambda b, i: (tok(b, i), 1)),
            pl.BlockSpec((tb, v_w), lambda b, i: (tok(b, i), 2)),
            pl.BlockSpec((tb, GLA_GATE_RANK), lambda b, i: (tok(b, i), 0)),
            pl.BlockSpec((GLA_GATE_RANK, qk_w), lambda b, i: (0, 0)),
            pl.BlockSpec((1, qk_w), lambda b, i: (0, 0)),
            pl.BlockSpec((1, GLA_DV), lambda b, i: (0, 0)),
        ],
        out_specs=pl.BlockSpec((tb, v_w), lambda b, i: (tok(b, i), 0)),
        scratch_shapes=[pltpu.VMEM((GLA_HEADS, GLA_DV, GLA_DK), F32), pltpu.VMEM((tb, qk_w), F32)],
        compiler_params=_params(("parallel", "arbitrary")),
        name="gla",
    )(zm, zm, zm, zm, ga, wa2, ba, ng)


def _last_kb(qb, QB, KB):
    return ((qb + 1) * QB - 1) // KB


def _causal_steps(S, QB, KB):
    pairs = [(q, k) for q in range(S // QB) for k in range(_last_kb(q, QB, KB) + 1)]
    qs, ks = zip(*pairs)
    return jnp.asarray(qs, I32), jnp.asarray(ks, I32)


def _select_kernel(qb_ref, kb_ref, iq_ref, wit_ref, ki_ref, mask_ref, key_ref, gm_ref, *, QB, KB, S, topk):
    qb = qb_ref[pl.program_id(1)]
    kb = kb_ref[pl.program_id(1)]
    last = _last_kb(qb, QB, KB)
    row_iota = lax.broadcasted_iota(I32, (KB, QB), 0)

    def rows_of(i):
        return pl.ds(pl.multiple_of(i * KB, KB), KB)

    @pl.when(kb == 0)
    def _():
        gm_ref[...] = jnp.full(gm_ref.shape, INT_MIN, I32)

    ki = ki_ref[...]
    sc = jnp.zeros((KB, QB), F32)
    for hh in range(IDX_HEADS):
        lg = _mm_nt(ki, iq_ref[:, hh * IDX_DIM:(hh + 1) * IDX_DIM])
        sc = sc + jnp.maximum(lg, 0.0) * wit_ref[hh:hh + 1, :]
    s_idx = kb * KB + row_iota
    t_idx = qb * QB + lax.broadcasted_iota(I32, (KB, QB), 1)
    adm = lax.shift_right_logical(s_idx, 6) <= lax.shift_right_logical(t_idx, 6)
    bits = lax.bitcast_convert_type(sc, I32)
    bits = jnp.where(bits == INT_MIN, 0, bits)
    key = bits ^ (lax.shift_right_arithmetic(bits, 31) & INT_MAX)
    key = jnp.where(adm, key, INT_MIN)
    key_ref[rows_of(kb), :] = key
    gm = gm_ref[...]
    for r in range(KB // topk):
        gm = jnp.maximum(gm, key[r * topk:(r + 1) * topk])
    gm_ref[...] = gm

    @pl.when(kb == last)
    def _():
        nblk = last + 1
        slab = 32

        def colsum(m):
            part = m[0:slab]
            for r in range(1, KB // slab):
                part = part + m[r * slab:(r + 1) * slab]
            return part

        def count(pred):
            def body(i, acc):
                return acc + colsum(pred(key_ref[rows_of(i), :], i))
            acc = lax.fori_loop(0, nblk, body, jnp.zeros((slab, QB), I32))
            return jnp.sum(acc, axis=0, keepdims=True)

        def count_ge(thr):
            return count(lambda tile, i: jnp.where(tile >= thr, 1, 0))

        gm = gm_ref[...]
        lo0 = jnp.maximum(jnp.min(gm, axis=0, keepdims=True), INT_MIN + 1)
        hi0 = jnp.max(gm, axis=0, keepdims=True) + 1
        state0 = (lo0, count_ge(lo0), hi0, jnp.zeros((1, QB), I32))

        def unfinished(st):
            lo, c_lo, hi, _c_hi = st
            return jnp.max(jnp.where(c_lo > topk, jnp.where(hi != lo + 1, 1, 0), 0)) > 0

        def bisect(st):
            lo, c_lo, hi, c_hi = st
            mid = (lo & hi) + lax.shift_right_arithmetic(lo ^ hi, 1)
            c = count_ge(mid)
            ok = c >= topk
            return jnp.where(ok, mid, lo), jnp.where(ok, c, c_lo), jnp.where(ok, hi, mid), jnp.where(ok, c_hi, c)

        vstar, c_ge, _hi, c_gt = lax.while_loop(unfinished, bisect, state0)
        excess = c_ge > topk
        need = jnp.where(excess, topk - c_gt, S + 1)

        def count_tie_lt(j):
            return count(lambda tile, i: jnp.where(tile == vstar, jnp.where(i * KB + row_iota < j, 1, 0), 0))

        def tie_cut():
            def step(_, c):
                lo, hi = c
                mid = lax.shift_right_arithmetic(lo + hi, 1)
                ok = count_tie_lt(mid) >= need
                return jnp.where(ok, lo, mid + 1), jnp.where(ok, mid, hi)
            n_steps = max(1, math.ceil(math.log2(S + 1)))
            lo, _hi = lax.fori_loop(0, n_steps, step, (jnp.zeros((1, QB), I32), jnp.full((1, QB), S, I32)))
            return lo

        any_excess = jnp.max(jnp.where(excess, 1, 0)) > 0
        jcut = lax.cond(any_excess, tie_cut, lambda: jnp.full((1, QB), S, I32))

        def write(i, carry):
            tile = key_ref[rows_of(i), :]
            tie = jnp.where(i * KB + row_iota < jcut, 1, 0)
            sel = jnp.where(tile > vstar, 1, jnp.where(tile == vstar, tie, 0))
            mask_ref[rows_of(i), :] = sel.astype(mask_ref.dtype)
            return carry

        lax.fori_loop(0, nblk, write, 0)

        def clear(i, carry):
            mask_ref[rows_of(i), :] = jnp.zeros((KB, QB), mask_ref.dtype)
            return carry

        lax.fori_loop(nblk, S // KB, clear, 0)


def _dsa_select(iq, wit, ki, B, S, QB, KB):
    nq, nk = S // QB, S // KB
    topk = min(TOPK_MAX, S // 4)
    assert KB % topk == 0
    qs, ks = _causal_steps(S, QB, KB)
    return pl.pallas_call(
        functools.partial(_select_kernel, QB=QB, KB=KB, S=S, topk=topk),
        out_shape=jax.ShapeDtypeStruct((B, S, S), jnp.int8),
        grid_spec=pltpu.PrefetchScalarGridSpec(
            num_scalar_prefetch=2,
            grid=(B, qs.shape[0]),
            in_specs=[
                pl.BlockSpec((QB, IDX_Q_W), lambda b, s, qs, ks: (b * nq + qs[s], 0)),
                pl.BlockSpec((IDX_HEADS, QB), lambda b, s, qs, ks: (0, b * nq + qs[s])),
                pl.BlockSpec((KB, IDX_DIM), lambda b, s, qs, ks: (b * nk + ks[s], 0)),
            ],
            out_specs=pl.BlockSpec((None, S, QB), lambda b, s, qs, ks: (b, 0, qs[s])),
            scratch_shapes=[pltpu.VMEM((S, QB), I32), pltpu.VMEM((topk, QB), I32)],
        ),
        compiler_params=_params(("parallel", "arbitrary")),
        name="dsa_select",
    )(qs, ks, iq, wit, ki)


def _attn_kernel(qb_ref, kb_ref, q_ref, k_ref, vt_ref, m_ref, o_ref, acc_ref, *, QB, KB):
    qb = qb_ref[pl.program_id(1)]
    kb = kb_ref[pl.program_id(1)]
    last = _last_kb(qb, QB, KB)

    @pl.when(kb == 0)
    def _():
        acc_ref[...] = jnp.zeros_like(acc_ref)

    mb = m_ref[...].astype(MXU_DTYPE)
    head = lambda hh: slice(hh * DSA_HEAD_DIM, (hh + 1) * DSA_HEAD_DIM)
    logits = lambda hh: _mm_nt(k_ref[:, head(hh)], q_ref[:, head(hh)])
    lg = logits(0)
    for hh in range(DSA_HEADS):
        lg_next = logits(hh + 1) if hh + 1 < DSA_HEADS else None
        p = jnp.exp2(lg).astype(MXU_DTYPE) * mb
        acc_ref[hh] += jnp.dot(vt_ref[hh], p, preferred_element_type=F32)
        lg = lg_next

    @pl.when(kb == last)
    def _():
        for hh in range(DSA_HEADS):
            acc = acc_ref[hh]
            o = acc[:DSA_HEAD_DIM] / acc[DSA_HEAD_DIM:DSA_HEAD_DIM + 1]
            o_ref[:, hh * DSA_HEAD_DIM:(hh + 1) * DSA_HEAD_DIM] = o.T.astype(o_ref.dtype)


def _dsa_attn(qk, vt, mask, B, S, QB, KB):
    T = B * S
    nq, nk = S // QB, S // KB
    W = DSA_HEADS * DSA_HEAD_DIM
    qs, ks = _causal_steps(S, QB, KB)
    return pl.pallas_call(
        functools.partial(_attn_kernel, QB=QB, KB=KB),
        out_shape=jax.ShapeDtypeStruct((T, W), MXU_DTYPE),
        grid_spec=pltpu.PrefetchScalarGridSpec(
            num_scalar_prefetch=2,
            grid=(B, qs.shape[0]),
            in_specs=[
                pl.BlockSpec((QB, W), lambda b, s, qs, ks: (b * nq + qs[s], 0)),
                pl.BlockSpec((KB, W), lambda b, s, qs, ks: (b * nk + ks[s], 1)),
                pl.BlockSpec((None, DSA_HEADS, VT_ROWS, KB), lambda b, s, qs, ks: (b, 0, 0, ks[s])),
                pl.BlockSpec((None, KB, QB), lambda b, s, qs, ks: (b, ks[s], qs[s])),
            ],
            out_specs=pl.BlockSpec((QB, W), lambda b, s, qs, ks: (b * nq + qs[s], 0)),
            scratch_shapes=[pltpu.VMEM((DSA_HEADS, VT_ROWS, QB), F32)],
        ),
        compiler_params=_params(("parallel", "arbitrary")),
        name="dsa_attn",
    )(qs, ks, qk, qk, vt, mask)


def _post_kernel(og_ref, od_ref, ga_ref, gb_ref, x_ref, pa_ref, pb_ref, wo_ref, g2_ref, wr_ref, br_ref,
                 x1_ref, h2_ref, route_ref):
    sig = lambda v: 1.0 / (1.0 + jnp.exp(-v))
    a = jnp.dot(og_ref[...], pa_ref[...], preferred_element_type=F32)
    b = jnp.dot(od_ref[...], pb_ref[...], preferred_element_type=F32)
    mix = sig(ga_ref[...].astype(F32)) * a + sig(gb_ref[...].astype(F32)) * b
    x1 = x_ref[...] + _mm(mix, wo_ref[...])
    x1_ref[...] = x1
    ms = jnp.mean(x1 * x1, axis=-1, keepdims=True)
    h2 = x1 * lax.rsqrt(ms + EPS) * g2_ref[...]
    h2_ref[...] = h2.astype(h2_ref.dtype)

    logits = _dot3(h2, wr_ref[...]) + br_ref[...]
    tm = logits.shape[0]
    lane = lax.broadcasted_iota(I32, (tm, LANES), 1).astype(F32)
    neg = -jnp.inf
    far = float(2 * LANES)
    rmax = lambda v: jnp.max(v, axis=1, keepdims=True)
    rmin = lambda v: jnp.min(v, axis=1, keepdims=True)
    gl = jnp.where(lane >= N_EXPERTS, jnp.where(lane < N_EXPERTS + N_GROUPS, logits, neg), neg)
    gmax = rmax(gl)
    g_w = 1.0 / jnp.sum(jnp.exp(gl - gmax), axis=1, keepdims=True)
    g_idx = rmin(jnp.where(gl == gmax, lane, far)) - N_EXPERTS
    e_lo = g_idx * EXPERTS_PER_GROUP
    el = jnp.where(lane >= e_lo, jnp.where(lane < e_lo + EXPERTS_PER_GROUP, logits, neg), neg)
    m1 = rmax(el)
    e1 = rmin(jnp.where(el == m1, lane, far))
    el2 = jnp.where(lane == e1, neg, el)
    m2 = rmax(el2)
    e2 = rmin(jnp.where(el2 == m2, lane, far))
    p2 = jnp.exp(m2 - m1)
    w1 = g_w / (1.0 + p2)
    route_ref[...] = jnp.where(lane == 0.0, e1, jnp.where(lane == 1.0, e2, jnp.where(
        lane == 2.0, w1, jnp.where(lane == 3.0, w1 * p2, 0.0))))


def _post(og, od, zm, x2, pa, pb, wo, g2, wr, br, tm):
    T, D = x2.shape
    row = lambda i: (i, 0)
    full = lambda i: (0, 0)
    return pl.pallas_call(
        _post_kernel,
        out_shape=(
            jax.ShapeDtypeStruct((T, D), F32),
            jax.ShapeDtypeStruct((T, D), F32),
            jax.ShapeDtypeStruct((T, LANES), F32),
        ),
        grid=(T // tm,),
        in_specs=[
            pl.BlockSpec((tm, D), row),
            pl.BlockSpec((tm, D), row),
            pl.BlockSpec((tm, D), lambda i: (i, 3)),
            pl.BlockSpec((tm, D), lambda i: (i, 4)),
            pl.BlockSpec((tm, D), row),
            pl.BlockSpec((D, D), full),
            pl.BlockSpec((D, D), full),
            pl.BlockSpec((D, D), full),
            pl.BlockSpec((1, D), full),
            pl.BlockSpec((D, LANES), full),
            pl.BlockSpec((1, LANES), full),
        ],
        out_specs=(pl.BlockSpec((tm, D), row), pl.BlockSpec((tm, D), row), pl.BlockSpec((tm, LANES), row)),
        compiler_params=_params(("parallel",)),
        name="merge_router",
    )(og, od, zm, zm, x2, pa, pb, wo, g2, wr, br)


MOE_TM = 256


def _gather_kernel(idx_ref, src_hbm, o_ref, buf_ref, sem):
    tm = o_ref.shape[0]

    def row_copy(r, src_row):
        return pltpu.make_async_copy(src_hbm.at[pl.ds(src_row, 1), :], buf_ref.at[pl.ds(r, 1), :], sem)

    def start(r, carry):
        row_copy(r, idx_ref[0, 0, r]).start()
        return carry

    lax.fori_loop(0, tm, start, 0, unroll=8)

    def wait(r, carry):
        row_copy(r, 0).wait()
        return carry

    lax.fori_loop(0, tm, wait, 0, unroll=8)
    o_ref[...] = buf_ref[...].astype(o_ref.dtype)


def _row_gather(src, idx, out_dtype, tm, name):
    n = idx.shape[0]
    D = src.shape[1]
    return pl.pallas_call(
        _gather_kernel,
        out_shape=jax.ShapeDtypeStruct((n, D), out_dtype),
        grid=(n // tm,),
        in_specs=[
            pl.BlockSpec((1, 1, tm), lambda i: (i, 0, 0), memory_space=pltpu.SMEM),
            pl.BlockSpec(memory_space=pl.ANY),
        ],
        out_specs=pl.BlockSpec((tm, D), lambda i: (i, 0)),
        scratch_shapes=[pltpu.VMEM((tm, D), src.dtype), pltpu.SemaphoreType.DMA(())],
        compiler_params=_params(("arbitrary",)),
        name=name,
    )(idx.reshape(n // tm, 1, tm), src)


def _route_plan(route, tm):
    n = 2 * route.shape[0]
    n_tiles = n // tm
    ids = route[:, :2].astype(I32).reshape(n)
    order = jnp.argsort(ids, stable=True).astype(I32)
    counts = jnp.sum((ids[:, None] == jnp.arange(N_EXPERTS, dtype=I32)[None, :]).astype(I32), axis=0)
    starts = (jnp.cumsum(counts) - counts).astype(I32)
    inv = jnp.zeros((n,), I32).at[order].set(jnp.arange(n, dtype=I32))
    bounds = jnp.sort(jnp.concatenate([jnp.arange(n_tiles, dtype=I32) * tm, starts]))
    ends = jnp.concatenate([bounds[1:], jnp.full((1,), n, I32)])
    seg_tile = jnp.minimum(bounds // tm, n_tiles - 1)
    seg_exp = jnp.clip(jnp.searchsorted(starts, bounds, side="right").astype(I32) - 1, 0, N_EXPERTS - 1)
    seg_first = jnp.concatenate([jnp.ones((1,), I32), (seg_tile[1:] != seg_tile[:-1]).astype(I32)])
    return order, inv, (seg_tile, seg_exp, bounds - seg_tile * tm, ends - seg_tile * tm, seg_first)


def _experts_kernel(tile_ref, exp_ref, lo_ref, hi_ref, first_ref, x_ref, wg_ref, wu_ref, wd_ref, o_ref):
    s = pl.program_id(0)
    lo, hi = lo_ref[s], hi_ref[s]

    @pl.when(first_ref[s] == 1)
    def _():
        o_ref[...] = jnp.zeros_like(o_ref)

    @pl.when(hi > lo)
    def _():
        x = x_ref[...]
        a = jnp.dot(x, wg_ref[...], preferred_element_type=F32)
        u = jnp.dot(x, wu_ref[...], preferred_element_type=F32)
        y = _mm(a / (1.0 + jnp.exp(-a)) * u, wd_ref[...])
        row = lax.broadcasted_iota(I32, y.shape, 0)
        o_ref[...] += jnp.where(row >= lo, jnp.where(row < hi, y, 0.0), 0.0)


def _experts(xs, segs, wg, wu, wd, tm):
    n, D = xs.shape
    F = wg.shape[2]
    tile_of = lambda s, tile, exp, lo, hi, first: (tile[s], 0)
    w_of = lambda s, tile, exp, lo, hi, first: (exp[s], 0, 0)
    return pl.pallas_call(
        _experts_kernel,
        out_shape=jax.ShapeDtypeStruct((n, D), F32),
        grid_spec=pltpu.PrefetchScalarGridSpec(
            num_scalar_prefetch=5,
            grid=(segs[0].shape[0],),
            in_specs=[
                pl.BlockSpec((tm, D), tile_of),
                pl.BlockSpec((None, D, F), w_of),
                pl.BlockSpec((None, D, F), w_of),
                pl.BlockSpec((None, F, D), w_of),
            ],
            out_specs=pl.BlockSpec((tm, D), tile_of),
        ),
        compiler_params=_params(("arbitrary",)),
        name="experts",
    )(*segs, xs, wg, wu, wd)


def _combine_kernel(x1_ref, y_ref, route_ref, o_ref):
    D = x1_ref.shape[1]
    r = route_ref[...]
    lane = lax.broadcasted_iota(I32, r.shape, 1)
    w1 = jnp.sum(jnp.where(lane == 2, r, 0.0), axis=1, keepdims=True)
    w2 = jnp.sum(jnp.where(lane == 3, r, 0.0), axis=1, keepdims=True)
    o_ref[...] = x1_ref[...] + (w1 * y_ref[:, :D] + w2 * y_ref[:, D:])


def _combine(x1, y2, route, tm):
    T, D = x1.shape
    row = lambda i: (i, 0)
    return pl.pallas_call(
        _combine_kernel,
        out_shape=jax.ShapeDtypeStruct((T, D), F32),
        grid=(T // tm,),
        in_specs=[pl.BlockSpec((tm, D), row), pl.BlockSpec((tm, 2 * D), row), pl.BlockSpec((tm, LANES), row)],
        out_specs=pl.BlockSpec((tm, D), row),
        compiler_params=_params(("parallel",)),
        name="moe_combine",
    )(x1, y2, route)


def _moe(h2, route, x1, wg, wu, wd):
    T, D = x1.shape
    tm = min(MOE_TM, 2 * T)
    order, inv, segs = _route_plan(route, tm)
    xs = _row_gather(h2, order // 2, MXU_DTYPE, tm, "moe_gather")
    ys = _experts(xs, segs, wg, wu, wd, tm)
    y2 = _row_gather(ys, inv, F32, tm, "moe_ungather").reshape(T, 2 * D)
    return _combine(x1, y2, route, min(512, T))


def _pad_cols(w, n):
    return jnp.pad(w, ((0, 0), (0, n - w.shape[1])))


def _layer(x2, B, S, p):
    T, D = x2.shape
    assert D == 2 * GLA_HEADS * GLA_DK == GLA_HEADS * GLA_DV == DSA_HEADS * DSA_HEAD_DIM
    assert S % 512 == 0
    cd = MXU_DTYPE
    qk_w, v_w, dsa_w = GLA_HEADS * GLA_DK, GLA_HEADS * GLA_DV, DSA_HEADS * DSA_HEAD_DIM
    splits = (qk_w, qk_w, v_w, GLA_GATE_RANK, v_w, dsa_w, dsa_w, dsa_w, IDX_Q_W, IDX_DIM, IDX_HEADS, D, D)
    offs = [0]
    for s in splits:
        offs.append(offs[-1] + s)
    w_in = p["w_in"]
    (w_gq, w_gk, w_gv, w_ga, w_gr, w_dq, w_dk, w_dv, w_iq, w_ik, w_iw, w_ta, w_tb) = [
        w_in[:, offs[i]:offs[i + 1]] for i in range(len(splits))]

    w_main = jnp.concatenate([w_gq, w_gk, w_gv, w_gr, w_ta, w_tb], axis=1).astype(cd)
    w_qk = jnp.concatenate([w_dq, w_dk], axis=1).astype(cd)
    g_qk = jnp.concatenate([jnp.tile(p["dsa_q_norm_g"] * (DSA_HEAD_DIM ** -0.5 * math.log2(math.e)), DSA_HEADS),
                            jnp.tile(p["dsa_k_norm_g"], DSA_HEADS)]).reshape(1, 2 * dsa_w).astype(F32)
    w_vt = w_dv.T.astype(cd)
    w_idx = jnp.concatenate([w_iq, _pad_cols(w_ik, LANES), _pad_cols(w_ga, LANES)], axis=1).astype(cd)
    w_iwt = jnp.pad(w_iw.T, ((0, 16 - IDX_HEADS), (0, 0))).astype(cd)

    tm = min(1024, T)
    h = _rmsnorm(x2, p["norm1_g"], tm)
    zm = _matmul(h, w_main, tm, 512, cd)
    qk = _qk_proj(h, w_qk, g_qk, min(512, T))
    vt = _vt_proj(h, w_vt, B, S, min(512, S))
    iq, ki, ga, wit = _idx_proj(h, w_idx, w_iwt, p["idx_k_ln_g"].reshape(1, -1), p["idx_k_ln_b"].reshape(1, -1),
                                min(512, T))

    o_gla = _gla(zm, ga, p["gla_w_a2"], p["gla_b_a"].reshape(1, -1), p["gla_norm_g"].reshape(1, -1), B, S, 512)

    QB, KB = 256, 512
    mask = _dsa_select(iq, wit, ki, B, S, QB, KB)
    o_dsa = _dsa_attn(qk, vt, mask, B, S, QB, KB)

    w_r = _pad_cols(jnp.concatenate([p["w_router_expert"], p["w_router_group"]], axis=1), LANES)
    b_r = _pad_cols(jnp.concatenate([p["b_router_expert"], p["b_router_group"]]).reshape(1, -1), LANES)
    x1, h2, route = _post(o_gla, o_dsa, zm, x2, p["w_branch_gla"].astype(cd), p["w_branch_dsa"].astype(cd),
                          p["w_out"].astype(cd), p["norm2_g"].reshape(1, -1), w_r, b_r, min(512, T))
    return _moe(h2, route, x1, p["w_exp_gate"].astype(cd), p["w_exp_up"].astype(cd), p["w_exp_down"].astype(cd))


def kernel(x, norm1_g, w_in, gla_w_a2, gla_b_a, gla_norm_g, dsa_q_norm_g, dsa_k_norm_g, idx_k_ln_g, idx_k_ln_b,
           w_branch_gla, w_branch_dsa, w_out, norm2_g, w_router_group, b_router_group, w_router_expert,
           b_router_expert, w_exp_gate, w_exp_up, w_exp_down):
    B, S, D = x.shape
    stacked = dict(norm1_g=norm1_g, w_in=w_in, gla_w_a2=gla_w_a2, gla_b_a=gla_b_a, gla_norm_g=gla_norm_g,
                   dsa_q_norm_g=dsa_q_norm_g, dsa_k_norm_g=dsa_k_norm_g, idx_k_ln_g=idx_k_ln_g,
                   idx_k_ln_b=idx_k_ln_b, w_branch_gla=w_branch_gla, w_branch_dsa=w_branch_dsa, w_out=w_out,
                   norm2_g=norm2_g, w_router_group=w_router_group, b_router_group=b_router_group,
                   w_router_expert=w_router_expert, b_router_expert=b_router_expert, w_exp_gate=w_exp_gate,
                   w_exp_up=w_exp_up, w_exp_down=w_exp_down)
    x2 = x.reshape(B * S, D).astype(F32)
    for l in range(w_in.shape[0]):
        x2 = _layer(x2, B, S, {k: v[l] for k, v in stacked.items()})
    return x2.reshape(B, S, D).astype(x.dtype)
```

```python
import functools
import math

import jax
import jax.numpy as jnp
from jax import lax
from jax.experimental import pallas as pl
from jax.experimental.pallas import tpu as pltpu

F32 = jnp.float32
BF16 = jnp.bfloat16
I32 = jnp.int32
MXU_DTYPE = BF16

CHUNK = 64
EPS = 1e-6
GLA_HEADS, GLA_DK, GLA_DV = 4, 128, 256
GLA_GATE_RANK = 16
GLA_GATE_TEMP = 16.0
DSA_HEADS, DSA_HEAD_DIM = 8, 128
IDX_HEADS, IDX_DIM = 8, 64
TOPK_MAX = 256
N_GROUPS, EXPERTS_PER_GROUP = 4, 8
N_EXPERTS = N_GROUPS * EXPERTS_PER_GROUP
LANES = 128

INT_MIN = -(2 ** 31)
INT_MAX = 2 ** 31 - 1

VMEM_LIMIT = 56 * 1024 * 1024

_NT = (((1,), (1,)), ((), ()))
_TN = (((0,), (0,)), ((), ()))


def _params(sem):
    return pltpu.CompilerParams(dimension_semantics=sem, vmem_limit_bytes=VMEM_LIMIT)


def _mm(a, b):
    return jnp.dot(a.astype(MXU_DTYPE), b.astype(MXU_DTYPE), preferred_element_type=F32)


def _mm_nt(a, b):
    return lax.dot_general(a.astype(MXU_DTYPE), b.astype(MXU_DTYPE), _NT, preferred_element_type=F32)


def _mm_tn(a, b):
    return lax.dot_general(a.astype(MXU_DTYPE), b.astype(MXU_DTYPE), _TN, preferred_element_type=F32)


def _split(a):
    hi = a.astype(BF16)
    lo = (a - hi.astype(F32)).astype(BF16)
    return hi, lo


def _dot3(a, b):
    ah, al = _split(a)
    bh, bl = _split(b)
    d = lambda u, v: jnp.dot(u, v, preferred_element_type=F32)
    return d(ah, bh) + (d(ah, bl) + d(al, bh))


def _rmsnorm_kernel(x_ref, g_ref, o_ref):
    x = x_ref[...]
    ms = jnp.mean(x * x, axis=-1, keepdims=True)
    o_ref[...] = (x * lax.rsqrt(ms + EPS) * g_ref[...]).astype(o_ref.dtype)


def _rmsnorm(x2, g, tm):
    T, D = x2.shape
    return pl.pallas_call(
        _rmsnorm_kernel,
        out_shape=jax.ShapeDtypeStruct((T, D), MXU_DTYPE),
        grid=(T // tm,),
        in_specs=[pl.BlockSpec((tm, D), lambda i: (i, 0)), pl.BlockSpec((1, D), lambda i: (0, 0))],
        out_specs=pl.BlockSpec((tm, D), lambda i: (i, 0)),
        compiler_params=_params(("parallel",)),
        name="rmsnorm",
    )(x2, g.reshape(1, D))


def _matmul_kernel(h_ref, w_ref, o_ref):
    o_ref[...] = jnp.dot(h_ref[...], w_ref[...], preferred_element_type=F32).astype(o_ref.dtype)


def _matmul(h, w, tm, tn, out_dtype):
    T, D = h.shape
    N = w.shape[1]
    return pl.pallas_call(
        _matmul_kernel,
        out_shape=jax.ShapeDtypeStruct((T, N), out_dtype),
        grid=(T // tm, N // tn),
        in_specs=[pl.BlockSpec((tm, D), lambda i, j: (i, 0)), pl.BlockSpec((D, tn), lambda i, j: (0, j))],
        out_specs=pl.BlockSpec((tm, tn), lambda i, j: (i, j)),
        compiler_params=_params(("parallel", "arbitrary")),
        name="proj_main",
    )(h, w)


def _qk_kernel(h_ref, w_ref, g_ref, o_ref):
    z = jnp.dot(h_ref[...], w_ref[...], preferred_element_type=F32)
    for hh in range(DSA_HEADS):
        sl = slice(hh * DSA_HEAD_DIM, (hh + 1) * DSA_HEAD_DIM)
        zh = z[:, sl]
        ms = jnp.mean(zh * zh, axis=-1, keepdims=True)
        o_ref[:, sl] = (zh * lax.rsqrt(ms + EPS) * g_ref[:, sl]).astype(o_ref.dtype)


def _qk_proj(h, w, g, tm):
    T, D = h.shape
    W = DSA_HEADS * DSA_HEAD_DIM
    return pl.pallas_call(
        _qk_kernel,
        out_shape=jax.ShapeDtypeStruct((T, 2 * W), MXU_DTYPE),
        grid=(T // tm, 2),
        in_specs=[
            pl.BlockSpec((tm, D), lambda i, j: (i, 0)),
            pl.BlockSpec((D, W), lambda i, j: (0, j)),
            pl.BlockSpec((1, W), lambda i, j: (0, j)),
        ],
        out_specs=pl.BlockSpec((tm, W), lambda i, j: (i, j)),
        compiler_params=_params(("parallel", "arbitrary")),
        name="proj_qk",
    )(h, w, g)


VT_ONES = 16
VT_ROWS = DSA_HEAD_DIM + VT_ONES


def _vt_kernel(h_ref, wt_ref, o_ref):
    vt = lax.dot_general(wt_ref[...], h_ref[...], _NT, preferred_element_type=F32).astype(o_ref.dtype)
    ones = jnp.ones((VT_ONES, vt.shape[1]), o_ref.dtype)
    for hh in range(DSA_HEADS):
        o_ref[hh, :DSA_HEAD_DIM, :] = vt[hh * DSA_HEAD_DIM:(hh + 1) * DSA_HEAD_DIM]
        o_ref[hh, DSA_HEAD_DIM:, :] = ones


def _vt_proj(h, wt, B, S, tm):
    T, D = h.shape
    W = wt.shape[0]
    nt = S // tm
    return pl.pallas_call(
        _vt_kernel,
        out_shape=jax.ShapeDtypeStruct((B, DSA_HEADS, VT_ROWS, S), MXU_DTYPE),
        grid=(T // tm,),
        in_specs=[pl.BlockSpec((tm, D), lambda i: (i, 0)), pl.BlockSpec((W, D), lambda i: (0, 0))],
        out_specs=pl.BlockSpec((None, DSA_HEADS, VT_ROWS, tm), lambda i: (i // nt, 0, 0, i % nt)),
        compiler_params=_params(("parallel",)),
        name="proj_vt",
    )(h, wt)


IDX_Q_W = IDX_HEADS * IDX_DIM


def _idx_kernel(h_ref, w_ref, wt_ref, lng_ref, lnb_ref, iqt_ref, ki_ref, ga_ref, wit_ref):
    h = h_ref[...]
    z = jnp.dot(h, w_ref[...], preferred_element_type=F32)
    ik = z[:, :IDX_DIM]
    mu = jnp.mean(ik, axis=-1, keepdims=True)
    var = jnp.mean(jnp.square(ik - mu), axis=-1, keepdims=True)
    ki = (ik - mu) * lax.rsqrt(var + EPS) * lng_ref[...] + lnb_ref[...]
    ki_ref[...] = ki.astype(ki_ref.dtype)
    ga_ref[...] = z[:, LANES:LANES + GLA_GATE_RANK]
    zt = lax.dot_general(wt_ref[...], h, _NT, preferred_element_type=F32)
    iqt_ref[...] = zt[:IDX_Q_W].astype(iqt_ref.dtype)
    wit_ref[...] = zt[IDX_Q_W:IDX_Q_W + IDX_HEADS] * (IDX_HEADS ** -0.5 * IDX_DIM ** -0.5)


def _idx_proj(h, w, wt, lng, lnb, tm):
    T, D = h.shape
    return pl.pallas_call(
        _idx_kernel,
        out_shape=(
            jax.ShapeDtypeStruct((IDX_Q_W, T), MXU_DTYPE),
            jax.ShapeDtypeStruct((T, IDX_DIM), MXU_DTYPE),
            jax.ShapeDtypeStruct((T, GLA_GATE_RANK), F32),
            jax.ShapeDtypeStruct((IDX_HEADS, T), F32),
        ),
        grid=(T // tm,),
        in_specs=[
            pl.BlockSpec((tm, D), lambda i: (i, 0)),
            pl.BlockSpec(w.shape, lambda i: (0, 0)),
            pl.BlockSpec(wt.shape, lambda i: (0, 0)),
            pl.BlockSpec((1, IDX_DIM), lambda i: (0, 0)),
            pl.BlockSpec((1, IDX_DIM), lambda i: (0, 0)),
        ],
        out_specs=(
            pl.BlockSpec((IDX_Q_W, tm), lambda i: (0, i)),
            pl.BlockSpec((tm, IDX_DIM), lambda i: (i, 0)),
            pl.BlockSpec((tm, GLA_GATE_RANK), lambda i: (i, 0)),
            pl.BlockSpec((IDX_HEADS, tm), lambda i: (0, i)),
        ),
        compiler_params=_params(("parallel",)),
        name="proj_idx",
    )(h, w, wt, lng, lnb)


def _gla_kernel(q_ref, k_ref, v_ref, r_ref, ga_ref, wa2_ref, ba_ref, ng_ref, o_ref, st_ref, tot_ref, kd_ref,
                oraw_ref, *, n_chunks):
    tb = q_ref.shape[0]

    @pl.when(pl.program_id(1) == 0)
    def _():
        st_ref[...] = jnp.zeros_like(st_ref)

    x = _dot3(ga_ref[...], wa2_ref[...]) + ba_ref[...]
    la = (jnp.minimum(x, 0.0) - jnp.log(1.0 + jnp.exp(-jnp.abs(x)))) * (1.0 / GLA_GATE_TEMP)
    row = lax.broadcasted_iota(I32, (tb, tb), 0)
    col = lax.broadcasted_iota(I32, (tb, tb), 1)
    same = lax.shift_right_logical(row, 6) == lax.shift_right_logical(col, 6)
    ones_blk = jnp.where(same, 1.0, 0.0).astype(BF16)
    tril_blk = jnp.where(same, jnp.where(row >= col, 1.0, 0.0), 0.0).astype(BF16)
    la_hi, la_lo = _split(la)
    d = lambda u, v: jnp.dot(u, v, preferred_element_type=F32)
    tot = d(ones_blk, la_hi) + d(ones_blk, la_lo)
    cum = d(tril_blk, la_hi) + d(tril_blk, la_lo)
    tot_ref[...] = tot
    kd_ref[...] = (k_ref[...].astype(F32) * jnp.exp(tot - cum)).astype(kd_ref.dtype)

    heads = range(GLA_HEADS)
    ksl = lambda hh: slice(hh * GLA_DK, (hh + 1) * GLA_DK)
    vsl = lambda hh: slice(hh * GLA_DV, (hh + 1) * GLA_DV)

    def chunk(c, carry):
        rows = pl.ds(pl.multiple_of(c * CHUNK, CHUNK), CHUNK)
        first = pl.ds(pl.multiple_of(c * CHUNK, CHUNK), 1)
        upd = [_mm_tn(v_ref[rows, vsl(hh)], kd_ref[rows, ksl(hh)]) for hh in heads]
        st = [st_ref[hh] * jnp.exp(tot_ref[first, ksl(hh)]) + upd[hh] for hh in heads]
        for hh in heads:
            st_ref[hh] = st[hh]
        for hh in heads:
            oraw_ref[rows, vsl(hh)] = _mm_nt(q_ref[rows, ksl(hh)], st[hh])
        return carry

    lax.fori_loop(0, n_chunks, chunk, 0)

    ng = ng_ref[...]
    for hh in heads:
        o = oraw_ref[:, vsl(hh)] * (GLA_DK ** -0.5)
        ms = jnp.mean(o * o, axis=-1, keepdims=True)
        r = r_ref[:, vsl(hh)].astype(F32)
        o_ref[:, vsl(hh)] = (o * lax.rsqrt(ms + EPS) * ng * (r / (1.0 + jnp.exp(-r)))).astype(o_ref.dtype)


def _gla(zm, ga, wa2, ba, ng, B, S, tb):
    T = B * S
    nb = S // tb
    qk_w = GLA_HEADS * GLA_DK
    v_w = GLA_HEADS * GLA_DV
    tok = lambda b, i: b * nb + i
    return pl.pallas_call(
        functools.partial(_gla_kernel, n_chunks=tb // CHUNK),
        out_shape=jax.ShapeDtypeStruct((T, v_w), MXU_DTYPE),
        grid=(B, nb),
        in_specs=[
            pl.BlockSpec((tb, qk_w), lambda b, i: (tok(b, i), 0)),
            pl.BlockSpec((tb, qk_w), lambda b, i: (tok(b, i), 1)),
            pl.BlockSpec((tb, v_w), lambda b, i: (tok(b, i), 1)),
            pl.BlockSpec((tb, v_w), lambda b, i: (tok(b, i), 2)),
            pl.BlockSpec((tb, GLA_GATE_RANK), lambda b, i: (tok(b, i), 0)),
            pl.BlockSpec((GLA_GATE_RANK, qk_w), lambda b, i: (0, 0)),
            pl.BlockSpec((1, qk_w), lambda b, i: (0, 0)),
            pl.BlockSpec((1, GLA_DV), lambda b, i: (0, 0)),
        ],
        out_specs=pl.BlockSpec((tb, v_w), lambda b, i: (tok(b, i), 0)),
        scratch_shapes=[pltpu.VMEM((GLA_HEADS, GLA_DV, GLA_DK), F32), pltpu.VMEM((tb, qk_w), F32),
                        pltpu.VMEM((tb, qk_w), MXU_DTYPE), pltpu.VMEM((tb, v_w), F32)],
        compiler_params=_params(("parallel", "arbitrary")),
        name="gla",
    )(zm, zm, zm, zm, ga, wa2, ba, ng)


def _last_kb(qb, QB, KB):
    return ((qb + 1) * QB - 1) // KB


def _causal_steps(S, QB, KB):
    pairs = [(q, k) for q in range(S // QB) for k in range(_last_kb(q, QB, KB) + 1)]
    qs, ks = zip(*pairs)
    return jnp.asarray(qs, I32), jnp.asarray(ks, I32)


def _select_kernel(qb_ref, kb_ref, iqt_ref, wit_ref, ki_ref, mask_ref, key_ref, gm_ref, *, QB, KB, S, topk):
    qb = qb_ref[pl.program_id(1)]
    kb = kb_ref[pl.program_id(1)]
    last = _last_kb(qb, QB, KB)
    row_iota = lax.broadcasted_iota(I32, (KB, QB), 0)

    def rows_of(i):
        return pl.ds(pl.multiple_of(i * KB, KB), KB)

    @pl.when(kb == 0)
    def _():
        gm_ref[...] = jnp.full(gm_ref.shape, INT_MIN, I32)

    rc = 128
    assert topk % rc == 0 and KB % rc == 0
    t_chunk = lax.shift_right_logical(qb * QB + lax.broadcasted_iota(I32, (rc, QB), 1), 6)
    for c in range(KB // rc):
        ki = ki_ref[c * rc:(c + 1) * rc, :]
        sc = jnp.zeros((rc, QB), F32)
        for hh in range(IDX_HEADS):
            lg = _mm(ki, iqt_ref[hh * IDX_DIM:(hh + 1) * IDX_DIM, :])
            sc = sc + jnp.maximum(lg, 0.0) * wit_ref[hh:hh + 1, :]
        s_chunk = lax.shift_right_logical(kb * KB + c * rc + lax.broadcasted_iota(I32, (rc, QB), 0), 6)
        bits = lax.bitcast_convert_type(sc, I32)
        sign = lax.shift_right_arithmetic(bits, 31)
        key = (bits ^ (sign & INT_MAX)) - sign
        key = jnp.where(s_chunk <= t_chunk, key, INT_MIN)
        key_ref[pl.ds(pl.multiple_of(kb * KB + c * rc, rc), rc), :] = key
        g0 = (c * rc) % topk
        gm_ref[g0:g0 + rc, :] = jnp.maximum(gm_ref[g0:g0 + rc, :], key)

    @pl.when(kb == last)
    def _():
        nblk = last + 1
        slab = 32

        def colsum(m):
            part = m[0:slab]
            for r in range(1, KB // slab):
                part = part + m[r * slab:(r + 1) * slab]
            return part

        def count(pred):
            def body(i, acc):
                return acc + colsum(pred(key_ref[rows_of(i), :], i))
            acc = lax.fori_loop(0, nblk, body, jnp.zeros((slab, QB), I32))
            return jnp.sum(acc, axis=0, keepdims=True)

        def count_ge(thr):
            return count(lambda tile, i: jnp.where(tile >= thr, 1, 0))

        gm = gm_ref[...]
        lo0 = jnp.maximum(jnp.min(gm, axis=0, keepdims=True), INT_MIN + 1)
        hi0 = jnp.max(gm, axis=0, keepdims=True) + 1
        state0 = (jnp.int32(0), lo0, count_ge(lo0), hi0, jnp.zeros((1, QB), I32))

        def is_open(lo, c_lo, hi):
            return jnp.where(c_lo > topk, jnp.where(hi != lo + 1, 1, 0), 0)

        def unfinished(st):
            _it, lo, c_lo, hi, _c_hi = st
            return jnp.max(is_open(lo, c_lo, hi)) > 0

        def extreme(pick, reduce, fill):
            def body(i, acc):
                m = jnp.where(pick(key_ref[rows_of(i), :]), key_ref[rows_of(i), :], fill)
                part = m[0:slab]
                for r in range(1, KB // slab):
                    part = reduce(part, m[r * slab:(r + 1) * slab])
                return reduce(acc, part)
            acc = lax.fori_loop(0, nblk, body, jnp.full((slab, QB), fill, I32))
            return acc

        def tighten(lo, c_lo, hi):
            lo_t = jnp.min(extreme(lambda t: t >= lo, jnp.minimum, INT_MAX), axis=0, keepdims=True)
            hi_t = jnp.max(extreme(lambda t: t < hi, jnp.maximum, INT_MIN), axis=0, keepdims=True) + 1
            open_ = is_open(lo, c_lo, hi) > 0
            return jnp.where(open_, lo_t, lo), jnp.where(open_, hi_t, hi)

        tighten_at = 16

        def bisect(st):
            it, lo, c_lo, hi, c_hi = st
            lo, hi = lax.cond(it == tighten_at, lambda: tighten(lo, c_lo, hi), lambda: (lo, hi))
            mid = (lo & hi) + lax.shift_right_arithmetic(lo ^ hi, 1)
            c = count_ge(mid)
            ok = c >= topk
            return (it + 1, jnp.where(ok, mid, lo), jnp.where(ok, c, c_lo), jnp.where(ok, hi, mid),
                    jnp.where(ok, c_hi, c))

        _it, vstar, c_ge, _hi, c_gt = lax.while_loop(unfinished, bisect, state0)
        excess = c_ge > topk
        need = jnp.where(excess, topk - c_gt, S + 1)

        def count_tie_lt(j):
            return count(lambda tile, i: jnp.where(tile == vstar, jnp.where(i * KB + row_iota < j, 1, 0), 0))

        def tie_cut():
            def step(_, c):
                lo, hi = c
                mid = lax.shift_right_arithmetic(lo + hi, 1)
                ok = count_tie_lt(mid) >= need
                return jnp.where(ok, lo, mid + 1), jnp.where(ok, mid, hi)
            n_steps = max(1, math.ceil(math.log2(S + 1)))
            lo, _hi = lax.fori_loop(0, n_steps, step, (jnp.zeros((1, QB), I32), jnp.full((1, QB), S, I32)))
            return lo

        any_excess = jnp.max(jnp.where(excess, 1, 0)) > 0
        jcut = lax.cond(any_excess, tie_cut, lambda: jnp.full((1, QB), S, I32))

        def write(i, carry):
            tile = key_ref[rows_of(i), :]
            tie = jnp.where(i * KB + row_iota < jcut, 1, 0)
            sel = jnp.where(tile > vstar, 1, jnp.where(tile == vstar, tie, 0))
            mask_ref[rows_of(i), :] = sel.astype(mask_ref.dtype)
            return carry

        lax.fori_loop(0, nblk, write, 0)

        def clear(i, carry):
            mask_ref[rows_of(i), :] = jnp.zeros((KB, QB), mask_ref.dtype)
            return carry

        lax.fori_loop(nblk, S // KB, clear, 0)


def _dsa_select(iqt, wit, ki, B, S, QB, KB):
    nq, nk = S // QB, S // KB
    topk = min(TOPK_MAX, S // 4)
    assert KB % topk == 0
    qs, ks = _causal_steps(S, QB, KB)
    return pl.pallas_call(
        functools.partial(_select_kernel, QB=QB, KB=KB, S=S, topk=topk),
        out_shape=jax.ShapeDtypeStruct((B, S, S), jnp.int8),
        grid_spec=pltpu.PrefetchScalarGridSpec(
            num_scalar_prefetch=2,
            grid=(B, qs.shape[0]),
            in_specs=[
                pl.BlockSpec((IDX_Q_W, QB), lambda b, s, qs, ks: (0, b * nq + qs[s])),
                pl.BlockSpec((IDX_HEADS, QB), lambda b, s, qs, ks: (0, b * nq + qs[s])),
                pl.BlockSpec((KB, IDX_DIM), lambda b, s, qs, ks: (b * nk + ks[s], 0)),
            ],
            out_specs=pl.BlockSpec((None, S, QB), lambda b, s, qs, ks: (b, 0, qs[s])),
            scratch_shapes=[pltpu.VMEM((S, QB), I32), pltpu.VMEM((topk, QB), I32)],
        ),
        compiler_params=_params(("parallel", "arbitrary")),
        name="dsa_select",
    )(qs, ks, iqt, wit, ki)


def _attn_kernel(qb_ref, kb_ref, q_ref, k_ref, vt_ref, m_ref, o_ref, acc_ref, *, QB, KB):
    qb = qb_ref[pl.program_id(1)]
    kb = kb_ref[pl.program_id(1)]
    last = _last_kb(qb, QB, KB)

    @pl.when(kb == 0)
    def _():
        acc_ref[...] = jnp.zeros_like(acc_ref)

    mb = m_ref[...].astype(MXU_DTYPE)
    head = lambda hh: slice(hh * DSA_HEAD_DIM, (hh + 1) * DSA_HEAD_DIM)
    logits = lambda hh: _mm_nt(k_ref[:, head(hh)], q_ref[:, head(hh)])
    lg = logits(0)
    for hh in range(DSA_HEADS):
        lg_next = logits(hh + 1) if hh + 1 < DSA_HEADS else None
        p = jnp.exp2(lg).astype(MXU_DTYPE) * mb
        acc_ref[hh] += jnp.dot(vt_ref[hh], p, preferred_element_type=F32)
        lg = lg_next

    @pl.when(kb == last)
    def _():
        for hh in range(DSA_HEADS):
            acc = acc_ref[hh]
            o = acc[:DSA_HEAD_DIM] / acc[DSA_HEAD_DIM:DSA_HEAD_DIM + 1]
            o_ref[:, hh * DSA_HEAD_DIM:(hh + 1) * DSA_HEAD_DIM] = o.T.astype(o_ref.dtype)


def _dsa_attn(qk, vt, mask, B, S, QB, KB):
    T = B * S
    nq, nk = S // QB, S // KB
    W = DSA_HEADS * DSA_HEAD_DIM
    qs, ks = _causal_steps(S, QB, KB)
    return pl.pallas_call(
        functools.partial(_attn_kernel, QB=QB, KB=KB),
        out_shape=jax.ShapeDtypeStruct((T, W), MXU_DTYPE),
        grid_spec=pltpu.PrefetchScalarGridSpec(
            num_scalar_prefetch=2,
            grid=(B, qs.shape[0]),
            in_specs=[
                pl.BlockSpec((QB, W), lambda b, s, qs, ks: (b * nq + qs[s], 0)),
                pl.BlockSpec((KB, W), lambda b, s, qs, ks: (b * nk + ks[s], 1)),
                pl.BlockSpec((None, DSA_HEADS, VT_ROWS, KB), lambda b, s, qs, ks: (b, 0, 0, ks[s])),
                pl.BlockSpec((None, KB, QB), lambda b, s, qs, ks: (b, ks[s], qs[s])),
            ],
            out_specs=pl.BlockSpec((QB, W), lambda b, s, qs, ks: (b * nq + qs[s], 0)),
            scratch_shapes=[pltpu.VMEM((DSA_HEADS, VT_ROWS, QB), F32)],
        ),
        compiler_params=_params(("parallel", "arbitrary")),
        name="dsa_attn",
    )(qs, ks, qk, qk, vt, mask)


def _post_kernel(og_ref, od_ref, ga_ref, gb_ref, x_ref, pa_ref, pb_ref, wo_ref, g2_ref, wr_ref, br_ref,
                 x1_ref, h2_ref, route_ref):
    sig = lambda v: 1.0 / (1.0 + jnp.exp(-v))
    a = jnp.dot(og_ref[...], pa_ref[...], preferred_element_type=F32)
    b = jnp.dot(od_ref[...], pb_ref[...], preferred_element_type=F32)
    mix = sig(ga_ref[...].astype(F32)) * a + sig(gb_ref[...].astype(F32)) * b
    x1 = x_ref[...] + _mm(mix, wo_ref[...])
    x1_ref[...] = x1
    ms = jnp.mean(x1 * x1, axis=-1, keepdims=True)
    h2 = x1 * lax.rsqrt(ms + EPS) * g2_ref[...]
    h2_ref[...] = h2.astype(h2_ref.dtype)

    logits = _dot3(h2, wr_ref[...]) + br_ref[...]
    tm = logits.shape[0]
    lane = lax.broadcasted_iota(I32, (tm, LANES), 1).astype(F32)
    neg = -jnp.inf
    far = float(2 * LANES)
    rmax = lambda v: jnp.max(v, axis=1, keepdims=True)
    rmin = lambda v: jnp.min(v, axis=1, keepdims=True)
    gl = jnp.where(lane >= N_EXPERTS, jnp.where(lane < N_EXPERTS + N_GROUPS, logits, neg), neg)
    gmax = rmax(gl)
    g_w = 1.0 / jnp.sum(jnp.exp(gl - gmax), axis=1, keepdims=True)
    g_idx = rmin(jnp.where(gl == gmax, lane, far)) - N_EXPERTS
    e_lo = g_idx * EXPERTS_PER_GROUP
    el = jnp.where(lane >= e_lo, jnp.where(lane < e_lo + EXPERTS_PER_GROUP, logits, neg), neg)
    m1 = rmax(el)
    e1 = rmin(jnp.where(el == m1, lane, far))
    el2 = jnp.where(lane == e1, neg, el)
    m2 = rmax(el2)
    e2 = rmin(jnp.where(el2 == m2, lane, far))
    p2 = jnp.exp(m2 - m1)
    w1 = g_w / (1.0 + p2)
    route_ref[...] = jnp.where(lane == 0.0, e1, jnp.where(lane == 1.0, e2, jnp.where(
        lane == 2.0, w1, jnp.where(lane == 3.0, w1 * p2, 0.0))))


def _post(og, od, zm, x2, pa, pb, wo, g2, wr, br, tm):
    T, D = x2.shape
    row = lambda i: (i, 0)
    full = lambda i: (0, 0)
    return pl.pallas_call(
        _post_kernel,
        out_shape=(
            jax.ShapeDtypeStruct((T, D), F32),
            jax.ShapeDtypeStruct((T, D), F32),
            jax.ShapeDtypeStruct((T, LANES), F32),
        ),
        grid=(T // tm,),
        in_specs=[
            pl.BlockSpec((tm, D), row),
            pl.BlockSpec((tm, D), row),
            pl.BlockSpec((tm, D), lambda i: (i, 3)),
            pl.BlockSpec((tm, D), lambda i: (i, 4)),
            pl.BlockSpec((tm, D), row),
            pl.BlockSpec((D, D), full),
            pl.BlockSpec((D, D), full),
            pl.BlockSpec((D, D), full),
            pl.BlockSpec((1, D), full),
            pl.BlockSpec((D, LANES), full),
            pl.BlockSpec((1, LANES), full),
        ],
        out_specs=(pl.BlockSpec((tm, D), row), pl.BlockSpec((tm, D), row), pl.BlockSpec((tm, LANES), row)),
        compiler_params=_params(("parallel",)),
        name="merge_router",
    )(og, od, zm, zm, x2, pa, pb, wo, g2, wr, br)


MOE_TM = 256


def _gather_kernel(idx_ref, nxt_ref, src_hbm, o_ref, buf_ref, sem):
    i = pl.program_id(0)
    n = pl.num_programs(0)
    tm = o_ref.shape[0]
    slot = lax.rem(i, 2)

    def fetch(ids_ref, s):
        def start(r, carry):
            pltpu.make_async_copy(src_hbm.at[pl.ds(ids_ref[0, 0, r], 1), :], buf_ref.at[s, pl.ds(r, 1), :],
                                  sem.at[s]).start()
            return carry
        lax.fori_loop(0, tm, start, 0, unroll=8)

    @pl.when(i == 0)
    def _():
        fetch(idx_ref, 0)

    @pl.when(i + 1 < n)
    def _():
        fetch(nxt_ref, 1 - slot)

    pltpu.make_async_copy(src_hbm.at[pl.ds(0, tm), :], buf_ref.at[slot], sem.at[slot]).wait()
    o_ref[...] = buf_ref[slot].astype(o_ref.dtype)


def _row_gather(src, idx, out_dtype, tm, name):
    n = idx.shape[0]
    D = src.shape[1]
    nt = n // tm
    idx3 = idx.reshape(nt, 1, tm)
    return pl.pallas_call(
        _gather_kernel,
        out_shape=jax.ShapeDtypeStruct((n, D), out_dtype),
        grid=(nt,),
        in_specs=[
            pl.BlockSpec((1, 1, tm), lambda i: (i, 0, 0), memory_space=pltpu.SMEM),
            pl.BlockSpec((1, 1, tm), lambda i: (jnp.minimum(i + 1, nt - 1), 0, 0), memory_space=pltpu.SMEM),
            pl.BlockSpec(memory_space=pl.ANY),
        ],
        out_specs=pl.BlockSpec((tm, D), lambda i: (i, 0)),
        scratch_shapes=[pltpu.VMEM((2, tm, D), src.dtype), pltpu.SemaphoreType.DMA((2,))],
        compiler_params=_params(("arbitrary",)),
        name=name,
    )(idx3, idx3, src)


def _scatter_kernel(idx_ref, x_ref, o_hbm, sem, *, fanout):
    tm = x_ref.shape[0]

    def start(r, carry):
        for k in range(fanout):
            dst = idx_ref[0, 0, fanout * r + k]
            pltpu.make_async_copy(x_ref.at[pl.ds(r, 1), :], o_hbm.at[pl.ds(dst, 1), :], sem).start()
        return carry

    lax.fori_loop(0, tm, start, 0, unroll=8)
    for k in range(fanout):
        pltpu.make_async_copy(x_ref, o_hbm.at[pl.ds(0, tm), :], sem).wait()


def _row_scatter(x, idx, fanout, tm, name):
    n, D = x.shape
    return pl.pallas_call(
        functools.partial(_scatter_kernel, fanout=fanout),
        out_shape=jax.ShapeDtypeStruct((fanout * n, D), x.dtype),
        grid=(n // tm,),
        in_specs=[
            pl.BlockSpec((1, 1, fanout * tm), lambda i: (i, 0, 0), memory_space=pltpu.SMEM),
            pl.BlockSpec((tm, D), lambda i: (i, 0)),
        ],
        out_specs=pl.BlockSpec(memory_space=pl.ANY),
        scratch_shapes=[pltpu.SemaphoreType.DMA(())],
        compiler_params=_params(("arbitrary",)),
        name=name,
    )(idx.reshape(n // tm, 1, fanout * tm), x)


def _route_plan(route, tm):
    n = 2 * route.shape[0]
    n_tiles = n // tm
    ids = route[:, :2].astype(I32).reshape(n)
    onehot = (ids[:, None] == jnp.arange(N_EXPERTS, dtype=I32)[None, :]).astype(I32)
    seen = jnp.cumsum(onehot, axis=0)
    counts = seen[-1]
    starts = (jnp.cumsum(counts) - counts).astype(I32)
    pos = jnp.sum(onehot * (starts[None, :] + seen - 1), axis=1).astype(I32)
    bounds = jnp.sort(jnp.concatenate([jnp.arange(n_tiles, dtype=I32) * tm, starts]))
    ends = jnp.concatenate([bounds[1:], jnp.full((1,), n, I32)])
    seg_tile = jnp.minimum(bounds // tm, n_tiles - 1)
    seg_exp = jnp.clip(jnp.searchsorted(starts, bounds, side="right").astype(I32) - 1, 0, N_EXPERTS - 1)
    seg_first = jnp.concatenate([jnp.ones((1,), I32), (seg_tile[1:] != seg_tile[:-1]).astype(I32)])
    return pos, (seg_tile, seg_exp, bounds - seg_tile * tm, ends - seg_tile * tm, seg_first)


def _experts_kernel(tile_ref, exp_ref, lo_ref, hi_ref, first_ref, x_ref, wg_ref, wu_ref, wd_ref, o_ref):
    s = pl.program_id(0)
    lo, hi = lo_ref[s], hi_ref[s]

    @pl.when(first_ref[s] == 1)
    def _():
        o_ref[...] = jnp.zeros_like(o_ref)

    @pl.when(hi > lo)
    def _():
        x = x_ref[...].astype(MXU_DTYPE)
        a = jnp.dot(x, wg_ref[...], preferred_element_type=F32)
        u = jnp.dot(x, wu_ref[...], preferred_element_type=F32)
        y = _mm(a / (1.0 + jnp.exp(-a)) * u, wd_ref[...])
        row = lax.broadcasted_iota(I32, y.shape, 0)
        o_ref[...] += jnp.where(row >= lo, jnp.where(row < hi, y, 0.0), 0.0)


def _experts(xs, segs, wg, wu, wd, tm):
    n, D = xs.shape
    F = wg.shape[2]
    tile_of = lambda s, tile, exp, lo, hi, first: (tile[s], 0)
    w_of = lambda s, tile, exp, lo, hi, first: (exp[s], 0, 0)
    return pl.pallas_call(
        _experts_kernel,
        out_shape=jax.ShapeDtypeStruct((n, D), F32),
        grid_spec=pltpu.PrefetchScalarGridSpec(
            num_scalar_prefetch=5,
            grid=(segs[0].shape[0],),
            in_specs=[
                pl.BlockSpec((tm, D), tile_of),
                pl.BlockSpec((None, D, F), w_of),
                pl.BlockSpec((None, D, F), w_of),
                pl.BlockSpec((None, F, D), w_of),
            ],
            out_specs=pl.BlockSpec((tm, D), tile_of),
        ),
        compiler_params=_params(("arbitrary",)),
        name="experts",
    )(*segs, xs, wg, wu, wd)


def _combine_kernel(x1_ref, ya_ref, yb_ref, route_ref, o_ref):
    r = route_ref[...]
    lane = lax.broadcasted_iota(I32, r.shape, 1)
    w1 = jnp.sum(jnp.where(lane == 2, r, 0.0), axis=1, keepdims=True)
    w2 = jnp.sum(jnp.where(lane == 3, r, 0.0), axis=1, keepdims=True)
    o_ref[...] = x1_ref[...] + (w1 * ya_ref[...] + w2 * yb_ref[...])


def _combine(x1, y2, route, tm):
    T, D = x1.shape
    nt = T // tm
    row = lambda i: (i, 0)
    return pl.pallas_call(
        _combine_kernel,
        out_shape=jax.ShapeDtypeStruct((T, D), F32),
        grid=(nt,),
        in_specs=[pl.BlockSpec((tm, D), row), pl.BlockSpec((tm, D), row), pl.BlockSpec((tm, D), lambda i: (i + nt, 0)),
                  pl.BlockSpec((tm, LANES), row)],
        out_specs=pl.BlockSpec((tm, D), row),
        compiler_params=_params(("parallel",)),
        name="moe_combine",
    )(x1, y2, y2, route)


def _moe(h2, route, x1, wg, wu, wd):
    T, D = x1.shape
    tm = min(MOE_TM, T)
    pos, segs = _route_plan(route, tm)
    xs = _row_scatter(h2, pos, 2, tm, "moe_scatter")
    ys = _experts(xs, segs, wg, wu, wd, tm)
    y2 = _row_gather(ys, jnp.concatenate([pos[0::2], pos[1::2]]), F32, tm, "moe_gather")
    return _combine(x1, y2, route, min(512, T))


def _pad_cols(w, n):
    return jnp.pad(w, ((0, 0), (0, n - w.shape[1])))


def _layer(x2, B, S, p):
    T, D = x2.shape
    assert D == 2 * GLA_HEADS * GLA_DK == GLA_HEADS * GLA_DV == DSA_HEADS * DSA_HEAD_DIM
    assert S % 512 == 0
    cd = MXU_DTYPE
    qk_w, v_w, dsa_w = GLA_HEADS * GLA_DK, GLA_HEADS * GLA_DV, DSA_HEADS * DSA_HEAD_DIM
    splits = (qk_w, qk_w, v_w, GLA_GATE_RANK, v_w, dsa_w, dsa_w, dsa_w, IDX_Q_W, IDX_DIM, IDX_HEADS, D, D)
    offs = [0]
    for s in splits:
        offs.append(offs[-1] + s)
    w_in = p["w_in"]
    (w_gq, w_gk, w_gv, w_ga, w_gr, w_dq, w_dk, w_dv, w_iq, w_ik, w_iw, w_ta, w_tb) = [
        w_in[:, offs[i]:offs[i + 1]] for i in range(len(splits))]

    w_main = jnp.concatenate([w_gq, w_gk, w_gv, w_gr, w_ta, w_tb], axis=1).astype(cd)
    w_qk = jnp.concatenate([w_dq, w_dk], axis=1).astype(cd)
    g_qk = jnp.concatenate([jnp.tile(p["dsa_q_norm_g"] * (DSA_HEAD_DIM ** -0.5 * math.log2(math.e)), DSA_HEADS),
                            jnp.tile(p["dsa_k_norm_g"], DSA_HEADS)]).reshape(1, 2 * dsa_w).astype(F32)
    w_vt = w_dv.T.astype(cd)
    w_idx = jnp.concatenate([_pad_cols(w_ik, LANES), _pad_cols(w_ga, LANES)], axis=1).astype(cd)
    w_idxt = jnp.pad(jnp.concatenate([w_iq, w_iw], axis=1).T, ((0, 16 - IDX_HEADS), (0, 0))).astype(cd)

    tm = min(1024, T)
    h = _rmsnorm(x2, p["norm1_g"], tm)
    zm = _matmul(h, w_main, tm, 512, cd)
    qk = _qk_proj(h, w_qk, g_qk, min(512, T))
    vt = _vt_proj(h, w_vt, B, S, min(512, S))
    iqt, ki, ga, wit = _idx_proj(h, w_idx, w_idxt, p["idx_k_ln_g"].reshape(1, -1), p["idx_k_ln_b"].reshape(1, -1),
                                 min(512, T))

    o_gla = _gla(zm, ga, p["gla_w_a2"], p["gla_b_a"].reshape(1, -1), p["gla_norm_g"].reshape(1, -1), B, S, 512)

    QB, KB = 256, 512
    mask = _dsa_select(iqt, wit, ki, B, S, QB, KB)
    o_dsa = _dsa_attn(qk, vt, mask, B, S, QB, KB)

    w_r = _pad_cols(jnp.concatenate([p["w_router_expert"], p["w_router_group"]], axis=1), LANES)
    b_r = _pad_cols(jnp.concatenate([p["b_router_expert"], p["b_router_group"]]).reshape(1, -1), LANES)
    x1, h2, route = _post(o_gla, o_dsa, zm, x2, p["w_branch_gla"].astype(cd), p["w_branch_dsa"].astype(cd),
                          p["w_out"].astype(cd), p["norm2_g"].reshape(1, -1), w_r, b_r, min(512, T))
    return _moe(h2, route, x1, p["w_exp_gate"].astype(cd), p["w_exp_up"].astype(cd), p["w_exp_down"].astype(cd))


def kernel(x, norm1_g, w_in, gla_w_a2, gla_b_a, gla_norm_g, dsa_q_norm_g, dsa_k_norm_g, idx_k_ln_g, idx_k_ln_b,
           w_branch_gla, w_branch_dsa, w_out, norm2_g, w_router_group, b_router_group, w_router_expert,
           b_router_expert, w_exp_gate, w_exp_up, w_exp_down):
    B, S, D = x.shape
    stacked = dict(norm1_g=norm1_g, w_in=w_in, gla_w_a2=gla_w_a2, gla_b_a=gla_b_a, gla_norm_g=gla_norm_g,
                   dsa_q_norm_g=dsa_q_norm_g, dsa_k_norm_g=dsa_k_norm_g, idx_k_ln_g=idx_k_ln_g,
                   idx_k_ln_b=idx_k_ln_b, w_branch_gla=w_branch_gla, w_branch_dsa=w_branch_dsa, w_out=w_out,
                   norm2_g=norm2_g, w_router_group=w_router_group, b_router_group=b_router_group,
                   w_router_expert=w_router_expert, b_router_expert=b_router_expert, w_exp_gate=w_exp_gate,
                   w_exp_up=w_exp_up, w_exp_down=w_exp_down)
    x2 = x.reshape(B * S, D).astype(F32)
    for l in range(w_in.shape[0]):
        x2 = _layer(x2, B, S, {k: v[l] for k, v in stacked.items()})
    return x2.reshape(B, S, D).astype(x.dtype)
```

```python
import functools
import math

import jax
import jax.numpy as jnp
from jax import lax
from jax.experimental import pallas as pl
from jax.experimental.pallas import tpu as pltpu

F32 = jnp.float32
BF16 = jnp.bfloat16
I32 = jnp.int32
MXU_DTYPE = BF16

CHUNK = 64
EPS = 1e-6
GLA_HEADS, GLA_DK, GLA_DV = 4, 128, 256
GLA_GATE_RANK = 16
GLA_GATE_TEMP = 16.0
DSA_HEADS, DSA_HEAD_DIM = 8, 128
IDX_HEADS, IDX_DIM = 8, 64
TOPK_MAX = 256
N_GROUPS, EXPERTS_PER_GROUP = 4, 8
N_EXPERTS = N_GROUPS * EXPERTS_PER_GROUP
LANES = 128

INT_MIN = -(2 ** 31)
INT_MAX = 2 ** 31 - 1
I16 = jnp.int16
I16_MIN, I16_MAX = -(2 ** 15), 2 ** 15 - 1

VMEM_LIMIT = 56 * 1024 * 1024

_NT = (((1,), (1,)), ((), ()))
_TN = (((0,), (0,)), ((), ()))


def _params(sem):
    return pltpu.CompilerParams(dimension_semantics=sem, vmem_limit_bytes=VMEM_LIMIT)


def _mm(a, b):
    return jnp.dot(a.astype(MXU_DTYPE), b.astype(MXU_DTYPE), preferred_element_type=F32)


def _mm_nt(a, b):
    return lax.dot_general(a.astype(MXU_DTYPE), b.astype(MXU_DTYPE), _NT, preferred_element_type=F32)


def _mm_tn(a, b):
    return lax.dot_general(a.astype(MXU_DTYPE), b.astype(MXU_DTYPE), _TN, preferred_element_type=F32)


def _split(a):
    hi = a.astype(BF16)
    lo = (a - hi.astype(F32)).astype(BF16)
    return hi, lo


def _dot3(a, b):
    ah, al = _split(a)
    bh, bl = _split(b)
    d = lambda u, v: jnp.dot(u, v, preferred_element_type=F32)
    return d(ah, bh) + (d(ah, bl) + d(al, bh))


def _rmsnorm_kernel(x_ref, g_ref, o_ref):
    x = x_ref[...]
    ms = jnp.mean(x * x, axis=-1, keepdims=True)
    o_ref[...] = (x * lax.rsqrt(ms + EPS) * g_ref[...]).astype(o_ref.dtype)


def _rmsnorm(x2, g, tm):
    T, D = x2.shape
    return pl.pallas_call(
        _rmsnorm_kernel,
        out_shape=jax.ShapeDtypeStruct((T, D), MXU_DTYPE),
        grid=(T // tm,),
        in_specs=[pl.BlockSpec((tm, D), lambda i: (i, 0)), pl.BlockSpec((1, D), lambda i: (0, 0))],
        out_specs=pl.BlockSpec((tm, D), lambda i: (i, 0)),
        compiler_params=_params(("parallel",)),
        name="rmsnorm",
    )(x2, g.reshape(1, D))


def _matmul_kernel(h_ref, w_ref, o_ref):
    o_ref[...] = jnp.dot(h_ref[...], w_ref[...], preferred_element_type=F32).astype(o_ref.dtype)


def _matmul(h, w, tm, tn, out_dtype):
    T, D = h.shape
    N = w.shape[1]
    return pl.pallas_call(
        _matmul_kernel,
        out_shape=jax.ShapeDtypeStruct((T, N), out_dtype),
        grid=(T // tm, N // tn),
        in_specs=[pl.BlockSpec((tm, D), lambda i, j: (i, 0)), pl.BlockSpec((D, tn), lambda i, j: (0, j))],
        out_specs=pl.BlockSpec((tm, tn), lambda i, j: (i, j)),
        compiler_params=_params(("parallel", "arbitrary")),
        name="proj_main",
    )(h, w)


def _qk_kernel(h_ref, w_ref, g_ref, o_ref):
    z = jnp.dot(h_ref[...], w_ref[...], preferred_element_type=F32)
    for hh in range(DSA_HEADS):
        sl = slice(hh * DSA_HEAD_DIM, (hh + 1) * DSA_HEAD_DIM)
        zh = z[:, sl]
        ms = jnp.mean(zh * zh, axis=-1, keepdims=True)
        o_ref[:, sl] = (zh * lax.rsqrt(ms + EPS) * g_ref[:, sl]).astype(o_ref.dtype)


def _qk_proj(h, w, g, tm):
    T, D = h.shape
    W = DSA_HEADS * DSA_HEAD_DIM
    return pl.pallas_call(
        _qk_kernel,
        out_shape=jax.ShapeDtypeStruct((T, 2 * W), MXU_DTYPE),
        grid=(T // tm, 2),
        in_specs=[
            pl.BlockSpec((tm, D), lambda i, j: (i, 0)),
            pl.BlockSpec((D, W), lambda i, j: (0, j)),
            pl.BlockSpec((1, W), lambda i, j: (0, j)),
        ],
        out_specs=pl.BlockSpec((tm, W), lambda i, j: (i, j)),
        compiler_params=_params(("parallel", "arbitrary")),
        name="proj_qk",
    )(h, w, g)


VT_ONES = 16
VT_ROWS = DSA_HEAD_DIM + VT_ONES


def _vt_kernel(h_ref, wt_ref, o_ref):
    vt = lax.dot_general(wt_ref[...], h_ref[...], _NT, preferred_element_type=F32).astype(o_ref.dtype)
    ones = jnp.ones((VT_ONES, vt.shape[1]), o_ref.dtype)
    for hh in range(DSA_HEADS):
        o_ref[hh, :DSA_HEAD_DIM, :] = vt[hh * DSA_HEAD_DIM:(hh + 1) * DSA_HEAD_DIM]
        o_ref[hh, DSA_HEAD_DIM:, :] = ones


def _vt_proj(h, wt, B, S, tm):
    T, D = h.shape
    W = wt.shape[0]
    nt = S // tm
    return pl.pallas_call(
        _vt_kernel,
        out_shape=jax.ShapeDtypeStruct((B, DSA_HEADS, VT_ROWS, S), MXU_DTYPE),
        grid=(T // tm,),
        in_specs=[pl.BlockSpec((tm, D), lambda i: (i, 0)), pl.BlockSpec((W, D), lambda i: (0, 0))],
        out_specs=pl.BlockSpec((None, DSA_HEADS, VT_ROWS, tm), lambda i: (i // nt, 0, 0, i % nt)),
        compiler_params=_params(("parallel",)),
        name="proj_vt",
    )(h, wt)


IDX_Q_W = IDX_HEADS * IDX_DIM


def _idx_kernel(h_ref, w_ref, wt_ref, lng_ref, lnb_ref, iqt_ref, ki_ref, ga_ref, wit_ref):
    h = h_ref[...]
    z = jnp.dot(h, w_ref[...], preferred_element_type=F32)
    ik = z[:, :IDX_DIM]
    mu = jnp.mean(ik, axis=-1, keepdims=True)
    var = jnp.mean(jnp.square(ik - mu), axis=-1, keepdims=True)
    ki = (ik - mu) * lax.rsqrt(var + EPS) * lng_ref[...] + lnb_ref[...]
    ki_ref[...] = ki.astype(ki_ref.dtype)
    ga_ref[...] = z[:, LANES:LANES + GLA_GATE_RANK]
    zt = lax.dot_general(wt_ref[...], h, _NT, preferred_element_type=F32)
    iqt_ref[...] = zt[:IDX_Q_W].astype(iqt_ref.dtype)
    wit_ref[...] = zt[IDX_Q_W:IDX_Q_W + IDX_HEADS] * (IDX_HEADS ** -0.5 * IDX_DIM ** -0.5)


def _idx_proj(h, w, wt, lng, lnb, tm):
    T, D = h.shape
    return pl.pallas_call(
        _idx_kernel,
        out_shape=(
            jax.ShapeDtypeStruct((IDX_Q_W, T), MXU_DTYPE),
            jax.ShapeDtypeStruct((T, IDX_DIM), MXU_DTYPE),
            jax.ShapeDtypeStruct((T, GLA_GATE_RANK), F32),
            jax.ShapeDtypeStruct((IDX_HEADS, T), F32),
        ),
        grid=(T // tm,),
        in_specs=[
            pl.BlockSpec((tm, D), lambda i: (i, 0)),
            pl.BlockSpec(w.shape, lambda i: (0, 0)),
            pl.BlockSpec(wt.shape, lambda i: (0, 0)),
            pl.BlockSpec((1, IDX_DIM), lambda i: (0, 0)),
            pl.BlockSpec((1, IDX_DIM), lambda i: (0, 0)),
        ],
        out_specs=(
            pl.BlockSpec((IDX_Q_W, tm), lambda i: (0, i)),
            pl.BlockSpec((tm, IDX_DIM), lambda i: (i, 0)),
            pl.BlockSpec((tm, GLA_GATE_RANK), lambda i: (i, 0)),
            pl.BlockSpec((IDX_HEADS, tm), lambda i: (0, i)),
        ),
        compiler_params=_params(("parallel",)),
        name="proj_idx",
    )(h, w, wt, lng, lnb)


def _gla_kernel(q_ref, k_ref, v_ref, r_ref, ga_ref, wa2_ref, ba_ref, ng_ref, o_ref, st_ref, tot_ref, kd_ref,
                oraw_ref, *, n_chunks):
    tb = q_ref.shape[0]

    @pl.when(pl.program_id(1) == 0)
    def _():
        st_ref[...] = jnp.zeros_like(st_ref)

    x = _dot3(ga_ref[...], wa2_ref[...]) + ba_ref[...]
    la = (jnp.minimum(x, 0.0) - jnp.log(1.0 + jnp.exp(-jnp.abs(x)))) * (1.0 / GLA_GATE_TEMP)
    row = lax.broadcasted_iota(I32, (tb, tb), 0)
    col = lax.broadcasted_iota(I32, (tb, tb), 1)
    same = lax.shift_right_logical(row, 6) == lax.shift_right_logical(col, 6)
    ones_blk = jnp.where(same, 1.0, 0.0).astype(BF16)
    tril_blk = jnp.where(same, jnp.where(row >= col, 1.0, 0.0), 0.0).astype(BF16)
    la_hi, la_lo = _split(la)
    d = lambda u, v: jnp.dot(u, v, preferred_element_type=F32)
    tot = d(ones_blk, la_hi) + d(ones_blk, la_lo)
    cum = d(tril_blk, la_hi) + d(tril_blk, la_lo)
    tot_ref[...] = tot
    kd_ref[...] = (k_ref[...].astype(F32) * jnp.exp(tot - cum)).astype(kd_ref.dtype)

    heads = range(GLA_HEADS)
    ksl = lambda hh: slice(hh * GLA_DK, (hh + 1) * GLA_DK)
    vsl = lambda hh: slice(hh * GLA_DV, (hh + 1) * GLA_DV)

    def chunk(c, carry):
        rows = pl.ds(pl.multiple_of(c * CHUNK, CHUNK), CHUNK)
        first = pl.ds(pl.multiple_of(c * CHUNK, CHUNK), 1)
        upd = [_mm_tn(v_ref[rows, vsl(hh)], kd_ref[rows, ksl(hh)]) for hh in heads]
        st = [st_ref[hh] * jnp.exp(tot_ref[first, ksl(hh)]) + upd[hh] for hh in heads]
        for hh in heads:
            st_ref[hh] = st[hh]
        for hh in heads:
            oraw_ref[rows, vsl(hh)] = _mm_nt(q_ref[rows, ksl(hh)], st[hh])
        return carry

    lax.fori_loop(0, n_chunks, chunk, 0)

    ng = ng_ref[...]
    for hh in heads:
        o = oraw_ref[:, vsl(hh)] * (GLA_DK ** -0.5)
        ms = jnp.mean(o * o, axis=-1, keepdims=True)
        r = r_ref[:, vsl(hh)].astype(F32)
        o_ref[:, vsl(hh)] = (o * lax.rsqrt(ms + EPS) * ng * (r / (1.0 + jnp.exp(-r)))).astype(o_ref.dtype)


def _gla(zm, ga, wa2, ba, ng, B, S, tb):
    T = B * S
    nb = S // tb
    qk_w = GLA_HEADS * GLA_DK
    v_w = GLA_HEADS * GLA_DV
    tok = lambda b, i: b * nb + i
    return pl.pallas_call(
        functools.partial(_gla_kernel, n_chunks=tb // CHUNK),
        out_shape=jax.ShapeDtypeStruct((T, v_w), MXU_DTYPE),
        grid=(B, nb),
        in_specs=[
            pl.BlockSpec((tb, qk_w), lambda b, i: (tok(b, i), 0)),
            pl.BlockSpec((tb, qk_w), lambda b, i: (tok(b, i), 1)),
            pl.BlockSpec((tb, v_w), lambda b, i: (tok(b, i), 1)),
            pl.BlockSpec((tb, v_w), lambda b, i: (tok(b, i), 2)),
            pl.BlockSpec((tb, GLA_GATE_RANK), lambda b, i: (tok(b, i), 0)),
            pl.BlockSpec((GLA_GATE_RANK, qk_w), lambda b, i: (0, 0)),
            pl.BlockSpec((1, qk_w), lambda b, i: (0, 0)),
            pl.BlockSpec((1, GLA_DV), lambda b, i: (0, 0)),
        ],
        out_specs=pl.BlockSpec((tb, v_w), lambda b, i: (tok(b, i), 0)),
        scratch_shapes=[pltpu.VMEM((GLA_HEADS, GLA_DV, GLA_DK), F32), pltpu.VMEM((tb, qk_w), F32),
                        pltpu.VMEM((tb, qk_w), MXU_DTYPE), pltpu.VMEM((tb, v_w), F32)],
        compiler_params=_params(("parallel", "arbitrary")),
        name="gla",
    )(zm, zm, zm, zm, ga, wa2, ba, ng)


def _last_kb(qb, QB, KB):
    return ((qb + 1) * QB - 1) // KB


def _causal_steps(S, QB, KB):
    pairs = [(q, k) for q in range(S // QB) for k in range(_last_kb(q, QB, KB) + 1)]
    qs, ks = zip(*pairs)
    return jnp.asarray(qs, I32), jnp.asarray(ks, I32)


def _select_kernel(qb_ref, kb_ref, iqt_ref, wit_ref, ki_ref, mask_ref, hi_ref, lo_ref, l2_ref, gm_ref, *,
                   QB, KB, S, topk):
    qb = qb_ref[pl.program_id(1)]
    kb = kb_ref[pl.program_id(1)]
    last = _last_kb(qb, QB, KB)

    def rows_of(i):
        return pl.ds(pl.multiple_of(i * KB, KB), KB)

    @pl.when(kb == 0)
    def _():
        gm_ref[...] = jnp.full(gm_ref.shape, INT_MIN, I32)

    rc = 128
    assert topk % rc == 0 and KB % rc == 0
    t_chunk = lax.shift_right_logical(qb * QB + lax.broadcasted_iota(I32, (rc, QB), 1), 6)
    for c in range(KB // rc):
        ki = ki_ref[c * rc:(c + 1) * rc, :]
        sc = jnp.zeros((rc, QB), F32)
        for hh in range(IDX_HEADS):
            lg = _mm(ki, iqt_ref[hh * IDX_DIM:(hh + 1) * IDX_DIM, :])
            sc = sc + jnp.maximum(lg, 0.0) * wit_ref[hh:hh + 1, :]
        s_chunk = lax.shift_right_logical(kb * KB + c * rc + lax.broadcasted_iota(I32, (rc, QB), 0), 6)
        bits = lax.bitcast_convert_type(sc, I32)
        sign = lax.shift_right_arithmetic(bits, 31)
        key = (bits ^ (sign & INT_MAX)) - sign
        key = jnp.where(s_chunk <= t_chunk, key, INT_MIN)
        rows = pl.ds(pl.multiple_of(kb * KB + c * rc, rc), rc)
        hi_ref[rows, :] = lax.shift_right_arithmetic(key, 16).astype(I16)
        lo_ref[rows, :] = ((key & 0xFFFF) + I16_MIN).astype(I16)
        g0 = (c * rc) % topk
        gm_ref[g0:g0 + rc, :] = jnp.maximum(gm_ref[g0:g0 + rc, :], key)

    @pl.when(kb == last)
    def _():
        nblk = last + 1
        slab = 32
        one, zero = jnp.int16(1), jnp.int16(0)

        def fold(m, reduce):
            part = m[0:slab]
            for r in range(1, KB // slab):
                part = reduce(part, m[r * slab:(r + 1) * slab])
            return part

        def count(pred):
            def body(i, acc):
                return acc + fold(pred(i), jnp.add)
            acc = lax.fori_loop(0, nblk, body, jnp.zeros((slab, QB), I16))
            return jnp.sum(acc.astype(I32), axis=0, keepdims=True)

        def count_ge(ref, thr):
            t16 = thr.astype(I16)
            return count(lambda i: jnp.where(ref[rows_of(i), :] >= t16, one, zero))

        def extreme(ref, pick, reduce, fill):
            def body(i, acc):
                t = ref[rows_of(i), :]
                return reduce(acc, fold(jnp.where(pick(t), t, jnp.int16(fill)), reduce))
            return lax.fori_loop(0, nblk, body, jnp.full((slab, QB), fill, I16)).astype(I32)

        def search(ref, target, lo, c_lo, hi, c_hi, tighten_at):
            def is_open(lo, c_lo, hi):
                return jnp.where(c_lo > target, jnp.where(hi != lo + 1, 1, 0), 0)

            def tighten(lo, c_lo, hi):
                lo16, top16 = lo.astype(I16), (hi - 1).astype(I16)
                min16 = lambda a, b: jnp.where(a < b, a, b)
                max16 = lambda a, b: jnp.where(a > b, a, b)
                lo_t = jnp.min(extreme(ref, lambda t: t >= lo16, min16, I16_MAX), axis=0, keepdims=True)
                hi_t = jnp.max(extreme(ref, lambda t: t <= top16, max16, I16_MIN), axis=0, keepdims=True) + 1
                open_ = is_open(lo, c_lo, hi) > 0
                return jnp.where(open_, lo_t, lo), jnp.where(open_, hi_t, hi)

            def step(st):
                it, lo, c_lo, hi, c_hi = st
                do_tighten = functools.reduce(jnp.logical_or, [it == k for k in tighten_at])
                lo, hi = lax.cond(do_tighten, lambda: tighten(lo, c_lo, hi), lambda: (lo, hi))
                open_ = is_open(lo, c_lo, hi) > 0
                mid = lax.shift_right_arithmetic(lo + hi, 1)
                c = count_ge(ref, mid)
                up = jnp.logical_and(open_, c >= target)
                dn = jnp.logical_and(open_, c < target)
                return (it + 1, jnp.where(up, mid, lo), jnp.where(up, c, c_lo), jnp.where(dn, mid, hi),
                        jnp.where(dn, c, c_hi))

            def unfinished(st):
                _it, lo, c_lo, hi, _c_hi = st
                return jnp.max(is_open(lo, c_lo, hi)) > 0

            _it, lo, c_lo, hi, c_hi = lax.while_loop(unfinished, step, (jnp.int32(0), lo, c_lo, hi, c_hi))
            return lo, c_lo, c_hi

        zeros = jnp.zeros((1, QB), I32)
        gm = gm_ref[...]
        g_lo = jnp.maximum(lax.shift_right_arithmetic(jnp.min(gm, axis=0, keepdims=True), 16), I16_MIN + 1)
        g_hi = lax.shift_right_arithmetic(jnp.max(gm, axis=0, keepdims=True), 16) + 1
        hstar, ch_ge, ch_gt = search(hi_ref, topk, g_lo, count_ge(hi_ref, g_lo), g_hi, zeros, (12,))
        split = ch_ge > topk
        h16 = hstar.astype(I16)

        def low_half():
            def build(i, carry):
                l2_ref[rows_of(i), :] = jnp.where(hi_ref[rows_of(i), :] == h16, lo_ref[rows_of(i), :],
                                                  jnp.int16(I16_MIN))
                return carry
            lax.fori_loop(0, nblk, build, 0)
            target = jnp.where(split, topk - ch_gt, INT_MAX)
            return search(l2_ref, target, jnp.full((1, QB), I16_MIN, I32), ch_ge - ch_gt,
                          jnp.full((1, QB), I16_MAX + 1, I32), zeros, (0, 6))

        any_split = jnp.max(jnp.where(split, 1, 0)) > 0
        lstar, cl_ge, cl_gt = lax.cond(any_split, low_half,
                                       lambda: (jnp.full((1, QB), I16_MIN, I32), zeros, zeros))
        lstar = jnp.where(split, lstar, I16_MIN)
        l16 = lstar.astype(I16)
        c_ge = jnp.where(split, ch_gt + cl_ge, ch_ge)
        c_gt = jnp.where(split, ch_gt + cl_gt, ch_gt)
        excess = c_ge > topk
        need = jnp.where(excess, topk - c_gt, S + 1)
        row16 = lax.broadcasted_iota(I32, (KB, QB), 0).astype(I16)

        def before(i, j):
            return jnp.where(row16 + (i * KB).astype(I16) < j.astype(I16), one, zero)

        def tied(i, then):
            return jnp.where(hi_ref[rows_of(i), :] == h16, jnp.where(lo_ref[rows_of(i), :] == l16, then, zero), zero)

        def tie_cut():
            def step(_, c):
                lo, hi = c
                mid = lax.shift_right_arithmetic(lo + hi, 1)
                ok = count(lambda i: tied(i, before(i, mid))) >= need
                return jnp.where(ok, lo, mid + 1), jnp.where(ok, mid, hi)
            n_steps = max(1, math.ceil(math.log2(S + 1)))
            lo, _hi = lax.fori_loop(0, n_steps, step, (zeros, jnp.full((1, QB), S, I32)))
            return lo

        any_excess = jnp.max(jnp.where(excess, 1, 0)) > 0
        jcut = lax.cond(any_excess, tie_cut, lambda: jnp.full((1, QB), S, I32))

        def write(i, carry):
            hi_t, lo_t = hi_ref[rows_of(i), :], lo_ref[rows_of(i), :]
            in_bucket = jnp.where(lo_t > l16, one, jnp.where(lo_t == l16, before(i, jcut), zero))
            sel = jnp.where(hi_t > h16, one, jnp.where(hi_t == h16, in_bucket, zero))
            mask_ref[rows_of(i), :] = sel.astype(mask_ref.dtype)
            return carry

        lax.fori_loop(0, nblk, write, 0)

        def clear(i, carry):
            mask_ref[rows_of(i), :] = jnp.zeros((KB, QB), mask_ref.dtype)
            return carry

        lax.fori_loop(nblk, S // KB, clear, 0)


def _dsa_select(iqt, wit, ki, B, S, QB, KB):
    nq, nk = S // QB, S // KB
    topk = min(TOPK_MAX, S // 4)
    assert KB % topk == 0
    qs, ks = _causal_steps(S, QB, KB)
    return pl.pallas_call(
        functools.partial(_select_kernel, QB=QB, KB=KB, S=S, topk=topk),
        out_shape=jax.ShapeDtypeStruct((B, S, S), jnp.int8),
        grid_spec=pltpu.PrefetchScalarGridSpec(
            num_scalar_prefetch=2,
            grid=(B, qs.shape[0]),
            in_specs=[
                pl.BlockSpec((IDX_Q_W, QB), lambda b, s, qs, ks: (0, b * nq + qs[s])),
                pl.BlockSpec((IDX_HEADS, QB), lambda b, s, qs, ks: (0, b * nq + qs[s])),
                pl.BlockSpec((KB, IDX_DIM), lambda b, s, qs, ks: (b * nk + ks[s], 0)),
            ],
            out_specs=pl.BlockSpec((None, S, QB), lambda b, s, qs, ks: (b, 0, qs[s])),
            scratch_shapes=[pltpu.VMEM((S, QB), I16), pltpu.VMEM((S, QB), I16), pltpu.VMEM((S, QB), I16),
                            pltpu.VMEM((topk, QB), I32)],
        ),
        compiler_params=_params(("parallel", "arbitrary")),
        name="dsa_select",
    )(qs, ks, iqt, wit, ki)


def _attn_kernel(qb_ref, kb_ref, q_ref, k_ref, vt_ref, m_ref, o_ref, acc_ref, *, QB, KB):
    qb = qb_ref[pl.program_id(1)]
    kb = kb_ref[pl.program_id(1)]
    last = _last_kb(qb, QB, KB)

    @pl.when(kb == 0)
    def _():
        acc_ref[...] = jnp.zeros_like(acc_ref)

    mb = m_ref[...].astype(MXU_DTYPE)
    head = lambda hh: slice(hh * DSA_HEAD_DIM, (hh + 1) * DSA_HEAD_DIM)
    logits = lambda hh: _mm_nt(k_ref[:, head(hh)], q_ref[:, head(hh)])
    lg = logits(0)
    for hh in range(DSA_HEADS):
        lg_next = logits(hh + 1) if hh + 1 < DSA_HEADS else None
        p = jnp.exp2(lg).astype(MXU_DTYPE) * mb
        acc_ref[hh] += jnp.dot(vt_ref[hh], p, preferred_element_type=F32)
        lg = lg_next

    @pl.when(kb == last)
    def _():
        for hh in range(DSA_HEADS):
            acc = acc_ref[hh]
            o = acc[:DSA_HEAD_DIM] / acc[DSA_HEAD_DIM:DSA_HEAD_DIM + 1]
            o_ref[:, hh * DSA_HEAD_DIM:(hh + 1) * DSA_HEAD_DIM] = o.T.astype(o_ref.dtype)


def _dsa_attn(qk, vt, mask, B, S, QB, KB):
    T = B * S
    nq, nk = S // QB, S // KB
    W = DSA_HEADS * DSA_HEAD_DIM
    qs, ks = _causal_steps(S, QB, KB)
    return pl.pallas_call(
        functools.partial(_attn_kernel, QB=QB, KB=KB),
        out_shape=jax.ShapeDtypeStruct((T, W), MXU_DTYPE),
        grid_spec=pltpu.PrefetchScalarGridSpec(
            num_scalar_prefetch=2,
            grid=(B, qs.shape[0]),
            in_specs=[
                pl.BlockSpec((QB, W), lambda b, s, qs, ks: (b * nq + qs[s], 0)),
                pl.BlockSpec((KB, W), lambda b, s, qs, ks: (b * nk + ks[s], 1)),
                pl.BlockSpec((None, DSA_HEADS, VT_ROWS, KB), lambda b, s, qs, ks: (b, 0, 0, ks[s])),
                pl.BlockSpec((None, KB, QB), lambda b, s, qs, ks: (b, ks[s], qs[s])),
            ],
            out_specs=pl.BlockSpec((QB, W), lambda b, s, qs, ks: (b * nq + qs[s], 0)),
            scratch_shapes=[pltpu.VMEM((DSA_HEADS, VT_ROWS, QB), F32)],
        ),
        compiler_params=_params(("parallel", "arbitrary")),
        name="dsa_attn",
    )(qs, ks, qk, qk, vt, mask)


def _post_kernel(og_ref, od_ref, ga_ref, gb_ref, x_ref, pa_ref, pb_ref, wo_ref, g2_ref, wr_ref, br_ref,
                 x1_ref, h2_ref, route_ref):
    sig = lambda v: 1.0 / (1.0 + jnp.exp(-v))
    a = jnp.dot(og_ref[...], pa_ref[...], preferred_element_type=F32)
    b = jnp.dot(od_ref[...], pb_ref[...], preferred_element_type=F32)
    mix = sig(ga_ref[...].astype(F32)) * a + sig(gb_ref[...].astype(F32)) * b
    x1 = x_ref[...] + _mm(mix, wo_ref[...])
    x1_ref[...] = x1
    ms = jnp.mean(x1 * x1, axis=-1, keepdims=True)
    h2 = x1 * lax.rsqrt(ms + EPS) * g2_ref[...]
    h2_ref[...] = h2.astype(h2_ref.dtype)

    logits = _dot3(h2, wr_ref[...]) + br_ref[...]
    tm = logits.shape[0]
    lane = lax.broadcasted_iota(I32, (tm, LANES), 1).astype(F32)
    neg = -jnp.inf
    far = float(2 * LANES)
    rmax = lambda v: jnp.max(v, axis=1, keepdims=True)
    rmin = lambda v: jnp.min(v, axis=1, keepdims=True)
    gl = jnp.where(lane >= N_EXPERTS, jnp.where(lane < N_EXPERTS + N_GROUPS, logits, neg), neg)
    gmax = rmax(gl)
    g_w = 1.0 / jnp.sum(jnp.exp(gl - gmax), axis=1, keepdims=True)
    g_idx = rmin(jnp.where(gl == gmax, lane, far)) - N_EXPERTS
    e_lo = g_idx * EXPERTS_PER_GROUP
    el = jnp.where(lane >= e_lo, jnp.where(lane < e_lo + EXPERTS_PER_GROUP, logits, neg), neg)
    m1 = rmax(el)
    e1 = rmin(jnp.where(el == m1, lane, far))
    el2 = jnp.where(lane == e1, neg, el)
    m2 = rmax(el2)
    e2 = rmin(jnp.where(el2 == m2, lane, far))
    p2 = jnp.exp(m2 - m1)
    w1 = g_w / (1.0 + p2)
    route_ref[...] = jnp.where(lane == 0.0, e1, jnp.where(lane == 1.0, e2, jnp.where(
        lane == 2.0, w1, jnp.where(lane == 3.0, w1 * p2, 0.0))))


def _post(og, od, zm, x2, pa, pb, wo, g2, wr, br, tm):
    T, D = x2.shape
    row = lambda i: (i, 0)
    full = lambda i: (0, 0)
    return pl.pallas_call(
        _post_kernel,
        out_shape=(
            jax.ShapeDtypeStruct((T, D), F32),
            jax.ShapeDtypeStruct((T, D), F32),
            jax.ShapeDtypeStruct((T, LANES), F32),
        ),
        grid=(T // tm,),
        in_specs=[
            pl.BlockSpec((tm, D), row),
            pl.BlockSpec((tm, D), row),
            pl.BlockSpec((tm, D), lambda i: (i, 3)),
            pl.BlockSpec((tm, D), lambda i: (i, 4)),
            pl.BlockSpec((tm, D), row),
            pl.BlockSpec((D, D), full),
            pl.BlockSpec((D, D), full),
            pl.BlockSpec((D, D), full),
            pl.BlockSpec((1, D), full),
            pl.BlockSpec((D, LANES), full),
            pl.BlockSpec((1, LANES), full),
        ],
        out_specs=(pl.BlockSpec((tm, D), row), pl.BlockSpec((tm, D), row), pl.BlockSpec((tm, LANES), row)),
        compiler_params=_params(("parallel",)),
        name="merge_router",
    )(og, od, zm, zm, x2, pa, pb, wo, g2, wr, br)


MOE_TM = 256


def _gather_kernel(idx_ref, nxt_ref, src_hbm, o_ref, buf_ref, sem):
    i = pl.program_id(0)
    n = pl.num_programs(0)
    tm = o_ref.shape[0]
    slot = lax.rem(i, 2)

    def fetch(ids_ref, s):
        def start(r, carry):
            pltpu.make_async_copy(src_hbm.at[pl.ds(ids_ref[0, 0, r], 1), :], buf_ref.at[s, pl.ds(r, 1), :],
                                  sem.at[s]).start()
            return carry
        lax.fori_loop(0, tm, start, 0, unroll=8)

    @pl.when(i == 0)
    def _():
        fetch(idx_ref, 0)

    @pl.when(i + 1 < n)
    def _():
        fetch(nxt_ref, 1 - slot)

    pltpu.make_async_copy(src_hbm.at[pl.ds(0, tm), :], buf_ref.at[slot], sem.at[slot]).wait()
    o_ref[...] = buf_ref[slot].astype(o_ref.dtype)


def _row_gather(src, idx, out_dtype, tm, name):
    n = idx.shape[0]
    D = src.shape[1]
    nt = n // tm
    idx3 = idx.reshape(nt, 1, tm)
    return pl.pallas_call(
        _gather_kernel,
        out_shape=jax.ShapeDtypeStruct((n, D), out_dtype),
        grid=(nt,),
        in_specs=[
            pl.BlockSpec((1, 1, tm), lambda i: (i, 0, 0), memory_space=pltpu.SMEM),
            pl.BlockSpec((1, 1, tm), lambda i: (jnp.minimum(i + 1, nt - 1), 0, 0), memory_space=pltpu.SMEM),
            pl.BlockSpec(memory_space=pl.ANY),
        ],
        out_specs=pl.BlockSpec((tm, D), lambda i: (i, 0)),
        scratch_shapes=[pltpu.VMEM((2, tm, D), src.dtype), pltpu.SemaphoreType.DMA((2,))],
        compiler_params=_params(("arbitrary",)),
        name=name,
    )(idx3, idx3, src)


def _scatter_kernel(idx_ref, x_ref, o_hbm, sem, *, fanout):
    tm = x_ref.shape[0]

    def start(r, carry):
        for k in range(fanout):
            dst = idx_ref[0, 0, fanout * r + k]
            pltpu.make_async_copy(x_ref.at[pl.ds(r, 1), :], o_hbm.at[pl.ds(dst, 1), :], sem).start()
        return carry

    lax.fori_loop(0, tm, start, 0, unroll=8)
    for k in range(fanout):
        pltpu.make_async_copy(x_ref, o_hbm.at[pl.ds(0, tm), :], sem).wait()


def _row_scatter(x, idx, fanout, tm, name):
    n, D = x.shape
    return pl.pallas_call(
        functools.partial(_scatter_kernel, fanout=fanout),
        out_shape=jax.ShapeDtypeStruct((fanout * n, D), x.dtype),
        grid=(n // tm,),
        in_specs=[
            pl.BlockSpec((1, 1, fanout * tm), lambda i: (i, 0, 0), memory_space=pltpu.SMEM),
            pl.BlockSpec((tm, D), lambda i: (i, 0)),
        ],
        out_specs=pl.BlockSpec(memory_space=pl.ANY),
        scratch_shapes=[pltpu.SemaphoreType.DMA(())],
        compiler_params=_params(("arbitrary",)),
        name=name,
    )(idx.reshape(n // tm, 1, fanout * tm), x)


def _route_plan(route, tm):
    n = 2 * route.shape[0]
    n_tiles = n // tm
    ids = route[:, :2].astype(I32).reshape(n)
    onehot = (ids[:, None] == jnp.arange(N_EXPERTS, dtype=I32)[None, :]).astype(I32)
    seen = jnp.cumsum(onehot, axis=0)
    counts = seen[-1]
    starts = (jnp.cumsum(counts) - counts).astype(I32)
    pos = jnp.sum(onehot * (starts[None, :] + seen - 1), axis=1).astype(I32)
    bounds = jnp.sort(jnp.concatenate([jnp.arange(n_tiles, dtype=I32) * tm, starts]))
    ends = jnp.concatenate([bounds[1:], jnp.full((1,), n, I32)])
    seg_tile = jnp.minimum(bounds // tm, n_tiles - 1)
    seg_exp = jnp.clip(jnp.searchsorted(starts, bounds, side="right").astype(I32) - 1, 0, N_EXPERTS - 1)
    seg_first = jnp.concatenate([jnp.ones((1,), I32), (seg_tile[1:] != seg_tile[:-1]).astype(I32)])
    return pos, (seg_tile, seg_exp, bounds - seg_tile * tm, ends - seg_tile * tm, seg_first)


def _experts_kernel(tile_ref, exp_ref, lo_ref, hi_ref, first_ref, x_ref, wg_ref, wu_ref, wd_ref, o_ref):
    s = pl.program_id(0)
    lo, hi = lo_ref[s], hi_ref[s]

    @pl.when(first_ref[s] == 1)
    def _():
        o_ref[...] = jnp.zeros_like(o_ref)

    @pl.when(hi > lo)
    def _():
        x = x_ref[...].astype(MXU_DTYPE)
        a = jnp.dot(x, wg_ref[...], preferred_element_type=F32)
        u = jnp.dot(x, wu_ref[...], preferred_element_type=F32)
        y = _mm(a / (1.0 + jnp.exp(-a)) * u, wd_ref[...])
        row = lax.broadcasted_iota(I32, y.shape, 0)
        o_ref[...] += jnp.where(row >= lo, jnp.where(row < hi, y, 0.0), 0.0)


def _experts(xs, segs, wg, wu, wd, tm):
    n, D = xs.shape
    F = wg.shape[2]
    tile_of = lambda s, tile, exp, lo, hi, first: (tile[s], 0)
    w_of = lambda s, tile, exp, lo, hi, first: (exp[s], 0, 0)
    return pl.pallas_call(
        _experts_kernel,
        out_shape=jax.ShapeDtypeStruct((n, D), F32),
        grid_spec=pltpu.PrefetchScalarGridSpec(
            num_scalar_prefetch=5,
            grid=(segs[0].shape[0],),
            in_specs=[
                pl.BlockSpec((tm, D), tile_of),
                pl.BlockSpec((None, D, F), w_of),
                pl.BlockSpec((None, D, F), w_of),
                pl.BlockSpec((None, F, D), w_of),
            ],
            out_specs=pl.BlockSpec((tm, D), tile_of),
        ),
        compiler_params=_params(("arbitrary",)),
        name="experts",
    )(*segs, xs, wg, wu, wd)


def _combine_kernel(x1_ref, ya_ref, yb_ref, route_ref, o_ref):
    r = route_ref[...]
    lane = lax.broadcasted_iota(I32, r.shape, 1)
    w1 = jnp.sum(jnp.where(lane == 2, r, 0.0), axis=1, keepdims=True)
    w2 = jnp.sum(jnp.where(lane == 3, r, 0.0), axis=1, keepdims=True)
    o_ref[...] = x1_ref[...] + (w1 * ya_ref[...] + w2 * yb_ref[...])


def _combine(x1, y2, route, tm):
    T, D = x1.shape
    nt = T // tm
    row = lambda i: (i, 0)
    return pl.pallas_call(
        _combine_kernel,
        out_shape=jax.ShapeDtypeStruct((T, D), F32),
        grid=(nt,),
        in_specs=[pl.BlockSpec((tm, D), row), pl.BlockSpec((tm, D), row), pl.BlockSpec((tm, D), lambda i: (i + nt, 0)),
                  pl.BlockSpec((tm, LANES), row)],
        out_specs=pl.BlockSpec((tm, D), row),
        compiler_params=_params(("parallel",)),
        name="moe_combine",
    )(x1, y2, y2, route)


def _moe(h2, route, x1, wg, wu, wd):
    T, D = x1.shape
    tm = min(MOE_TM, T)
    pos, segs = _route_plan(route, tm)
    xs = _row_scatter(h2, pos, 2, tm, "moe_scatter")
    ys = _experts(xs, segs, wg, wu, wd, tm)
    y2 = _row_gather(ys, jnp.concatenate([pos[0::2], pos[1::2]]), F32, tm, "moe_gather")
    return _combine(x1, y2, route, min(512, T))


def _pad_cols(w, n):
    return jnp.pad(w, ((0, 0), (0, n - w.shape[1])))


def _layer(x2, B, S, p):
    T, D = x2.shape
    assert D == 2 * GLA_HEADS * GLA_DK == GLA_HEADS * GLA_DV == DSA_HEADS * DSA_HEAD_DIM
    assert S % 512 == 0
    cd = MXU_DTYPE
    qk_w, v_w, dsa_w = GLA_HEADS * GLA_DK, GLA_HEADS * GLA_DV, DSA_HEADS * DSA_HEAD_DIM
    splits = (qk_w, qk_w, v_w, GLA_GATE_RANK, v_w, dsa_w, dsa_w, dsa_w, IDX_Q_W, IDX_DIM, IDX_HEADS, D, D)
    offs = [0]
    for s in splits:
        offs.append(offs[-1] + s)
    w_in = p["w_in"]
    (w_gq, w_gk, w_gv, w_ga, w_gr, w_dq, w_dk, w_dv, w_iq, w_ik, w_iw, w_ta, w_tb) = [
        w_in[:, offs[i]:offs[i + 1]] for i in range(len(splits))]

    w_main = jnp.concatenate([w_gq, w_gk, w_gv, w_gr, w_ta, w_tb], axis=1).astype(cd)
    w_qk = jnp.concatenate([w_dq, w_dk], axis=1).astype(cd)
    g_qk = jnp.concatenate([jnp.tile(p["dsa_q_norm_g"] * (DSA_HEAD_DIM ** -0.5 * math.log2(math.e)), DSA_HEADS),
                            jnp.tile(p["dsa_k_norm_g"], DSA_HEADS)]).reshape(1, 2 * dsa_w).astype(F32)
    w_vt = w_dv.T.astype(cd)
    w_idx = jnp.concatenate([_pad_cols(w_ik, LANES), _pad_cols(w_ga, LANES)], axis=1).astype(cd)
    w_idxt = jnp.pad(jnp.concatenate([w_iq, w_iw], axis=1).T, ((0, 16 - IDX_HEADS), (0, 0))).astype(cd)

    tm = min(1024, T)
    h = _rmsnorm(x2, p["norm1_g"], tm)
    zm = _matmul(h, w_main, tm, 1024, cd)
    qk = _qk_proj(h, w_qk, g_qk, tm)
    vt = _vt_proj(h, w_vt, B, S, min(512, S))
    iqt, ki, ga, wit = _idx_proj(h, w_idx, w_idxt, p["idx_k_ln_g"].reshape(1, -1), p["idx_k_ln_b"].reshape(1, -1),
                                 min(512, T))

    o_gla = _gla(zm, ga, p["gla_w_a2"], p["gla_b_a"].reshape(1, -1), p["gla_norm_g"].reshape(1, -1), B, S, 512)

    QB, KB = 256, 512
    mask = _dsa_select(iqt, wit, ki, B, S, QB, KB)
    o_dsa = _dsa_attn(qk, vt, mask, B, S, QB, KB)

    w_r = _pad_cols(jnp.concatenate([p["w_router_expert"], p["w_router_group"]], axis=1), LANES)
    b_r = _pad_cols(jnp.concatenate([p["b_router_expert"], p["b_router_group"]]).reshape(1, -1), LANES)
    x1, h2, route = _post(o_gla, o_dsa, zm, x2, p["w_branch_gla"].astype(cd), p["w_branch_dsa"].astype(cd),
                          p["w_out"].astype(cd), p["norm2_g"].reshape(1, -1), w_r, b_r, min(512, T))
    return _moe(h2, route, x1, p["w_exp_gate"].astype(cd), p["w_exp_up"].astype(cd), p["w_exp_down"].astype(cd))


def kernel(x, norm1_g, w_in, gla_w_a2, gla_b_a, gla_norm_g, dsa_q_norm_g, dsa_k_norm_g, idx_k_ln_g, idx_k_ln_b,
           w_branch_gla, w_branch_dsa, w_out, norm2_g, w_router_group, b_router_group, w_router_expert,
           b_router_expert, w_exp_gate, w_exp_up, w_exp_down):
    B, S, D = x.shape
    stacked = dict(norm1_g=norm1_g, w_in=w_in, gla_w_a2=gla_w_a2, gla_b_a=gla_b_a, gla_norm_g=gla_norm_g,
                   dsa_q_norm_g=dsa_q_norm_g, dsa_k_norm_g=dsa_k_norm_g, idx_k_ln_g=idx_k_ln_g,
                   idx_k_ln_b=idx_k_ln_b, w_branch_gla=w_branch_gla, w_branch_dsa=w_branch_dsa, w_out=w_out,
                   norm2_g=norm2_g, w_router_group=w_router_group, b_router_group=b_router_group,
                   w_router_expert=w_router_expert, b_router_expert=b_router_expert, w_exp_gate=w_exp_gate,
                   w_exp_up=w_exp_up, w_exp_down=w_exp_down)
    x2 = x.reshape(B * S, D).astype(F32)
    for l in range(w_in.shape[0]):
        x2 = _layer(x2, B, S, {k: v[l] for k, v in stacked.items()})
    return x2.reshape(B, S, D).astype(x.dtype)
```

```python
import functools
import math

import jax
import jax.numpy as jnp
from jax import lax
from jax.experimental import pallas as pl
from jax.experimental.pallas import tpu as pltpu

F32 = jnp.float32
BF16 = jnp.bfloat16
I32 = jnp.int32
MXU_DTYPE = BF16

CHUNK = 64
EPS = 1e-6
GLA_HEADS, GLA_DK, GLA_DV = 4, 128, 256
GLA_GATE_RANK = 16
GLA_GATE_TEMP = 16.0
DSA_HEADS, DSA_HEAD_DIM = 8, 128
IDX_HEADS, IDX_DIM = 8, 64
TOPK_MAX = 256
N_GROUPS, EXPERTS_PER_GROUP = 4, 8
N_EXPERTS = N_GROUPS * EXPERTS_PER_GROUP
LANES = 128

INT_MIN = -(2 ** 31)
INT_MAX = 2 ** 31 - 1
I16 = jnp.int16
I16_MIN, I16_MAX = -(2 ** 15), 2 ** 15 - 1

VMEM_LIMIT = 56 * 1024 * 1024

_NT = (((1,), (1,)), ((), ()))
_TN = (((0,), (0,)), ((), ()))


def _params(sem):
    return pltpu.CompilerParams(dimension_semantics=sem, vmem_limit_bytes=VMEM_LIMIT)


def _mm(a, b):
    return jnp.dot(a.astype(MXU_DTYPE), b.astype(MXU_DTYPE), preferred_element_type=F32)


def _mm_nt(a, b):
    return lax.dot_general(a.astype(MXU_DTYPE), b.astype(MXU_DTYPE), _NT, preferred_element_type=F32)


def _mm_tn(a, b):
    return lax.dot_general(a.astype(MXU_DTYPE), b.astype(MXU_DTYPE), _TN, preferred_element_type=F32)


def _split(a):
    hi = a.astype(BF16)
    lo = (a - hi.astype(F32)).astype(BF16)
    return hi, lo


def _dot3(a, b):
    ah, al = _split(a)
    bh, bl = _split(b)
    d = lambda u, v: jnp.dot(u, v, preferred_element_type=F32)
    return d(ah, bh) + (d(ah, bl) + d(al, bh))


def _rmsnorm_kernel(x_ref, g_ref, o_ref):
    x = x_ref[...]
    ms = jnp.mean(x * x, axis=-1, keepdims=True)
    o_ref[...] = (x * lax.rsqrt(ms + EPS) * g_ref[...]).astype(o_ref.dtype)


def _rmsnorm(x2, g, tm):
    T, D = x2.shape
    return pl.pallas_call(
        _rmsnorm_kernel,
        out_shape=jax.ShapeDtypeStruct((T, D), MXU_DTYPE),
        grid=(T // tm,),
        in_specs=[pl.BlockSpec((tm, D), lambda i: (i, 0)), pl.BlockSpec((1, D), lambda i: (0, 0))],
        out_specs=pl.BlockSpec((tm, D), lambda i: (i, 0)),
        compiler_params=_params(("parallel",)),
        name="rmsnorm",
    )(x2, g.reshape(1, D))


def _matmul_kernel(h_ref, w_ref, o_ref):
    o_ref[...] = jnp.dot(h_ref[...], w_ref[...], preferred_element_type=F32).astype(o_ref.dtype)


def _matmul(h, w, tm, tn, out_dtype):
    T, D = h.shape
    N = w.shape[1]
    return pl.pallas_call(
        _matmul_kernel,
        out_shape=jax.ShapeDtypeStruct((T, N), out_dtype),
        grid=(T // tm, N // tn),
        in_specs=[pl.BlockSpec((tm, D), lambda i, j: (i, 0)), pl.BlockSpec((D, tn), lambda i, j: (0, j))],
        out_specs=pl.BlockSpec((tm, tn), lambda i, j: (i, j)),
        compiler_params=_params(("parallel", "arbitrary")),
        name="proj_main",
    )(h, w)


def _qk_kernel(h_ref, w_ref, g_ref, o_ref):
    z = jnp.dot(h_ref[...], w_ref[...], preferred_element_type=F32)
    for hh in range(DSA_HEADS):
        sl = slice(hh * DSA_HEAD_DIM, (hh + 1) * DSA_HEAD_DIM)
        zh = z[:, sl]
        ms = jnp.mean(zh * zh, axis=-1, keepdims=True)
        o_ref[:, sl] = (zh * lax.rsqrt(ms + EPS) * g_ref[:, sl]).astype(o_ref.dtype)


def _qk_proj(h, w, g, tm):
    T, D = h.shape
    W = DSA_HEADS * DSA_HEAD_DIM
    return pl.pallas_call(
        _qk_kernel,
        out_shape=jax.ShapeDtypeStruct((T, 2 * W), MXU_DTYPE),
        grid=(T // tm, 2),
        in_specs=[
            pl.BlockSpec((tm, D), lambda i, j: (i, 0)),
            pl.BlockSpec((D, W), lambda i, j: (0, j)),
            pl.BlockSpec((1, W), lambda i, j: (0, j)),
        ],
        out_specs=pl.BlockSpec((tm, W), lambda i, j: (i, j)),
        compiler_params=_params(("parallel", "arbitrary")),
        name="proj_qk",
    )(h, w, g)


VT_ONES = 16
VT_ROWS = DSA_HEAD_DIM + VT_ONES


def _vt_kernel(h_ref, wt_ref, o_ref):
    vt = lax.dot_general(wt_ref[...], h_ref[...], _NT, preferred_element_type=F32).astype(o_ref.dtype)
    ones = jnp.ones((VT_ONES, vt.shape[1]), o_ref.dtype)
    for hh in range(DSA_HEADS):
        o_ref[hh, :DSA_HEAD_DIM, :] = vt[hh * DSA_HEAD_DIM:(hh + 1) * DSA_HEAD_DIM]
        o_ref[hh, DSA_HEAD_DIM:, :] = ones


def _vt_proj(h, wt, B, S, tm):
    T, D = h.shape
    W = wt.shape[0]
    nt = S // tm
    return pl.pallas_call(
        _vt_kernel,
        out_shape=jax.ShapeDtypeStruct((B, DSA_HEADS, VT_ROWS, S), MXU_DTYPE),
        grid=(T // tm,),
        in_specs=[pl.BlockSpec((tm, D), lambda i: (i, 0)), pl.BlockSpec((W, D), lambda i: (0, 0))],
        out_specs=pl.BlockSpec((None, DSA_HEADS, VT_ROWS, tm), lambda i: (i // nt, 0, 0, i % nt)),
        compiler_params=_params(("parallel",)),
        name="proj_vt",
    )(h, wt)


IDX_Q_W = IDX_HEADS * IDX_DIM


def _idx_kernel(h_ref, w_ref, wt_ref, lng_ref, lnb_ref, iqt_ref, ki_ref, ga_ref, wit_ref):
    h = h_ref[...]
    z = jnp.dot(h, w_ref[...], preferred_element_type=F32)
    ik = z[:, :IDX_DIM]
    mu = jnp.mean(ik, axis=-1, keepdims=True)
    var = jnp.mean(jnp.square(ik - mu), axis=-1, keepdims=True)
    ki = (ik - mu) * lax.rsqrt(var + EPS) * lng_ref[...] + lnb_ref[...]
    ki_ref[...] = ki.astype(ki_ref.dtype)
    ga_ref[...] = z[:, LANES:LANES + GLA_GATE_RANK]
    zt = lax.dot_general(wt_ref[...], h, _NT, preferred_element_type=F32)
    iqt_ref[...] = zt[:IDX_Q_W].astype(iqt_ref.dtype)
    wit_ref[...] = zt[IDX_Q_W:IDX_Q_W + IDX_HEADS] * (IDX_HEADS ** -0.5 * IDX_DIM ** -0.5)


def _idx_proj(h, w, wt, lng, lnb, tm):
    T, D = h.shape
    return pl.pallas_call(
        _idx_kernel,
        out_shape=(
            jax.ShapeDtypeStruct((IDX_Q_W, T), MXU_DTYPE),
            jax.ShapeDtypeStruct((T, IDX_DIM), MXU_DTYPE),
            jax.ShapeDtypeStruct((T, GLA_GATE_RANK), F32),
            jax.ShapeDtypeStruct((IDX_HEADS, T), F32),
        ),
        grid=(T // tm,),
        in_specs=[
            pl.BlockSpec((tm, D), lambda i: (i, 0)),
            pl.BlockSpec(w.shape, lambda i: (0, 0)),
            pl.BlockSpec(wt.shape, lambda i: (0, 0)),
            pl.BlockSpec((1, IDX_DIM), lambda i: (0, 0)),
            pl.BlockSpec((1, IDX_DIM), lambda i: (0, 0)),
        ],
        out_specs=(
            pl.BlockSpec((IDX_Q_W, tm), lambda i: (0, i)),
            pl.BlockSpec((tm, IDX_DIM), lambda i: (i, 0)),
            pl.BlockSpec((tm, GLA_GATE_RANK), lambda i: (i, 0)),
            pl.BlockSpec((IDX_HEADS, tm), lambda i: (0, i)),
        ),
        compiler_params=_params(("parallel",)),
        name="proj_idx",
    )(h, w, wt, lng, lnb)


def _gla_kernel(q_ref, k_ref, v_ref, r_ref, ga_ref, wa2_ref, ba_ref, ng_ref, o_ref, st_ref, tot_ref, kd_ref,
                oraw_ref, *, n_chunks):
    tb = q_ref.shape[0]

    @pl.when(pl.program_id(1) == 0)
    def _():
        st_ref[...] = jnp.zeros_like(st_ref)

    x = _dot3(ga_ref[...], wa2_ref[...]) + ba_ref[...]
    la = (jnp.minimum(x, 0.0) - jnp.log(1.0 + jnp.exp(-jnp.abs(x)))) * (1.0 / GLA_GATE_TEMP)
    row = lax.broadcasted_iota(I32, (tb, tb), 0)
    col = lax.broadcasted_iota(I32, (tb, tb), 1)
    same = lax.shift_right_logical(row, 6) == lax.shift_right_logical(col, 6)
    ones_blk = jnp.where(same, 1.0, 0.0).astype(BF16)
    tril_blk = jnp.where(same, jnp.where(row >= col, 1.0, 0.0), 0.0).astype(BF16)
    la_hi, la_lo = _split(la)
    d = lambda u, v: jnp.dot(u, v, preferred_element_type=F32)
    tot = d(ones_blk, la_hi) + d(ones_blk, la_lo)
    cum = d(tril_blk, la_hi) + d(tril_blk, la_lo)
    tot_ref[...] = tot
    kd_ref[...] = (k_ref[...].astype(F32) * jnp.exp(tot - cum)).astype(kd_ref.dtype)

    heads = range(GLA_HEADS)
    ksl = lambda hh: slice(hh * GLA_DK, (hh + 1) * GLA_DK)
    vsl = lambda hh: slice(hh * GLA_DV, (hh + 1) * GLA_DV)

    def chunk(c, carry):
        rows = pl.ds(pl.multiple_of(c * CHUNK, CHUNK), CHUNK)
        first = pl.ds(pl.multiple_of(c * CHUNK, CHUNK), 1)
        upd = [_mm_tn(v_ref[rows, vsl(hh)], kd_ref[rows, ksl(hh)]) for hh in heads]
        st = [st_ref[hh] * jnp.exp(tot_ref[first, ksl(hh)]) + upd[hh] for hh in heads]
        for hh in heads:
            st_ref[hh] = st[hh]
        for hh in heads:
            oraw_ref[rows, vsl(hh)] = _mm_nt(q_ref[rows, ksl(hh)], st[hh])
        return carry

    lax.fori_loop(0, n_chunks, chunk, 0)

    ng = ng_ref[...]
    for hh in heads:
        o = oraw_ref[:, vsl(hh)] * (GLA_DK ** -0.5)
        ms = jnp.mean(o * o, axis=-1, keepdims=True)
        r = r_ref[:, vsl(hh)].astype(F32)
        o_ref[:, vsl(hh)] = (o * lax.rsqrt(ms + EPS) * ng * (r / (1.0 + jnp.exp(-r)))).astype(o_ref.dtype)


def _gla(zm, ga, wa2, ba, ng, B, S, tb):
    T = B * S
    nb = S // tb
    qk_w = GLA_HEADS * GLA_DK
    v_w = GLA_HEADS * GLA_DV
    tok = lambda b, i: b * nb + i
    return pl.pallas_call(
        functools.partial(_gla_kernel, n_chunks=tb // CHUNK),
        out_shape=jax.ShapeDtypeStruct((T, v_w), MXU_DTYPE),
        grid=(B, nb),
        in_specs=[
            pl.BlockSpec((tb, qk_w), lambda b, i: (tok(b, i), 0)),
            pl.BlockSpec((tb, qk_w), lambda b, i: (tok(b, i), 1)),
            pl.BlockSpec((tb, v_w), lambda b, i: (tok(b, i), 1)),
            pl.BlockSpec((tb, v_w), lambda b, i: (tok(b, i), 2)),
            pl.BlockSpec((tb, GLA_GATE_RANK), lambda b, i: (tok(b, i), 0)),
            pl.BlockSpec((GLA_GATE_RANK, qk_w), lambda b, i: (0, 0)),
            pl.BlockSpec((1, qk_w), lambda b, i: (0, 0)),
            pl.BlockSpec((1, GLA_DV), lambda b, i: (0, 0)),
        ],
        out_specs=pl.BlockSpec((tb, v_w), lambda b, i: (tok(b, i), 0)),
        scratch_shapes=[pltpu.VMEM((GLA_HEADS, GLA_DV, GLA_DK), F32), pltpu.VMEM((tb, qk_w), F32),
                        pltpu.VMEM((tb, qk_w), MXU_DTYPE), pltpu.VMEM((tb, v_w), F32)],
        compiler_params=_params(("parallel", "arbitrary")),
        name="gla",
    )(zm, zm, zm, zm, ga, wa2, ba, ng)


def _last_kb(qb, QB, KB):
    return ((qb + 1) * QB - 1) // KB


def _causal_steps(S, QB, KB):
    pairs = [(q, k) for q in range(S // QB) for k in range(_last_kb(q, QB, KB) + 1)]
    qs, ks = zip(*pairs)
    return jnp.asarray(qs, I32), jnp.asarray(ks, I32)


def _select_kernel(qb_ref, kb_ref, iqt_ref, wit_ref, ki_ref, mask_ref, hi_ref, lo_ref, l2_ref, gm_ref, *,
                   QB, KB, S, topk):
    qb = qb_ref[pl.program_id(1)]
    kb = kb_ref[pl.program_id(1)]
    last = _last_kb(qb, QB, KB)

    def rows_of(i):
        return pl.ds(pl.multiple_of(i * KB, KB), KB)

    @pl.when(kb == 0)
    def _():
        gm_ref[...] = jnp.full(gm_ref.shape, INT_MIN, I32)

    rc = 128
    assert topk % rc == 0 and KB % rc == 0
    t_chunk = lax.shift_right_logical(qb * QB + lax.broadcasted_iota(I32, (rc, QB), 1), 6)
    for c in range(KB // rc):
        ki = ki_ref[c * rc:(c + 1) * rc, :]
        sc = jnp.zeros((rc, QB), F32)
        for hh in range(IDX_HEADS):
            lg = _mm(ki, iqt_ref[hh * IDX_DIM:(hh + 1) * IDX_DIM, :])
            sc = sc + jnp.maximum(lg, 0.0) * wit_ref[hh:hh + 1, :]
        s_chunk = lax.shift_right_logical(kb * KB + c * rc + lax.broadcasted_iota(I32, (rc, QB), 0), 6)
        bits = lax.bitcast_convert_type(sc, I32)
        sign = lax.shift_right_arithmetic(bits, 31)
        key = (bits ^ (sign & INT_MAX)) - sign
        key = jnp.where(s_chunk <= t_chunk, key, INT_MIN)
        rows = pl.ds(pl.multiple_of(kb * KB + c * rc, rc), rc)
        hi_ref[rows, :] = lax.shift_right_arithmetic(key, 16).astype(I16)
        lo_ref[rows, :] = ((key & 0xFFFF) + I16_MIN).astype(I16)
        g0 = (c * rc) % topk
        gm_ref[g0:g0 + rc, :] = jnp.maximum(gm_ref[g0:g0 + rc, :], key)

    @pl.when(kb == last)
    def _():
        nblk = last + 1
        slab = 32
        one, zero = jnp.int16(1), jnp.int16(0)

        def fold(m, reduce):
            part = m[0:slab]
            for r in range(1, KB // slab):
                part = reduce(part, m[r * slab:(r + 1) * slab])
            return part

        def count(pred):
            def body(i, acc):
                return acc + fold(pred(i), jnp.add)
            acc = lax.fori_loop(0, nblk, body, jnp.zeros((slab, QB), I16))
            return jnp.sum(acc.astype(I32), axis=0, keepdims=True)

        def count_ge(ref, thr):
            t16 = thr.astype(I16)
            return count(lambda i: jnp.where(ref[rows_of(i), :] >= t16, one, zero))

        def extreme(ref, pick, reduce, fill):
            def body(i, acc):
                t = ref[rows_of(i), :]
                return reduce(acc, fold(jnp.where(pick(t), t, jnp.int16(fill)), reduce))
            return lax.fori_loop(0, nblk, body, jnp.full((slab, QB), fill, I16)).astype(I32)

        burst = 4

        def search(ref, target, lo, c_lo, hi, c_hi, tighten_at):
            def is_open(lo, c_lo, hi):
                return jnp.where(c_lo > target, jnp.where(hi != lo + 1, 1, 0), 0)

            def tighten(lo, c_lo, hi):
                lo16, top16 = lo.astype(I16), (hi - 1).astype(I16)
                min16 = lambda a, b: jnp.where(a < b, a, b)
                max16 = lambda a, b: jnp.where(a > b, a, b)
                lo_t = jnp.min(extreme(ref, lambda t: t >= lo16, min16, I16_MAX), axis=0, keepdims=True)
                hi_t = jnp.max(extreme(ref, lambda t: t <= top16, max16, I16_MIN), axis=0, keepdims=True) + 1
                open_ = is_open(lo, c_lo, hi) > 0
                return jnp.where(open_, lo_t, lo), jnp.where(open_, hi_t, hi)

            def step(st):
                it, lo, c_lo, hi, c_hi = st
                do_tighten = functools.reduce(jnp.logical_or, [it == k for k in tighten_at])
                lo, hi = lax.cond(do_tighten, lambda: tighten(lo, c_lo, hi), lambda: (lo, hi))
                for _ in range(burst):
                    open_ = is_open(lo, c_lo, hi) > 0
                    mid = lax.shift_right_arithmetic(lo + hi, 1)
                    c = count_ge(ref, mid)
                    up = jnp.logical_and(open_, c >= target)
                    dn = jnp.logical_and(open_, c < target)
                    lo, c_lo = jnp.where(up, mid, lo), jnp.where(up, c, c_lo)
                    hi, c_hi = jnp.where(dn, mid, hi), jnp.where(dn, c, c_hi)
                return it + 1, lo, c_lo, hi, c_hi

            def unfinished(st):
                _it, lo, c_lo, hi, _c_hi = st
                return jnp.max(is_open(lo, c_lo, hi)) > 0

            _it, lo, c_lo, hi, c_hi = lax.while_loop(unfinished, step, (jnp.int32(0), lo, c_lo, hi, c_hi))
            return lo, c_lo, c_hi

        zeros = jnp.zeros((1, QB), I32)
        gm = gm_ref[...]
        g_lo = jnp.maximum(lax.shift_right_arithmetic(jnp.min(gm, axis=0, keepdims=True), 16), I16_MIN + 1)
        g_hi = lax.shift_right_arithmetic(jnp.max(gm, axis=0, keepdims=True), 16) + 1
        hstar, ch_ge, ch_gt = search(hi_ref, topk, g_lo, count_ge(hi_ref, g_lo), g_hi, zeros, (3,))
        split = ch_ge > topk
        h16 = hstar.astype(I16)

        def low_half():
            def build(i, carry):
                l2_ref[rows_of(i), :] = jnp.where(hi_ref[rows_of(i), :] == h16, lo_ref[rows_of(i), :],
                                                  jnp.int16(I16_MIN))
                return carry
            lax.fori_loop(0, nblk, build, 0)
            target = jnp.where(split, topk - ch_gt, INT_MAX)
            return search(l2_ref, target, jnp.full((1, QB), I16_MIN, I32), ch_ge - ch_gt,
                          jnp.full((1, QB), I16_MAX + 1, I32), zeros, (0, 2))

        any_split = jnp.max(jnp.where(split, 1, 0)) > 0
        lstar, cl_ge, cl_gt = lax.cond(any_split, low_half,
                                       lambda: (jnp.full((1, QB), I16_MIN, I32), zeros, zeros))
        lstar = jnp.where(split, lstar, I16_MIN)
        l16 = lstar.astype(I16)
        c_ge = jnp.where(split, ch_gt + cl_ge, ch_ge)
        c_gt = jnp.where(split, ch_gt + cl_gt, ch_gt)
        excess = c_ge > topk
        need = jnp.where(excess, topk - c_gt, S + 1)
        row16 = lax.broadcasted_iota(I32, (KB, QB), 0).astype(I16)

        def before(i, j):
            return jnp.where(row16 + (i * KB).astype(I16) < j.astype(I16), one, zero)

        def tied(i, then):
            return jnp.where(hi_ref[rows_of(i), :] == h16, jnp.where(lo_ref[rows_of(i), :] == l16, then, zero), zero)

        def tie_cut():
            def step(_, c):
                lo, hi = c
                mid = lax.shift_right_arithmetic(lo + hi, 1)
                ok = count(lambda i: tied(i, before(i, mid))) >= need
                return jnp.where(ok, lo, mid + 1), jnp.where(ok, mid, hi)
            n_steps = max(1, math.ceil(math.log2(S + 1)))
            lo, _hi = lax.fori_loop(0, n_steps, step, (zeros, jnp.full((1, QB), S, I32)))
            return lo

        any_excess = jnp.max(jnp.where(excess, 1, 0)) > 0
        jcut = lax.cond(any_excess, tie_cut, lambda: jnp.full((1, QB), S, I32))

        def write(i, carry):
            hi_t, lo_t = hi_ref[rows_of(i), :], lo_ref[rows_of(i), :]
            in_bucket = jnp.where(lo_t > l16, one, jnp.where(lo_t == l16, before(i, jcut), zero))
            sel = jnp.where(hi_t > h16, one, jnp.where(hi_t == h16, in_bucket, zero))
            mask_ref[rows_of(i), :] = sel.astype(mask_ref.dtype)
            return carry

        lax.fori_loop(0, nblk, write, 0)

        def clear(i, carry):
            mask_ref[rows_of(i), :] = jnp.zeros((KB, QB), mask_ref.dtype)
            return carry

        lax.fori_loop(nblk, S // KB, clear, 0)


def _dsa_select(iqt, wit, ki, B, S, QB, KB):
    nq, nk = S // QB, S // KB
    topk = min(TOPK_MAX, S // 4)
    assert KB % topk == 0
    qs, ks = _causal_steps(S, QB, KB)
    return pl.pallas_call(
        functools.partial(_select_kernel, QB=QB, KB=KB, S=S, topk=topk),
        out_shape=jax.ShapeDtypeStruct((B, S, S), jnp.int8),
        grid_spec=pltpu.PrefetchScalarGridSpec(
            num_scalar_prefetch=2,
            grid=(B, qs.shape[0]),
            in_specs=[
                pl.BlockSpec((IDX_Q_W, QB), lambda b, s, qs, ks: (0, b * nq + qs[s])),
                pl.BlockSpec((IDX_HEADS, QB), lambda b, s, qs, ks: (0, b * nq + qs[s])),
                pl.BlockSpec((KB, IDX_DIM), lambda b, s, qs, ks: (b * nk + ks[s], 0)),
            ],
            out_specs=pl.BlockSpec((None, S, QB), lambda b, s, qs, ks: (b, 0, qs[s])),
            scratch_shapes=[pltpu.VMEM((S, QB), I16), pltpu.VMEM((S, QB), I16), pltpu.VMEM((S, QB), I16),
                            pltpu.VMEM((topk, QB), I32)],
        ),
        compiler_params=_params(("parallel", "arbitrary")),
        name="dsa_select",
    )(qs, ks, iqt, wit, ki)


def _attn_kernel(qb_ref, kb_ref, q_ref, k_ref, vt_ref, m_ref, o_ref, acc_ref, *, QB, KB):
    qb = qb_ref[pl.program_id(1)]
    kb = kb_ref[pl.program_id(1)]
    last = _last_kb(qb, QB, KB)

    @pl.when(kb == 0)
    def _():
        acc_ref[...] = jnp.zeros_like(acc_ref)

    mb = m_ref[...].astype(MXU_DTYPE)
    head = lambda hh: slice(hh * DSA_HEAD_DIM, (hh + 1) * DSA_HEAD_DIM)
    logits = lambda hh: _mm_nt(k_ref[:, head(hh)], q_ref[:, head(hh)])
    lg = logits(0)
    for hh in range(DSA_HEADS):
        lg_next = logits(hh + 1) if hh + 1 < DSA_HEADS else None
        p = jnp.exp2(lg).astype(MXU_DTYPE) * mb
        acc_ref[hh] += jnp.dot(vt_ref[hh], p, preferred_element_type=F32)
        lg = lg_next

    @pl.when(kb == last)
    def _():
        for hh in range(DSA_HEADS):
            acc = acc_ref[hh]
            o = acc[:DSA_HEAD_DIM] / acc[DSA_HEAD_DIM:DSA_HEAD_DIM + 1]
            o_ref[:, hh * DSA_HEAD_DIM:(hh + 1) * DSA_HEAD_DIM] = o.T.astype(o_ref.dtype)


def _dsa_attn(qk, vt, mask, B, S, QB, KB):
    T = B * S
    nq, nk = S // QB, S // KB
    W = DSA_HEADS * DSA_HEAD_DIM
    qs, ks = _causal_steps(S, QB, KB)
    return pl.pallas_call(
        functools.partial(_attn_kernel, QB=QB, KB=KB),
        out_shape=jax.ShapeDtypeStruct((T, W), MXU_DTYPE),
        grid_spec=pltpu.PrefetchScalarGridSpec(
            num_scalar_prefetch=2,
            grid=(B, qs.shape[0]),
            in_specs=[
                pl.BlockSpec((QB, W), lambda b, s, qs, ks: (b * nq + qs[s], 0)),
                pl.BlockSpec((KB, W), lambda b, s, qs, ks: (b * nk + ks[s], 1)),
                pl.BlockSpec((None, DSA_HEADS, VT_ROWS, KB), lambda b, s, qs, ks: (b, 0, 0, ks[s])),
                pl.BlockSpec((None, KB, QB), lambda b, s, qs, ks: (b, ks[s], qs[s])),
            ],
            out_specs=pl.BlockSpec((QB, W), lambda b, s, qs, ks: (b * nq + qs[s], 0)),
            scratch_shapes=[pltpu.VMEM((DSA_HEADS, VT_ROWS, QB), F32)],
        ),
        compiler_params=_params(("parallel", "arbitrary")),
        name="dsa_attn",
    )(qs, ks, qk, qk, vt, mask)


def _post_kernel(og_ref, od_ref, ga_ref, gb_ref, x_ref, pa_ref, pb_ref, wo_ref, g2_ref, wr_ref, br_ref,
                 x1_ref, h2_ref, route_ref):
    sig = lambda v: 1.0 / (1.0 + jnp.exp(-v))
    a = jnp.dot(og_ref[...], pa_ref[...], preferred_element_type=F32)
    b = jnp.dot(od_ref[...], pb_ref[...], preferred_element_type=F32)
    mix = sig(ga_ref[...].astype(F32)) * a + sig(gb_ref[...].astype(F32)) * b
    x1 = x_ref[...] + _mm(mix, wo_ref[...])
    x1_ref[...] = x1
    ms = jnp.mean(x1 * x1, axis=-1, keepdims=True)
    h2 = x1 * lax.rsqrt(ms + EPS) * g2_ref[...]
    h2_ref[...] = h2.astype(h2_ref.dtype)

    logits = _dot3(h2, wr_ref[...]) + br_ref[...]
    tm = logits.shape[0]
    lane = lax.broadcasted_iota(I32, (tm, LANES), 1).astype(F32)
    neg = -jnp.inf
    far = float(2 * LANES)
    rmax = lambda v: jnp.max(v, axis=1, keepdims=True)
    rmin = lambda v: jnp.min(v, axis=1, keepdims=True)
    gl = jnp.where(lane >= N_EXPERTS, jnp.where(lane < N_EXPERTS + N_GROUPS, logits, neg), neg)
    gmax = rmax(gl)
    g_w = 1.0 / jnp.sum(jnp.exp(gl - gmax), axis=1, keepdims=True)
    g_idx = rmin(jnp.where(gl == gmax, lane, far)) - N_EXPERTS
    e_lo = g_idx * EXPERTS_PER_GROUP
    el = jnp.where(lane >= e_lo, jnp.where(lane < e_lo + EXPERTS_PER_GROUP, logits, neg), neg)
    m1 = rmax(el)
    e1 = rmin(jnp.where(el == m1, lane, far))
    el2 = jnp.where(lane == e1, neg, el)
    m2 = rmax(el2)
    e2 = rmin(jnp.where(el2 == m2, lane, far))
    p2 = jnp.exp(m2 - m1)
    w1 = g_w / (1.0 + p2)
    route_ref[...] = jnp.where(lane == 0.0, e1, jnp.where(lane == 1.0, e2, jnp.where(
        lane == 2.0, w1, jnp.where(lane == 3.0, w1 * p2, 0.0))))


def _post(og, od, zm, x2, pa, pb, wo, g2, wr, br, tm):
    T, D = x2.shape
    row = lambda i: (i, 0)
    full = lambda i: (0, 0)
    return pl.pallas_call(
        _post_kernel,
        out_shape=(
            jax.ShapeDtypeStruct((T, D), F32),
            jax.ShapeDtypeStruct((T, D), F32),
            jax.ShapeDtypeStruct((T, LANES), F32),
        ),
        grid=(T // tm,),
        in_specs=[
            pl.BlockSpec((tm, D), row),
            pl.BlockSpec((tm, D), row),
            pl.BlockSpec((tm, D), lambda i: (i, 3)),
            pl.BlockSpec((tm, D), lambda i: (i, 4)),
            pl.BlockSpec((tm, D), row),
            pl.BlockSpec((D, D), full),
            pl.BlockSpec((D, D), full),
            pl.BlockSpec((D, D), full),
            pl.BlockSpec((1, D), full),
            pl.BlockSpec((D, LANES), full),
            pl.BlockSpec((1, LANES), full),
        ],
        out_specs=(pl.BlockSpec((tm, D), row), pl.BlockSpec((tm, D), row), pl.BlockSpec((tm, LANES), row)),
        compiler_params=_params(("parallel",)),
        name="merge_router",
    )(og, od, zm, zm, x2, pa, pb, wo, g2, wr, br)


MOE_TM = 256


def _gather_combine_kernel(idx_ref, nxt_ref, x1_ref, route_ref, src_hbm, o_ref, buf_ref, sem):
    i = pl.program_id(0)
    n = pl.num_programs(0)
    tm = o_ref.shape[0]
    slot = lax.rem(i, 2)

    def fetch(ids_ref, s):
        def start(r, carry):
            for k in range(2):
                pltpu.make_async_copy(src_hbm.at[pl.ds(ids_ref[0, 0, 2 * r + k], 1), :],
                                      buf_ref.at[s, pl.ds(k * tm + r, 1), :], sem.at[s]).start()
            return carry
        lax.fori_loop(0, tm, start, 0, unroll=8)

    @pl.when(i == 0)
    def _():
        fetch(idx_ref, 0)

    @pl.when(i + 1 < n)
    def _():
        fetch(nxt_ref, 1 - slot)

    pltpu.make_async_copy(src_hbm.at[pl.ds(0, 2 * tm), :], buf_ref.at[slot], sem.at[slot]).wait()
    r = route_ref[...]
    lane = lax.broadcasted_iota(I32, r.shape, 1)
    w1 = jnp.sum(jnp.where(lane == 2, r, 0.0), axis=1, keepdims=True)
    w2 = jnp.sum(jnp.where(lane == 3, r, 0.0), axis=1, keepdims=True)
    o_ref[...] = x1_ref[...] + (w1 * buf_ref[slot, :tm, :] + w2 * buf_ref[slot, tm:, :])


def _gather_combine(x1, ys, pos, route, tm):
    T, D = x1.shape
    nt = T // tm
    idx3 = pos.reshape(nt, 1, 2 * tm)
    row = lambda i: (i, 0)
    return pl.pallas_call(
        _gather_combine_kernel,
        out_shape=jax.ShapeDtypeStruct((T, D), F32),
        grid=(nt,),
        in_specs=[
            pl.BlockSpec((1, 1, 2 * tm), lambda i: (i, 0, 0), memory_space=pltpu.SMEM),
            pl.BlockSpec((1, 1, 2 * tm), lambda i: (jnp.minimum(i + 1, nt - 1), 0, 0), memory_space=pltpu.SMEM),
            pl.BlockSpec((tm, D), row),
            pl.BlockSpec((tm, LANES), row),
            pl.BlockSpec(memory_space=pl.ANY),
        ],
        out_specs=pl.BlockSpec((tm, D), row),
        scratch_shapes=[pltpu.VMEM((2, 2 * tm, D), ys.dtype), pltpu.SemaphoreType.DMA((2,))],
        compiler_params=_params(("arbitrary",)),
        name="moe_gather_combine",
    )(idx3, idx3, x1, route, ys)


def _scatter_kernel(idx_ref, x_ref, o_hbm, sem, *, fanout):
    tm = x_ref.shape[0]

    def start(r, carry):
        for k in range(fanout):
            dst = idx_ref[0, 0, fanout * r + k]
            pltpu.make_async_copy(x_ref.at[pl.ds(r, 1), :], o_hbm.at[pl.ds(dst, 1), :], sem).start()
        return carry

    lax.fori_loop(0, tm, start, 0, unroll=8)
    for k in range(fanout):
        pltpu.make_async_copy(x_ref, o_hbm.at[pl.ds(0, tm), :], sem).wait()


def _row_scatter(x, idx, fanout, tm, name):
    n, D = x.shape
    return pl.pallas_call(
        functools.partial(_scatter_kernel, fanout=fanout),
        out_shape=jax.ShapeDtypeStruct((fanout * n, D), x.dtype),
        grid=(n // tm,),
        in_specs=[
            pl.BlockSpec((1, 1, fanout * tm), lambda i: (i, 0, 0), memory_space=pltpu.SMEM),
            pl.BlockSpec((tm, D), lambda i: (i, 0)),
        ],
        out_specs=pl.BlockSpec(memory_space=pl.ANY),
        scratch_shapes=[pltpu.SemaphoreType.DMA(())],
        compiler_params=_params(("arbitrary",)),
        name=name,
    )(idx.reshape(n // tm, 1, fanout * tm), x)


def _route_plan(route, tm):
    n = 2 * route.shape[0]
    n_tiles = n // tm
    ids = route[:, :2].astype(I32).reshape(n)
    onehot = (ids[:, None] == jnp.arange(N_EXPERTS, dtype=I32)[None, :]).astype(F32).reshape(n_tiles, tm, N_EXPERTS)
    tril = (jnp.arange(tm)[:, None] >= jnp.arange(tm)[None, :]).astype(F32)
    in_tile = jnp.einsum("rc,tce->tre", tril, onehot)
    per_tile = in_tile[:, -1, :]
    before = jnp.cumsum(per_tile, axis=0) - per_tile
    counts = jnp.sum(per_tile, axis=0).astype(I32)
    starts = (jnp.cumsum(counts) - counts).astype(I32)
    rank = jnp.sum(onehot * (in_tile + before[:, None, :] - 1.0), axis=2).reshape(n)
    pos = (jnp.sum(onehot.reshape(n, N_EXPERTS) * starts[None, :].astype(F32), axis=1) + rank).astype(I32)
    bounds = jnp.sort(jnp.concatenate([jnp.arange(n_tiles, dtype=I32) * tm, starts]))
    ends = jnp.concatenate([bounds[1:], jnp.full((1,), n, I32)])
    seg_tile = jnp.minimum(bounds // tm, n_tiles - 1)
    seg_exp = jnp.clip(jnp.searchsorted(starts, bounds, side="right").astype(I32) - 1, 0, N_EXPERTS - 1)
    seg_first = jnp.concatenate([jnp.ones((1,), I32), (seg_tile[1:] != seg_tile[:-1]).astype(I32)])
    return pos, (seg_tile, seg_exp, bounds - seg_tile * tm, ends - seg_tile * tm, seg_first)


def _experts_kernel(tile_ref, exp_ref, lo_ref, hi_ref, first_ref, x_ref, wg_ref, wu_ref, wd_ref, o_ref):
    s = pl.program_id(0)
    lo, hi = lo_ref[s], hi_ref[s]

    @pl.when(first_ref[s] == 1)
    def _():
        o_ref[...] = jnp.zeros_like(o_ref)

    @pl.when(hi > lo)
    def _():
        x = x_ref[...].astype(MXU_DTYPE)
        a = jnp.dot(x, wg_ref[...], preferred_element_type=F32)
        u = jnp.dot(x, wu_ref[...], preferred_element_type=F32)
        y = _mm(a / (1.0 + jnp.exp(-a)) * u, wd_ref[...])
        row = lax.broadcasted_iota(I32, y.shape, 0)
        o_ref[...] += jnp.where(row >= lo, jnp.where(row < hi, y, 0.0), 0.0)


def _experts(xs, segs, wg, wu, wd, tm):
    n, D = xs.shape
    F = wg.shape[2]
    tile_of = lambda s, tile, exp, lo, hi, first: (tile[s], 0)
    w_of = lambda s, tile, exp, lo, hi, first: (exp[s], 0, 0)
    return pl.pallas_call(
        _experts_kernel,
        out_shape=jax.ShapeDtypeStruct((n, D), F32),
        grid_spec=pltpu.PrefetchScalarGridSpec(
            num_scalar_prefetch=5,
            grid=(segs[0].shape[0],),
            in_specs=[
                pl.BlockSpec((tm, D), tile_of),
                pl.BlockSpec((None, D, F), w_of),
                pl.BlockSpec((None, D, F), w_of),
                pl.BlockSpec((None, F, D), w_of),
            ],
            out_specs=pl.BlockSpec((tm, D), tile_of),
        ),
        compiler_params=_params(("arbitrary",)),
        name="experts",
    )(*segs, xs, wg, wu, wd)


def _moe(h2, route, x1, wg, wu, wd):
    T, D = x1.shape
    tm = min(MOE_TM, T)
    pos, segs = _route_plan(route, tm)
    xs = _row_scatter(h2, pos, 2, tm, "moe_scatter")
    ys = _experts(xs, segs, wg, wu, wd, tm)
    return _gather_combine(x1, ys, pos, route, tm)


def _pad_cols(w, n):
    return jnp.pad(w, ((0, 0), (0, n - w.shape[1])))


def _layer(x2, B, S, p):
    T, D = x2.shape
    assert D == 2 * GLA_HEADS * GLA_DK == GLA_HEADS * GLA_DV == DSA_HEADS * DSA_HEAD_DIM
    assert S % 512 == 0
    cd = MXU_DTYPE
    qk_w, v_w, dsa_w = GLA_HEADS * GLA_DK, GLA_HEADS * GLA_DV, DSA_HEADS * DSA_HEAD_DIM
    splits = (qk_w, qk_w, v_w, GLA_GATE_RANK, v_w, dsa_w, dsa_w, dsa_w, IDX_Q_W, IDX_DIM, IDX_HEADS, D, D)
    offs = [0]
    for s in splits:
        offs.append(offs[-1] + s)
    w_in = p["w_in"]
    (w_gq, w_gk, w_gv, w_ga, w_gr, w_dq, w_dk, w_dv, w_iq, w_ik, w_iw, w_ta, w_tb) = [
        w_in[:, offs[i]:offs[i + 1]] for i in range(len(splits))]

    w_main = jnp.concatenate([w_gq, w_gk, w_gv, w_gr, w_ta, w_tb], axis=1).astype(cd)
    w_qk = jnp.concatenate([w_dq, w_dk], axis=1).astype(cd)
    g_qk = jnp.concatenate([jnp.tile(p["dsa_q_norm_g"] * (DSA_HEAD_DIM ** -0.5 * math.log2(math.e)), DSA_HEADS),
                            jnp.tile(p["dsa_k_norm_g"], DSA_HEADS)]).reshape(1, 2 * dsa_w).astype(F32)
    w_vt = w_dv.T.astype(cd)
    w_idx = jnp.concatenate([_pad_cols(w_ik, LANES), _pad_cols(w_ga, LANES)], axis=1).astype(cd)
    w_idxt = jnp.pad(jnp.concatenate([w_iq, w_iw], axis=1).T, ((0, 16 - IDX_HEADS), (0, 0))).astype(cd)

    tm = min(1024, T)
    h = _rmsnorm(x2, p["norm1_g"], tm)
    zm = _matmul(h, w_main, tm, 1024, cd)
    qk = _qk_proj(h, w_qk, g_qk, tm)
    vt = _vt_proj(h, w_vt, B, S, min(512, S))
    iqt, ki, ga, wit = _idx_proj(h, w_idx, w_idxt, p["idx_k_ln_g"].reshape(1, -1), p["idx_k_ln_b"].reshape(1, -1),
                                 min(512, T))

    o_gla = _gla(zm, ga, p["gla_w_a2"], p["gla_b_a"].reshape(1, -1), p["gla_norm_g"].reshape(1, -1), B, S, 512)

    QB, KB = 256, 512
    mask = _dsa_select(iqt, wit, ki, B, S, QB, KB)
    o_dsa = _dsa_attn(qk, vt, mask, B, S, QB, KB)

    w_r = _pad_cols(jnp.concatenate([p["w_router_expert"], p["w_router_group"]], axis=1), LANES)
    b_r = _pad_cols(jnp.concatenate([p["b_router_expert"], p["b_router_group"]]).reshape(1, -1), LANES)
    x1, h2, route = _post(o_gla, o_dsa, zm, x2, p["w_branch_gla"].astype(cd), p["w_branch_dsa"].astype(cd),
                          p["w_out"].astype(cd), p["norm2_g"].reshape(1, -1), w_r, b_r, min(512, T))
    return _moe(h2, route, x1, p["w_exp_gate"].astype(cd), p["w_exp_up"].astype(cd), p["w_exp_down"].astype(cd))


def kernel(x, norm1_g, w_in, gla_w_a2, gla_b_a, gla_norm_g, dsa_q_norm_g, dsa_k_norm_g, idx_k_ln_g, idx_k_ln_b,
           w_branch_gla, w_branch_dsa, w_out, norm2_g, w_router_group, b_router_group, w_router_expert,
           b_router_expert, w_exp_gate, w_exp_up, w_exp_down):
    B, S, D = x.shape
    stacked = dict(norm1_g=norm1_g, w_in=w_in, gla_w_a2=gla_w_a2, gla_b_a=gla_b_a, gla_norm_g=gla_norm_g,
                   dsa_q_norm_g=dsa_q_norm_g, dsa_k_norm_g=dsa_k_norm_g, idx_k_ln_g=idx_k_ln_g,
                   idx_k_ln_b=idx_k_ln_b, w_branch_gla=w_branch_gla, w_branch_dsa=w_branch_dsa, w_out=w_out,
                   norm2_g=norm2_g, w_router_group=w_router_group, b_router_group=b_router_group,
                   w_router_expert=w_router_expert, b_router_expert=b_router_expert, w_exp_gate=w_exp_gate,
                   w_exp_up=w_exp_up, w_exp_down=w_exp_down)
    x2 = x.reshape(B * S, D).astype(F32)
    for l in range(w_in.shape[0]):
        x2 = _layer(x2, B, S, {k: v[l] for k, v in stacked.items()})
    return x2.reshape(B, S, D).astype(x.dtype)
```

```python
import functools
import math

import jax
import jax.numpy as jnp
from jax import lax
from jax.experimental import pallas as pl
from jax.experimental.pallas import tpu as pltpu

F32 = jnp.float32
BF16 = jnp.bfloat16
I32 = jnp.int32
MXU_DTYPE = BF16

CHUNK = 64
EPS = 1e-6
GLA_HEADS, GLA_DK, GLA_DV = 4, 128, 256
GLA_GATE_RANK = 16
GLA_GATE_TEMP = 16.0
DSA_HEADS, DSA_HEAD_DIM = 8, 128
IDX_HEADS, IDX_DIM = 8, 64
TOPK_MAX = 256
N_GROUPS, EXPERTS_PER_GROUP = 4, 8
N_EXPERTS = N_GROUPS * EXPERTS_PER_GROUP
LANES = 128

INT_MIN = -(2 ** 31)
INT_MAX = 2 ** 31 - 1
I16 = jnp.int16
I16_MIN, I16_MAX = -(2 ** 15), 2 ** 15 - 1

VMEM_LIMIT = 56 * 1024 * 1024

_NT = (((1,), (1,)), ((), ()))
_TN = (((0,), (0,)), ((), ()))


def _params(sem):
    return pltpu.CompilerParams(dimension_semantics=sem, vmem_limit_bytes=VMEM_LIMIT)


def _mm(a, b):
    return jnp.dot(a.astype(MXU_DTYPE), b.astype(MXU_DTYPE), preferred_element_type=F32)


def _mm_nt(a, b):
    return lax.dot_general(a.astype(MXU_DTYPE), b.astype(MXU_DTYPE), _NT, preferred_element_type=F32)


def _mm_tn(a, b):
    return lax.dot_general(a.astype(MXU_DTYPE), b.astype(MXU_DTYPE), _TN, preferred_element_type=F32)


def _split(a):
    hi = a.astype(BF16)
    lo = (a - hi.astype(F32)).astype(BF16)
    return hi, lo


def _dot3(a, b):
    ah, al = _split(a)
    bh, bl = _split(b)
    d = lambda u, v: jnp.dot(u, v, preferred_element_type=F32)
    return d(ah, bh) + (d(ah, bl) + d(al, bh))


def _rmsnorm_kernel(x_ref, g_ref, o_ref):
    x = x_ref[...]
    ms = jnp.mean(x * x, axis=-1, keepdims=True)
    o_ref[...] = (x * lax.rsqrt(ms + EPS) * g_ref[...]).astype(o_ref.dtype)


def _rmsnorm(x2, g, tm):
    T, D = x2.shape
    return pl.pallas_call(
        _rmsnorm_kernel,
        out_shape=jax.ShapeDtypeStruct((T, D), MXU_DTYPE),
        grid=(T // tm,),
        in_specs=[pl.BlockSpec((tm, D), lambda i: (i, 0)), pl.BlockSpec((1, D), lambda i: (0, 0))],
        out_specs=pl.BlockSpec((tm, D), lambda i: (i, 0)),
        compiler_params=_params(("parallel",)),
        name="rmsnorm",
    )(x2, g.reshape(1, D))


def _matmul_kernel(h_ref, w_ref, o_ref):
    o_ref[...] = jnp.dot(h_ref[...], w_ref[...], preferred_element_type=F32).astype(o_ref.dtype)


def _matmul(h, w, tm, tn, out_dtype):
    T, D = h.shape
    N = w.shape[1]
    return pl.pallas_call(
        _matmul_kernel,
        out_shape=jax.ShapeDtypeStruct((T, N), out_dtype),
        grid=(T // tm, N // tn),
        in_specs=[pl.BlockSpec((tm, D), lambda i, j: (i, 0)), pl.BlockSpec((D, tn), lambda i, j: (0, j))],
        out_specs=pl.BlockSpec((tm, tn), lambda i, j: (i, j)),
        compiler_params=_params(("parallel", "arbitrary")),
        name="proj_main",
    )(h, w)


def _qk_kernel(h_ref, w_ref, g_ref, o_ref):
    z = jnp.dot(h_ref[...], w_ref[...], preferred_element_type=F32)
    for hh in range(DSA_HEADS):
        sl = slice(hh * DSA_HEAD_DIM, (hh + 1) * DSA_HEAD_DIM)
        zh = z[:, sl]
        ms = jnp.mean(zh * zh, axis=-1, keepdims=True)
        o_ref[:, sl] = (zh * lax.rsqrt(ms + EPS) * g_ref[:, sl]).astype(o_ref.dtype)


def _qk_proj(h, w, g, tm):
    T, D = h.shape
    W = DSA_HEADS * DSA_HEAD_DIM
    return pl.pallas_call(
        _qk_kernel,
        out_shape=jax.ShapeDtypeStruct((T, 2 * W), MXU_DTYPE),
        grid=(T // tm, 2),
        in_specs=[
            pl.BlockSpec((tm, D), lambda i, j: (i, 0)),
            pl.BlockSpec((D, W), lambda i, j: (0, j)),
            pl.BlockSpec((1, W), lambda i, j: (0, j)),
        ],
        out_specs=pl.BlockSpec((tm, W), lambda i, j: (i, j)),
        compiler_params=_params(("parallel", "arbitrary")),
        name="proj_qk",
    )(h, w, g)


VT_ONES = 16
VT_ROWS = DSA_HEAD_DIM + VT_ONES


def _vt_kernel(h_ref, wt_ref, o_ref):
    vt = lax.dot_general(wt_ref[...], h_ref[...], _NT, preferred_element_type=F32).astype(o_ref.dtype)
    ones = jnp.ones((VT_ONES, vt.shape[1]), o_ref.dtype)
    for hh in range(DSA_HEADS):
        o_ref[hh, :DSA_HEAD_DIM, :] = vt[hh * DSA_HEAD_DIM:(hh + 1) * DSA_HEAD_DIM]
        o_ref[hh, DSA_HEAD_DIM:, :] = ones


def _vt_proj(h, wt, B, S, tm):
    T, D = h.shape
    W = wt.shape[0]
    nt = S // tm
    return pl.pallas_call(
        _vt_kernel,
        out_shape=jax.ShapeDtypeStruct((B, DSA_HEADS, VT_ROWS, S), MXU_DTYPE),
        grid=(T // tm,),
        in_specs=[pl.BlockSpec((tm, D), lambda i: (i, 0)), pl.BlockSpec((W, D), lambda i: (0, 0))],
        out_specs=pl.BlockSpec((None, DSA_HEADS, VT_ROWS, tm), lambda i: (i // nt, 0, 0, i % nt)),
        compiler_params=_params(("parallel",)),
        name="proj_vt",
    )(h, wt)


IDX_Q_W = IDX_HEADS * IDX_DIM


def _idx_kernel(h_ref, w_ref, wt_ref, lng_ref, lnb_ref, iqt_ref, ki_ref, ga_ref, wit_ref):
    h = h_ref[...]
    z = jnp.dot(h, w_ref[...], preferred_element_type=F32)
    ik = z[:, :IDX_DIM]
    mu = jnp.mean(ik, axis=-1, keepdims=True)
    var = jnp.mean(jnp.square(ik - mu), axis=-1, keepdims=True)
    ki = (ik - mu) * lax.rsqrt(var + EPS) * lng_ref[...] + lnb_ref[...]
    ki_ref[...] = ki.astype(ki_ref.dtype)
    ga_ref[...] = z[:, LANES:LANES + GLA_GATE_RANK]
    zt = lax.dot_general(wt_ref[...], h, _NT, preferred_element_type=F32)
    iqt_ref[...] = zt[:IDX_Q_W].astype(iqt_ref.dtype)
    wit_ref[...] = zt[IDX_Q_W:IDX_Q_W + IDX_HEADS] * (IDX_HEADS ** -0.5 * IDX_DIM ** -0.5)


def _idx_proj(h, w, wt, lng, lnb, tm):
    T, D = h.shape
    return pl.pallas_call(
        _idx_kernel,
        out_shape=(
            jax.ShapeDtypeStruct((IDX_Q_W, T), MXU_DTYPE),
            jax.ShapeDtypeStruct((T, IDX_DIM), MXU_DTYPE),
            jax.ShapeDtypeStruct((T, GLA_GATE_RANK), F32),
            jax.ShapeDtypeStruct((IDX_HEADS, T), F32),
        ),
        grid=(T // tm,),
        in_specs=[
            pl.BlockSpec((tm, D), lambda i: (i, 0)),
            pl.BlockSpec(w.shape, lambda i: (0, 0)),
            pl.BlockSpec(wt.shape, lambda i: (0, 0)),
            pl.BlockSpec((1, IDX_DIM), lambda i: (0, 0)),
            pl.BlockSpec((1, IDX_DIM), lambda i: (0, 0)),
        ],
        out_specs=(
            pl.BlockSpec((IDX_Q_W, tm), lambda i: (0, i)),
            pl.BlockSpec((tm, IDX_DIM), lambda i: (i, 0)),
            pl.BlockSpec((tm, GLA_GATE_RANK), lambda i: (i, 0)),
            pl.BlockSpec((IDX_HEADS, tm), lambda i: (0, i)),
        ),
        compiler_params=_params(("parallel",)),
        name="proj_idx",
    )(h, w, wt, lng, lnb)


def _gla_kernel(q_ref, k_ref, v_ref, r_ref, ga_ref, wa2_ref, ba_ref, ng_ref, o_ref, st_ref, tot_ref, kd_ref,
                oraw_ref, *, n_chunks):
    tb = q_ref.shape[0]

    @pl.when(pl.program_id(1) == 0)
    def _():
        st_ref[...] = jnp.zeros_like(st_ref)

    x = _dot3(ga_ref[...], wa2_ref[...]) + ba_ref[...]
    la = (jnp.minimum(x, 0.0) - jnp.log(1.0 + jnp.exp(-jnp.abs(x)))) * (1.0 / GLA_GATE_TEMP)
    row = lax.broadcasted_iota(I32, (tb, tb), 0)
    col = lax.broadcasted_iota(I32, (tb, tb), 1)
    same = lax.shift_right_logical(row, 6) == lax.shift_right_logical(col, 6)
    ones_blk = jnp.where(same, 1.0, 0.0).astype(BF16)
    tril_blk = jnp.where(same, jnp.where(row >= col, 1.0, 0.0), 0.0).astype(BF16)
    la_hi, la_lo = _split(la)
    d = lambda u, v: jnp.dot(u, v, preferred_element_type=F32)
    tot = d(ones_blk, la_hi) + d(ones_blk, la_lo)
    cum = d(tril_blk, la_hi) + d(tril_blk, la_lo)
    tot_ref[...] = tot
    kd_ref[...] = (k_ref[...].astype(F32) * jnp.exp(tot - cum)).astype(kd_ref.dtype)

    heads = range(GLA_HEADS)
    ksl = lambda hh: slice(hh * GLA_DK, (hh + 1) * GLA_DK)
    vsl = lambda hh: slice(hh * GLA_DV, (hh + 1) * GLA_DV)

    def chunk(c, carry):
        rows = pl.ds(pl.multiple_of(c * CHUNK, CHUNK), CHUNK)
        first = pl.ds(pl.multiple_of(c * CHUNK, CHUNK), 1)
        upd = [_mm_tn(v_ref[rows, vsl(hh)], kd_ref[rows, ksl(hh)]) for hh in heads]
        st = [st_ref[hh] * jnp.exp(tot_ref[first, ksl(hh)]) + upd[hh] for hh in heads]
        for hh in heads:
            st_ref[hh] = st[hh]
        for hh in heads:
            oraw_ref[rows, vsl(hh)] = _mm_nt(q_ref[rows, ksl(hh)], st[hh])
        return carry

    lax.fori_loop(0, n_chunks, chunk, 0)

    ng = ng_ref[...]
    for hh in heads:
        o = oraw_ref[:, vsl(hh)] * (GLA_DK ** -0.5)
        ms = jnp.mean(o * o, axis=-1, keepdims=True)
        r = r_ref[:, vsl(hh)].astype(F32)
        o_ref[:, vsl(hh)] = (o * lax.rsqrt(ms + EPS) * ng * (r / (1.0 + jnp.exp(-r)))).astype(o_ref.dtype)


def _gla(zm, ga, wa2, ba, ng, B, S, tb):
    T = B * S
    nb = S // tb
    qk_w = GLA_HEADS * GLA_DK
    v_w = GLA_HEADS * GLA_DV
    tok = lambda b, i: b * nb + i
    return pl.pallas_call(
        functools.partial(_gla_kernel, n_chunks=tb // CHUNK),
        out_shape=jax.ShapeDtypeStruct((T, v_w), MXU_DTYPE),
        grid=(B, nb),
        in_specs=[
            pl.BlockSpec((tb, qk_w), lambda b, i: (tok(b, i), 0)),
            pl.BlockSpec((tb, qk_w), lambda b, i: (tok(b, i), 1)),
            pl.BlockSpec((tb, v_w), lambda b, i: (tok(b, i), 1)),
            pl.BlockSpec((tb, v_w), lambda b, i: (tok(b, i), 2)),
            pl.BlockSpec((tb, GLA_GATE_RANK), lambda b, i: (tok(b, i), 0)),
            pl.BlockSpec((GLA_GATE_RANK, qk_w), lambda b, i: (0, 0)),
            pl.BlockSpec((1, qk_w), lambda b, i: (0, 0)),
            pl.BlockSpec((1, GLA_DV), lambda b, i: (0, 0)),
        ],
        out_specs=pl.BlockSpec((tb, v_w), lambda b, i: (tok(b, i), 0)),
        scratch_shapes=[pltpu.VMEM((GLA_HEADS, GLA_DV, GLA_DK), F32), pltpu.VMEM((tb, qk_w), F32),
                        pltpu.VMEM((tb, qk_w), MXU_DTYPE), pltpu.VMEM((tb, v_w), F32)],
        compiler_params=_params(("parallel", "arbitrary")),
        name="gla",
    )(zm, zm, zm, zm, ga, wa2, ba, ng)


def _last_kb(qb, QB, KB):
    return ((qb + 1) * QB - 1) // KB


def _causal_steps(S, QB, KB):
    pairs = [(q, k) for q in range(S // QB) for k in range(_last_kb(q, QB, KB) + 1)]
    qs, ks = zip(*pairs)
    return jnp.asarray(qs, I32), jnp.asarray(ks, I32)


def _select_kernel(qb_ref, kb_ref, iqt_ref, wit_ref, ki_ref, mask_ref, hi_ref, lo_ref, l2_ref, gm_ref, *,
                   QB, KB, S, topk):
    qb = qb_ref[pl.program_id(1)]
    kb = kb_ref[pl.program_id(1)]
    last = _last_kb(qb, QB, KB)

    def rows_of(i):
        return pl.ds(pl.multiple_of(i * KB, KB), KB)

    @pl.when(kb == 0)
    def _():
        gm_ref[...] = jnp.full(gm_ref.shape, INT_MIN, I32)

    rc = 128
    assert topk % rc == 0 and KB % rc == 0
    t_chunk = lax.shift_right_logical(qb * QB + lax.broadcasted_iota(I32, (rc, QB), 1), 6)
    for c in range(KB // rc):
        ki = ki_ref[c * rc:(c + 1) * rc, :]
        sc = jnp.zeros((rc, QB), F32)
        for hh in range(IDX_HEADS):
            lg = _mm(ki, iqt_ref[hh * IDX_DIM:(hh + 1) * IDX_DIM, :])
            sc = sc + jnp.maximum(lg, 0.0) * wit_ref[hh:hh + 1, :]
        s_chunk = lax.shift_right_logical(kb * KB + c * rc + lax.broadcasted_iota(I32, (rc, QB), 0), 6)
        bits = lax.bitcast_convert_type(sc, I32)
        sign = lax.shift_right_arithmetic(bits, 31)
        key = (bits ^ (sign & INT_MAX)) - sign
        key = jnp.where(s_chunk <= t_chunk, key, INT_MIN)
        rows = pl.ds(pl.multiple_of(kb * KB + c * rc, rc), rc)
        hi_ref[rows, :] = lax.shift_right_arithmetic(key, 16).astype(I16)
        lo_ref[rows, :] = ((key & 0xFFFF) + I16_MIN).astype(I16)
        g0 = (c * rc) % topk
        gm_ref[g0:g0 + rc, :] = jnp.maximum(gm_ref[g0:g0 + rc, :], key)

    @pl.when(kb == last)
    def _():
        nblk = last + 1
        slab = 32
        one, zero = jnp.int16(1), jnp.int16(0)

        def fold(m, reduce):
            part = m[0:slab]
            for r in range(1, KB // slab):
                part = reduce(part, m[r * slab:(r + 1) * slab])
            return part

        def count(pred):
            def body(i, acc):
                return acc + fold(pred(i), jnp.add)
            acc = lax.fori_loop(0, nblk, body, jnp.zeros((slab, QB), I16))
            return jnp.sum(acc.astype(I32), axis=0, keepdims=True)

        def count_ge(ref, thr):
            t16 = thr.astype(I16)
            return count(lambda i: jnp.where(ref[rows_of(i), :] >= t16, one, zero))

        def search(ref, target, lo, c_lo, hi, c_hi):
            def is_open(lo, c_lo, hi):
                return jnp.where(c_lo > target, jnp.where(hi != lo + 1, 1, 0), 0)

            def step(_, st):
                lo, c_lo, hi, c_hi = st
                open_ = is_open(lo, c_lo, hi) > 0
                mid = lax.shift_right_arithmetic(lo + hi, 1)
                c = count_ge(ref, mid)
                up = jnp.logical_and(open_, c >= target)
                dn = jnp.logical_and(open_, c < target)
                return jnp.where(up, mid, lo), jnp.where(up, c, c_lo), jnp.where(dn, mid, hi), jnp.where(dn, c, c_hi)

            width = jnp.where(is_open(lo, c_lo, hi) > 0, hi - lo, 1)
            n_halvings = jnp.max(32 - lax.clz(width - 1))
            lo, c_lo, hi, c_hi = lax.fori_loop(0, n_halvings, step, (lo, c_lo, hi, c_hi))
            return lo, c_lo, c_hi

        zeros = jnp.zeros((1, QB), I32)
        gm = gm_ref[...]
        g_lo = jnp.maximum(lax.shift_right_arithmetic(jnp.min(gm, axis=0, keepdims=True), 16), I16_MIN + 1)
        g_hi = lax.shift_right_arithmetic(jnp.max(gm, axis=0, keepdims=True), 16) + 1
        hstar, ch_ge, ch_gt = search(hi_ref, topk, g_lo, count_ge(hi_ref, g_lo), g_hi, zeros)
        split = ch_ge > topk
        h16 = hstar.astype(I16)

        def low_half():
            def build(i, carry):
                l2_ref[rows_of(i), :] = jnp.where(hi_ref[rows_of(i), :] == h16, lo_ref[rows_of(i), :],
                                                  jnp.int16(I16_MIN))
                return carry
            lax.fori_loop(0, nblk, build, 0)
            target = jnp.where(split, topk - ch_gt, INT_MAX)
            return search(l2_ref, target, jnp.full((1, QB), I16_MIN, I32), ch_ge - ch_gt,
                          jnp.full((1, QB), I16_MAX + 1, I32), zeros)

        any_split = jnp.max(jnp.where(split, 1, 0)) > 0
        lstar, cl_ge, cl_gt = lax.cond(any_split, low_half,
                                       lambda: (jnp.full((1, QB), I16_MIN, I32), zeros, zeros))
        lstar = jnp.where(split, lstar, I16_MIN)
        l16 = lstar.astype(I16)
        c_ge = jnp.where(split, ch_gt + cl_ge, ch_ge)
        c_gt = jnp.where(split, ch_gt + cl_gt, ch_gt)
        excess = c_ge > topk
        need = jnp.where(excess, topk - c_gt, S + 1)
        row16 = lax.broadcasted_iota(I32, (KB, QB), 0).astype(I16)

        def before(i, j):
            return jnp.where(row16 + (i * KB).astype(I16) < j.astype(I16), one, zero)

        def tied(i, then):
            return jnp.where(hi_ref[rows_of(i), :] == h16, jnp.where(lo_ref[rows_of(i), :] == l16, then, zero), zero)

        def tie_cut():
            def step(_, c):
                lo, hi = c
                mid = lax.shift_right_arithmetic(lo + hi, 1)
                ok = count(lambda i: tied(i, before(i, mid))) >= need
                return jnp.where(ok, lo, mid + 1), jnp.where(ok, mid, hi)
            n_steps = max(1, math.ceil(math.log2(S + 1)))
            lo, _hi = lax.fori_loop(0, n_steps, step, (zeros, jnp.full((1, QB), S, I32)))
            return lo

        any_excess = jnp.max(jnp.where(excess, 1, 0)) > 0
        jcut = lax.cond(any_excess, tie_cut, lambda: jnp.full((1, QB), S, I32))

        def write(i, carry):
            hi_t, lo_t = hi_ref[rows_of(i), :], lo_ref[rows_of(i), :]
            in_bucket = jnp.where(lo_t > l16, one, jnp.where(lo_t == l16, before(i, jcut), zero))
            sel = jnp.where(hi_t > h16, one, jnp.where(hi_t == h16, in_bucket, zero))
            mask_ref[rows_of(i), :] = sel.astype(mask_ref.dtype)
            return carry

        lax.fori_loop(0, nblk, write, 0)

        def clear(i, carry):
            mask_ref[rows_of(i), :] = jnp.zeros((KB, QB), mask_ref.dtype)
            return carry

        lax.fori_loop(nblk, S // KB, clear, 0)


def _dsa_select(iqt, wit, ki, B, S, QB, KB):
    nq, nk = S // QB, S // KB
    topk = min(TOPK_MAX, S // 4)
    assert KB % topk == 0
    qs, ks = _causal_steps(S, QB, KB)
    return pl.pallas_call(
        functools.partial(_select_kernel, QB=QB, KB=KB, S=S, topk=topk),
        out_shape=jax.ShapeDtypeStruct((B, S, S), jnp.int8),
        grid_spec=pltpu.PrefetchScalarGridSpec(
            num_scalar_prefetch=2,
            grid=(B, qs.shape[0]),
            in_specs=[
                pl.BlockSpec((IDX_Q_W, QB), lambda b, s, qs, ks: (0, b * nq + qs[s])),
                pl.BlockSpec((IDX_HEADS, QB), lambda b, s, qs, ks: (0, b * nq + qs[s])),
                pl.BlockSpec((KB, IDX_DIM), lambda b, s, qs, ks: (b * nk + ks[s], 0)),
            ],
            out_specs=pl.BlockSpec((None, S, QB), lambda b, s, qs, ks: (b, 0, qs[s])),
            scratch_shapes=[pltpu.VMEM((S, QB), I16), pltpu.VMEM((S, QB), I16), pltpu.VMEM((S, QB), I16),
                            pltpu.VMEM((topk, QB), I32)],
        ),
        compiler_params=_params(("parallel", "arbitrary")),
        name="dsa_select",
    )(qs, ks, iqt, wit, ki)


def _attn_kernel(qb_ref, kb_ref, q_ref, k_ref, vt_ref, m_ref, o_ref, acc_ref, *, QB, KB):
    qb = qb_ref[pl.program_id(1)]
    kb = kb_ref[pl.program_id(1)]
    last = _last_kb(qb, QB, KB)

    @pl.when(kb == 0)
    def _():
        acc_ref[...] = jnp.zeros_like(acc_ref)

    mb = m_ref[...].astype(MXU_DTYPE)
    head = lambda hh: slice(hh * DSA_HEAD_DIM, (hh + 1) * DSA_HEAD_DIM)
    logits = lambda hh: _mm_nt(k_ref[:, head(hh)], q_ref[:, head(hh)])
    lg = logits(0)
    for hh in range(DSA_HEADS):
        lg_next = logits(hh + 1) if hh + 1 < DSA_HEADS else None
        p = jnp.exp2(lg).astype(MXU_DTYPE) * mb
        acc_ref[hh] += jnp.dot(vt_ref[hh], p, preferred_element_type=F32)
        lg = lg_next

    @pl.when(kb == last)
    def _():
        for hh in range(DSA_HEADS):
            acc = acc_ref[hh]
            o = acc[:DSA_HEAD_DIM] / acc[DSA_HEAD_DIM:DSA_HEAD_DIM + 1]
            o_ref[:, hh * DSA_HEAD_DIM:(hh + 1) * DSA_HEAD_DIM] = o.T.astype(o_ref.dtype)


def _dsa_attn(qk, vt, mask, B, S, QB, KB):
    T = B * S
    nq, nk = S // QB, S // KB
    W = DSA_HEADS * DSA_HEAD_DIM
    qs, ks = _causal_steps(S, QB, KB)
    return pl.pallas_call(
        functools.partial(_attn_kernel, QB=QB, KB=KB),
        out_shape=jax.ShapeDtypeStruct((T, W), MXU_DTYPE),
        grid_spec=pltpu.PrefetchScalarGridSpec(
            num_scalar_prefetch=2,
            grid=(B, qs.shape[0]),
            in_specs=[
                pl.BlockSpec((QB, W), lambda b, s, qs, ks: (b * nq + qs[s], 0)),
                pl.BlockSpec((KB, W), lambda b, s, qs, ks: (b * nk + ks[s], 1)),
                pl.BlockSpec((None, DSA_HEADS, VT_ROWS, KB), lambda b, s, qs, ks: (b, 0, 0, ks[s])),
                pl.BlockSpec((None, KB, QB), lambda b, s, qs, ks: (b, ks[s], qs[s])),
            ],
            out_specs=pl.BlockSpec((QB, W), lambda b, s, qs, ks: (b * nq + qs[s], 0)),
            scratch_shapes=[pltpu.VMEM((DSA_HEADS, VT_ROWS, QB), F32)],
        ),
        compiler_params=_params(("parallel", "arbitrary")),
        name="dsa_attn",
    )(qs, ks, qk, qk, vt, mask)


def _post_kernel(og_ref, od_ref, ga_ref, gb_ref, x_ref, pa_ref, pb_ref, wo_ref, g2_ref, wr_ref, br_ref,
                 x1_ref, h2_ref, route_ref):
    sig = lambda v: 1.0 / (1.0 + jnp.exp(-v))
    a = jnp.dot(og_ref[...], pa_ref[...], preferred_element_type=F32)
    b = jnp.dot(od_ref[...], pb_ref[...], preferred_element_type=F32)
    mix = sig(ga_ref[...].astype(F32)) * a + sig(gb_ref[...].astype(F32)) * b
    x1 = x_ref[...] + _mm(mix, wo_ref[...])
    x1_ref[...] = x1
    ms = jnp.mean(x1 * x1, axis=-1, keepdims=True)
    h2 = x1 * lax.rsqrt(ms + EPS) * g2_ref[...]
    h2_ref[...] = h2.astype(h2_ref.dtype)

    logits = _dot3(h2, wr_ref[...]) + br_ref[...]
    tm = logits.shape[0]
    lane = lax.broadcasted_iota(I32, (tm, LANES), 1).astype(F32)
    neg = -jnp.inf
    far = float(2 * LANES)
    rmax = lambda v: jnp.max(v, axis=1, keepdims=True)
    rmin = lambda v: jnp.min(v, axis=1, keepdims=True)
    gl = jnp.where(lane >= N_EXPERTS, jnp.where(lane < N_EXPERTS + N_GROUPS, logits, neg), neg)
    gmax = rmax(gl)
    g_w = 1.0 / jnp.sum(jnp.exp(gl - gmax), axis=1, keepdims=True)
    g_idx = rmin(jnp.where(gl == gmax, lane, far)) - N_EXPERTS
    e_lo = g_idx * EXPERTS_PER_GROUP
    el = jnp.where(lane >= e_lo, jnp.where(lane < e_lo + EXPERTS_PER_GROUP, logits, neg), neg)
    m1 = rmax(el)
    e1 = rmin(jnp.where(el == m1, lane, far))
    el2 = jnp.where(lane == e1, neg, el)
    m2 = rmax(el2)
    e2 = rmin(jnp.where(el2 == m2, lane, far))
    p2 = jnp.exp(m2 - m1)
    w1 = g_w / (1.0 + p2)
    route_ref[...] = jnp.where(lane == 0.0, e1, jnp.where(lane == 1.0, e2, jnp.where(
        lane == 2.0, w1, jnp.where(lane == 3.0, w1 * p2, 0.0))))


def _post(og, od, zm, x2, pa, pb, wo, g2, wr, br, tm):
    T, D = x2.shape
    row = lambda i: (i, 0)
    full = lambda i: (0, 0)
    return pl.pallas_call(
        _post_kernel,
        out_shape=(
            jax.ShapeDtypeStruct((T, D), F32),
            jax.ShapeDtypeStruct((T, D), F32),
            jax.ShapeDtypeStruct((T, LANES), F32),
        ),
        grid=(T // tm,),
        in_specs=[
            pl.BlockSpec((tm, D), row),
            pl.BlockSpec((tm, D), row),
            pl.BlockSpec((tm, D), lambda i: (i, 3)),
            pl.BlockSpec((tm, D), lambda i: (i, 4)),
            pl.BlockSpec((tm, D), row),
            pl.BlockSpec((D, D), full),
            pl.BlockSpec((D, D), full),
            pl.BlockSpec((D, D), full),
            pl.BlockSpec((1, D), full),
            pl.BlockSpec((D, LANES), full),
            pl.BlockSpec((1, LANES), full),
        ],
        out_specs=(pl.BlockSpec((tm, D), row), pl.BlockSpec((tm, D), row), pl.BlockSpec((tm, LANES), row)),
        compiler_params=_params(("parallel",)),
        name="merge_router",
    )(og, od, zm, zm, x2, pa, pb, wo, g2, wr, br)


MOE_TM = 256


def _gather_combine_kernel(idx_ref, nxt_ref, x1_ref, route_ref, src_hbm, o_ref, buf_ref, sem):
    i = pl.program_id(0)
    n = pl.num_programs(0)
    tm = o_ref.shape[0]
    slot = lax.rem(i, 2)

    def fetch(ids_ref, s):
        def start(r, carry):
            for k in range(2):
                pltpu.make_async_copy(src_hbm.at[pl.ds(ids_ref[0, 0, 2 * r + k], 1), :],
                                      buf_ref.at[s, pl.ds(k * tm + r, 1), :], sem.at[s]).start(priority=k)
            return carry
        lax.fori_loop(0, tm, start, 0, unroll=8)

    @pl.when(i == 0)
    def _():
        fetch(idx_ref, 0)

    @pl.when(i + 1 < n)
    def _():
        fetch(nxt_ref, 1 - slot)

    pltpu.make_async_copy(src_hbm.at[pl.ds(0, 2 * tm), :], buf_ref.at[slot], sem.at[slot]).wait()
    r = route_ref[...]
    lane = lax.broadcasted_iota(I32, r.shape, 1)
    w1 = jnp.sum(jnp.where(lane == 2, r, 0.0), axis=1, keepdims=True)
    w2 = jnp.sum(jnp.where(lane == 3, r, 0.0), axis=1, keepdims=True)
    o_ref[...] = x1_ref[...] + (w1 * buf_ref[slot, :tm, :] + w2 * buf_ref[slot, tm:, :])


def _gather_combine(x1, ys, pos, route, tm):
    T, D = x1.shape
    nt = T // tm
    idx3 = pos.reshape(nt, 1, 2 * tm)
    row = lambda i: (i, 0)
    return pl.pallas_call(
        _gather_combine_kernel,
        out_shape=jax.ShapeDtypeStruct((T, D), F32),
        grid=(nt,),
        in_specs=[
            pl.BlockSpec((1, 1, 2 * tm), lambda i: (i, 0, 0), memory_space=pltpu.SMEM),
            pl.BlockSpec((1, 1, 2 * tm), lambda i: (jnp.minimum(i + 1, nt - 1), 0, 0), memory_space=pltpu.SMEM),
            pl.BlockSpec((tm, D), row),
            pl.BlockSpec((tm, LANES), row),
            pl.BlockSpec(memory_space=pl.ANY),
        ],
        out_specs=pl.BlockSpec((tm, D), row),
        scratch_shapes=[pltpu.VMEM((2, 2 * tm, D), ys.dtype), pltpu.SemaphoreType.DMA((2,))],
        compiler_params=_params(("arbitrary",)),
        name="moe_gather_combine",
    )(idx3, idx3, x1, route, ys)


def _scatter_kernel(idx_ref, x_ref, o_hbm, sem, *, fanout):
    tm = x_ref.shape[0]

    def start(r, carry):
        for k in range(fanout):
            dst = idx_ref[0, 0, fanout * r + k]
            pltpu.make_async_copy(x_ref.at[pl.ds(r, 1), :], o_hbm.at[pl.ds(dst, 1), :], sem).start(priority=k)
        return carry

    lax.fori_loop(0, tm, start, 0, unroll=8)
    for k in range(fanout):
        pltpu.make_async_copy(x_ref, o_hbm.at[pl.ds(0, tm), :], sem).wait()


def _row_scatter(x, idx, fanout, tm, name):
    n, D = x.shape
    return pl.pallas_call(
        functools.partial(_scatter_kernel, fanout=fanout),
        out_shape=jax.ShapeDtypeStruct((fanout * n, D), x.dtype),
        grid=(n // tm,),
        in_specs=[
            pl.BlockSpec((1, 1, fanout * tm), lambda i: (i, 0, 0), memory_space=pltpu.SMEM),
            pl.BlockSpec((tm, D), lambda i: (i, 0)),
        ],
        out_specs=pl.BlockSpec(memory_space=pl.ANY),
        scratch_shapes=[pltpu.SemaphoreType.DMA(())],
        compiler_params=_params(("arbitrary",)),
        name=name,
    )(idx.reshape(n // tm, 1, fanout * tm), x)


def _route_plan(route, tm):
    n = 2 * route.shape[0]
    n_tiles = n // tm
    ids = route[:, :2].astype(I32).reshape(n)
    onehot = (ids[:, None] == jnp.arange(N_EXPERTS, dtype=I32)[None, :]).astype(F32).reshape(n_tiles, tm, N_EXPERTS)
    tril = (jnp.arange(tm)[:, None] >= jnp.arange(tm)[None, :]).astype(F32)
    in_tile = jnp.einsum("rc,tce->tre", tril, onehot)
    per_tile = in_tile[:, -1, :]
    before = jnp.cumsum(per_tile, axis=0) - per_tile
    counts = jnp.sum(per_tile, axis=0).astype(I32)
    starts = (jnp.cumsum(counts) - counts).astype(I32)
    rank = jnp.sum(onehot * (in_tile + before[:, None, :] - 1.0), axis=2).reshape(n)
    pos = (jnp.sum(onehot.reshape(n, N_EXPERTS) * starts[None, :].astype(F32), axis=1) + rank).astype(I32)
    bounds = jnp.sort(jnp.concatenate([jnp.arange(n_tiles, dtype=I32) * tm, starts]))
    ends = jnp.concatenate([bounds[1:], jnp.full((1,), n, I32)])
    seg_tile = jnp.minimum(bounds // tm, n_tiles - 1)
    seg_exp = jnp.clip(jnp.sum((starts[None, :] <= bounds[:, None]).astype(I32), axis=1) - 1, 0, N_EXPERTS - 1)
    seg_first = jnp.concatenate([jnp.ones((1,), I32), (seg_tile[1:] != seg_tile[:-1]).astype(I32)])
    return pos, (seg_tile, seg_exp, bounds - seg_tile * tm, ends - seg_tile * tm, seg_first)


def _experts_kernel(tile_ref, exp_ref, lo_ref, hi_ref, first_ref, x_ref, wg_ref, wu_ref, wd_ref, o_ref):
    s = pl.program_id(0)
    lo, hi = lo_ref[s], hi_ref[s]

    @pl.when(first_ref[s] == 1)
    def _():
        o_ref[...] = jnp.zeros_like(o_ref)

    @pl.when(hi > lo)
    def _():
        x = x_ref[...].astype(MXU_DTYPE)
        a = jnp.dot(x, wg_ref[...], preferred_element_type=F32)
        u = jnp.dot(x, wu_ref[...], preferred_element_type=F32)
        y = _mm(a / (1.0 + jnp.exp(-a)) * u, wd_ref[...])
        row = lax.broadcasted_iota(I32, y.shape, 0)
        o_ref[...] += jnp.where(row >= lo, jnp.where(row < hi, y, 0.0), 0.0)


def _experts(xs, segs, wg, wu, wd, tm):
    n, D = xs.shape
    F = wg.shape[2]
    tile_of = lambda s, tile, exp, lo, hi, first: (tile[s], 0)
    w_of = lambda s, tile, exp, lo, hi, first: (exp[s], 0, 0)
    return pl.pallas_call(
        _experts_kernel,
        out_shape=jax.ShapeDtypeStruct((n, D), F32),
        grid_spec=pltpu.PrefetchScalarGridSpec(
            num_scalar_prefetch=5,
            grid=(segs[0].shape[0],),
            in_specs=[
                pl.BlockSpec((tm, D), tile_of),
                pl.BlockSpec((None, D, F), w_of),
                pl.BlockSpec((None, D, F), w_of),
                pl.BlockSpec((None, F, D), w_of),
            ],
            out_specs=pl.BlockSpec((tm, D), tile_of),
        ),
        compiler_params=_params(("arbitrary",)),
        name="experts",
    )(*segs, xs, wg, wu, wd)


def _moe(h2, route, x1, wg, wu, wd):
    T, D = x1.shape
    tm = min(MOE_TM, T)
    pos, segs = _route_plan(route, tm)
    xs = _row_scatter(h2, pos, 2, tm, "moe_scatter")
    ys = _experts(xs, segs, wg, wu, wd, tm)
    return _gather_combine(x1, ys, pos, route, tm)


def _pad_cols(w, n):
    return jnp.pad(w, ((0, 0), (0, n - w.shape[1])))


def _layer(x2, B, S, p):
    T, D = x2.shape
    assert D == 2 * GLA_HEADS * GLA_DK == GLA_HEADS * GLA_DV == DSA_HEADS * DSA_HEAD_DIM
    assert S % 512 == 0
    cd = MXU_DTYPE
    qk_w, v_w, dsa_w = GLA_HEADS * GLA_DK, GLA_HEADS * GLA_DV, DSA_HEADS * DSA_HEAD_DIM
    splits = (qk_w, qk_w, v_w, GLA_GATE_RANK, v_w, dsa_w, dsa_w, dsa_w, IDX_Q_W, IDX_DIM, IDX_HEADS, D, D)
    offs = [0]
    for s in splits:
        offs.append(offs[-1] + s)
    w_in = p["w_in"]
    (w_gq, w_gk, w_gv, w_ga, w_gr, w_dq, w_dk, w_dv, w_iq, w_ik, w_iw, w_ta, w_tb) = [
        w_in[:, offs[i]:offs[i + 1]] for i in range(len(splits))]

    w_main = jnp.concatenate([w_gq, w_gk, w_gv, w_gr, w_ta, w_tb], axis=1).astype(cd)
    w_qk = jnp.concatenate([w_dq, w_dk], axis=1).astype(cd)
    g_qk = jnp.concatenate([jnp.tile(p["dsa_q_norm_g"] * (DSA_HEAD_DIM ** -0.5 * math.log2(math.e)), DSA_HEADS),
                            jnp.tile(p["dsa_k_norm_g"], DSA_HEADS)]).reshape(1, 2 * dsa_w).astype(F32)
    w_vt = w_dv.T.astype(cd)
    w_idx = jnp.concatenate([_pad_cols(w_ik, LANES), _pad_cols(w_ga, LANES)], axis=1).astype(cd)
    w_idxt = jnp.pad(jnp.concatenate([w_iq, w_iw], axis=1).T, ((0, 16 - IDX_HEADS), (0, 0))).astype(cd)

    tm = min(1024, T)
    h = _rmsnorm(x2, p["norm1_g"], tm)
    zm = _matmul(h, w_main, tm, 1024, cd)
    qk = _qk_proj(h, w_qk, g_qk, tm)
    vt = _vt_proj(h, w_vt, B, S, min(512, S))
    iqt, ki, ga, wit = _idx_proj(h, w_idx, w_idxt, p["idx_k_ln_g"].reshape(1, -1), p["idx_k_ln_b"].reshape(1, -1),
                                 min(512, T))

    o_gla = _gla(zm, ga, p["gla_w_a2"], p["gla_b_a"].reshape(1, -1), p["gla_norm_g"].reshape(1, -1), B, S, 512)

    QB, KB = 256, min(1024, S)
    mask = _dsa_select(iqt, wit, ki, B, S, QB, KB)
    o_dsa = _dsa_attn(qk, vt, mask, B, S, QB, KB)

    w_r = _pad_cols(jnp.concatenate([p["w_router_expert"], p["w_router_group"]], axis=1), LANES)
    b_r = _pad_cols(jnp.concatenate([p["b_router_expert"], p["b_router_group"]]).reshape(1, -1), LANES)
    x1, h2, route = _post(o_gla, o_dsa, zm, x2, p["w_branch_gla"].astype(cd), p["w_branch_dsa"].astype(cd),
                          p["w_out"].astype(cd), p["norm2_g"].reshape(1, -1), w_r, b_r, min(512, T))
    return _moe(h2, route, x1, p["w_exp_gate"].astype(cd), p["w_exp_up"].astype(cd), p["w_exp_down"].astype(cd))


def kernel(x, norm1_g, w_in, gla_w_a2, gla_b_a, gla_norm_g, dsa_q_norm_g, dsa_k_norm_g, idx_k_ln_g, idx_k_ln_b,
           w_branch_gla, w_branch_dsa, w_out, norm2_g, w_router_group, b_router_group, w_router_expert,
           b_router_expert, w_exp_gate, w_exp_up, w_exp_down):
    B, S, D = x.shape
    stacked = dict(norm1_g=norm1_g, w_in=w_in, gla_w_a2=gla_w_a2, gla_b_a=gla_b_a, gla_norm_g=gla_norm_g,
                   dsa_q_norm_g=dsa_q_norm_g, dsa_k_norm_g=dsa_k_norm_g, idx_k_ln_g=idx_k_ln_g,
                   idx_k_ln_b=idx_k_ln_b, w_branch_gla=w_branch_gla, w_branch_dsa=w_branch_dsa, w_out=w_out,
                   norm2_g=norm2_g, w_router_group=w_router_group, b_router_group=b_router_group,
                   w_router_expert=w_router_expert, b_router_expert=b_router_expert, w_exp_gate=w_exp_gate,
                   w_exp_up=w_exp_up, w_exp_down=w_exp_down)
    x2 = x.reshape(B * S, D).astype(F32)
    for l in range(w_in.shape[0]):
        x2 = _layer(x2, B, S, {k: v[l] for k, v in stacked.items()})
    return x2.reshape(B, S, D).astype(x.dtype)
```

```python
import functools
import math

import jax
import jax.numpy as jnp
from jax import lax
from jax.experimental import pallas as pl
from jax.experimental.pallas import tpu as pltpu

F32 = jnp.float32
BF16 = jnp.bfloat16
I32 = jnp.int32
MXU_DTYPE = BF16

CHUNK = 64
EPS = 1e-6
GLA_HEADS, GLA_DK, GLA_DV = 4, 128, 256
GLA_GATE_RANK = 16
GLA_GATE_TEMP = 16.0
DSA_HEADS, DSA_HEAD_DIM = 8, 128
IDX_HEADS, IDX_DIM = 8, 64
TOPK_MAX = 256
N_GROUPS, EXPERTS_PER_GROUP = 4, 8
N_EXPERTS = N_GROUPS * EXPERTS_PER_GROUP
LANES = 128

INT_MIN = -(2 ** 31)
INT_MAX = 2 ** 31 - 1
I16 = jnp.int16
I16_MIN, I16_MAX = -(2 ** 15), 2 ** 15 - 1

VMEM_LIMIT = 56 * 1024 * 1024

_NT = (((1,), (1,)), ((), ()))
_TN = (((0,), (0,)), ((), ()))


def _params(sem):
    return pltpu.CompilerParams(dimension_semantics=sem, vmem_limit_bytes=VMEM_LIMIT)


def _mm(a, b):
    return jnp.dot(a.astype(MXU_DTYPE), b.astype(MXU_DTYPE), preferred_element_type=F32)


def _mm_nt(a, b):
    return lax.dot_general(a.astype(MXU_DTYPE), b.astype(MXU_DTYPE), _NT, preferred_element_type=F32)


def _mm_tn(a, b):
    return lax.dot_general(a.astype(MXU_DTYPE), b.astype(MXU_DTYPE), _TN, preferred_element_type=F32)


def _split(a):
    hi = a.astype(BF16)
    lo = (a - hi.astype(F32)).astype(BF16)
    return hi, lo


def _dot3(a, b):
    ah, al = _split(a)
    bh, bl = _split(b)
    d = lambda u, v: jnp.dot(u, v, preferred_element_type=F32)
    return d(ah, bh) + (d(ah, bl) + d(al, bh))


def _rmsnorm_kernel(x_ref, g_ref, o_ref):
    x = x_ref[...]
    ms = jnp.mean(x * x, axis=-1, keepdims=True)
    o_ref[...] = (x * lax.rsqrt(ms + EPS) * g_ref[...]).astype(o_ref.dtype)


def _rmsnorm(x2, g, tm):
    T, D = x2.shape
    return pl.pallas_call(
        _rmsnorm_kernel,
        out_shape=jax.ShapeDtypeStruct((T, D), MXU_DTYPE),
        grid=(T // tm,),
        in_specs=[pl.BlockSpec((tm, D), lambda i: (i, 0)), pl.BlockSpec((1, D), lambda i: (0, 0))],
        out_specs=pl.BlockSpec((tm, D), lambda i: (i, 0)),
        compiler_params=_params(("parallel",)),
        name="rmsnorm",
    )(x2, g.reshape(1, D))


def _matmul_kernel(h_ref, w_ref, o_ref):
    o_ref[...] = jnp.dot(h_ref[...], w_ref[...], preferred_element_type=F32).astype(o_ref.dtype)


def _matmul(h, w, tm, tn, out_dtype):
    T, D = h.shape
    N = w.shape[1]
    return pl.pallas_call(
        _matmul_kernel,
        out_shape=jax.ShapeDtypeStruct((T, N), out_dtype),
        grid=(T // tm, N // tn),
        in_specs=[pl.BlockSpec((tm, D), lambda i, j: (i, 0)), pl.BlockSpec((D, tn), lambda i, j: (0, j))],
        out_specs=pl.BlockSpec((tm, tn), lambda i, j: (i, j)),
        compiler_params=_params(("parallel", "arbitrary")),
        name="proj_main",
    )(h, w)


def _qk_kernel(h_ref, w_ref, g_ref, o_ref):
    z = jnp.dot(h_ref[...], w_ref[...], preferred_element_type=F32)
    for hh in range(DSA_HEADS):
        sl = slice(hh * DSA_HEAD_DIM, (hh + 1) * DSA_HEAD_DIM)
        zh = z[:, sl]
        ms = jnp.mean(zh * zh, axis=-1, keepdims=True)
        o_ref[:, sl] = (zh * lax.rsqrt(ms + EPS) * g_ref[:, sl]).astype(o_ref.dtype)


def _qk_proj(h, w, g, tm):
    T, D = h.shape
    W = DSA_HEADS * DSA_HEAD_DIM
    return pl.pallas_call(
        _qk_kernel,
        out_shape=jax.ShapeDtypeStruct((T, 2 * W), MXU_DTYPE),
        grid=(T // tm, 2),
        in_specs=[
            pl.BlockSpec((tm, D), lambda i, j: (i, 0)),
            pl.BlockSpec((D, W), lambda i, j: (0, j)),
            pl.BlockSpec((1, W), lambda i, j: (0, j)),
        ],
        out_specs=pl.BlockSpec((tm, W), lambda i, j: (i, j)),
        compiler_params=_params(("parallel", "arbitrary")),
        name="proj_qk",
    )(h, w, g)


VT_ONES = 16
VT_ROWS = DSA_HEAD_DIM + VT_ONES


def _vt_kernel(h_ref, wt_ref, o_ref):
    vt = lax.dot_general(wt_ref[...], h_ref[...], _NT, preferred_element_type=F32).astype(o_ref.dtype)
    ones = jnp.ones((VT_ONES, vt.shape[1]), o_ref.dtype)
    for hh in range(DSA_HEADS):
        o_ref[hh, :DSA_HEAD_DIM, :] = vt[hh * DSA_HEAD_DIM:(hh + 1) * DSA_HEAD_DIM]
        o_ref[hh, DSA_HEAD_DIM:, :] = ones


def _vt_proj(h, wt, B, S, tm):
    T, D = h.shape
    W = wt.shape[0]
    nt = S // tm
    return pl.pallas_call(
        _vt_kernel,
        out_shape=jax.ShapeDtypeStruct((B, DSA_HEADS, VT_ROWS, S), MXU_DTYPE),
        grid=(T // tm,),
        in_specs=[pl.BlockSpec((tm, D), lambda i: (i, 0)), pl.BlockSpec((W, D), lambda i: (0, 0))],
        out_specs=pl.BlockSpec((None, DSA_HEADS, VT_ROWS, tm), lambda i: (i // nt, 0, 0, i % nt)),
        compiler_params=_params(("parallel",)),
        name="proj_vt",
    )(h, wt)


IDX_Q_W = IDX_HEADS * IDX_DIM


def _idx_kernel(h_ref, w_ref, wt_ref, lng_ref, lnb_ref, iqt_ref, ki_ref, ga_ref, wit_ref):
    h = h_ref[...]
    z = jnp.dot(h, w_ref[...], preferred_element_type=F32)
    ik = z[:, :IDX_DIM]
    mu = jnp.mean(ik, axis=-1, keepdims=True)
    var = jnp.mean(jnp.square(ik - mu), axis=-1, keepdims=True)
    ki = (ik - mu) * lax.rsqrt(var + EPS) * lng_ref[...] + lnb_ref[...]
    ki_ref[...] = ki.astype(ki_ref.dtype)
    ga_ref[...] = z[:, LANES:LANES + GLA_GATE_RANK]
    zt = lax.dot_general(wt_ref[...], h, _NT, preferred_element_type=F32)
    iqt_ref[...] = zt[:IDX_Q_W].astype(iqt_ref.dtype)
    wit_ref[...] = zt[IDX_Q_W:IDX_Q_W + IDX_HEADS] * (IDX_HEADS ** -0.5 * IDX_DIM ** -0.5)


def _idx_proj(h, w, wt, lng, lnb, tm):
    T, D = h.shape
    return pl.pallas_call(
        _idx_kernel,
        out_shape=(
            jax.ShapeDtypeStruct((IDX_Q_W, T), MXU_DTYPE),
            jax.ShapeDtypeStruct((T, IDX_DIM), MXU_DTYPE),
            jax.ShapeDtypeStruct((T, GLA_GATE_RANK), F32),
            jax.ShapeDtypeStruct((IDX_HEADS, T), F32),
        ),
        grid=(T // tm,),
        in_specs=[
            pl.BlockSpec((tm, D), lambda i: (i, 0)),
            pl.BlockSpec(w.shape, lambda i: (0, 0)),
            pl.BlockSpec(wt.shape, lambda i: (0, 0)),
            pl.BlockSpec((1, IDX_DIM), lambda i: (0, 0)),
            pl.BlockSpec((1, IDX_DIM), lambda i: (0, 0)),
        ],
        out_specs=(
            pl.BlockSpec((IDX_Q_W, tm), lambda i: (0, i)),
            pl.BlockSpec((tm, IDX_DIM), lambda i: (i, 0)),
            pl.BlockSpec((tm, GLA_GATE_RANK), lambda i: (i, 0)),
            pl.BlockSpec((IDX_HEADS, tm), lambda i: (0, i)),
        ),
        compiler_params=_params(("parallel",)),
        name="proj_idx",
    )(h, w, wt, lng, lnb)


def _gla_kernel(q_ref, k_ref, v_ref, r_ref, ga_ref, wa2_ref, ba_ref, ng_ref, o_ref, st_ref, tot_ref, kd_ref,
                oraw_ref, *, n_chunks):
    tb = q_ref.shape[0]

    @pl.when(pl.program_id(1) == 0)
    def _():
        st_ref[...] = jnp.zeros_like(st_ref)

    x = _dot3(ga_ref[...], wa2_ref[...]) + ba_ref[...]
    la = (jnp.minimum(x, 0.0) - jnp.log(1.0 + jnp.exp(-jnp.abs(x)))) * (1.0 / GLA_GATE_TEMP)
    row = lax.broadcasted_iota(I32, (tb, tb), 0)
    col = lax.broadcasted_iota(I32, (tb, tb), 1)
    same = lax.shift_right_logical(row, 6) == lax.shift_right_logical(col, 6)
    ones_blk = jnp.where(same, 1.0, 0.0).astype(BF16)
    tril_blk = jnp.where(same, jnp.where(row >= col, 1.0, 0.0), 0.0).astype(BF16)
    la_hi, la_lo = _split(la)
    d = lambda u, v: jnp.dot(u, v, preferred_element_type=F32)
    tot = d(ones_blk, la_hi) + d(ones_blk, la_lo)
    cum = d(tril_blk, la_hi) + d(tril_blk, la_lo)
    tot_ref[...] = tot
    kd_ref[...] = (k_ref[...].astype(F32) * jnp.exp(tot - cum)).astype(kd_ref.dtype)

    heads = range(GLA_HEADS)
    ksl = lambda hh: slice(hh * GLA_DK, (hh + 1) * GLA_DK)
    vsl = lambda hh: slice(hh * GLA_DV, (hh + 1) * GLA_DV)

    def chunk(c, carry):
        rows = pl.ds(pl.multiple_of(c * CHUNK, CHUNK), CHUNK)
        first = pl.ds(pl.multiple_of(c * CHUNK, CHUNK), 1)
        upd = [_mm_tn(v_ref[rows, vsl(hh)], kd_ref[rows, ksl(hh)]) for hh in heads]
        st = [st_ref[hh] * jnp.exp(tot_ref[first, ksl(hh)]) + upd[hh] for hh in heads]
        for hh in heads:
            st_ref[hh] = st[hh]
        for hh in heads:
            oraw_ref[rows, vsl(hh)] = _mm_nt(q_ref[rows, ksl(hh)], st[hh])
        return carry

    lax.fori_loop(0, n_chunks, chunk, 0)

    ng = ng_ref[...]
    for hh in heads:
        o = oraw_ref[:, vsl(hh)] * (GLA_DK ** -0.5)
        ms = jnp.mean(o * o, axis=-1, keepdims=True)
        r = r_ref[:, vsl(hh)].astype(F32)
        o_ref[:, vsl(hh)] = (o * lax.rsqrt(ms + EPS) * ng * (r / (1.0 + jnp.exp(-r)))).astype(o_ref.dtype)


def _gla(zm, ga, wa2, ba, ng, B, S, tb):
    T = B * S
    nb = S // tb
    qk_w = GLA_HEADS * GLA_DK
    v_w = GLA_HEADS * GLA_DV
    tok = lambda b, i: b * nb + i
    return pl.pallas_call(
        functools.partial(_gla_kernel, n_chunks=tb // CHUNK),
        out_shape=jax.ShapeDtypeStruct((T, v_w), MXU_DTYPE),
        grid=(B, nb),
        in_specs=[
            pl.BlockSpec((tb, qk_w), lambda b, i: (tok(b, i), 0)),
            pl.BlockSpec((tb, qk_w), lambda b, i: (tok(b, i), 1)),
            pl.BlockSpec((tb, v_w), lambda b, i: (tok(b, i), 1)),
            pl.BlockSpec((tb, v_w), lambda b, i: (tok(b, i), 2)),
            pl.BlockSpec((tb, GLA_GATE_RANK), lambda b, i: (tok(b, i), 0)),
            pl.BlockSpec((GLA_GATE_RANK, qk_w), lambda b, i: (0, 0)),
            pl.BlockSpec((1, qk_w), lambda b, i: (0, 0)),
            pl.BlockSpec((1, GLA_DV), lambda b, i: (0, 0)),
        ],
        out_specs=pl.BlockSpec((tb, v_w), lambda b, i: (tok(b, i), 0)),
        scratch_shapes=[pltpu.VMEM((GLA_HEADS, GLA_DV, GLA_DK), F32), pltpu.VMEM((tb, qk_w), F32),
                        pltpu.VMEM((tb, qk_w), MXU_DTYPE), pltpu.VMEM((tb, v_w), F32)],
        compiler_params=_params(("parallel", "arbitrary")),
        name="gla",
    )(zm, zm, zm, zm, ga, wa2, ba, ng)


def _last_kb(qb, QB, KB):
    return ((qb + 1) * QB - 1) // KB


def _causal_steps(S, QB, KB):
    pairs = [(q, k) for q in range(S // QB) for k in range(_last_kb(q, QB, KB) + 1)]
    qs, ks = zip(*pairs)
    return jnp.asarray(qs, I32), jnp.asarray(ks, I32)


def _select_kernel(qb_ref, kb_ref, iqt_ref, wit_ref, ki_ref, mask_ref, hi_ref, lo_ref, l2_ref, gm_ref, *,
                   QB, KB, S, topk):
    qb = qb_ref[pl.program_id(1)]
    kb = kb_ref[pl.program_id(1)]
    last = _last_kb(qb, QB, KB)

    def rows_of(i):
        return pl.ds(pl.multiple_of(i * KB, KB), KB)

    @pl.when(kb == 0)
    def _():
        gm_ref[...] = jnp.full(gm_ref.shape, INT_MIN, I32)

    rc = 128
    assert topk % rc == 0 and KB % rc == 0
    t_chunk = lax.shift_right_logical(qb * QB + lax.broadcasted_iota(I32, (rc, QB), 1), 6)
    for c in range(KB // rc):
        ki = ki_ref[c * rc:(c + 1) * rc, :]
        sc = jnp.zeros((rc, QB), F32)
        for hh in range(IDX_HEADS):
            lg = _mm(ki, iqt_ref[hh * IDX_DIM:(hh + 1) * IDX_DIM, :])
            sc = sc + jnp.maximum(lg, 0.0) * wit_ref[hh:hh + 1, :]
        s_chunk = lax.shift_right_logical(kb * KB + c * rc + lax.broadcasted_iota(I32, (rc, QB), 0), 6)
        bits = lax.bitcast_convert_type(sc, I32)
        sign = lax.shift_right_arithmetic(bits, 31)
        key = (bits ^ (sign & INT_MAX)) - sign
        key = jnp.where(s_chunk <= t_chunk, key, INT_MIN)
        rows = pl.ds(pl.multiple_of(kb * KB + c * rc, rc), rc)
        hi_ref[rows, :] = lax.shift_right_arithmetic(key, 16).astype(I16)
        lo_ref[rows, :] = ((key & 0xFFFF) + I16_MIN).astype(I16)
        g0 = (c * rc) % topk
        gm_ref[g0:g0 + rc, :] = jnp.maximum(gm_ref[g0:g0 + rc, :], key)

    @pl.when(kb == last)
    def _():
        nblk = last + 1
        slab = 32
        one, zero = jnp.int16(1), jnp.int16(0)

        def fold(m, reduce):
            part = m[0:slab]
            for r in range(1, KB // slab):
                part = reduce(part, m[r * slab:(r + 1) * slab])
            return part

        def count(pred):
            def body(i, acc):
                return acc + fold(pred(i), jnp.add)
            acc = lax.fori_loop(0, nblk, body, jnp.zeros((slab, QB), I16))
            return jnp.sum(acc.astype(I32), axis=0, keepdims=True)

        def count_ge(ref, thr):
            t16 = thr.astype(I16)
            return count(lambda i: jnp.where(ref[rows_of(i), :] >= t16, one, zero))

        def search(ref, target, lo, c_lo, hi, c_hi):
            def is_open(lo, c_lo, hi):
                return jnp.where(c_lo > target, jnp.where(hi != lo + 1, 1, 0), 0)

            def step(_, st):
                lo, c_lo, hi, c_hi = st
                open_ = is_open(lo, c_lo, hi) > 0
                mid = lax.shift_right_arithmetic(lo + hi, 1)
                c = count_ge(ref, mid)
                up = jnp.logical_and(open_, c >= target)
                dn = jnp.logical_and(open_, c < target)
                return jnp.where(up, mid, lo), jnp.where(up, c, c_lo), jnp.where(dn, mid, hi), jnp.where(dn, c, c_hi)

            width = jnp.where(is_open(lo, c_lo, hi) > 0, hi - lo, 1)
            n_halvings = jnp.max(32 - lax.clz(width - 1))
            lo, c_lo, hi, c_hi = lax.fori_loop(0, n_halvings, step, (lo, c_lo, hi, c_hi))
            return lo, c_lo, c_hi

        zeros = jnp.zeros((1, QB), I32)
        gm = gm_ref[...]
        g_lo = jnp.maximum(lax.shift_right_arithmetic(jnp.min(gm, axis=0, keepdims=True), 16), I16_MIN + 1)
        g_hi = lax.shift_right_arithmetic(jnp.max(gm, axis=0, keepdims=True), 16) + 1
        hstar, ch_ge, ch_gt = search(hi_ref, topk, g_lo, count_ge(hi_ref, g_lo), g_hi, zeros)
        split = ch_ge > topk
        h16 = hstar.astype(I16)

        def low_half():
            def build(i, carry):
                l2_ref[rows_of(i), :] = jnp.where(hi_ref[rows_of(i), :] == h16, lo_ref[rows_of(i), :],
                                                  jnp.int16(I16_MIN))
                return carry
            lax.fori_loop(0, nblk, build, 0)
            target = jnp.where(split, topk - ch_gt, INT_MAX)
            return search(l2_ref, target, jnp.full((1, QB), I16_MIN, I32), ch_ge - ch_gt,
                          jnp.full((1, QB), I16_MAX + 1, I32), zeros)

        any_split = jnp.max(jnp.where(split, 1, 0)) > 0
        lstar, cl_ge, cl_gt = lax.cond(any_split, low_half,
                                       lambda: (jnp.full((1, QB), I16_MIN, I32), zeros, zeros))
        lstar = jnp.where(split, lstar, I16_MIN)
        l16 = lstar.astype(I16)
        c_ge = jnp.where(split, ch_gt + cl_ge, ch_ge)
        c_gt = jnp.where(split, ch_gt + cl_gt, ch_gt)
        excess = c_ge > topk
        need = jnp.where(excess, topk - c_gt, S + 1)
        row16 = lax.broadcasted_iota(I32, (KB, QB), 0).astype(I16)

        def before(i, j):
            return jnp.where(row16 + (i * KB).astype(I16) < j.astype(I16), one, zero)

        def tied(i, then):
            return jnp.where(hi_ref[rows_of(i), :] == h16, jnp.where(lo_ref[rows_of(i), :] == l16, then, zero), zero)

        def tie_cut():
            def step(_, c):
                lo, hi = c
                mid = lax.shift_right_arithmetic(lo + hi, 1)
                ok = count(lambda i: tied(i, before(i, mid))) >= need
                return jnp.where(ok, lo, mid + 1), jnp.where(ok, mid, hi)
            n_steps = max(1, math.ceil(math.log2(S + 1)))
            lo, _hi = lax.fori_loop(0, n_steps, step, (zeros, jnp.full((1, QB), S, I32)))
            return lo

        any_excess = jnp.max(jnp.where(excess, 1, 0)) > 0
        jcut = lax.cond(any_excess, tie_cut, lambda: jnp.full((1, QB), S, I32))

        def write(i, carry):
            hi_t, lo_t = hi_ref[rows_of(i), :], lo_ref[rows_of(i), :]
            in_bucket = jnp.where(lo_t > l16, one, jnp.where(lo_t == l16, before(i, jcut), zero))
            sel = jnp.where(hi_t > h16, one, jnp.where(hi_t == h16, in_bucket, zero))
            mask_ref[rows_of(i), :] = sel.astype(mask_ref.dtype)
            return carry

        lax.fori_loop(0, nblk, write, 0)

        def clear(i, carry):
            mask_ref[rows_of(i), :] = jnp.zeros((KB, QB), mask_ref.dtype)
            return carry

        lax.fori_loop(nblk, S // KB, clear, 0)


def _dsa_select(iqt, wit, ki, B, S, QB, KB):
    nq, nk = S // QB, S // KB
    topk = min(TOPK_MAX, S // 4)
    assert KB % topk == 0
    qs, ks = _causal_steps(S, QB, KB)
    return pl.pallas_call(
        functools.partial(_select_kernel, QB=QB, KB=KB, S=S, topk=topk),
        out_shape=jax.ShapeDtypeStruct((B, S, S), jnp.int8),
        grid_spec=pltpu.PrefetchScalarGridSpec(
            num_scalar_prefetch=2,
            grid=(B, qs.shape[0]),
            in_specs=[
                pl.BlockSpec((IDX_Q_W, QB), lambda b, s, qs, ks: (0, b * nq + qs[s])),
                pl.BlockSpec((IDX_HEADS, QB), lambda b, s, qs, ks: (0, b * nq + qs[s])),
                pl.BlockSpec((KB, IDX_DIM), lambda b, s, qs, ks: (b * nk + ks[s], 0)),
            ],
            out_specs=pl.BlockSpec((None, S, QB), lambda b, s, qs, ks: (b, 0, qs[s])),
            scratch_shapes=[pltpu.VMEM((S, QB), I16), pltpu.VMEM((S, QB), I16), pltpu.VMEM((S, QB), I16),
                            pltpu.VMEM((topk, QB), I32)],
        ),
        compiler_params=_params(("parallel", "arbitrary")),
        name="dsa_select",
    )(qs, ks, iqt, wit, ki)


def _attn_kernel(qb_ref, kb_ref, q_ref, k_ref, vt_ref, m_ref, o_ref, acc_ref, run_ref, *, QB, KB):
    qb = qb_ref[pl.program_id(1)]
    kb = kb_ref[pl.program_id(1)]
    last = _last_kb(qb, QB, KB)

    @pl.when(kb == 0)
    def _():
        acc_ref[...] = jnp.zeros_like(acc_ref)
        run_ref[...] = jnp.full(run_ref.shape, -1e30, F32)

    selected = m_ref[...].astype(F32) > 0.0
    head = lambda hh: slice(hh * DSA_HEAD_DIM, (hh + 1) * DSA_HEAD_DIM)
    def logits(hh):
        lg = _mm_nt(k_ref[:, head(hh)], q_ref[:, head(hh)])
        lg = jnp.where(selected, lg, -jnp.inf)
        return lg, jnp.max(lg, axis=0, keepdims=True)

    nxt = logits(0)
    for hh in range(DSA_HEADS):
        lg, top = nxt
        nxt = logits(hh + 1) if hh + 1 < DSA_HEADS else None
        run_old = run_ref[hh]
        run_new = jnp.maximum(run_old, top)
        p = jnp.exp2(lg - run_new).astype(MXU_DTYPE)
        acc_ref[hh] = acc_ref[hh] * jnp.exp2(run_old - run_new) + jnp.dot(vt_ref[hh], p, preferred_element_type=F32)
        run_ref[hh] = run_new

    @pl.when(kb == last)
    def _():
        for hh in range(DSA_HEADS):
            acc = acc_ref[hh]
            o = acc[:DSA_HEAD_DIM] / acc[DSA_HEAD_DIM:DSA_HEAD_DIM + 1]
            o_ref[:, hh * DSA_HEAD_DIM:(hh + 1) * DSA_HEAD_DIM] = o.T.astype(o_ref.dtype)


def _dsa_attn(qk, vt, mask, B, S, QB, KB):
    T = B * S
    nq, nk = S // QB, S // KB
    W = DSA_HEADS * DSA_HEAD_DIM
    qs, ks = _causal_steps(S, QB, KB)
    return pl.pallas_call(
        functools.partial(_attn_kernel, QB=QB, KB=KB),
        out_shape=jax.ShapeDtypeStruct((T, W), MXU_DTYPE),
        grid_spec=pltpu.PrefetchScalarGridSpec(
            num_scalar_prefetch=2,
            grid=(B, qs.shape[0]),
            in_specs=[
                pl.BlockSpec((QB, W), lambda b, s, qs, ks: (b * nq + qs[s], 0)),
                pl.BlockSpec((KB, W), lambda b, s, qs, ks: (b * nk + ks[s], 1)),
                pl.BlockSpec((None, DSA_HEADS, VT_ROWS, KB), lambda b, s, qs, ks: (b, 0, 0, ks[s])),
                pl.BlockSpec((None, KB, QB), lambda b, s, qs, ks: (b, ks[s], qs[s])),
            ],
            out_specs=pl.BlockSpec((QB, W), lambda b, s, qs, ks: (b * nq + qs[s], 0)),
            scratch_shapes=[pltpu.VMEM((DSA_HEADS, VT_ROWS, QB), F32), pltpu.VMEM((DSA_HEADS, 1, QB), F32)],
        ),
        compiler_params=_params(("parallel", "arbitrary")),
        name="dsa_attn",
    )(qs, ks, qk, qk, vt, mask)


def _post_kernel(og_ref, od_ref, ga_ref, gb_ref, x_ref, pa_ref, pb_ref, wo_ref, g2_ref, wr_ref, br_ref,
                 x1_ref, h2_ref, route_ref):
    sig = lambda v: 1.0 / (1.0 + jnp.exp(-v))
    a = jnp.dot(og_ref[...], pa_ref[...], preferred_element_type=F32)
    b = jnp.dot(od_ref[...], pb_ref[...], preferred_element_type=F32)
    mix = sig(ga_ref[...].astype(F32)) * a + sig(gb_ref[...].astype(F32)) * b
    x1 = x_ref[...] + _mm(mix, wo_ref[...])
    x1_ref[...] = x1
    ms = jnp.mean(x1 * x1, axis=-1, keepdims=True)
    h2 = x1 * lax.rsqrt(ms + EPS) * g2_ref[...]
    h2_ref[...] = h2.astype(h2_ref.dtype)

    logits = _dot3(h2, wr_ref[...]) + br_ref[...]
    tm = logits.shape[0]
    lane = lax.broadcasted_iota(I32, (tm, LANES), 1).astype(F32)
    neg = -jnp.inf
    far = float(2 * LANES)
    rmax = lambda v: jnp.max(v, axis=1, keepdims=True)
    rmin = lambda v: jnp.min(v, axis=1, keepdims=True)
    gl = jnp.where(lane >= N_EXPERTS, jnp.where(lane < N_EXPERTS + N_GROUPS, logits, neg), neg)
    gmax = rmax(gl)
    g_w = 1.0 / jnp.sum(jnp.exp(gl - gmax), axis=1, keepdims=True)
    g_idx = rmin(jnp.where(gl == gmax, lane, far)) - N_EXPERTS
    e_lo = g_idx * EXPERTS_PER_GROUP
    el = jnp.where(lane >= e_lo, jnp.where(lane < e_lo + EXPERTS_PER_GROUP, logits, neg), neg)
    m1 = rmax(el)
    e1 = rmin(jnp.where(el == m1, lane, far))
    el2 = jnp.where(lane == e1, neg, el)
    m2 = rmax(el2)
    e2 = rmin(jnp.where(el2 == m2, lane, far))
    p2 = jnp.exp(m2 - m1)
    w1 = g_w / (1.0 + p2)
    route_ref[...] = jnp.where(lane == 0.0, e1, jnp.where(lane == 1.0, e2, jnp.where(
        lane == 2.0, w1, jnp.where(lane == 3.0, w1 * p2, 0.0))))


def _post(og, od, zm, x2, pa, pb, wo, g2, wr, br, tm):
    T, D = x2.shape
    row = lambda i: (i, 0)
    full = lambda i: (0, 0)
    return pl.pallas_call(
        _post_kernel,
        out_shape=(
            jax.ShapeDtypeStruct((T, D), F32),
            jax.ShapeDtypeStruct((T, D), F32),
            jax.ShapeDtypeStruct((T, LANES), F32),
        ),
        grid=(T // tm,),
        in_specs=[
            pl.BlockSpec((tm, D), row),
            pl.BlockSpec((tm, D), row),
            pl.BlockSpec((tm, D), lambda i: (i, 3)),
            pl.BlockSpec((tm, D), lambda i: (i, 4)),
            pl.BlockSpec((tm, D), row),
            pl.BlockSpec((D, D), full),
            pl.BlockSpec((D, D), full),
            pl.BlockSpec((D, D), full),
            pl.BlockSpec((1, D), full),
            pl.BlockSpec((D, LANES), full),
            pl.BlockSpec((1, LANES), full),
        ],
        out_specs=(pl.BlockSpec((tm, D), row), pl.BlockSpec((tm, D), row), pl.BlockSpec((tm, LANES), row)),
        compiler_params=_params(("parallel",)),
        name="merge_router",
    )(og, od, zm, zm, x2, pa, pb, wo, g2, wr, br)


MOE_TM = 256


def _gather_combine_kernel(idx_ref, nxt_ref, x1_ref, route_ref, src_hbm, o_ref, buf_ref, sem):
    i = pl.program_id(0)
    n = pl.num_programs(0)
    tm = o_ref.shape[0]
    slot = lax.rem(i, 2)

    def fetch(ids_ref, s):
        def start(r, carry):
            for k in range(2):
                pltpu.make_async_copy(src_hbm.at[pl.ds(ids_ref[0, 0, 2 * r + k], 1), :],
                                      buf_ref.at[s, pl.ds(k * tm + r, 1), :], sem.at[s]).start(priority=k)
            return carry
        lax.fori_loop(0, tm, start, 0, unroll=8)

    @pl.when(i == 0)
    def _():
        fetch(idx_ref, 0)

    @pl.when(i + 1 < n)
    def _():
        fetch(nxt_ref, 1 - slot)

    pltpu.make_async_copy(src_hbm.at[pl.ds(0, 2 * tm), :], buf_ref.at[slot], sem.at[slot]).wait()
    r = route_ref[...]
    lane = lax.broadcasted_iota(I32, r.shape, 1)
    w1 = jnp.sum(jnp.where(lane == 2, r, 0.0), axis=1, keepdims=True)
    w2 = jnp.sum(jnp.where(lane == 3, r, 0.0), axis=1, keepdims=True)
    o_ref[...] = x1_ref[...] + (w1 * buf_ref[slot, :tm, :] + w2 * buf_ref[slot, tm:, :])


def _gather_combine(x1, ys, pos, route, tm):
    T, D = x1.shape
    nt = T // tm
    idx3 = pos.reshape(nt, 1, 2 * tm)
    row = lambda i: (i, 0)
    return pl.pallas_call(
        _gather_combine_kernel,
        out_shape=jax.ShapeDtypeStruct((T, D), F32),
        grid=(nt,),
        in_specs=[
            pl.BlockSpec((1, 1, 2 * tm), lambda i: (i, 0, 0), memory_space=pltpu.SMEM),
            pl.BlockSpec((1, 1, 2 * tm), lambda i: (jnp.minimum(i + 1, nt - 1), 0, 0), memory_space=pltpu.SMEM),
            pl.BlockSpec((tm, D), row),
            pl.BlockSpec((tm, LANES), row),
            pl.BlockSpec(memory_space=pl.ANY),
        ],
        out_specs=pl.BlockSpec((tm, D), row),
        scratch_shapes=[pltpu.VMEM((2, 2 * tm, D), ys.dtype), pltpu.SemaphoreType.DMA((2,))],
        compiler_params=_params(("arbitrary",)),
        name="moe_gather_combine",
    )(idx3, idx3, x1, route, ys)


def _scatter_kernel(idx_ref, x_ref, o_hbm, sem, *, fanout):
    tm = x_ref.shape[0]

    def start(r, carry):
        for k in range(fanout):
            dst = idx_ref[0, 0, fanout * r + k]
            pltpu.make_async_copy(x_ref.at[pl.ds(r, 1), :], o_hbm.at[pl.ds(dst, 1), :], sem).start(priority=k)
        return carry

    lax.fori_loop(0, tm, start, 0, unroll=8)
    for k in range(fanout):
        pltpu.make_async_copy(x_ref, o_hbm.at[pl.ds(0, tm), :], sem).wait()


def _row_scatter(x, idx, fanout, tm, name):
    n, D = x.shape
    return pl.pallas_call(
        functools.partial(_scatter_kernel, fanout=fanout),
        out_shape=jax.ShapeDtypeStruct((fanout * n, D), x.dtype),
        grid=(n // tm,),
        in_specs=[
            pl.BlockSpec((1, 1, fanout * tm), lambda i: (i, 0, 0), memory_space=pltpu.SMEM),
            pl.BlockSpec((tm, D), lambda i: (i, 0)),
        ],
        out_specs=pl.BlockSpec(memory_space=pl.ANY),
        scratch_shapes=[pltpu.SemaphoreType.DMA(())],
        compiler_params=_params(("arbitrary",)),
        name=name,
    )(idx.reshape(n // tm, 1, fanout * tm), x)


def _route_plan(route, tm):
    n = 2 * route.shape[0]
    n_tiles = n // tm
    ids = route[:, :2].astype(I32).reshape(n)
    onehot = (ids[:, None] == jnp.arange(N_EXPERTS, dtype=I32)[None, :]).astype(F32).reshape(n_tiles, tm, N_EXPERTS)
    tril = (jnp.arange(tm)[:, None] >= jnp.arange(tm)[None, :]).astype(F32)
    in_tile = jnp.einsum("rc,tce->tre", tril, onehot)
    per_tile = in_tile[:, -1, :]
    before = jnp.cumsum(per_tile, axis=0) - per_tile
    counts = jnp.sum(per_tile, axis=0).astype(I32)
    starts = (jnp.cumsum(counts) - counts).astype(I32)
    rank = jnp.sum(onehot * (in_tile + before[:, None, :] - 1.0), axis=2).reshape(n)
    pos = (jnp.sum(onehot.reshape(n, N_EXPERTS) * starts[None, :].astype(F32), axis=1) + rank).astype(I32)
    bounds = jnp.sort(jnp.concatenate([jnp.arange(n_tiles, dtype=I32) * tm, starts]))
    ends = jnp.concatenate([bounds[1:], jnp.full((1,), n, I32)])
    seg_tile = jnp.minimum(bounds // tm, n_tiles - 1)
    seg_exp = jnp.clip(jnp.sum((starts[None, :] <= bounds[:, None]).astype(I32), axis=1) - 1, 0, N_EXPERTS - 1)
    seg_first = jnp.concatenate([jnp.ones((1,), I32), (seg_tile[1:] != seg_tile[:-1]).astype(I32)])
    return pos, (seg_tile, seg_exp, bounds - seg_tile * tm, ends - seg_tile * tm, seg_first)


def _experts_kernel(tile_ref, exp_ref, lo_ref, hi_ref, first_ref, x_ref, wg_ref, wu_ref, wd_ref, o_ref):
    s = pl.program_id(0)
    lo, hi = lo_ref[s], hi_ref[s]

    @pl.when(first_ref[s] == 1)
    def _():
        o_ref[...] = jnp.zeros_like(o_ref)

    @pl.when(hi > lo)
    def _():
        x = x_ref[...].astype(MXU_DTYPE)
        a = jnp.dot(x, wg_ref[...], preferred_element_type=F32)
        u = jnp.dot(x, wu_ref[...], preferred_element_type=F32)
        y = _mm(a / (1.0 + jnp.exp(-a)) * u, wd_ref[...])
        row = lax.broadcasted_iota(I32, y.shape, 0)
        o_ref[...] += jnp.where(row >= lo, jnp.where(row < hi, y, 0.0), 0.0)


def _experts(xs, segs, wg, wu, wd, tm):
    n, D = xs.shape
    F = wg.shape[2]
    tile_of = lambda s, tile, exp, lo, hi, first: (tile[s], 0)
    w_of = lambda s, tile, exp, lo, hi, first: (exp[s], 0, 0)
    return pl.pallas_call(
        _experts_kernel,
        out_shape=jax.ShapeDtypeStruct((n, D), F32),
        grid_spec=pltpu.PrefetchScalarGridSpec(
            num_scalar_prefetch=5,
            grid=(segs[0].shape[0],),
            in_specs=[
                pl.BlockSpec((tm, D), tile_of),
                pl.BlockSpec((None, D, F), w_of),
                pl.BlockSpec((None, D, F), w_of),
                pl.BlockSpec((None, F, D), w_of),
            ],
            out_specs=pl.BlockSpec((tm, D), tile_of),
        ),
        compiler_params=_params(("arbitrary",)),
        name="experts",
    )(*segs, xs, wg, wu, wd)


def _moe(h2, route, x1, wg, wu, wd):
    T, D = x1.shape
    tm = min(MOE_TM, T)
    pos, segs = _route_plan(route, tm)
    xs = _row_scatter(h2, pos, 2, tm, "moe_scatter")
    ys = _experts(xs, segs, wg, wu, wd, tm)
    return _gather_combine(x1, ys, pos, route, tm)


def _pad_cols(w, n):
    return jnp.pad(w, ((0, 0), (0, n - w.shape[1])))


def _layer(x2, B, S, p):
    T, D = x2.shape
    assert D == 2 * GLA_HEADS * GLA_DK == GLA_HEADS * GLA_DV == DSA_HEADS * DSA_HEAD_DIM
    assert S % 512 == 0
    cd = MXU_DTYPE
    qk_w, v_w, dsa_w = GLA_HEADS * GLA_DK, GLA_HEADS * GLA_DV, DSA_HEADS * DSA_HEAD_DIM
    splits = (qk_w, qk_w, v_w, GLA_GATE_RANK, v_w, dsa_w, dsa_w, dsa_w, IDX_Q_W, IDX_DIM, IDX_HEADS, D, D)
    offs = [0]
    for s in splits:
        offs.append(offs[-1] + s)
    w_in = p["w_in"]
    (w_gq, w_gk, w_gv, w_ga, w_gr, w_dq, w_dk, w_dv, w_iq, w_ik, w_iw, w_ta, w_tb) = [
        w_in[:, offs[i]:offs[i + 1]] for i in range(len(splits))]

    w_main = jnp.concatenate([w_gq, w_gk, w_gv, w_gr, w_ta, w_tb], axis=1).astype(cd)
    w_qk = jnp.concatenate([w_dq, w_dk], axis=1).astype(cd)
    g_qk = jnp.concatenate([jnp.tile(p["dsa_q_norm_g"] * (DSA_HEAD_DIM ** -0.5 * math.log2(math.e)), DSA_HEADS),
                            jnp.tile(p["dsa_k_norm_g"], DSA_HEADS)]).reshape(1, 2 * dsa_w).astype(F32)
    w_vt = w_dv.T.astype(cd)
    w_idx = jnp.concatenate([_pad_cols(w_ik, LANES), _pad_cols(w_ga, LANES)], axis=1).astype(cd)
    w_idxt = jnp.pad(jnp.concatenate([w_iq, w_iw], axis=1).T, ((0, 16 - IDX_HEADS), (0, 0))).astype(cd)

    tm = min(1024, T)
    h = _rmsnorm(x2, p["norm1_g"], tm)
    zm = _matmul(h, w_main, tm, 1024, cd)
    qk = _qk_proj(h, w_qk, g_qk, tm)
    vt = _vt_proj(h, w_vt, B, S, min(512, S))
    iqt, ki, ga, wit = _idx_proj(h, w_idx, w_idxt, p["idx_k_ln_g"].reshape(1, -1), p["idx_k_ln_b"].reshape(1, -1),
                                 min(512, T))

    o_gla = _gla(zm, ga, p["gla_w_a2"], p["gla_b_a"].reshape(1, -1), p["gla_norm_g"].reshape(1, -1), B, S, 512)

    QB, KB = 256, min(1024, S)
    mask = _dsa_select(iqt, wit, ki, B, S, QB, KB)
    o_dsa = _dsa_attn(qk, vt, mask, B, S, QB, KB)

    w_r = _pad_cols(jnp.concatenate([p["w_router_expert"], p["w_router_group"]], axis=1), LANES)
    b_r = _pad_cols(jnp.concatenate([p["b_router_expert"], p["b_router_group"]]).reshape(1, -1), LANES)
    x1, h2, route = _post(o_gla, o_dsa, zm, x2, p["w_branch_gla"].astype(cd), p["w_branch_dsa"].astype(cd),
                          p["w_out"].astype(cd), p["norm2_g"].reshape(1, -1), w_r, b_r, min(512, T))
    return _moe(h2, route, x1, p["w_exp_gate"].astype(cd), p["w_exp_up"].astype(cd), p["w_exp_down"].astype(cd))


def kernel(x, norm1_g, w_in, gla_w_a2, gla_b_a, gla_norm_g, dsa_q_norm_g, dsa_k_norm_g, idx_k_ln_g, idx_k_ln_b,
           w_branch_gla, w_branch_dsa, w_out, norm2_g, w_router_group, b_router_group, w_router_expert,
           b_router_expert, w_exp_gate, w_exp_up, w_exp_down):
    B, S, D = x.shape
    stacked = dict(norm1_g=norm1_g, w_in=w_in, gla_w_a2=gla_w_a2, gla_b_a=gla_b_a, gla_norm_g=gla_norm_g,
                   dsa_q_norm_g=dsa_q_norm_g, dsa_k_norm_g=dsa_k_norm_g, idx_k_ln_g=idx_k_ln_g,
                   idx_k_ln_b=idx_k_ln_b, w_branch_gla=w_branch_gla, w_branch_dsa=w_branch_dsa, w_out=w_out,
                   norm2_g=norm2_g, w_router_group=w_router_group, b_router_group=b_router_group,
                   w_router_expert=w_router_expert, b_router_expert=b_router_expert, w_exp_gate=w_exp_gate,
                   w_exp_up=w_exp_up, w_exp_down=w_exp_down)
    x2 = x.reshape(B * S, D).astype(F32)
    for l in range(w_in.shape[0]):
        x2 = _layer(x2, B, S, {k: v[l] for k, v in stacked.items()})
    return x2.reshape(B, S, D).astype(x.dtype)
```

```python
import functools
import math

import jax
import jax.numpy as jnp
from jax import lax
from jax.experimental import pallas as pl
from jax.experimental.pallas import tpu as pltpu

F32 = jnp.float32
BF16 = jnp.bfloat16
I32 = jnp.int32
MXU_DTYPE = BF16

CHUNK = 64
EPS = 1e-6
GLA_HEADS, GLA_DK, GLA_DV = 4, 128, 256
GLA_GATE_RANK = 16
GLA_GATE_TEMP = 16.0
DSA_HEADS, DSA_HEAD_DIM = 8, 128
IDX_HEADS, IDX_DIM = 8, 64
TOPK_MAX = 256
N_GROUPS, EXPERTS_PER_GROUP = 4, 8
N_EXPERTS = N_GROUPS * EXPERTS_PER_GROUP
LANES = 128

INT_MIN = -(2 ** 31)
INT_MAX = 2 ** 31 - 1
I16 = jnp.int16
I16_MIN, I16_MAX = -(2 ** 15), 2 ** 15 - 1

VMEM_LIMIT = 56 * 1024 * 1024

_NT = (((1,), (1,)), ((), ()))
_TN = (((0,), (0,)), ((), ()))


def _params(sem):
    return pltpu.CompilerParams(dimension_semantics=sem, vmem_limit_bytes=VMEM_LIMIT)


def _mm(a, b):
    return jnp.dot(a.astype(MXU_DTYPE), b.astype(MXU_DTYPE), preferred_element_type=F32)


def _mm_nt(a, b):
    return lax.dot_general(a.astype(MXU_DTYPE), b.astype(MXU_DTYPE), _NT, preferred_element_type=F32)


def _mm_tn(a, b):
    return lax.dot_general(a.astype(MXU_DTYPE), b.astype(MXU_DTYPE), _TN, preferred_element_type=F32)


def _split(a):
    hi = a.astype(BF16)
    lo = (a - hi.astype(F32)).astype(BF16)
    return hi, lo


def _dot3(a, b):
    ah, al = _split(a)
    bh, bl = _split(b)
    d = lambda u, v: jnp.dot(u, v, preferred_element_type=F32)
    return d(ah, bh) + (d(ah, bl) + d(al, bh))


def _rmsnorm_kernel(x_ref, g_ref, o_ref):
    x = x_ref[...]
    ms = jnp.mean(x * x, axis=-1, keepdims=True)
    o_ref[...] = (x * lax.rsqrt(ms + EPS) * g_ref[...]).astype(o_ref.dtype)


def _rmsnorm(x2, g, tm):
    T, D = x2.shape
    return pl.pallas_call(
        _rmsnorm_kernel,
        out_shape=jax.ShapeDtypeStruct((T, D), MXU_DTYPE),
        grid=(T // tm,),
        in_specs=[pl.BlockSpec((tm, D), lambda i: (i, 0)), pl.BlockSpec((1, D), lambda i: (0, 0))],
        out_specs=pl.BlockSpec((tm, D), lambda i: (i, 0)),
        compiler_params=_params(("parallel",)),
        name="rmsnorm",
    )(x2, g.reshape(1, D))


def _matmul_kernel(h_ref, w_ref, o_ref):
    o_ref[...] = jnp.dot(h_ref[...], w_ref[...], preferred_element_type=F32).astype(o_ref.dtype)


def _matmul(h, w, tm, tn, out_dtype):
    T, D = h.shape
    N = w.shape[1]
    return pl.pallas_call(
        _matmul_kernel,
        out_shape=jax.ShapeDtypeStruct((T, N), out_dtype),
        grid=(T // tm, N // tn),
        in_specs=[pl.BlockSpec((tm, D), lambda i, j: (i, 0)), pl.BlockSpec((D, tn), lambda i, j: (0, j))],
        out_specs=pl.BlockSpec((tm, tn), lambda i, j: (i, j)),
        compiler_params=_params(("parallel", "arbitrary")),
        name="proj_main",
    )(h, w)


def _qk_kernel(h_ref, w_ref, g_ref, o_ref):
    z = jnp.dot(h_ref[...], w_ref[...], preferred_element_type=F32)
    for hh in range(DSA_HEADS):
        sl = slice(hh * DSA_HEAD_DIM, (hh + 1) * DSA_HEAD_DIM)
        zh = z[:, sl]
        ms = jnp.mean(zh * zh, axis=-1, keepdims=True)
        o_ref[:, sl] = (zh * lax.rsqrt(ms + EPS) * g_ref[:, sl]).astype(o_ref.dtype)


def _qk_proj(h, w, g, tm):
    T, D = h.shape
    W = DSA_HEADS * DSA_HEAD_DIM
    return pl.pallas_call(
        _qk_kernel,
        out_shape=jax.ShapeDtypeStruct((T, 2 * W), MXU_DTYPE),
        grid=(T // tm, 2),
        in_specs=[
            pl.BlockSpec((tm, D), lambda i, j: (i, 0)),
            pl.BlockSpec((D, W), lambda i, j: (0, j)),
            pl.BlockSpec((1, W), lambda i, j: (0, j)),
        ],
        out_specs=pl.BlockSpec((tm, W), lambda i, j: (i, j)),
        compiler_params=_params(("parallel", "arbitrary")),
        name="proj_qk",
    )(h, w, g)


VT_ONES = 16
VT_ROWS = DSA_HEAD_DIM + VT_ONES


def _vt_kernel(h_ref, wt_ref, o_ref):
    vt = lax.dot_general(wt_ref[...], h_ref[...], _NT, preferred_element_type=F32).astype(o_ref.dtype)
    ones = jnp.ones((VT_ONES, vt.shape[1]), o_ref.dtype)
    for hh in range(DSA_HEADS):
        o_ref[hh, :DSA_HEAD_DIM, :] = vt[hh * DSA_HEAD_DIM:(hh + 1) * DSA_HEAD_DIM]
        o_ref[hh, DSA_HEAD_DIM:, :] = ones


def _vt_proj(h, wt, B, S, tm):
    T, D = h.shape
    W = wt.shape[0]
    nt = S // tm
    return pl.pallas_call(
        _vt_kernel,
        out_shape=jax.ShapeDtypeStruct((B, DSA_HEADS, VT_ROWS, S), MXU_DTYPE),
        grid=(T // tm,),
        in_specs=[pl.BlockSpec((tm, D), lambda i: (i, 0)), pl.BlockSpec((W, D), lambda i: (0, 0))],
        out_specs=pl.BlockSpec((None, DSA_HEADS, VT_ROWS, tm), lambda i: (i // nt, 0, 0, i % nt)),
        compiler_params=_params(("parallel",)),
        name="proj_vt",
    )(h, wt)


IDX_Q_W = IDX_HEADS * IDX_DIM


def _idx_kernel(h_ref, w_ref, wt_ref, lng_ref, lnb_ref, iqt_ref, ki_ref, ga_ref, wit_ref):
    h = h_ref[...]
    z = jnp.dot(h, w_ref[...], preferred_element_type=F32)
    ik = z[:, :IDX_DIM]
    mu = jnp.mean(ik, axis=-1, keepdims=True)
    var = jnp.mean(jnp.square(ik - mu), axis=-1, keepdims=True)
    ki = (ik - mu) * lax.rsqrt(var + EPS) * lng_ref[...] + lnb_ref[...]
    ki_ref[...] = ki.astype(ki_ref.dtype)
    ga_ref[...] = z[:, LANES:LANES + GLA_GATE_RANK]
    zt = lax.dot_general(wt_ref[...], h, _NT, preferred_element_type=F32)
    iqt_ref[...] = zt[:IDX_Q_W].astype(iqt_ref.dtype)
    wit_ref[...] = zt[IDX_Q_W:IDX_Q_W + IDX_HEADS] * (IDX_HEADS ** -0.5 * IDX_DIM ** -0.5)


def _idx_proj(h, w, wt, lng, lnb, tm):
    T, D = h.shape
    return pl.pallas_call(
        _idx_kernel,
        out_shape=(
            jax.ShapeDtypeStruct((IDX_Q_W, T), MXU_DTYPE),
            jax.ShapeDtypeStruct((T, IDX_DIM), MXU_DTYPE),
            jax.ShapeDtypeStruct((T, GLA_GATE_RANK), F32),
            jax.ShapeDtypeStruct((IDX_HEADS, T), F32),
        ),
        grid=(T // tm,),
        in_specs=[
            pl.BlockSpec((tm, D), lambda i: (i, 0)),
            pl.BlockSpec(w.shape, lambda i: (0, 0)),
            pl.BlockSpec(wt.shape, lambda i: (0, 0)),
            pl.BlockSpec((1, IDX_DIM), lambda i: (0, 0)),
            pl.BlockSpec((1, IDX_DIM), lambda i: (0, 0)),
        ],
        out_specs=(
            pl.BlockSpec((IDX_Q_W, tm), lambda i: (0, i)),
            pl.BlockSpec((tm, IDX_DIM), lambda i: (i, 0)),
            pl.BlockSpec((tm, GLA_GATE_RANK), lambda i: (i, 0)),
            pl.BlockSpec((IDX_HEADS, tm), lambda i: (0, i)),
        ),
        compiler_params=_params(("parallel",)),
        name="proj_idx",
    )(h, w, wt, lng, lnb)


def _gla_kernel(q_ref, k_ref, v_ref, r_ref, ga_ref, wa2_ref, ba_ref, ng_ref, o_ref, st_ref, tot_ref, kd_ref,
                oraw_ref, *, n_chunks):
    tb = q_ref.shape[0]

    @pl.when(pl.program_id(1) == 0)
    def _():
        st_ref[...] = jnp.zeros_like(st_ref)

    x = _dot3(ga_ref[...], wa2_ref[...]) + ba_ref[...]
    la = (jnp.minimum(x, 0.0) - jnp.log(1.0 + jnp.exp(-jnp.abs(x)))) * (1.0 / GLA_GATE_TEMP)
    row = lax.broadcasted_iota(I32, (tb, tb), 0)
    col = lax.broadcasted_iota(I32, (tb, tb), 1)
    same = lax.shift_right_logical(row, 6) == lax.shift_right_logical(col, 6)
    ones_blk = jnp.where(same, 1.0, 0.0).astype(BF16)
    tril_blk = jnp.where(same, jnp.where(row >= col, 1.0, 0.0), 0.0).astype(BF16)
    la_hi, la_lo = _split(la)
    d = lambda u, v: jnp.dot(u, v, preferred_element_type=F32)
    tot = d(ones_blk, la_hi) + d(ones_blk, la_lo)
    cum = d(tril_blk, la_hi) + d(tril_blk, la_lo)
    tot_ref[...] = tot
    kd_ref[...] = (k_ref[...].astype(F32) * jnp.exp(tot - cum)).astype(kd_ref.dtype)

    heads = range(GLA_HEADS)
    ksl = lambda hh: slice(hh * GLA_DK, (hh + 1) * GLA_DK)
    vsl = lambda hh: slice(hh * GLA_DV, (hh + 1) * GLA_DV)

    def chunk(c, carry):
        rows = pl.ds(pl.multiple_of(c * CHUNK, CHUNK), CHUNK)
        first = pl.ds(pl.multiple_of(c * CHUNK, CHUNK), 1)
        upd = [_mm_tn(v_ref[rows, vsl(hh)], kd_ref[rows, ksl(hh)]) for hh in heads]
        st = [st_ref[hh] * jnp.exp(tot_ref[first, ksl(hh)]) + upd[hh] for hh in heads]
        for hh in heads:
            st_ref[hh] = st[hh]
        for hh in heads:
            oraw_ref[rows, vsl(hh)] = _mm_nt(q_ref[rows, ksl(hh)], st[hh])
        return carry

    lax.fori_loop(0, n_chunks, chunk, 0)

    ng = ng_ref[...]
    for hh in heads:
        o = oraw_ref[:, vsl(hh)] * (GLA_DK ** -0.5)
        ms = jnp.mean(o * o, axis=-1, keepdims=True)
        r = r_ref[:, vsl(hh)].astype(F32)
        o_ref[:, vsl(hh)] = (o * lax.rsqrt(ms + EPS) * ng * (r / (1.0 + jnp.exp(-r)))).astype(o_ref.dtype)


def _gla(zm, ga, wa2, ba, ng, B, S, tb):
    T = B * S
    nb = S // tb
    qk_w = GLA_HEADS * GLA_DK
    v_w = GLA_HEADS * GLA_DV
    tok = lambda b, i: b * nb + i
    return pl.pallas_call(
        functools.partial(_gla_kernel, n_chunks=tb // CHUNK),
        out_shape=jax.ShapeDtypeStruct((T, v_w), MXU_DTYPE),
        grid=(B, nb),
        in_specs=[
            pl.BlockSpec((tb, qk_w), lambda b, i: (tok(b, i), 0)),
            pl.BlockSpec((tb, qk_w), lambda b, i: (tok(b, i), 1)),
            pl.BlockSpec((tb, v_w), lambda b, i: (tok(b, i), 1)),
            pl.BlockSpec((tb, v_w), lambda b, i: (tok(b, i), 2)),
            pl.BlockSpec((tb, GLA_GATE_RANK), lambda b, i: (tok(b, i), 0)),
            pl.BlockSpec((GLA_GATE_RANK, qk_w), lambda b, i: (0, 0)),
            pl.BlockSpec((1, qk_w), lambda b, i: (0, 0)),
            pl.BlockSpec((1, GLA_DV), lambda b, i: (0, 0)),
        ],
        out_specs=pl.BlockSpec((tb, v_w), lambda b, i: (tok(b, i), 0)),
        scratch_shapes=[pltpu.VMEM((GLA_HEADS, GLA_DV, GLA_DK), F32), pltpu.VMEM((tb, qk_w), F32),
                        pltpu.VMEM((tb, qk_w), MXU_DTYPE), pltpu.VMEM((tb, v_w), F32)],
        compiler_params=_params(("parallel", "arbitrary")),
        name="gla",
    )(zm, zm, zm, zm, ga, wa2, ba, ng)


def _last_kb(qb, QB, KB):
    return ((qb + 1) * QB - 1) // KB


def _causal_steps(S, QB, KB):
    pairs = [(q, k) for q in range(S // QB) for k in range(_last_kb(q, QB, KB) + 1)]
    qs, ks = zip(*pairs)
    return jnp.asarray(qs, I32), jnp.asarray(ks, I32)


def _select_kernel(qb_ref, kb_ref, iqt_ref, wit_ref, ki_ref, mask_ref, hi_ref, lo_ref, l2_ref, gm_ref, *,
                   QB, KB, S, topk):
    qb = qb_ref[pl.program_id(1)]
    kb = kb_ref[pl.program_id(1)]
    last = _last_kb(qb, QB, KB)

    def rows_of(i):
        return pl.ds(pl.multiple_of(i * KB, KB), KB)

    @pl.when(kb == 0)
    def _():
        gm_ref[...] = jnp.full(gm_ref.shape, INT_MIN, I32)

    rc = 128
    assert topk % rc == 0 and KB % rc == 0
    t_chunk = lax.shift_right_logical(qb * QB + lax.broadcasted_iota(I32, (rc, QB), 1), 6)
    for c in range(KB // rc):
        ki = ki_ref[c * rc:(c + 1) * rc, :]
        sc = jnp.zeros((rc, QB), F32)
        for hh in range(IDX_HEADS):
            lg = _mm(ki, iqt_ref[hh * IDX_DIM:(hh + 1) * IDX_DIM, :])
            sc = sc + jnp.maximum(lg, 0.0) * wit_ref[hh:hh + 1, :]
        s_chunk = lax.shift_right_logical(kb * KB + c * rc + lax.broadcasted_iota(I32, (rc, QB), 0), 6)
        bits = lax.bitcast_convert_type(sc, I32)
        sign = lax.shift_right_arithmetic(bits, 31)
        key = (bits ^ (sign & INT_MAX)) - sign
        key = jnp.where(s_chunk <= t_chunk, key, INT_MIN)
        rows = pl.ds(pl.multiple_of(kb * KB + c * rc, rc), rc)
        hi_ref[rows, :] = lax.shift_right_arithmetic(key, 16).astype(I16)
        lo_ref[rows, :] = ((key & 0xFFFF) + I16_MIN).astype(I16)
        g0 = (c * rc) % topk
        gm_ref[g0:g0 + rc, :] = jnp.maximum(gm_ref[g0:g0 + rc, :], key)

    @pl.when(kb == last)
    def _():
        nblk = last + 1
        slab = 32
        one, zero = jnp.int16(1), jnp.int16(0)

        def fold(m, reduce):
            part = m[0:slab]
            for r in range(1, KB // slab):
                part = reduce(part, m[r * slab:(r + 1) * slab])
            return part

        def count(pred):
            def body(i, acc):
                return acc + fold(pred(i), jnp.add)
            acc = lax.fori_loop(0, nblk, body, jnp.zeros((slab, QB), I16))
            return jnp.sum(acc.astype(I32), axis=0, keepdims=True)

        def count_ge(ref, thr):
            t16 = thr.astype(I16)
            return count(lambda i: jnp.where(ref[rows_of(i), :] >= t16, one, zero))

        def search(ref, target, lo, c_lo, hi, c_hi):
            def is_open(lo, c_lo, hi):
                return jnp.where(c_lo > target, jnp.where(hi != lo + 1, 1, 0), 0)

            def step(_, st):
                lo, c_lo, hi, c_hi = st
                open_ = is_open(lo, c_lo, hi) > 0
                mid = lax.shift_right_arithmetic(lo + hi, 1)
                c = count_ge(ref, mid)
                up = jnp.logical_and(open_, c >= target)
                dn = jnp.logical_and(open_, c < target)
                return jnp.where(up, mid, lo), jnp.where(up, c, c_lo), jnp.where(dn, mid, hi), jnp.where(dn, c, c_hi)

            width = jnp.where(is_open(lo, c_lo, hi) > 0, hi - lo, 1)
            n_halvings = jnp.max(32 - lax.clz(width - 1))
            lo, c_lo, hi, c_hi = lax.fori_loop(0, n_halvings, step, (lo, c_lo, hi, c_hi))
            return lo, c_lo, c_hi

        zeros = jnp.zeros((1, QB), I32)
        gm = gm_ref[...]
        g_lo = jnp.maximum(lax.shift_right_arithmetic(jnp.min(gm, axis=0, keepdims=True), 16), I16_MIN + 1)
        g_hi = lax.shift_right_arithmetic(jnp.max(gm, axis=0, keepdims=True), 16) + 1
        hstar, ch_ge, ch_gt = search(hi_ref, topk, g_lo, count_ge(hi_ref, g_lo), g_hi, zeros)
        split = ch_ge > topk
        h16 = hstar.astype(I16)

        def low_half():
            def build(i, carry):
                l2_ref[rows_of(i), :] = jnp.where(hi_ref[rows_of(i), :] == h16, lo_ref[rows_of(i), :],
                                                  jnp.int16(I16_MIN))
                return carry
            lax.fori_loop(0, nblk, build, 0)
            target = jnp.where(split, topk - ch_gt, INT_MAX)
            return search(l2_ref, target, jnp.full((1, QB), I16_MIN, I32), ch_ge - ch_gt,
                          jnp.full((1, QB), I16_MAX + 1, I32), zeros)

        any_split = jnp.max(jnp.where(split, 1, 0)) > 0
        lstar, cl_ge, cl_gt = lax.cond(any_split, low_half,
                                       lambda: (jnp.full((1, QB), I16_MIN, I32), zeros, zeros))
        lstar = jnp.where(split, lstar, I16_MIN)
        l16 = lstar.astype(I16)
        c_ge = jnp.where(split, ch_gt + cl_ge, ch_ge)
        c_gt = jnp.where(split, ch_gt + cl_gt, ch_gt)
        excess = c_ge > topk
        need = jnp.where(excess, topk - c_gt, S + 1)
        row16 = lax.broadcasted_iota(I32, (KB, QB), 0).astype(I16)

        def before(i, j):
            return jnp.where(row16 + (i * KB).astype(I16) < j.astype(I16), one, zero)

        def tied(i, then):
            return jnp.where(hi_ref[rows_of(i), :] == h16, jnp.where(lo_ref[rows_of(i), :] == l16, then, zero), zero)

        def tie_cut():
            def step(_, c):
                lo, hi = c
                mid = lax.shift_right_arithmetic(lo + hi, 1)
                ok = count(lambda i: tied(i, before(i, mid))) >= need
                return jnp.where(ok, lo, mid + 1), jnp.where(ok, mid, hi)
            n_steps = max(1, math.ceil(math.log2(S + 1)))
            lo, _hi = lax.fori_loop(0, n_steps, step, (zeros, jnp.full((1, QB), S, I32)))
            return lo

        any_excess = jnp.max(jnp.where(excess, 1, 0)) > 0
        jcut = lax.cond(any_excess, tie_cut, lambda: jnp.full((1, QB), S, I32))

        def write(i, carry):
            hi_t, lo_t = hi_ref[rows_of(i), :], lo_ref[rows_of(i), :]
            in_bucket = jnp.where(lo_t > l16, one, jnp.where(lo_t == l16, before(i, jcut), zero))
            sel = jnp.where(hi_t > h16, one, jnp.where(hi_t == h16, in_bucket, zero))
            mask_ref[rows_of(i), :] = sel.astype(mask_ref.dtype)
            return carry

        lax.fori_loop(0, nblk, write, 0)

        def clear(i, carry):
            mask_ref[rows_of(i), :] = jnp.zeros((KB, QB), mask_ref.dtype)
            return carry

        lax.fori_loop(nblk, S // KB, clear, 0)


def _dsa_select(iqt, wit, ki, B, S, QB, KB):
    nq, nk = S // QB, S // KB
    topk = min(TOPK_MAX, S // 4)
    assert KB % topk == 0
    qs, ks = _causal_steps(S, QB, KB)
    return pl.pallas_call(
        functools.partial(_select_kernel, QB=QB, KB=KB, S=S, topk=topk),
        out_shape=jax.ShapeDtypeStruct((B, S, S), jnp.int8),
        grid_spec=pltpu.PrefetchScalarGridSpec(
            num_scalar_prefetch=2,
            grid=(B, qs.shape[0]),
            in_specs=[
                pl.BlockSpec((IDX_Q_W, QB), lambda b, s, qs, ks: (0, b * nq + qs[s])),
                pl.BlockSpec((IDX_HEADS, QB), lambda b, s, qs, ks: (0, b * nq + qs[s])),
                pl.BlockSpec((KB, IDX_DIM), lambda b, s, qs, ks: (b * nk + ks[s], 0)),
            ],
            out_specs=pl.BlockSpec((None, S, QB), lambda b, s, qs, ks: (b, 0, qs[s])),
            scratch_shapes=[pltpu.VMEM((S, QB), I16), pltpu.VMEM((S, QB), I16), pltpu.VMEM((S, QB), I16),
                            pltpu.VMEM((topk, QB), I32)],
        ),
        compiler_params=_params(("parallel", "arbitrary")),
        name="dsa_select",
    )(qs, ks, iqt, wit, ki)


LOGIT_SAFE = 120.0


def _attn_kernel(qb_ref, kb_ref, small_ref, q_ref, k_ref, vt_ref, m_ref, o_ref, acc_ref, run_ref, *, QB, KB):
    qb = qb_ref[pl.program_id(1)]
    kb = kb_ref[pl.program_id(1)]
    last = _last_kb(qb, QB, KB)

    @pl.when(kb == 0)
    def _():
        acc_ref[...] = jnp.zeros_like(acc_ref)
        run_ref[...] = jnp.full(run_ref.shape, -1e30, F32)

    head = lambda hh: slice(hh * DSA_HEAD_DIM, (hh + 1) * DSA_HEAD_DIM)
    qk_dot = lambda hh: _mm_nt(k_ref[:, head(hh)], q_ref[:, head(hh)])

    @pl.when(small_ref[0] == 1)
    def _():
        mb = m_ref[...].astype(MXU_DTYPE)
        lg = qk_dot(0)
        for hh in range(DSA_HEADS):
            lg_next = qk_dot(hh + 1) if hh + 1 < DSA_HEADS else None
            p = jnp.exp2(lg).astype(MXU_DTYPE) * mb
            acc_ref[hh] += jnp.dot(vt_ref[hh], p, preferred_element_type=F32)
            lg = lg_next

    @pl.when(small_ref[0] != 1)
    def _():
        selected = m_ref[...].astype(F32) > 0.0

        def logits(hh):
            lg = jnp.where(selected, qk_dot(hh), -jnp.inf)
            return lg, jnp.max(lg, axis=0, keepdims=True)

        nxt = logits(0)
        for hh in range(DSA_HEADS):
            lg, top = nxt
            nxt = logits(hh + 1) if hh + 1 < DSA_HEADS else None
            run_old = run_ref[hh]
            run_new = jnp.maximum(run_old, top)
            p = jnp.exp2(lg - run_new).astype(MXU_DTYPE)
            acc_ref[hh] = acc_ref[hh] * jnp.exp2(run_old - run_new) + jnp.dot(vt_ref[hh], p,
                                                                              preferred_element_type=F32)
            run_ref[hh] = run_new

    @pl.when(kb == last)
    def _():
        for hh in range(DSA_HEADS):
            acc = acc_ref[hh]
            o = acc[:DSA_HEAD_DIM] / acc[DSA_HEAD_DIM:DSA_HEAD_DIM + 1]
            o_ref[:, hh * DSA_HEAD_DIM:(hh + 1) * DSA_HEAD_DIM] = o.T.astype(o_ref.dtype)


def _dsa_attn(qk, vt, mask, logit_bound, B, S, QB, KB):
    T = B * S
    nq, nk = S // QB, S // KB
    W = DSA_HEADS * DSA_HEAD_DIM
    qs, ks = _causal_steps(S, QB, KB)
    small = (logit_bound <= LOGIT_SAFE).astype(I32).reshape(1)
    return pl.pallas_call(
        functools.partial(_attn_kernel, QB=QB, KB=KB),
        out_shape=jax.ShapeDtypeStruct((T, W), MXU_DTYPE),
        grid_spec=pltpu.PrefetchScalarGridSpec(
            num_scalar_prefetch=3,
            grid=(B, qs.shape[0]),
            in_specs=[
                pl.BlockSpec((QB, W), lambda b, s, qs, ks, sm: (b * nq + qs[s], 0)),
                pl.BlockSpec((KB, W), lambda b, s, qs, ks, sm: (b * nk + ks[s], 1)),
                pl.BlockSpec((None, DSA_HEADS, VT_ROWS, KB), lambda b, s, qs, ks, sm: (b, 0, 0, ks[s])),
                pl.BlockSpec((None, KB, QB), lambda b, s, qs, ks, sm: (b, ks[s], qs[s])),
            ],
            out_specs=pl.BlockSpec((QB, W), lambda b, s, qs, ks, sm: (b * nq + qs[s], 0)),
            scratch_shapes=[pltpu.VMEM((DSA_HEADS, VT_ROWS, QB), F32), pltpu.VMEM((DSA_HEADS, 1, QB), F32)],
        ),
        compiler_params=_params(("parallel", "arbitrary")),
        name="dsa_attn",
    )(qs, ks, small, qk, qk, vt, mask)


def _post_kernel(og_ref, od_ref, ga_ref, gb_ref, x_ref, pa_ref, pb_ref, wo_ref, g2_ref, wr_ref, br_ref,
                 x1_ref, h2_ref, route_ref):
    sig = lambda v: 1.0 / (1.0 + jnp.exp(-v))
    a = jnp.dot(og_ref[...], pa_ref[...], preferred_element_type=F32)
    b = jnp.dot(od_ref[...], pb_ref[...], preferred_element_type=F32)
    mix = sig(ga_ref[...].astype(F32)) * a + sig(gb_ref[...].astype(F32)) * b
    x1 = x_ref[...] + _mm(mix, wo_ref[...])
    x1_ref[...] = x1
    ms = jnp.mean(x1 * x1, axis=-1, keepdims=True)
    h2 = x1 * lax.rsqrt(ms + EPS) * g2_ref[...]
    h2_ref[...] = h2.astype(h2_ref.dtype)

    logits = _dot3(h2, wr_ref[...]) + br_ref[...]
    tm = logits.shape[0]
    lane = lax.broadcasted_iota(I32, (tm, LANES), 1).astype(F32)
    neg = -jnp.inf
    far = float(2 * LANES)
    rmax = lambda v: jnp.max(v, axis=1, keepdims=True)
    rmin = lambda v: jnp.min(v, axis=1, keepdims=True)
    gl = jnp.where(lane >= N_EXPERTS, jnp.where(lane < N_EXPERTS + N_GROUPS, logits, neg), neg)
    gmax = rmax(gl)
    g_w = 1.0 / jnp.sum(jnp.exp(gl - gmax), axis=1, keepdims=True)
    g_idx = rmin(jnp.where(gl == gmax, lane, far)) - N_EXPERTS
    e_lo = g_idx * EXPERTS_PER_GROUP
    el = jnp.where(lane >= e_lo, jnp.where(lane < e_lo + EXPERTS_PER_GROUP, logits, neg), neg)
    m1 = rmax(el)
    e1 = rmin(jnp.where(el == m1, lane, far))
    el2 = jnp.where(lane == e1, neg, el)
    m2 = rmax(el2)
    e2 = rmin(jnp.where(el2 == m2, lane, far))
    p2 = jnp.exp(m2 - m1)
    w1 = g_w / (1.0 + p2)
    route_ref[...] = jnp.where(lane == 0.0, e1, jnp.where(lane == 1.0, e2, jnp.where(
        lane == 2.0, w1, jnp.where(lane == 3.0, w1 * p2, 0.0))))


def _post(og, od, zm, x2, pa, pb, wo, g2, wr, br, tm):
    T, D = x2.shape
    row = lambda i: (i, 0)
    full = lambda i: (0, 0)
    return pl.pallas_call(
        _post_kernel,
        out_shape=(
            jax.ShapeDtypeStruct((T, D), F32),
            jax.ShapeDtypeStruct((T, D), F32),
            jax.ShapeDtypeStruct((T, LANES), F32),
        ),
        grid=(T // tm,),
        in_specs=[
            pl.BlockSpec((tm, D), row),
            pl.BlockSpec((tm, D), row),
            pl.BlockSpec((tm, D), lambda i: (i, 3)),
            pl.BlockSpec((tm, D), lambda i: (i, 4)),
            pl.BlockSpec((tm, D), row),
            pl.BlockSpec((D, D), full),
            pl.BlockSpec((D, D), full),
            pl.BlockSpec((D, D), full),
            pl.BlockSpec((1, D), full),
            pl.BlockSpec((D, LANES), full),
            pl.BlockSpec((1, LANES), full),
        ],
        out_specs=(pl.BlockSpec((tm, D), row), pl.BlockSpec((tm, D), row), pl.BlockSpec((tm, LANES), row)),
        compiler_params=_params(("parallel",)),
        name="merge_router",
    )(og, od, zm, zm, x2, pa, pb, wo, g2, wr, br)


MOE_TM = 256


def _gather_combine_kernel(idx_ref, nxt_ref, x1_ref, route_ref, src_hbm, o_ref, buf_ref, sem):
    i = pl.program_id(0)
    n = pl.num_programs(0)
    tm = o_ref.shape[0]
    slot = lax.rem(i, 2)

    def fetch(ids_ref, s):
        def start(r, carry):
            for k in range(2):
                pltpu.make_async_copy(src_hbm.at[pl.ds(ids_ref[0, 0, 2 * r + k], 1), :],
                                      buf_ref.at[s, pl.ds(k * tm + r, 1), :], sem.at[s]).start(priority=k)
            return carry
        lax.fori_loop(0, tm, start, 0, unroll=8)

    @pl.when(i == 0)
    def _():
        fetch(idx_ref, 0)

    @pl.when(i + 1 < n)
    def _():
        fetch(nxt_ref, 1 - slot)

    pltpu.make_async_copy(src_hbm.at[pl.ds(0, 2 * tm), :], buf_ref.at[slot], sem.at[slot]).wait()
    r = route_ref[...]
    lane = lax.broadcasted_iota(I32, r.shape, 1)
    w1 = jnp.sum(jnp.where(lane == 2, r, 0.0), axis=1, keepdims=True)
    w2 = jnp.sum(jnp.where(lane == 3, r, 0.0), axis=1, keepdims=True)
    o_ref[...] = x1_ref[...] + (w1 * buf_ref[slot, :tm, :] + w2 * buf_ref[slot, tm:, :])


def _gather_combine(x1, ys, pos, route, tm):
    T, D = x1.shape
    nt = T // tm
    idx3 = pos.reshape(nt, 1, 2 * tm)
    row = lambda i: (i, 0)
    return pl.pallas_call(
        _gather_combine_kernel,
        out_shape=jax.ShapeDtypeStruct((T, D), F32),
        grid=(nt,),
        in_specs=[
            pl.BlockSpec((1, 1, 2 * tm), lambda i: (i, 0, 0), memory_space=pltpu.SMEM),
            pl.BlockSpec((1, 1, 2 * tm), lambda i: (jnp.minimum(i + 1, nt - 1), 0, 0), memory_space=pltpu.SMEM),
            pl.BlockSpec((tm, D), row),
            pl.BlockSpec((tm, LANES), row),
            pl.BlockSpec(memory_space=pl.ANY),
        ],
        out_specs=pl.BlockSpec((tm, D), row),
        scratch_shapes=[pltpu.VMEM((2, 2 * tm, D), ys.dtype), pltpu.SemaphoreType.DMA((2,))],
        compiler_params=_params(("arbitrary",)),
        name="moe_gather_combine",
    )(idx3, idx3, x1, route, ys)


def _scatter_kernel(idx_ref, x_ref, o_hbm, sem, *, fanout):
    tm = x_ref.shape[0]

    def start(r, carry):
        for k in range(fanout):
            dst = idx_ref[0, 0, fanout * r + k]
            pltpu.make_async_copy(x_ref.at[pl.ds(r, 1), :], o_hbm.at[pl.ds(dst, 1), :], sem).start(priority=k)
        return carry

    lax.fori_loop(0, tm, start, 0, unroll=8)
    for k in range(fanout):
        pltpu.make_async_copy(x_ref, o_hbm.at[pl.ds(0, tm), :], sem).wait()


def _row_scatter(x, idx, fanout, tm, name):
    n, D = x.shape
    return pl.pallas_call(
        functools.partial(_scatter_kernel, fanout=fanout),
        out_shape=jax.ShapeDtypeStruct((fanout * n, D), x.dtype),
        grid=(n // tm,),
        in_specs=[
            pl.BlockSpec((1, 1, fanout * tm), lambda i: (i, 0, 0), memory_space=pltpu.SMEM),
            pl.BlockSpec((tm, D), lambda i: (i, 0)),
        ],
        out_specs=pl.BlockSpec(memory_space=pl.ANY),
        scratch_shapes=[pltpu.SemaphoreType.DMA(())],
        compiler_params=_params(("arbitrary",)),
        name=name,
    )(idx.reshape(n // tm, 1, fanout * tm), x)


def _route_plan(route, tm):
    n = 2 * route.shape[0]
    n_tiles = n // tm
    ids = route[:, :2].astype(I32).reshape(n)
    onehot = (ids[:, None] == jnp.arange(N_EXPERTS, dtype=I32)[None, :]).astype(F32).reshape(n_tiles, tm, N_EXPERTS)
    tril = (jnp.arange(tm)[:, None] >= jnp.arange(tm)[None, :]).astype(F32)
    in_tile = jnp.einsum("rc,tce->tre", tril, onehot)
    per_tile = in_tile[:, -1, :]
    before = jnp.cumsum(per_tile, axis=0) - per_tile
    counts = jnp.sum(per_tile, axis=0).astype(I32)
    starts = (jnp.cumsum(counts) - counts).astype(I32)
    rank = jnp.sum(onehot * (in_tile + before[:, None, :] - 1.0), axis=2).reshape(n)
    pos = (jnp.sum(onehot.reshape(n, N_EXPERTS) * starts[None, :].astype(F32), axis=1) + rank).astype(I32)
    bounds = jnp.sort(jnp.concatenate([jnp.arange(n_tiles, dtype=I32) * tm, starts]))
    ends = jnp.concatenate([bounds[1:], jnp.full((1,), n, I32)])
    seg_tile = jnp.minimum(bounds // tm, n_tiles - 1)
    seg_exp = jnp.clip(jnp.sum((starts[None, :] <= bounds[:, None]).astype(I32), axis=1) - 1, 0, N_EXPERTS - 1)
    seg_first = jnp.concatenate([jnp.ones((1,), I32), (seg_tile[1:] != seg_tile[:-1]).astype(I32)])
    return pos, (seg_tile, seg_exp, bounds - seg_tile * tm, ends - seg_tile * tm, seg_first)


def _experts_kernel(tile_ref, exp_ref, lo_ref, hi_ref, first_ref, x_ref, wg_ref, wu_ref, wd_ref, o_ref):
    s = pl.program_id(0)
    lo, hi = lo_ref[s], hi_ref[s]

    @pl.when(first_ref[s] == 1)
    def _():
        o_ref[...] = jnp.zeros_like(o_ref)

    @pl.when(hi > lo)
    def _():
        x = x_ref[...].astype(MXU_DTYPE)
        a = jnp.dot(x, wg_ref[...], preferred_element_type=F32)
        u = jnp.dot(x, wu_ref[...], preferred_element_type=F32)
        y = _mm(a / (1.0 + jnp.exp(-a)) * u, wd_ref[...])
        row = lax.broadcasted_iota(I32, y.shape, 0)
        o_ref[...] += jnp.where(row >= lo, jnp.where(row < hi, y, 0.0), 0.0)


def _experts(xs, segs, wg, wu, wd, tm):
    n, D = xs.shape
    F = wg.shape[2]
    tile_of = lambda s, tile, exp, lo, hi, first: (tile[s], 0)
    w_of = lambda s, tile, exp, lo, hi, first: (exp[s], 0, 0)
    return pl.pallas_call(
        _experts_kernel,
        out_shape=jax.ShapeDtypeStruct((n, D), F32),
        grid_spec=pltpu.PrefetchScalarGridSpec(
            num_scalar_prefetch=5,
            grid=(segs[0].shape[0],),
            in_specs=[
                pl.BlockSpec((tm, D), tile_of),
                pl.BlockSpec((None, D, F), w_of),
                pl.BlockSpec((None, D, F), w_of),
                pl.BlockSpec((None, F, D), w_of),
            ],
            out_specs=pl.BlockSpec((tm, D), tile_of),
        ),
        compiler_params=_params(("arbitrary",)),
        name="experts",
    )(*segs, xs, wg, wu, wd)


def _moe(h2, route, x1, wg, wu, wd):
    T, D = x1.shape
    tm = min(MOE_TM, T)
    pos, segs = _route_plan(route, tm)
    xs = _row_scatter(h2, pos, 2, tm, "moe_scatter")
    ys = _experts(xs, segs, wg, wu, wd, tm)
    return _gather_combine(x1, ys, pos, route, tm)


def _pad_cols(w, n):
    return jnp.pad(w, ((0, 0), (0, n - w.shape[1])))


def _layer(x2, B, S, p):
    T, D = x2.shape
    assert D == 2 * GLA_HEADS * GLA_DK == GLA_HEADS * GLA_DV == DSA_HEADS * DSA_HEAD_DIM
    assert S % 512 == 0
    cd = MXU_DTYPE
    qk_w, v_w, dsa_w = GLA_HEADS * GLA_DK, GLA_HEADS * GLA_DV, DSA_HEADS * DSA_HEAD_DIM
    splits = (qk_w, qk_w, v_w, GLA_GATE_RANK, v_w, dsa_w, dsa_w, dsa_w, IDX_Q_W, IDX_DIM, IDX_HEADS, D, D)
    offs = [0]
    for s in splits:
        offs.append(offs[-1] + s)
    w_in = p["w_in"]
    (w_gq, w_gk, w_gv, w_ga, w_gr, w_dq, w_dk, w_dv, w_iq, w_ik, w_iw, w_ta, w_tb) = [
        w_in[:, offs[i]:offs[i + 1]] for i in range(len(splits))]

    w_main = jnp.concatenate([w_gq, w_gk, w_gv, w_gr, w_ta, w_tb], axis=1).astype(cd)
    w_qk = jnp.concatenate([w_dq, w_dk], axis=1).astype(cd)
    g_qk = jnp.concatenate([jnp.tile(p["dsa_q_norm_g"] * (DSA_HEAD_DIM ** -0.5 * math.log2(math.e)), DSA_HEADS),
                            jnp.tile(p["dsa_k_norm_g"], DSA_HEADS)]).reshape(1, 2 * dsa_w).astype(F32)
    w_vt = w_dv.T.astype(cd)
    w_idx = jnp.concatenate([_pad_cols(w_ik, LANES), _pad_cols(w_ga, LANES)], axis=1).astype(cd)
    w_idxt = jnp.pad(jnp.concatenate([w_iq, w_iw], axis=1).T, ((0, 16 - IDX_HEADS), (0, 0))).astype(cd)

    tm = min(1024, T)
    h = _rmsnorm(x2, p["norm1_g"], tm)
    zm = _matmul(h, w_main, tm, 1024, cd)
    qk = _qk_proj(h, w_qk, g_qk, tm)
    vt = _vt_proj(h, w_vt, B, S, min(512, S))
    iqt, ki, ga, wit = _idx_proj(h, w_idx, w_idxt, p["idx_k_ln_g"].reshape(1, -1), p["idx_k_ln_b"].reshape(1, -1),
                                 min(512, T))

    o_gla = _gla(zm, ga, p["gla_w_a2"], p["gla_b_a"].reshape(1, -1), p["gla_norm_g"].reshape(1, -1), B, S, 512)

    QB, KB = 256, min(1024, S)
    mask = _dsa_select(iqt, wit, ki, B, S, QB, KB)
    logit_bound = (DSA_HEAD_DIM * jnp.max(jnp.abs(g_qk[0, :dsa_w])) * jnp.max(jnp.abs(g_qk[0, dsa_w:])))
    o_dsa = _dsa_attn(qk, vt, mask, logit_bound, B, S, 2 * QB, KB)

    w_r = _pad_cols(jnp.concatenate([p["w_router_expert"], p["w_router_group"]], axis=1), LANES)
    b_r = _pad_cols(jnp.concatenate([p["b_router_expert"], p["b_router_group"]]).reshape(1, -1), LANES)
    x1, h2, route = _post(o_gla, o_dsa, zm, x2, p["w_branch_gla"].astype(cd), p["w_branch_dsa"].astype(cd),
                          p["w_out"].astype(cd), p["norm2_g"].reshape(1, -1), w_r, b_r, min(512, T))
    return _moe(h2, route, x1, p["w_exp_gate"].astype(cd), p["w_exp_up"].astype(cd), p["w_exp_down"].astype(cd))


def kernel(x, norm1_g, w_in, gla_w_a2, gla_b_a, gla_norm_g, dsa_q_norm_g, dsa_k_norm_g, idx_k_ln_g, idx_k_ln_b,
           w_branch_gla, w_branch_dsa, w_out, norm2_g, w_router_group, b_router_group, w_router_expert,
           b_router_expert, w_exp_gate, w_exp_up, w_exp_down):
    B, S, D = x.shape
    stacked = dict(norm1_g=norm1_g, w_in=w_in, gla_w_a2=gla_w_a2, gla_b_a=gla_b_a, gla_norm_g=gla_norm_g,
                   dsa_q_norm_g=dsa_q_norm_g, dsa_k_norm_g=dsa_k_norm_g, idx_k_ln_g=idx_k_ln_g,
                   idx_k_ln_b=idx_k_ln_b, w_branch_gla=w_branch_gla, w_branch_dsa=w_branch_dsa, w_out=w_out,
                   norm2_g=norm2_g, w_router_group=w_router_group, b_router_group=b_router_group,
                   w_router_expert=w_router_expert, b_router_expert=b_router_expert, w_exp_gate=w_exp_gate,
                   w_exp_up=w_exp_up, w_exp_down=w_exp_down)
    x2 = x.reshape(B * S, D).astype(F32)
    for l in range(w_in.shape[0]):
        x2 = _layer(x2, B, S, {k: v[l] for k, v in stacked.items()})
    return x2.reshape(B, S, D).astype(x.dtype)
```

```python
import functools
import math

import jax
import jax.numpy as jnp
from jax import lax
from jax.experimental import pallas as pl
from jax.experimental.pallas import tpu as pltpu

F32 = jnp.float32
BF16 = jnp.bfloat16
I32 = jnp.int32
MXU_DTYPE = BF16

CHUNK = 64
EPS = 1e-6
GLA_HEADS, GLA_DK, GLA_DV = 4, 128, 256
GLA_GATE_RANK = 16
GLA_GATE_TEMP = 16.0
DSA_HEADS, DSA_HEAD_DIM = 8, 128
IDX_HEADS, IDX_DIM = 8, 64
TOPK_MAX = 256
N_GROUPS, EXPERTS_PER_GROUP = 4, 8
N_EXPERTS = N_GROUPS * EXPERTS_PER_GROUP
LANES = 128

INT_MIN = -(2 ** 31)
INT_MAX = 2 ** 31 - 1
I16 = jnp.int16
I16_MIN, I16_MAX = -(2 ** 15), 2 ** 15 - 1

VMEM_LIMIT = 56 * 1024 * 1024

_NT = (((1,), (1,)), ((), ()))
_TN = (((0,), (0,)), ((), ()))


def _params(sem):
    return pltpu.CompilerParams(dimension_semantics=sem, vmem_limit_bytes=VMEM_LIMIT)


def _mm(a, b):
    return jnp.dot(a.astype(MXU_DTYPE), b.astype(MXU_DTYPE), preferred_element_type=F32)


def _mm_nt(a, b):
    return lax.dot_general(a.astype(MXU_DTYPE), b.astype(MXU_DTYPE), _NT, preferred_element_type=F32)


def _mm_tn(a, b):
    return lax.dot_general(a.astype(MXU_DTYPE), b.astype(MXU_DTYPE), _TN, preferred_element_type=F32)


def _split(a):
    hi = a.astype(BF16)
    lo = (a - hi.astype(F32)).astype(BF16)
    return hi, lo


def _dot3(a, b):
    ah, al = _split(a)
    bh, bl = _split(b)
    d = lambda u, v: jnp.dot(u, v, preferred_element_type=F32)
    return d(ah, bh) + (d(ah, bl) + d(al, bh))


def _rmsnorm_kernel(x_ref, g_ref, o_ref):
    x = x_ref[...]
    ms = jnp.mean(x * x, axis=-1, keepdims=True)
    o_ref[...] = (x * lax.rsqrt(ms + EPS) * g_ref[...]).astype(o_ref.dtype)


def _rmsnorm(x2, g, tm):
    T, D = x2.shape
    return pl.pallas_call(
        _rmsnorm_kernel,
        out_shape=jax.ShapeDtypeStruct((T, D), MXU_DTYPE),
        grid=(T // tm,),
        in_specs=[pl.BlockSpec((tm, D), lambda i: (i, 0)), pl.BlockSpec((1, D), lambda i: (0, 0))],
        out_specs=pl.BlockSpec((tm, D), lambda i: (i, 0)),
        compiler_params=_params(("parallel",)),
        name="rmsnorm",
    )(x2, g.reshape(1, D))


def _matmul_kernel(h_ref, w_ref, o_ref):
    o_ref[...] = jnp.dot(h_ref[...], w_ref[...], preferred_element_type=F32).astype(o_ref.dtype)


def _matmul(h, w, tm, tn, out_dtype):
    T, D = h.shape
    N = w.shape[1]
    return pl.pallas_call(
        _matmul_kernel,
        out_shape=jax.ShapeDtypeStruct((T, N), out_dtype),
        grid=(T // tm, N // tn),
        in_specs=[pl.BlockSpec((tm, D), lambda i, j: (i, 0)), pl.BlockSpec((D, tn), lambda i, j: (0, j))],
        out_specs=pl.BlockSpec((tm, tn), lambda i, j: (i, j)),
        compiler_params=_params(("parallel", "arbitrary")),
        name="proj_main",
    )(h, w)


def _qk_kernel(h_ref, w_ref, g_ref, o_ref):
    z = jnp.dot(h_ref[...], w_ref[...], preferred_element_type=F32)
    for hh in range(DSA_HEADS):
        sl = slice(hh * DSA_HEAD_DIM, (hh + 1) * DSA_HEAD_DIM)
        zh = z[:, sl]
        ms = jnp.mean(zh * zh, axis=-1, keepdims=True)
        o_ref[:, sl] = (zh * lax.rsqrt(ms + EPS) * g_ref[:, sl]).astype(o_ref.dtype)


def _qk_proj(h, w, g, tm):
    T, D = h.shape
    W = DSA_HEADS * DSA_HEAD_DIM
    return pl.pallas_call(
        _qk_kernel,
        out_shape=jax.ShapeDtypeStruct((T, 2 * W), MXU_DTYPE),
        grid=(T // tm, 2),
        in_specs=[
            pl.BlockSpec((tm, D), lambda i, j: (i, 0)),
            pl.BlockSpec((D, W), lambda i, j: (0, j)),
            pl.BlockSpec((1, W), lambda i, j: (0, j)),
        ],
        out_specs=pl.BlockSpec((tm, W), lambda i, j: (i, j)),
        compiler_params=_params(("parallel", "arbitrary")),
        name="proj_qk",
    )(h, w, g)


VT_ONES = 16
VT_ROWS = DSA_HEAD_DIM + VT_ONES


def _vt_kernel(h_ref, wt_ref, o_ref):
    vt = lax.dot_general(wt_ref[...], h_ref[...], _NT, preferred_element_type=F32).astype(o_ref.dtype)
    ones = jnp.ones((VT_ONES, vt.shape[1]), o_ref.dtype)
    for hh in range(DSA_HEADS):
        o_ref[hh, :DSA_HEAD_DIM, :] = vt[hh * DSA_HEAD_DIM:(hh + 1) * DSA_HEAD_DIM]
        o_ref[hh, DSA_HEAD_DIM:, :] = ones


def _vt_proj(h, wt, B, S, tm):
    T, D = h.shape
    W = wt.shape[0]
    nt = S // tm
    return pl.pallas_call(
        _vt_kernel,
        out_shape=jax.ShapeDtypeStruct((B, DSA_HEADS, VT_ROWS, S), MXU_DTYPE),
        grid=(T // tm,),
        in_specs=[pl.BlockSpec((tm, D), lambda i: (i, 0)), pl.BlockSpec((W, D), lambda i: (0, 0))],
        out_specs=pl.BlockSpec((None, DSA_HEADS, VT_ROWS, tm), lambda i: (i // nt, 0, 0, i % nt)),
        compiler_params=_params(("parallel",)),
        name="proj_vt",
    )(h, wt)


IDX_Q_W = IDX_HEADS * IDX_DIM


def _idx_kernel(h_ref, w_ref, wt_ref, lng_ref, lnb_ref, iqt_ref, ki_ref, ga_ref, wit_ref):
    h = h_ref[...]
    z = jnp.dot(h, w_ref[...], preferred_element_type=F32)
    ik = z[:, :IDX_DIM]
    mu = jnp.mean(ik, axis=-1, keepdims=True)
    var = jnp.mean(jnp.square(ik - mu), axis=-1, keepdims=True)
    ki = (ik - mu) * lax.rsqrt(var + EPS) * lng_ref[...] + lnb_ref[...]
    ki_ref[...] = ki.astype(ki_ref.dtype)
    ga_ref[...] = z[:, LANES:LANES + GLA_GATE_RANK]
    zt = lax.dot_general(wt_ref[...], h, _NT, preferred_element_type=F32)
    iqt_ref[...] = zt[:IDX_Q_W].astype(iqt_ref.dtype)
    wit_ref[...] = zt[IDX_Q_W:IDX_Q_W + IDX_HEADS] * (IDX_HEADS ** -0.5 * IDX_DIM ** -0.5)


def _idx_proj(h, w, wt, lng, lnb, tm):
    T, D = h.shape
    return pl.pallas_call(
        _idx_kernel,
        out_shape=(
            jax.ShapeDtypeStruct((IDX_Q_W, T), MXU_DTYPE),
            jax.ShapeDtypeStruct((T, IDX_DIM), MXU_DTYPE),
            jax.ShapeDtypeStruct((T, GLA_GATE_RANK), F32),
            jax.ShapeDtypeStruct((IDX_HEADS, T), F32),
        ),
        grid=(T // tm,),
        in_specs=[
            pl.BlockSpec((tm, D), lambda i: (i, 0)),
            pl.BlockSpec(w.shape, lambda i: (0, 0)),
            pl.BlockSpec(wt.shape, lambda i: (0, 0)),
            pl.BlockSpec((1, IDX_DIM), lambda i: (0, 0)),
            pl.BlockSpec((1, IDX_DIM), lambda i: (0, 0)),
        ],
        out_specs=(
            pl.BlockSpec((IDX_Q_W, tm), lambda i: (0, i)),
            pl.BlockSpec((tm, IDX_DIM), lambda i: (i, 0)),
            pl.BlockSpec((tm, GLA_GATE_RANK), lambda i: (i, 0)),
            pl.BlockSpec((IDX_HEADS, tm), lambda i: (0, i)),
        ),
        compiler_params=_params(("parallel",)),
        name="proj_idx",
    )(h, w, wt, lng, lnb)


def _gla_kernel(q_ref, k_ref, v_ref, r_ref, ga_ref, wa2_ref, ba_ref, ng_ref, o_ref, st_ref, tot_ref, kd_ref,
                oraw_ref, *, n_chunks):
    tb = q_ref.shape[0]

    @pl.when(pl.program_id(1) == 0)
    def _():
        st_ref[...] = jnp.zeros_like(st_ref)

    x = _dot3(ga_ref[...], wa2_ref[...]) + ba_ref[...]
    la = (jnp.minimum(x, 0.0) - jnp.log(1.0 + jnp.exp(-jnp.abs(x)))) * (1.0 / GLA_GATE_TEMP)
    row = lax.broadcasted_iota(I32, (tb, tb), 0)
    col = lax.broadcasted_iota(I32, (tb, tb), 1)
    same = lax.shift_right_logical(row, 6) == lax.shift_right_logical(col, 6)
    ones_blk = jnp.where(same, 1.0, 0.0).astype(BF16)
    tril_blk = jnp.where(same, jnp.where(row >= col, 1.0, 0.0), 0.0).astype(BF16)
    la_hi, la_lo = _split(la)
    d = lambda u, v: jnp.dot(u, v, preferred_element_type=F32)
    tot = d(ones_blk, la_hi) + d(ones_blk, la_lo)
    cum = d(tril_blk, la_hi) + d(tril_blk, la_lo)
    tot_ref[...] = tot
    kd_ref[...] = (k_ref[...].astype(F32) * jnp.exp(tot - cum)).astype(kd_ref.dtype)

    heads = range(GLA_HEADS)
    ksl = lambda hh: slice(hh * GLA_DK, (hh + 1) * GLA_DK)
    vsl = lambda hh: slice(hh * GLA_DV, (hh + 1) * GLA_DV)

    def chunk(c, carry):
        rows = pl.ds(pl.multiple_of(c * CHUNK, CHUNK), CHUNK)
        first = pl.ds(pl.multiple_of(c * CHUNK, CHUNK), 1)
        upd = [_mm_tn(v_ref[rows, vsl(hh)], kd_ref[rows, ksl(hh)]) for hh in heads]
        st = [st_ref[hh] * jnp.exp(tot_ref[first, ksl(hh)]) + upd[hh] for hh in heads]
        for hh in heads:
            st_ref[hh] = st[hh]
        for hh in heads:
            oraw_ref[rows, vsl(hh)] = _mm_nt(q_ref[rows, ksl(hh)], st[hh])
        return carry

    lax.fori_loop(0, n_chunks, chunk, 0)

    ng = ng_ref[...]
    for hh in heads:
        o = oraw_ref[:, vsl(hh)] * (GLA_DK ** -0.5)
        ms = jnp.mean(o * o, axis=-1, keepdims=True)
        r = r_ref[:, vsl(hh)].astype(F32)
        o_ref[:, vsl(hh)] = (o * lax.rsqrt(ms + EPS) * ng * (r / (1.0 + jnp.exp(-r)))).astype(o_ref.dtype)


def _gla(zm, ga, wa2, ba, ng, B, S, tb):
    T = B * S
    nb = S // tb
    qk_w = GLA_HEADS * GLA_DK
    v_w = GLA_HEADS * GLA_DV
    tok = lambda b, i: b * nb + i
    return pl.pallas_call(
        functools.partial(_gla_kernel, n_chunks=tb // CHUNK),
        out_shape=jax.ShapeDtypeStruct((T, v_w), MXU_DTYPE),
        grid=(B, nb),
        in_specs=[
            pl.BlockSpec((tb, qk_w), lambda b, i: (tok(b, i), 0)),
            pl.BlockSpec((tb, qk_w), lambda b, i: (tok(b, i), 1)),
            pl.BlockSpec((tb, v_w), lambda b, i: (tok(b, i), 1)),
            pl.BlockSpec((tb, v_w), lambda b, i: (tok(b, i), 2)),
            pl.BlockSpec((tb, GLA_GATE_RANK), lambda b, i: (tok(b, i), 0)),
            pl.BlockSpec((GLA_GATE_RANK, qk_w), lambda b, i: (0, 0)),
            pl.BlockSpec((1, qk_w), lambda b, i: (0, 0)),
            pl.BlockSpec((1, GLA_DV), lambda b, i: (0, 0)),
        ],
        out_specs=pl.BlockSpec((tb, v_w), lambda b, i: (tok(b, i), 0)),
        scratch_shapes=[pltpu.VMEM((GLA_HEADS, GLA_DV, GLA_DK), F32), pltpu.VMEM((tb, qk_w), F32),
                        pltpu.VMEM((tb, qk_w), MXU_DTYPE), pltpu.VMEM((tb, v_w), F32)],
        compiler_params=_params(("parallel", "arbitrary")),
        name="gla",
    )(zm, zm, zm, zm, ga, wa2, ba, ng)


def _last_kb(qb, QB, KB):
    return ((qb + 1) * QB - 1) // KB


def _causal_steps(S, QB, KB):
    pairs = [(q, k) for q in range(S // QB) for k in range(_last_kb(q, QB, KB) + 1)]
    qs, ks = zip(*pairs)
    return jnp.asarray(qs, I32), jnp.asarray(ks, I32)


def _select_kernel(qb_ref, kb_ref, iqt_ref, wit_ref, ki_ref, mask_ref, hi_ref, lo_ref, gm_ref, *,
                   QB, KB, S, topk):
    qb = qb_ref[pl.program_id(1)]
    kb = kb_ref[pl.program_id(1)]
    last = _last_kb(qb, QB, KB)

    def rows_of(i):
        return pl.ds(pl.multiple_of(i * KB, KB), KB)

    @pl.when(kb == 0)
    def _():
        gm_ref[...] = jnp.full(gm_ref.shape, INT_MIN, I32)

    rc = 32 * 1024 // QB
    assert topk % rc == 0 and KB % rc == 0
    t_chunk = lax.shift_right_logical(qb * QB + lax.broadcasted_iota(I32, (rc, QB), 1), 6)
    for c in range(KB // rc):
        ki = ki_ref[c * rc:(c + 1) * rc, :]
        sc = jnp.zeros((rc, QB), F32)
        for hh in range(IDX_HEADS):
            lg = _mm(ki, iqt_ref[hh * IDX_DIM:(hh + 1) * IDX_DIM, :])
            sc = sc + jnp.maximum(lg, 0.0) * wit_ref[hh:hh + 1, :]
        s_chunk = lax.shift_right_logical(kb * KB + c * rc + lax.broadcasted_iota(I32, (rc, QB), 0), 6)
        bits = lax.bitcast_convert_type(sc, I32)
        sign = lax.shift_right_arithmetic(bits, 31)
        key = (bits ^ (sign & INT_MAX)) - sign
        key = jnp.where(s_chunk <= t_chunk, key, INT_MIN)
        rows = pl.ds(pl.multiple_of(kb * KB + c * rc, rc), rc)
        hi_ref[rows, :] = lax.shift_right_arithmetic(key, 16).astype(I16)
        lo_ref[rows, :] = ((key & 0xFFFF) + I16_MIN).astype(I16)
        g0 = (c * rc) % topk
        gm_ref[g0:g0 + rc, :] = jnp.maximum(gm_ref[g0:g0 + rc, :], key)

    @pl.when(kb == last)
    def _():
        nblk = last + 1
        slab = 32
        one, zero = jnp.int16(1), jnp.int16(0)

        def fold(m, reduce):
            part = m[0:slab]
            for r in range(1, KB // slab):
                part = reduce(part, m[r * slab:(r + 1) * slab])
            return part

        def count(pred):
            def body(i, acc):
                return acc + fold(pred(i), jnp.add)
            acc = lax.fori_loop(0, nblk, body, jnp.zeros((slab, QB), I16))
            return jnp.sum(acc.astype(I32), axis=0, keepdims=True)

        def count_ge(ref, thr):
            t16 = thr.astype(I16)
            return count(lambda i: jnp.where(ref[rows_of(i), :] >= t16, one, zero))

        def search(ref, target, lo, c_lo, hi, c_hi):
            def is_open(lo, c_lo, hi):
                return jnp.where(c_lo > target, jnp.where(hi != lo + 1, 1, 0), 0)

            def step(_, st):
                lo, c_lo, hi, c_hi = st
                open_ = is_open(lo, c_lo, hi) > 0
                mid = lax.shift_right_arithmetic(lo + hi, 1)
                c = count_ge(ref, mid)
                up = jnp.logical_and(open_, c >= target)
                dn = jnp.logical_and(open_, c < target)
                return jnp.where(up, mid, lo), jnp.where(up, c, c_lo), jnp.where(dn, mid, hi), jnp.where(dn, c, c_hi)

            width = jnp.where(is_open(lo, c_lo, hi) > 0, hi - lo, 1)
            n_halvings = jnp.max(32 - lax.clz(width - 1))
            lo, c_lo, hi, c_hi = lax.fori_loop(0, n_halvings, step, (lo, c_lo, hi, c_hi))
            return lo, c_lo, c_hi

        zeros = jnp.zeros((1, QB), I32)
        gm = gm_ref[...]
        g_lo = jnp.maximum(lax.shift_right_arithmetic(jnp.min(gm, axis=0, keepdims=True), 16), I16_MIN + 1)
        g_hi = lax.shift_right_arithmetic(jnp.max(gm, axis=0, keepdims=True), 16) + 1
        hstar, ch_ge, ch_gt = search(hi_ref, topk, g_lo, count_ge(hi_ref, g_lo), g_hi, zeros)
        split = ch_ge > topk
        h16 = hstar.astype(I16)

        def low_half():
            def build(i, carry):
                lo_ref[rows_of(i), :] = jnp.where(hi_ref[rows_of(i), :] == h16, lo_ref[rows_of(i), :],
                                                  jnp.int16(I16_MIN))
                return carry
            lax.fori_loop(0, nblk, build, 0)
            target = jnp.where(split, topk - ch_gt, INT_MAX)
            return search(lo_ref, target, jnp.full((1, QB), I16_MIN, I32), ch_ge - ch_gt,
                          jnp.full((1, QB), I16_MAX + 1, I32), zeros)

        any_split = jnp.max(jnp.where(split, 1, 0)) > 0
        lstar, cl_ge, cl_gt = lax.cond(any_split, low_half,
                                       lambda: (jnp.full((1, QB), I16_MIN, I32), zeros, zeros))
        lstar = jnp.where(split, lstar, I16_MIN)
        l16 = lstar.astype(I16)
        c_ge = jnp.where(split, ch_gt + cl_ge, ch_ge)
        c_gt = jnp.where(split, ch_gt + cl_gt, ch_gt)
        excess = c_ge > topk
        need = jnp.where(excess, topk - c_gt, S + 1)
        row16 = lax.broadcasted_iota(I32, (KB, QB), 0).astype(I16)

        def before(i, j):
            return jnp.where(row16 + (i * KB).astype(I16) < j.astype(I16), one, zero)

        def tied(i, then):
            return jnp.where(hi_ref[rows_of(i), :] == h16, jnp.where(lo_ref[rows_of(i), :] == l16, then, zero), zero)

        def tie_cut():
            def step(_, c):
                lo, hi = c
                mid = lax.shift_right_arithmetic(lo + hi, 1)
                ok = count(lambda i: tied(i, before(i, mid))) >= need
                return jnp.where(ok, lo, mid + 1), jnp.where(ok, mid, hi)
            n_steps = max(1, math.ceil(math.log2(S + 1)))
            lo, _hi = lax.fori_loop(0, n_steps, step, (zeros, jnp.full((1, QB), S, I32)))
            return lo

        any_excess = jnp.max(jnp.where(excess, 1, 0)) > 0
        jcut = lax.cond(any_excess, tie_cut, lambda: jnp.full((1, QB), S, I32))

        def write(i, carry):
            hi_t, lo_t = hi_ref[rows_of(i), :], lo_ref[rows_of(i), :]
            in_bucket = jnp.where(lo_t > l16, one, jnp.where(lo_t == l16, before(i, jcut), zero))
            sel = jnp.where(hi_t > h16, one, jnp.where(hi_t == h16, in_bucket, zero))
            mask_ref[rows_of(i), :] = sel.astype(mask_ref.dtype)
            return carry

        lax.fori_loop(0, nblk, write, 0)

        def clear(i, carry):
            mask_ref[rows_of(i), :] = jnp.zeros((KB, QB), mask_ref.dtype)
            return carry

        lax.fori_loop(nblk, S // KB, clear, 0)


def _dsa_select(iqt, wit, ki, B, S, QB, KB):
    nq, nk = S // QB, S // KB
    topk = min(TOPK_MAX, S // 4)
    assert KB % topk == 0
    qs, ks = _causal_steps(S, QB, KB)
    return pl.pallas_call(
        functools.partial(_select_kernel, QB=QB, KB=KB, S=S, topk=topk),
        out_shape=jax.ShapeDtypeStruct((B, S, S), jnp.int8),
        grid_spec=pltpu.PrefetchScalarGridSpec(
            num_scalar_prefetch=2,
            grid=(B, qs.shape[0]),
            in_specs=[
                pl.BlockSpec((IDX_Q_W, QB), lambda b, s, qs, ks: (0, b * nq + qs[s])),
                pl.BlockSpec((IDX_HEADS, QB), lambda b, s, qs, ks: (0, b * nq + qs[s])),
                pl.BlockSpec((KB, IDX_DIM), lambda b, s, qs, ks: (b * nk + ks[s], 0)),
            ],
            out_specs=pl.BlockSpec((None, S, QB), lambda b, s, qs, ks: (b, 0, qs[s])),
            scratch_shapes=[pltpu.VMEM((S, QB), I16), pltpu.VMEM((S, QB), I16), pltpu.VMEM((topk, QB), I32)],
        ),
        compiler_params=_params(("parallel", "arbitrary")),
        name="dsa_select",
    )(qs, ks, iqt, wit, ki)


LOGIT_SAFE = 120.0


def _attn_kernel(qb_ref, kb_ref, small_ref, q_ref, k_ref, vt_ref, m_ref, o_ref, acc_ref, run_ref, *, QB, KB):
    qb = qb_ref[pl.program_id(1)]
    kb = kb_ref[pl.program_id(1)]
    last = _last_kb(qb, QB, KB)

    @pl.when(kb == 0)
    def _():
        acc_ref[...] = jnp.zeros_like(acc_ref)
        run_ref[...] = jnp.full(run_ref.shape, -1e30, F32)

    head = lambda hh: slice(hh * DSA_HEAD_DIM, (hh + 1) * DSA_HEAD_DIM)
    qk_dot = lambda hh: _mm_nt(k_ref[:, head(hh)], q_ref[:, head(hh)])

    @pl.when(small_ref[0] == 1)
    def _():
        mb = m_ref[...].astype(MXU_DTYPE)
        lg = qk_dot(0)
        for hh in range(DSA_HEADS):
            lg_next = qk_dot(hh + 1) if hh + 1 < DSA_HEADS else None
            p = jnp.exp2(lg).astype(MXU_DTYPE) * mb
            acc_ref[hh] += jnp.dot(vt_ref[hh], p, preferred_element_type=F32)
            lg = lg_next

    @pl.when(small_ref[0] != 1)
    def _():
        selected = m_ref[...].astype(F32) > 0.0

        def logits(hh):
            lg = jnp.where(selected, qk_dot(hh), -jnp.inf)
            return lg, jnp.max(lg, axis=0, keepdims=True)

        nxt = logits(0)
        for hh in range(DSA_HEADS):
            lg, top = nxt
            nxt = logits(hh + 1) if hh + 1 < DSA_HEADS else None
            run_old = run_ref[hh]
            run_new = jnp.maximum(run_old, top)
            p = jnp.exp2(lg - run_new).astype(MXU_DTYPE)
            acc_ref[hh] = acc_ref[hh] * jnp.exp2(run_old - run_new) + jnp.dot(vt_ref[hh], p,
                                                                              preferred_element_type=F32)
            run_ref[hh] = run_new

    @pl.when(kb == last)
    def _():
        for hh in range(DSA_HEADS):
            acc = acc_ref[hh]
            o = acc[:DSA_HEAD_DIM] / acc[DSA_HEAD_DIM:DSA_HEAD_DIM + 1]
            o_ref[:, hh * DSA_HEAD_DIM:(hh + 1) * DSA_HEAD_DIM] = o.T.astype(o_ref.dtype)


def _dsa_attn(qk, vt, mask, logit_bound, B, S, QB, KB):
    T = B * S
    nq, nk = S // QB, S // KB
    W = DSA_HEADS * DSA_HEAD_DIM
    qs, ks = _causal_steps(S, QB, KB)
    small = (logit_bound <= LOGIT_SAFE).astype(I32).reshape(1)
    return pl.pallas_call(
        functools.partial(_attn_kernel, QB=QB, KB=KB),
        out_shape=jax.ShapeDtypeStruct((T, W), MXU_DTYPE),
        grid_spec=pltpu.PrefetchScalarGridSpec(
            num_scalar_prefetch=3,
            grid=(B, qs.shape[0]),
            in_specs=[
                pl.BlockSpec((QB, W), lambda b, s, qs, ks, sm: (b * nq + qs[s], 0)),
                pl.BlockSpec((KB, W), lambda b, s, qs, ks, sm: (b * nk + ks[s], 1)),
                pl.BlockSpec((None, DSA_HEADS, VT_ROWS, KB), lambda b, s, qs, ks, sm: (b, 0, 0, ks[s])),
                pl.BlockSpec((None, KB, QB), lambda b, s, qs, ks, sm: (b, ks[s], qs[s])),
            ],
            out_specs=pl.BlockSpec((QB, W), lambda b, s, qs, ks, sm: (b * nq + qs[s], 0)),
            scratch_shapes=[pltpu.VMEM((DSA_HEADS, VT_ROWS, QB), F32), pltpu.VMEM((DSA_HEADS, 1, QB), F32)],
        ),
        compiler_params=_params(("parallel", "arbitrary")),
        name="dsa_attn",
    )(qs, ks, small, qk, qk, vt, mask)


def _post_kernel(og_ref, od_ref, ga_ref, gb_ref, x_ref, pa_ref, pb_ref, wo_ref, g2_ref, wr_ref, br_ref,
                 x1_ref, h2_ref, route_ref):
    sig = lambda v: 1.0 / (1.0 + jnp.exp(-v))
    a = jnp.dot(og_ref[...], pa_ref[...], preferred_element_type=F32)
    b = jnp.dot(od_ref[...], pb_ref[...], preferred_element_type=F32)
    mix = sig(ga_ref[...].astype(F32)) * a + sig(gb_ref[...].astype(F32)) * b
    x1 = x_ref[...] + _mm(mix, wo_ref[...])
    x1_ref[...] = x1
    ms = jnp.mean(x1 * x1, axis=-1, keepdims=True)
    h2 = x1 * lax.rsqrt(ms + EPS) * g2_ref[...]
    h2_ref[...] = h2.astype(h2_ref.dtype)

    logits = _dot3(h2, wr_ref[...]) + br_ref[...]
    tm = logits.shape[0]
    lane = lax.broadcasted_iota(I32, (tm, LANES), 1).astype(F32)
    neg = -jnp.inf
    far = float(2 * LANES)
    rmax = lambda v: jnp.max(v, axis=1, keepdims=True)
    rmin = lambda v: jnp.min(v, axis=1, keepdims=True)
    gl = jnp.where(lane >= N_EXPERTS, jnp.where(lane < N_EXPERTS + N_GROUPS, logits, neg), neg)
    gmax = rmax(gl)
    g_w = 1.0 / jnp.sum(jnp.exp(gl - gmax), axis=1, keepdims=True)
    g_idx = rmin(jnp.where(gl == gmax, lane, far)) - N_EXPERTS
    e_lo = g_idx * EXPERTS_PER_GROUP
    el = jnp.where(lane >= e_lo, jnp.where(lane < e_lo + EXPERTS_PER_GROUP, logits, neg), neg)
    m1 = rmax(el)
    e1 = rmin(jnp.where(el == m1, lane, far))
    el2 = jnp.where(lane == e1, neg, el)
    m2 = rmax(el2)
    e2 = rmin(jnp.where(el2 == m2, lane, far))
    p2 = jnp.exp(m2 - m1)
    w1 = g_w / (1.0 + p2)
    route_ref[...] = jnp.where(lane == 0.0, e1, jnp.where(lane == 1.0, e2, jnp.where(
        lane == 2.0, w1, jnp.where(lane == 3.0, w1 * p2, 0.0))))


def _post(og, od, zm, x2, pa, pb, wo, g2, wr, br, tm):
    T, D = x2.shape
    row = lambda i: (i, 0)
    full = lambda i: (0, 0)
    return pl.pallas_call(
        _post_kernel,
        out_shape=(
            jax.ShapeDtypeStruct((T, D), F32),
            jax.ShapeDtypeStruct((T, D), F32),
            jax.ShapeDtypeStruct((T, LANES), F32),
        ),
        grid=(T // tm,),
        in_specs=[
            pl.BlockSpec((tm, D), row),
            pl.BlockSpec((tm, D), row),
            pl.BlockSpec((tm, D), lambda i: (i, 3)),
            pl.BlockSpec((tm, D), lambda i: (i, 4)),
            pl.BlockSpec((tm, D), row),
            pl.BlockSpec((D, D), full),
            pl.BlockSpec((D, D), full),
            pl.BlockSpec((D, D), full),
            pl.BlockSpec((1, D), full),
            pl.BlockSpec((D, LANES), full),
            pl.BlockSpec((1, LANES), full),
        ],
        out_specs=(pl.BlockSpec((tm, D), row), pl.BlockSpec((tm, D), row), pl.BlockSpec((tm, LANES), row)),
        compiler_params=_params(("parallel",)),
        name="merge_router",
    )(og, od, zm, zm, x2, pa, pb, wo, g2, wr, br)


MOE_TM = 256


def _gather_combine_kernel(idx_ref, nxt_ref, x1_ref, route_ref, src_hbm, o_ref, buf_ref, sem):
    i = pl.program_id(0)
    n = pl.num_programs(0)
    tm = o_ref.shape[0]
    slot = lax.rem(i, 2)

    def fetch(ids_ref, s):
        def start(r, carry):
            for k in range(2):
                pltpu.make_async_copy(src_hbm.at[pl.ds(ids_ref[0, 0, 2 * r + k], 1), :],
                                      buf_ref.at[s, pl.ds(k * tm + r, 1), :], sem.at[s]).start(priority=k)
            return carry
        lax.fori_loop(0, tm, start, 0, unroll=8)

    @pl.when(i == 0)
    def _():
        fetch(idx_ref, 0)

    @pl.when(i + 1 < n)
    def _():
        fetch(nxt_ref, 1 - slot)

    pltpu.make_async_copy(src_hbm.at[pl.ds(0, 2 * tm), :], buf_ref.at[slot], sem.at[slot]).wait()
    r = route_ref[...]
    lane = lax.broadcasted_iota(I32, r.shape, 1)
    w1 = jnp.sum(jnp.where(lane == 2, r, 0.0), axis=1, keepdims=True)
    w2 = jnp.sum(jnp.where(lane == 3, r, 0.0), axis=1, keepdims=True)
    o_ref[...] = x1_ref[...] + (w1 * buf_ref[slot, :tm, :] + w2 * buf_ref[slot, tm:, :])


def _gather_combine(x1, ys, pos, route, tm):
    T, D = x1.shape
    nt = T // tm
    idx3 = pos.reshape(nt, 1, 2 * tm)
    row = lambda i: (i, 0)
    return pl.pallas_call(
        _gather_combine_kernel,
        out_shape=jax.ShapeDtypeStruct((T, D), F32),
        grid=(nt,),
        in_specs=[
            pl.BlockSpec((1, 1, 2 * tm), lambda i: (i, 0, 0), memory_space=pltpu.SMEM),
            pl.BlockSpec((1, 1, 2 * tm), lambda i: (jnp.minimum(i + 1, nt - 1), 0, 0), memory_space=pltpu.SMEM),
            pl.BlockSpec((tm, D), row),
            pl.BlockSpec((tm, LANES), row),
            pl.BlockSpec(memory_space=pl.ANY),
        ],
        out_specs=pl.BlockSpec((tm, D), row),
        scratch_shapes=[pltpu.VMEM((2, 2 * tm, D), ys.dtype), pltpu.SemaphoreType.DMA((2,))],
        compiler_params=_params(("arbitrary",)),
        name="moe_gather_combine",
    )(idx3, idx3, x1, route, ys)


def _scatter_kernel(idx_ref, x_ref, o_hbm, sem, *, fanout):
    tm = x_ref.shape[0]

    def start(r, carry):
        for k in range(fanout):
            dst = idx_ref[0, 0, fanout * r + k]
            pltpu.make_async_copy(x_ref.at[pl.ds(r, 1), :], o_hbm.at[pl.ds(dst, 1), :], sem).start(priority=k)
        return carry

    lax.fori_loop(0, tm, start, 0, unroll=8)
    for k in range(fanout):
        pltpu.make_async_copy(x_ref, o_hbm.at[pl.ds(0, tm), :], sem).wait()


def _row_scatter(x, idx, fanout, tm, name):
    n, D = x.shape
    return pl.pallas_call(
        functools.partial(_scatter_kernel, fanout=fanout),
        out_shape=jax.ShapeDtypeStruct((fanout * n, D), x.dtype),
        grid=(n // tm,),
        in_specs=[
            pl.BlockSpec((1, 1, fanout * tm), lambda i: (i, 0, 0), memory_space=pltpu.SMEM),
            pl.BlockSpec((tm, D), lambda i: (i, 0)),
        ],
        out_specs=pl.BlockSpec(memory_space=pl.ANY),
        scratch_shapes=[pltpu.SemaphoreType.DMA(())],
        compiler_params=_params(("arbitrary",)),
        name=name,
    )(idx.reshape(n // tm, 1, fanout * tm), x)


def _route_plan(route, tm):
    n = 2 * route.shape[0]
    n_tiles = n // tm
    ids = route[:, :2].astype(I32).reshape(n)
    onehot = (ids[:, None] == jnp.arange(N_EXPERTS, dtype=I32)[None, :]).astype(F32).reshape(n_tiles, tm, N_EXPERTS)
    tril = (jnp.arange(tm)[:, None] >= jnp.arange(tm)[None, :]).astype(F32)
    in_tile = jnp.einsum("rc,tce->tre", tril, onehot)
    per_tile = in_tile[:, -1, :]
    before = jnp.cumsum(per_tile, axis=0) - per_tile
    counts = jnp.sum(per_tile, axis=0).astype(I32)
    starts = (jnp.cumsum(counts) - counts).astype(I32)
    rank = jnp.sum(onehot * (in_tile + before[:, None, :] - 1.0), axis=2).reshape(n)
    pos = (jnp.sum(onehot.reshape(n, N_EXPERTS) * starts[None, :].astype(F32), axis=1) + rank).astype(I32)
    bounds = jnp.sort(jnp.concatenate([jnp.arange(n_tiles, dtype=I32) * tm, starts]))
    ends = jnp.concatenate([bounds[1:], jnp.full((1,), n, I32)])
    seg_tile = jnp.minimum(bounds // tm, n_tiles - 1)
    seg_exp = jnp.clip(jnp.sum((starts[None, :] <= bounds[:, None]).astype(I32), axis=1) - 1, 0, N_EXPERTS - 1)
    seg_first = jnp.concatenate([jnp.ones((1,), I32), (seg_tile[1:] != seg_tile[:-1]).astype(I32)])
    return pos, (seg_tile, seg_exp, bounds - seg_tile * tm, ends - seg_tile * tm, seg_first)


def _experts_kernel(tile_ref, exp_ref, lo_ref, hi_ref, first_ref, x_ref, wg_ref, wu_ref, wd_ref, o_ref):
    s = pl.program_id(0)
    lo, hi = lo_ref[s], hi_ref[s]

    @pl.when(first_ref[s] == 1)
    def _():
        o_ref[...] = jnp.zeros_like(o_ref)

    @pl.when(hi > lo)
    def _():
        x = x_ref[...].astype(MXU_DTYPE)
        a = jnp.dot(x, wg_ref[...], preferred_element_type=F32)
        u = jnp.dot(x, wu_ref[...], preferred_element_type=F32)
        y = _mm(a / (1.0 + jnp.exp(-a)) * u, wd_ref[...])
        row = lax.broadcasted_iota(I32, y.shape, 0)
        o_ref[...] += jnp.where(row >= lo, jnp.where(row < hi, y, 0.0), 0.0)


def _experts(xs, segs, wg, wu, wd, tm):
    n, D = xs.shape
    F = wg.shape[2]
    tile_of = lambda s, tile, exp, lo, hi, first: (tile[s], 0)
    w_of = lambda s, tile, exp, lo, hi, first: (exp[s], 0, 0)
    return pl.pallas_call(
        _experts_kernel,
        out_shape=jax.ShapeDtypeStruct((n, D), F32),
        grid_spec=pltpu.PrefetchScalarGridSpec(
            num_scalar_prefetch=5,
            grid=(segs[0].shape[0],),
            in_specs=[
                pl.BlockSpec((tm, D), tile_of),
                pl.BlockSpec((None, D, F), w_of),
                pl.BlockSpec((None, D, F), w_of),
                pl.BlockSpec((None, F, D), w_of),
            ],
            out_specs=pl.BlockSpec((tm, D), tile_of),
        ),
        compiler_params=_params(("arbitrary",)),
        name="experts",
    )(*segs, xs, wg, wu, wd)


def _moe(h2, route, x1, wg, wu, wd):
    T, D = x1.shape
    tm = min(MOE_TM, T)
    pos, segs = _route_plan(route, tm)
    xs = _row_scatter(h2, pos, 2, tm, "moe_scatter")
    ys = _experts(xs, segs, wg, wu, wd, tm)
    return _gather_combine(x1, ys, pos, route, tm)


def _pad_cols(w, n):
    return jnp.pad(w, ((0, 0), (0, n - w.shape[1])))


def _layer(x2, B, S, p):
    T, D = x2.shape
    assert D == 2 * GLA_HEADS * GLA_DK == GLA_HEADS * GLA_DV == DSA_HEADS * DSA_HEAD_DIM
    assert S % 512 == 0
    cd = MXU_DTYPE
    qk_w, v_w, dsa_w = GLA_HEADS * GLA_DK, GLA_HEADS * GLA_DV, DSA_HEADS * DSA_HEAD_DIM
    splits = (qk_w, qk_w, v_w, GLA_GATE_RANK, v_w, dsa_w, dsa_w, dsa_w, IDX_Q_W, IDX_DIM, IDX_HEADS, D, D)
    offs = [0]
    for s in splits:
        offs.append(offs[-1] + s)
    w_in = p["w_in"]
    (w_gq, w_gk, w_gv, w_ga, w_gr, w_dq, w_dk, w_dv, w_iq, w_ik, w_iw, w_ta, w_tb) = [
        w_in[:, offs[i]:offs[i + 1]] for i in range(len(splits))]

    w_main = jnp.concatenate([w_gq, w_gk, w_gv, w_gr, w_ta, w_tb], axis=1).astype(cd)
    w_qk = jnp.concatenate([w_dq, w_dk], axis=1).astype(cd)
    g_qk = jnp.concatenate([jnp.tile(p["dsa_q_norm_g"] * (DSA_HEAD_DIM ** -0.5 * math.log2(math.e)), DSA_HEADS),
                            jnp.tile(p["dsa_k_norm_g"], DSA_HEADS)]).reshape(1, 2 * dsa_w).astype(F32)
    w_vt = w_dv.T.astype(cd)
    w_idx = jnp.concatenate([_pad_cols(w_ik, LANES), _pad_cols(w_ga, LANES)], axis=1).astype(cd)
    w_idxt = jnp.pad(jnp.concatenate([w_iq, w_iw], axis=1).T, ((0, 16 - IDX_HEADS), (0, 0))).astype(cd)

    tm = min(1024, T)
    h = _rmsnorm(x2, p["norm1_g"], tm)
    zm = _matmul(h, w_main, tm, 1024, cd)
    qk = _qk_proj(h, w_qk, g_qk, tm)
    vt = _vt_proj(h, w_vt, B, S, min(512, S))
    iqt, ki, ga, wit = _idx_proj(h, w_idx, w_idxt, p["idx_k_ln_g"].reshape(1, -1), p["idx_k_ln_b"].reshape(1, -1),
                                 min(512, T))

    o_gla = _gla(zm, ga, p["gla_w_a2"], p["gla_b_a"].reshape(1, -1), p["gla_norm_g"].reshape(1, -1), B, S, 512)

    QB, KB = 512, min(1024, S)
    mask = _dsa_select(iqt, wit, ki, B, S, QB, KB)
    logit_bound = (DSA_HEAD_DIM * jnp.max(jnp.abs(g_qk[0, :dsa_w])) * jnp.max(jnp.abs(g_qk[0, dsa_w:])))
    o_dsa = _dsa_attn(qk, vt, mask, logit_bound, B, S, QB, KB)

    w_r = _pad_cols(jnp.concatenate([p["w_router_expert"], p["w_router_group"]], axis=1), LANES)
    b_r = _pad_cols(jnp.concatenate([p["b_router_expert"], p["b_router_group"]]).reshape(1, -1), LANES)
    x1, h2, route = _post(o_gla, o_dsa, zm, x2, p["w_branch_gla"].astype(cd), p["w_branch_dsa"].astype(cd),
                          p["w_out"].astype(cd), p["norm2_g"].reshape(1, -1), w_r, b_r, min(512, T))
    return _moe(h2, route, x1, p["w_exp_gate"].astype(cd), p["w_exp_up"].astype(cd), p["w_exp_down"].astype(cd))


def kernel(x, norm1_g, w_in, gla_w_a2, gla_b_a, gla_norm_g, dsa_q_norm_g, dsa_k_norm_g, idx_k_ln_g, idx_k_ln_b,
           w_branch_gla, w_branch_dsa, w_out, norm2_g, w_router_group, b_router_group, w_router_expert,
           b_router_expert, w_exp_gate, w_exp_up, w_exp_down):
    B, S, D = x.shape
    stacked = dict(norm1_g=norm1_g, w_in=w_in, gla_w_a2=gla_w_a2, gla_b_a=gla_b_a, gla_norm_g=gla_norm_g,
                   dsa_q_norm_g=dsa_q_norm_g, dsa_k_norm_g=dsa_k_norm_g, idx_k_ln_g=idx_k_ln_g,
                   idx_k_ln_b=idx_k_ln_b, w_branch_gla=w_branch_gla, w_branch_dsa=w_branch_dsa, w_out=w_out,
                   norm2_g=norm2_g, w_router_group=w_router_group, b_router_group=b_router_group,
                   w_router_expert=w_router_expert, b_router_expert=b_router_expert, w_exp_gate=w_exp_gate,
                   w_exp_up=w_exp_up, w_exp_down=w_exp_down)
    x2 = x.reshape(B * S, D).astype(F32)
    for l in range(w_in.shape[0]):
        x2 = _layer(x2, B, S, {k: v[l] for k, v in stacked.items()})
    return x2.reshape(B, S, D).astype(x.dtype)
```

```python
import functools
import math

import jax
import jax.numpy as jnp
from jax import lax
from jax.experimental import pallas as pl
from jax.experimental.pallas import tpu as pltpu

F32 = jnp.float32
BF16 = jnp.bfloat16
I32 = jnp.int32
MXU_DTYPE = BF16

CHUNK = 64
EPS = 1e-6
GLA_HEADS, GLA_DK, GLA_DV = 4, 128, 256
GLA_GATE_RANK = 16
GLA_GATE_TEMP = 16.0
DSA_HEADS, DSA_HEAD_DIM = 8, 128
IDX_HEADS, IDX_DIM = 8, 64
TOPK_MAX = 256
N_GROUPS, EXPERTS_PER_GROUP = 4, 8
N_EXPERTS = N_GROUPS * EXPERTS_PER_GROUP
LANES = 128

INT_MIN = -(2 ** 31)
INT_MAX = 2 ** 31 - 1
I16 = jnp.int16
I16_MIN, I16_MAX = -(2 ** 15), 2 ** 15 - 1

VMEM_LIMIT = 56 * 1024 * 1024

_NT = (((1,), (1,)), ((), ()))
_TN = (((0,), (0,)), ((), ()))


def _params(sem):
    return pltpu.CompilerParams(dimension_semantics=sem, vmem_limit_bytes=VMEM_LIMIT)


def _mm(a, b):
    return jnp.dot(a.astype(MXU_DTYPE), b.astype(MXU_DTYPE), preferred_element_type=F32)


def _mm_nt(a, b):
    return lax.dot_general(a.astype(MXU_DTYPE), b.astype(MXU_DTYPE), _NT, preferred_element_type=F32)


def _mm_tn(a, b):
    return lax.dot_general(a.astype(MXU_DTYPE), b.astype(MXU_DTYPE), _TN, preferred_element_type=F32)


def _split(a):
    hi = a.astype(BF16)
    lo = (a - hi.astype(F32)).astype(BF16)
    return hi, lo


def _dot3(a, b):
    ah, al = _split(a)
    bh, bl = _split(b)
    d = lambda u, v: jnp.dot(u, v, preferred_element_type=F32)
    return d(ah, bh) + (d(ah, bl) + d(al, bh))


def _rmsnorm_kernel(x_ref, g_ref, o_ref):
    x = x_ref[...]
    ms = jnp.mean(x * x, axis=-1, keepdims=True)
    o_ref[...] = (x * lax.rsqrt(ms + EPS) * g_ref[...]).astype(o_ref.dtype)


def _rmsnorm(x2, g, tm):
    T, D = x2.shape
    return pl.pallas_call(
        _rmsnorm_kernel,
        out_shape=jax.ShapeDtypeStruct((T, D), MXU_DTYPE),
        grid=(T // tm,),
        in_specs=[pl.BlockSpec((tm, D), lambda i: (i, 0)), pl.BlockSpec((1, D), lambda i: (0, 0))],
        out_specs=pl.BlockSpec((tm, D), lambda i: (i, 0)),
        compiler_params=_params(("parallel",)),
        name="rmsnorm",
    )(x2, g.reshape(1, D))


def _matmul_kernel(h_ref, w_ref, o_ref):
    o_ref[...] = jnp.dot(h_ref[...], w_ref[...], preferred_element_type=F32).astype(o_ref.dtype)


def _matmul(h, w, tm, tn, out_dtype):
    T, D = h.shape
    N = w.shape[1]
    return pl.pallas_call(
        _matmul_kernel,
        out_shape=jax.ShapeDtypeStruct((T, N), out_dtype),
        grid=(T // tm, N // tn),
        in_specs=[pl.BlockSpec((tm, D), lambda i, j: (i, 0)), pl.BlockSpec((D, tn), lambda i, j: (0, j))],
        out_specs=pl.BlockSpec((tm, tn), lambda i, j: (i, j)),
        compiler_params=_params(("parallel", "arbitrary")),
        name="proj_main",
    )(h, w)


def _qk_kernel(h_ref, w_ref, g_ref, o_ref):
    z = jnp.dot(h_ref[...], w_ref[...], preferred_element_type=F32)
    for hh in range(DSA_HEADS):
        sl = slice(hh * DSA_HEAD_DIM, (hh + 1) * DSA_HEAD_DIM)
        zh = z[:, sl]
        ms = jnp.mean(zh * zh, axis=-1, keepdims=True)
        o_ref[:, sl] = (zh * lax.rsqrt(ms + EPS) * g_ref[:, sl]).astype(o_ref.dtype)


def _qk_proj(h, w, g, tm):
    T, D = h.shape
    W = DSA_HEADS * DSA_HEAD_DIM
    return pl.pallas_call(
        _qk_kernel,
        out_shape=jax.ShapeDtypeStruct((T, 2 * W), MXU_DTYPE),
        grid=(T // tm, 2),
        in_specs=[
            pl.BlockSpec((tm, D), lambda i, j: (i, 0)),
            pl.BlockSpec((D, W), lambda i, j: (0, j)),
            pl.BlockSpec((1, W), lambda i, j: (0, j)),
        ],
        out_specs=pl.BlockSpec((tm, W), lambda i, j: (i, j)),
        compiler_params=_params(("parallel", "arbitrary")),
        name="proj_qk",
    )(h, w, g)


VT_ONES = 16
VT_ROWS = DSA_HEAD_DIM + VT_ONES


def _vt_kernel(h_ref, wt_ref, o_ref):
    vt = lax.dot_general(wt_ref[...], h_ref[...], _NT, preferred_element_type=F32).astype(o_ref.dtype)
    ones = jnp.ones((VT_ONES, vt.shape[1]), o_ref.dtype)
    for hh in range(DSA_HEADS):
        o_ref[hh, :DSA_HEAD_DIM, :] = vt[hh * DSA_HEAD_DIM:(hh + 1) * DSA_HEAD_DIM]
        o_ref[hh, DSA_HEAD_DIM:, :] = ones


def _vt_proj(h, wt, B, S, tm):
    T, D = h.shape
    W = wt.shape[0]
    nt = S // tm
    return pl.pallas_call(
        _vt_kernel,
        out_shape=jax.ShapeDtypeStruct((B, DSA_HEADS, VT_ROWS, S), MXU_DTYPE),
        grid=(T // tm,),
        in_specs=[pl.BlockSpec((tm, D), lambda i: (i, 0)), pl.BlockSpec((W, D), lambda i: (0, 0))],
        out_specs=pl.BlockSpec((None, DSA_HEADS, VT_ROWS, tm), lambda i: (i // nt, 0, 0, i % nt)),
        compiler_params=_params(("parallel",)),
        name="proj_vt",
    )(h, wt)


IDX_Q_W = IDX_HEADS * IDX_DIM


def _idx_kernel(h_ref, w_ref, wt_ref, lng_ref, lnb_ref, iqt_ref, ki_ref, ga_ref, wit_ref):
    h = h_ref[...]
    z = jnp.dot(h, w_ref[...], preferred_element_type=F32)
    ik = z[:, :IDX_DIM]
    mu = jnp.mean(ik, axis=-1, keepdims=True)
    var = jnp.mean(jnp.square(ik - mu), axis=-1, keepdims=True)
    ki = (ik - mu) * lax.rsqrt(var + EPS) * lng_ref[...] + lnb_ref[...]
    ki_ref[...] = ki.astype(ki_ref.dtype)
    ga_ref[...] = z[:, LANES:LANES + GLA_GATE_RANK]
    zt = lax.dot_general(wt_ref[...], h, _NT, preferred_element_type=F32)
    iqt_ref[...] = zt[:IDX_Q_W].astype(iqt_ref.dtype)
    wit_ref[...] = zt[IDX_Q_W:IDX_Q_W + IDX_HEADS] * (IDX_HEADS ** -0.5 * IDX_DIM ** -0.5)


def _idx_proj(h, w, wt, lng, lnb, tm):
    T, D = h.shape
    return pl.pallas_call(
        _idx_kernel,
        out_shape=(
            jax.ShapeDtypeStruct((IDX_Q_W, T), MXU_DTYPE),
            jax.ShapeDtypeStruct((T, IDX_DIM), MXU_DTYPE),
            jax.ShapeDtypeStruct((T, GLA_GATE_RANK), F32),
            jax.ShapeDtypeStruct((IDX_HEADS, T), F32),
        ),
        grid=(T // tm,),
        in_specs=[
            pl.BlockSpec((tm, D), lambda i: (i, 0)),
            pl.BlockSpec(w.shape, lambda i: (0, 0)),
            pl.BlockSpec(wt.shape, lambda i: (0, 0)),
            pl.BlockSpec((1, IDX_DIM), lambda i: (0, 0)),
            pl.BlockSpec((1, IDX_DIM), lambda i: (0, 0)),
        ],
        out_specs=(
            pl.BlockSpec((IDX_Q_W, tm), lambda i: (0, i)),
            pl.BlockSpec((tm, IDX_DIM), lambda i: (i, 0)),
            pl.BlockSpec((tm, GLA_GATE_RANK), lambda i: (i, 0)),
            pl.BlockSpec((IDX_HEADS, tm), lambda i: (0, i)),
        ),
        compiler_params=_params(("parallel",)),
        name="proj_idx",
    )(h, w, wt, lng, lnb)


def _gla_kernel(q_ref, k_ref, v_ref, r_ref, ga_ref, wa2_ref, ba_ref, ng_ref, o_ref, st_ref, tot_ref, kd_ref,
                oraw_ref, *, n_chunks):
    tb = q_ref.shape[0]

    @pl.when(pl.program_id(1) == 0)
    def _():
        st_ref[...] = jnp.zeros_like(st_ref)

    x = _dot3(ga_ref[...], wa2_ref[...]) + ba_ref[...]
    la = (jnp.minimum(x, 0.0) - jnp.log(1.0 + jnp.exp(-jnp.abs(x)))) * (1.0 / GLA_GATE_TEMP)
    row = lax.broadcasted_iota(I32, (tb, tb), 0)
    col = lax.broadcasted_iota(I32, (tb, tb), 1)
    same = lax.shift_right_logical(row, 6) == lax.shift_right_logical(col, 6)
    ones_blk = jnp.where(same, 1.0, 0.0).astype(BF16)
    tril_blk = jnp.where(same, jnp.where(row >= col, 1.0, 0.0), 0.0).astype(BF16)
    la_hi, la_lo = _split(la)
    d = lambda u, v: jnp.dot(u, v, preferred_element_type=F32)
    tot = d(ones_blk, la_hi) + d(ones_blk, la_lo)
    cum = d(tril_blk, la_hi) + d(tril_blk, la_lo)
    tot_ref[...] = tot
    kd_ref[...] = (k_ref[...].astype(F32) * jnp.exp(tot - cum)).astype(kd_ref.dtype)

    heads = range(GLA_HEADS)
    ksl = lambda hh: slice(hh * GLA_DK, (hh + 1) * GLA_DK)
    vsl = lambda hh: slice(hh * GLA_DV, (hh + 1) * GLA_DV)

    def chunk(c, carry):
        rows = pl.ds(pl.multiple_of(c * CHUNK, CHUNK), CHUNK)
        first = pl.ds(pl.multiple_of(c * CHUNK, CHUNK), 1)
        upd = [_mm_tn(v_ref[rows, vsl(hh)], kd_ref[rows, ksl(hh)]) for hh in heads]
        st = [st_ref[hh] * jnp.exp(tot_ref[first, ksl(hh)]) + upd[hh] for hh in heads]
        for hh in heads:
            st_ref[hh] = st[hh]
        for hh in heads:
            oraw_ref[rows, vsl(hh)] = _mm_nt(q_ref[rows, ksl(hh)], st[hh])
        return carry

    lax.fori_loop(0, n_chunks, chunk, 0)

    ng = ng_ref[...]
    for hh in heads:
        o = oraw_ref[:, vsl(hh)] * (GLA_DK ** -0.5)
        ms = jnp.mean(o * o, axis=-1, keepdims=True)
        r = r_ref[:, vsl(hh)].astype(F32)
        o_ref[:, vsl(hh)] = (o * lax.rsqrt(ms + EPS) * ng * (r / (1.0 + jnp.exp(-r)))).astype(o_ref.dtype)


def _gla(zm, ga, wa2, ba, ng, B, S, tb):
    T = B * S
    nb = S // tb
    qk_w = GLA_HEADS * GLA_DK
    v_w = GLA_HEADS * GLA_DV
    tok = lambda b, i: b * nb + i
    return pl.pallas_call(
        functools.partial(_gla_kernel, n_chunks=tb // CHUNK),
        out_shape=jax.ShapeDtypeStruct((T, v_w), MXU_DTYPE),
        grid=(B, nb),
        in_specs=[
            pl.BlockSpec((tb, qk_w), lambda b, i: (tok(b, i), 0)),
            pl.BlockSpec((tb, qk_w), lambda b, i: (tok(b, i), 1)),
            pl.BlockSpec((tb, v_w), lambda b, i: (tok(b, i), 1)),
            pl.BlockSpec((tb, v_w), lambda b, i: (tok(b, i), 2)),
            pl.BlockSpec((tb, GLA_GATE_RANK), lambda b, i: (tok(b, i), 0)),
            pl.BlockSpec((GLA_GATE_RANK, qk_w), lambda b, i: (0, 0)),
            pl.BlockSpec((1, qk_w), lambda b, i: (0, 0)),
            pl.BlockSpec((1, GLA_DV), lambda b, i: (0, 0)),
        ],
        out_specs=pl.BlockSpec((tb, v_w), lambda b, i: (tok(b, i), 0)),
        scratch_shapes=[pltpu.VMEM((GLA_HEADS, GLA_DV, GLA_DK), F32), pltpu.VMEM((tb, qk_w), F32),
                        pltpu.VMEM((tb, qk_w), MXU_DTYPE), pltpu.VMEM((tb, v_w), F32)],
        compiler_params=_params(("parallel", "arbitrary")),
        name="gla",
    )(zm, zm, zm, zm, ga, wa2, ba, ng)


def _last_kb(qb, QB, KB):
    return ((qb + 1) * QB - 1) // KB


def _causal_steps(S, QB, KB):
    pairs = [(q, k) for q in range(S // QB) for k in range(_last_kb(q, QB, KB) + 1)]
    qs, ks = zip(*pairs)
    return jnp.asarray(qs, I32), jnp.asarray(ks, I32)


def _select_kernel(qb_ref, kb_ref, iqt_ref, wit_ref, ki_ref, mask_ref, hi_ref, lo_ref, gm_ref, *,
                   QB, KB, S, topk):
    qb = qb_ref[pl.program_id(1)]
    kb = kb_ref[pl.program_id(1)]
    last = _last_kb(qb, QB, KB)

    def rows_of(i):
        return pl.ds(pl.multiple_of(i * KB, KB), KB)

    @pl.when(kb == 0)
    def _():
        gm_ref[...] = jnp.full(gm_ref.shape, INT_MIN, I32)

    rc = 32 * 1024 // QB
    assert topk % rc == 0 and KB % rc == 0
    t_chunk = lax.shift_right_logical(qb * QB + lax.broadcasted_iota(I32, (rc, QB), 1), 6)
    for c in range(KB // rc):
        ki = ki_ref[c * rc:(c + 1) * rc, :]
        sc = jnp.zeros((rc, QB), F32)
        for hh in range(IDX_HEADS):
            lg = _mm(ki, iqt_ref[hh * IDX_DIM:(hh + 1) * IDX_DIM, :])
            sc = sc + jnp.maximum(lg, 0.0) * wit_ref[hh:hh + 1, :]
        s_chunk = lax.shift_right_logical(kb * KB + c * rc + lax.broadcasted_iota(I32, (rc, QB), 0), 6)
        bits = lax.bitcast_convert_type(sc, I32)
        sign = lax.shift_right_arithmetic(bits, 31)
        key = (bits ^ (sign & INT_MAX)) - sign
        key = jnp.where(s_chunk <= t_chunk, key, INT_MIN)
        rows = pl.ds(pl.multiple_of(kb * KB + c * rc, rc), rc)
        hi_ref[rows, :] = lax.shift_right_arithmetic(key, 16).astype(I16)
        lo_ref[rows, :] = ((key & 0xFFFF) + I16_MIN).astype(I16)
        g0 = (c * rc) % topk
        gm_ref[g0:g0 + rc, :] = jnp.maximum(gm_ref[g0:g0 + rc, :], key)

    @pl.when(kb == last)
    def _():
        nblk = last + 1
        slab = 32
        one, zero = jnp.int16(1), jnp.int16(0)

        def fold(m, reduce):
            part = m[0:slab]
            for r in range(1, KB // slab):
                part = reduce(part, m[r * slab:(r + 1) * slab])
            return part

        def count(pred):
            def body(i, acc):
                return acc + fold(pred(i), jnp.add)
            acc = lax.fori_loop(0, nblk, body, jnp.zeros((slab, QB), I16))
            return jnp.sum(acc.astype(I32), axis=0, keepdims=True)

        def count_ge(ref, thr):
            t16 = thr.astype(I16)
            return count(lambda i: jnp.where(ref[rows_of(i), :] >= t16, one, zero))

        def search(ref, target, lo, c_lo, hi, c_hi):
            def is_open(lo, c_lo, hi):
                return jnp.where(c_lo > target, jnp.where(hi != lo + 1, 1, 0), 0)

            def step(_, st):
                lo, c_lo, hi, c_hi = st
                open_ = is_open(lo, c_lo, hi) > 0
                mid = lax.shift_right_arithmetic(lo + hi, 1)
                c = count_ge(ref, mid)
                up = jnp.logical_and(open_, c >= target)
                dn = jnp.logical_and(open_, c < target)
                return jnp.where(up, mid, lo), jnp.where(up, c, c_lo), jnp.where(dn, mid, hi), jnp.where(dn, c, c_hi)

            width = jnp.where(is_open(lo, c_lo, hi) > 0, hi - lo, 1)
            n_halvings = jnp.max(32 - lax.clz(width - 1))
            lo, c_lo, hi, c_hi = lax.fori_loop(0, n_halvings, step, (lo, c_lo, hi, c_hi))
            return lo, c_lo, c_hi

        zeros = jnp.zeros((1, QB), I32)
        gm = gm_ref[...]
        g_lo = jnp.maximum(lax.shift_right_arithmetic(jnp.min(gm, axis=0, keepdims=True), 16), I16_MIN + 1)
        g_hi = lax.shift_right_arithmetic(jnp.max(gm, axis=0, keepdims=True), 16) + 1
        hstar, ch_ge, ch_gt = search(hi_ref, topk, g_lo, count_ge(hi_ref, g_lo), g_hi, zeros)
        split = ch_ge > topk
        h16 = hstar.astype(I16)

        def low_half():
            def build(i, carry):
                lo_ref[rows_of(i), :] = jnp.where(hi_ref[rows_of(i), :] == h16, lo_ref[rows_of(i), :],
                                                  jnp.int16(I16_MIN))
                return carry
            lax.fori_loop(0, nblk, build, 0)
            target = jnp.where(split, topk - ch_gt, INT_MAX)
            return search(lo_ref, target, jnp.full((1, QB), I16_MIN, I32), ch_ge - ch_gt,
                          jnp.full((1, QB), I16_MAX + 1, I32), zeros)

        any_split = jnp.max(jnp.where(split, 1, 0)) > 0
        lstar, cl_ge, cl_gt = lax.cond(any_split, low_half,
                                       lambda: (jnp.full((1, QB), I16_MIN, I32), zeros, zeros))
        lstar = jnp.where(split, lstar, I16_MIN)
        l16 = lstar.astype(I16)
        c_ge = jnp.where(split, ch_gt + cl_ge, ch_ge)
        c_gt = jnp.where(split, ch_gt + cl_gt, ch_gt)
        excess = c_ge > topk
        need = jnp.where(excess, topk - c_gt, S + 1)
        row16 = lax.broadcasted_iota(I32, (KB, QB), 0).astype(I16)

        def before(i, j):
            return jnp.where(row16 + (i * KB).astype(I16) < j.astype(I16), one, zero)

        def tied(i, then):
            return jnp.where(hi_ref[rows_of(i), :] == h16, jnp.where(lo_ref[rows_of(i), :] == l16, then, zero), zero)

        def tie_cut():
            def step(_, c):
                lo, hi = c
                mid = lax.shift_right_arithmetic(lo + hi, 1)
                ok = count(lambda i: tied(i, before(i, mid))) >= need
                return jnp.where(ok, lo, mid + 1), jnp.where(ok, mid, hi)
            n_steps = max(1, math.ceil(math.log2(S + 1)))
            lo, _hi = lax.fori_loop(0, n_steps, step, (zeros, jnp.full((1, QB), S, I32)))
            return lo

        any_excess = jnp.max(jnp.where(excess, 1, 0)) > 0
        jcut = lax.cond(any_excess, tie_cut, lambda: jnp.full((1, QB), S, I32))

        def write(i, carry):
            hi_t, lo_t = hi_ref[rows_of(i), :], lo_ref[rows_of(i), :]
            in_bucket = jnp.where(lo_t > l16, one, jnp.where(lo_t == l16, before(i, jcut), zero))
            sel = jnp.where(hi_t > h16, one, jnp.where(hi_t == h16, in_bucket, zero))
            mask_ref[rows_of(i), :] = sel.astype(mask_ref.dtype)
            return carry

        lax.fori_loop(0, nblk, write, 0)

        def clear(i, carry):
            mask_ref[rows_of(i), :] = jnp.zeros((KB, QB), mask_ref.dtype)
            return carry

        lax.fori_loop(nblk, S // KB, clear, 0)


def _dsa_select(iqt, wit, ki, B, S, QB, KB):
    nq, nk = S // QB, S // KB
    topk = min(TOPK_MAX, S // 4)
    assert KB % topk == 0
    qs, ks = _causal_steps(S, QB, KB)
    return pl.pallas_call(
        functools.partial(_select_kernel, QB=QB, KB=KB, S=S, topk=topk),
        out_shape=jax.ShapeDtypeStruct((B, S, S), jnp.int8),
        grid_spec=pltpu.PrefetchScalarGridSpec(
            num_scalar_prefetch=2,
            grid=(B, qs.shape[0]),
            in_specs=[
                pl.BlockSpec((IDX_Q_W, QB), lambda b, s, qs, ks: (0, b * nq + qs[s])),
                pl.BlockSpec((IDX_HEADS, QB), lambda b, s, qs, ks: (0, b * nq + qs[s])),
                pl.BlockSpec((KB, IDX_DIM), lambda b, s, qs, ks: (b * nk + ks[s], 0)),
            ],
            out_specs=pl.BlockSpec((None, S, QB), lambda b, s, qs, ks: (b, 0, qs[s])),
            scratch_shapes=[pltpu.VMEM((S, QB), I16), pltpu.VMEM((S, QB), I16), pltpu.VMEM((topk, QB), I32)],
        ),
        compiler_params=_params(("parallel", "arbitrary")),
        name="dsa_select",
    )(qs, ks, iqt, wit, ki)


LOGIT_SAFE = 120.0


def _attn_kernel(qb_ref, kb_ref, small_ref, q_ref, k_ref, vt_ref, m_ref, o_ref, acc_ref, run_ref, *, QB, KB):
    qb = qb_ref[pl.program_id(1)]
    kb = kb_ref[pl.program_id(1)]
    last = _last_kb(qb, QB, KB)

    @pl.when(kb == 0)
    def _():
        acc_ref[...] = jnp.zeros_like(acc_ref)
        run_ref[...] = jnp.full(run_ref.shape, -1e30, F32)

    head = lambda hh: slice(hh * DSA_HEAD_DIM, (hh + 1) * DSA_HEAD_DIM)
    qk_dot = lambda hh: _mm_nt(k_ref[:, head(hh)], q_ref[:, head(hh)])

    @pl.when(small_ref[0] == 1)
    def _():
        mb = m_ref[...].astype(MXU_DTYPE)
        lg = qk_dot(0)
        for hh in range(DSA_HEADS):
            lg_next = qk_dot(hh + 1) if hh + 1 < DSA_HEADS else None
            p = jnp.exp2(lg).astype(MXU_DTYPE) * mb
            acc_ref[hh] += jnp.dot(vt_ref[hh], p, preferred_element_type=F32)
            lg = lg_next

    @pl.when(small_ref[0] != 1)
    def _():
        selected = m_ref[...].astype(F32) > 0.0

        def logits(hh):
            lg = jnp.where(selected, qk_dot(hh), -jnp.inf)
            return lg, jnp.max(lg, axis=0, keepdims=True)

        nxt = logits(0)
        for hh in range(DSA_HEADS):
            lg, top = nxt
            nxt = logits(hh + 1) if hh + 1 < DSA_HEADS else None
            run_old = run_ref[hh]
            run_new = jnp.maximum(run_old, top)
            p = jnp.exp2(lg - run_new).astype(MXU_DTYPE)
            acc_ref[hh] = acc_ref[hh] * jnp.exp2(run_old - run_new) + jnp.dot(vt_ref[hh], p,
                                                                              preferred_element_type=F32)
            run_ref[hh] = run_new

    @pl.when(kb == last)
    def _():
        for hh in range(DSA_HEADS):
            acc = acc_ref[hh]
            o = acc[:DSA_HEAD_DIM] / acc[DSA_HEAD_DIM:DSA_HEAD_DIM + 1]
            o_ref[:, hh * DSA_HEAD_DIM:(hh + 1) * DSA_HEAD_DIM] = o.T.astype(o_ref.dtype)


def _dsa_attn(qk, vt, mask, logit_bound, B, S, QB, KB):
    T = B * S
    nq, nk = S // QB, S // KB
    W = DSA_HEADS * DSA_HEAD_DIM
    qs, ks = _causal_steps(S, QB, KB)
    small = (logit_bound <= LOGIT_SAFE).astype(I32).reshape(1)
    return pl.pallas_call(
        functools.partial(_attn_kernel, QB=QB, KB=KB),
        out_shape=jax.ShapeDtypeStruct((T, W), MXU_DTYPE),
        grid_spec=pltpu.PrefetchScalarGridSpec(
            num_scalar_prefetch=3,
            grid=(B, qs.shape[0]),
            in_specs=[
                pl.BlockSpec((QB, W), lambda b, s, qs, ks, sm: (b * nq + qs[s], 0)),
                pl.BlockSpec((KB, W), lambda b, s, qs, ks, sm: (b * nk + ks[s], 1)),
                pl.BlockSpec((None, DSA_HEADS, VT_ROWS, KB), lambda b, s, qs, ks, sm: (b, 0, 0, ks[s])),
                pl.BlockSpec((None, KB, QB), lambda b, s, qs, ks, sm: (b, ks[s], qs[s])),
            ],
            out_specs=pl.BlockSpec((QB, W), lambda b, s, qs, ks, sm: (b * nq + qs[s], 0)),
            scratch_shapes=[pltpu.VMEM((DSA_HEADS, VT_ROWS, QB), F32), pltpu.VMEM((DSA_HEADS, 1, QB), F32)],
        ),
        compiler_params=_params(("parallel", "arbitrary")),
        name="dsa_attn",
    )(qs, ks, small, qk, qk, vt, mask)


def _post_kernel(og_ref, od_ref, ga_ref, gb_ref, x_ref, pa_ref, pb_ref, wo_ref, g2_ref, wr_ref, br_ref,
                 x1_ref, h2_ref, route_ref):
    sig = lambda v: 1.0 / (1.0 + jnp.exp(-v))
    a = jnp.dot(og_ref[...], pa_ref[...], preferred_element_type=F32)
    b = jnp.dot(od_ref[...], pb_ref[...], preferred_element_type=F32)
    mix = sig(ga_ref[...].astype(F32)) * a + sig(gb_ref[...].astype(F32)) * b
    x1 = x_ref[...] + _mm(mix, wo_ref[...])
    x1_ref[...] = x1
    ms = jnp.mean(x1 * x1, axis=-1, keepdims=True)
    h2 = x1 * lax.rsqrt(ms + EPS) * g2_ref[...]
    h2_ref[...] = h2.astype(h2_ref.dtype)

    logits = _dot3(h2, wr_ref[...]) + br_ref[...]
    tm = logits.shape[0]
    lane = lax.broadcasted_iota(I32, (tm, LANES), 1).astype(F32)
    neg = -jnp.inf
    far = float(2 * LANES)
    rmax = lambda v: jnp.max(v, axis=1, keepdims=True)
    rmin = lambda v: jnp.min(v, axis=1, keepdims=True)
    gl = jnp.where(lane >= N_EXPERTS, jnp.where(lane < N_EXPERTS + N_GROUPS, logits, neg), neg)
    gmax = rmax(gl)
    g_w = 1.0 / jnp.sum(jnp.exp(gl - gmax), axis=1, keepdims=True)
    g_idx = rmin(jnp.where(gl == gmax, lane, far)) - N_EXPERTS
    e_lo = g_idx * EXPERTS_PER_GROUP
    el = jnp.where(lane >= e_lo, jnp.where(lane < e_lo + EXPERTS_PER_GROUP, logits, neg), neg)
    m1 = rmax(el)
    e1 = rmin(jnp.where(el == m1, lane, far))
    el2 = jnp.where(lane == e1, neg, el)
    m2 = rmax(el2)
    e2 = rmin(jnp.where(el2 == m2, lane, far))
    p2 = jnp.exp(m2 - m1)
    w1 = g_w / (1.0 + p2)
    route_ref[...] = jnp.where(lane == 0.0, e1, jnp.where(lane == 1.0, e2, jnp.where(
        lane == 2.0, w1, jnp.where(lane == 3.0, w1 * p2, 0.0))))


def _post(og, od, zm, x2, pa, pb, wo, g2, wr, br, tm):
    T, D = x2.shape
    row = lambda i: (i, 0)
    full = lambda i: (0, 0)
    return pl.pallas_call(
        _post_kernel,
        out_shape=(
            jax.ShapeDtypeStruct((T, D), F32),
            jax.ShapeDtypeStruct((T, D), F32),
            jax.ShapeDtypeStruct((T, LANES), F32),
        ),
        grid=(T // tm,),
        in_specs=[
            pl.BlockSpec((tm, D), row),
            pl.BlockSpec((tm, D), row),
            pl.BlockSpec((tm, D), lambda i: (i, 3)),
            pl.BlockSpec((tm, D), lambda i: (i, 4)),
            pl.BlockSpec((tm, D), row),
            pl.BlockSpec((D, D), full),
            pl.BlockSpec((D, D), full),
            pl.BlockSpec((D, D), full),
            pl.BlockSpec((1, D), full),
            pl.BlockSpec((D, LANES), full),
            pl.BlockSpec((1, LANES), full),
        ],
        out_specs=(pl.BlockSpec((tm, D), row), pl.BlockSpec((tm, D), row), pl.BlockSpec((tm, LANES), row)),
        compiler_params=_params(("parallel",)),
        name="merge_router",
    )(og, od, zm, zm, x2, pa, pb, wo, g2, wr, br)


MOE_TM = 256


def _gather_combine_kernel(idx_ref, nxt_ref, x1_ref, route_ref, src_hbm, o_ref, buf_ref, sem):
    i = pl.program_id(0)
    n = pl.num_programs(0)
    tm = o_ref.shape[0]
    slot = lax.rem(i, 2)

    def fetch(ids_ref, s):
        def start(r, carry):
            for k in range(2):
                pltpu.make_async_copy(src_hbm.at[pl.ds(ids_ref[0, 0, 2 * r + k], 1), :],
                                      buf_ref.at[s, pl.ds(k * tm + r, 1), :], sem.at[s]).start(priority=k)
            return carry
        lax.fori_loop(0, tm, start, 0, unroll=8)

    @pl.when(i == 0)
    def _():
        fetch(idx_ref, 0)

    @pl.when(i + 1 < n)
    def _():
        fetch(nxt_ref, 1 - slot)

    pltpu.make_async_copy(src_hbm.at[pl.ds(0, 2 * tm), :], buf_ref.at[slot], sem.at[slot]).wait()
    r = route_ref[...]
    lane = lax.broadcasted_iota(I32, r.shape, 1)
    w1 = jnp.sum(jnp.where(lane == 2, r, 0.0), axis=1, keepdims=True)
    w2 = jnp.sum(jnp.where(lane == 3, r, 0.0), axis=1, keepdims=True)
    o_ref[...] = x1_ref[...] + (w1 * buf_ref[slot, :tm, :] + w2 * buf_ref[slot, tm:, :])


def _gather_combine(x1, ys, pos, route, tm):
    T, D = x1.shape
    nt = T // tm
    idx3 = pos.reshape(nt, 1, 2 * tm)
    row = lambda i: (i, 0)
    return pl.pallas_call(
        _gather_combine_kernel,
        out_shape=jax.ShapeDtypeStruct((T, D), F32),
        grid=(nt,),
        in_specs=[
            pl.BlockSpec((1, 1, 2 * tm), lambda i: (i, 0, 0), memory_space=pltpu.SMEM),
            pl.BlockSpec((1, 1, 2 * tm), lambda i: (jnp.minimum(i + 1, nt - 1), 0, 0), memory_space=pltpu.SMEM),
            pl.BlockSpec((tm, D), row),
            pl.BlockSpec((tm, LANES), row),
            pl.BlockSpec(memory_space=pl.ANY),
        ],
        out_specs=pl.BlockSpec((tm, D), row),
        scratch_shapes=[pltpu.VMEM((2, 2 * tm, D), ys.dtype), pltpu.SemaphoreType.DMA((2,))],
        compiler_params=_params(("arbitrary",)),
        name="moe_gather_combine",
    )(idx3, idx3, x1, route, ys)


def _scatter_kernel(idx_ref, x_ref, o_hbm, sem, *, fanout):
    tm = x_ref.shape[0]

    def start(r, carry):
        for k in range(fanout):
            dst = idx_ref[0, 0, fanout * r + k]
            pltpu.make_async_copy(x_ref.at[pl.ds(r, 1), :], o_hbm.at[pl.ds(dst, 1), :], sem).start(priority=k)
        return carry

    lax.fori_loop(0, tm, start, 0, unroll=8)
    for k in range(fanout):
        pltpu.make_async_copy(x_ref, o_hbm.at[pl.ds(0, tm), :], sem).wait()


def _row_scatter(x, idx, fanout, tm, name):
    n, D = x.shape
    return pl.pallas_call(
        functools.partial(_scatter_kernel, fanout=fanout),
        out_shape=jax.ShapeDtypeStruct((fanout * n, D), x.dtype),
        grid=(n // tm,),
        in_specs=[
            pl.BlockSpec((1, 1, fanout * tm), lambda i: (i, 0, 0), memory_space=pltpu.SMEM),
            pl.BlockSpec((tm, D), lambda i: (i, 0)),
        ],
        out_specs=pl.BlockSpec(memory_space=pl.ANY),
        scratch_shapes=[pltpu.SemaphoreType.DMA(())],
        compiler_params=_params(("arbitrary",)),
        name=name,
    )(idx.reshape(n // tm, 1, fanout * tm), x)


def _route_plan(route, tm):
    n = 2 * route.shape[0]
    n_tiles = n // tm
    ids = route[:, :2].astype(I32).reshape(n)
    onehot = (ids[:, None] == jnp.arange(N_EXPERTS, dtype=I32)[None, :]).astype(F32).reshape(n_tiles, tm, N_EXPERTS)
    tril = (jnp.arange(tm)[:, None] >= jnp.arange(tm)[None, :]).astype(F32)
    in_tile = jnp.einsum("rc,tce->tre", tril, onehot)
    per_tile = in_tile[:, -1, :]
    before = jnp.cumsum(per_tile, axis=0) - per_tile
    counts = jnp.sum(per_tile, axis=0).astype(I32)
    starts = (jnp.cumsum(counts) - counts).astype(I32)
    rank = jnp.sum(onehot * (in_tile + before[:, None, :] - 1.0), axis=2).reshape(n)
    pos = (jnp.sum(onehot.reshape(n, N_EXPERTS) * starts[None, :].astype(F32), axis=1) + rank).astype(I32)
    bounds = jnp.sort(jnp.concatenate([jnp.arange(n_tiles, dtype=I32) * tm, starts]))
    ends = jnp.concatenate([bounds[1:], jnp.full((1,), n, I32)])
    seg_tile = jnp.minimum(bounds // tm, n_tiles - 1)
    seg_exp = jnp.clip(jnp.sum((starts[None, :] <= bounds[:, None]).astype(I32), axis=1) - 1, 0, N_EXPERTS - 1)
    seg_first = jnp.concatenate([jnp.ones((1,), I32), (seg_tile[1:] != seg_tile[:-1]).astype(I32)])
    return pos, (seg_tile, seg_exp, bounds - seg_tile * tm, ends - seg_tile * tm, seg_first)


def _experts_kernel(tile_ref, exp_ref, lo_ref, hi_ref, first_ref, x_ref, wg_ref, wu_ref, wd_ref, o_ref):
    s = pl.program_id(0)
    lo, hi = lo_ref[s], hi_ref[s]

    @pl.when(first_ref[s] == 1)
    def _():
        o_ref[...] = jnp.zeros_like(o_ref)

    @pl.when(hi > lo)
    def _():
        x = x_ref[...].astype(MXU_DTYPE)
        a = jnp.dot(x, wg_ref[...], preferred_element_type=F32)
        u = jnp.dot(x, wu_ref[...], preferred_element_type=F32)
        y = _mm(a / (1.0 + jnp.exp(-a)) * u, wd_ref[...])
        row = lax.broadcasted_iota(I32, y.shape, 0)
        o_ref[...] += jnp.where(row >= lo, jnp.where(row < hi, y, 0.0), 0.0)


def _experts(xs, segs, wg, wu, wd, tm):
    n, D = xs.shape
    F = wg.shape[2]
    tile_of = lambda s, tile, exp, lo, hi, first: (tile[s], 0)
    w_of = lambda s, tile, exp, lo, hi, first: (exp[s], 0, 0)
    return pl.pallas_call(
        _experts_kernel,
        out_shape=jax.ShapeDtypeStruct((n, D), F32),
        grid_spec=pltpu.PrefetchScalarGridSpec(
            num_scalar_prefetch=5,
            grid=(segs[0].shape[0],),
            in_specs=[
                pl.BlockSpec((tm, D), tile_of),
                pl.BlockSpec((None, D, F), w_of),
                pl.BlockSpec((None, D, F), w_of),
                pl.BlockSpec((None, F, D), w_of),
            ],
            out_specs=pl.BlockSpec((tm, D), tile_of),
        ),
        compiler_params=_params(("arbitrary",)),
        name="experts",
    )(*segs, xs, wg, wu, wd)


def _moe(h2, route, x1, wg, wu, wd):
    T, D = x1.shape
    tm = min(MOE_TM, T)
    pos, segs = _route_plan(route, tm)
    xs = _row_scatter(h2, pos, 2, tm, "moe_scatter")
    ys = _experts(xs, segs, wg, wu, wd, tm)
    return _gather_combine(x1, ys, pos, route, tm)


def _pad_cols(w, n):
    return jnp.pad(w, ((0, 0), (0, n - w.shape[1])))


def _layer(x2, B, S, p):
    T, D = x2.shape
    assert D == 2 * GLA_HEADS * GLA_DK == GLA_HEADS * GLA_DV == DSA_HEADS * DSA_HEAD_DIM
    assert S % 512 == 0
    cd = MXU_DTYPE
    qk_w, v_w, dsa_w = GLA_HEADS * GLA_DK, GLA_HEADS * GLA_DV, DSA_HEADS * DSA_HEAD_DIM
    splits = (qk_w, qk_w, v_w, GLA_GATE_RANK, v_w, dsa_w, dsa_w, dsa_w, IDX_Q_W, IDX_DIM, IDX_HEADS, D, D)
    offs = [0]
    for s in splits:
        offs.append(offs[-1] + s)
    w_in = p["w_in"]
    (w_gq, w_gk, w_gv, w_ga, w_gr, w_dq, w_dk, w_dv, w_iq, w_ik, w_iw, w_ta, w_tb) = [
        w_in[:, offs[i]:offs[i + 1]] for i in range(len(splits))]

    w_main = jnp.concatenate([w_gq, w_gk, w_gv, w_gr, w_ta, w_tb], axis=1).astype(cd)
    w_qk = jnp.concatenate([w_dq, w_dk], axis=1).astype(cd)
    g_qk = jnp.concatenate([jnp.tile(p["dsa_q_norm_g"] * (DSA_HEAD_DIM ** -0.5 * math.log2(math.e)), DSA_HEADS),
                            jnp.tile(p["dsa_k_norm_g"], DSA_HEADS)]).reshape(1, 2 * dsa_w).astype(F32)
    w_vt = w_dv.T.astype(cd)
    w_idx = jnp.concatenate([_pad_cols(w_ik, LANES), _pad_cols(w_ga, LANES)], axis=1).astype(cd)
    w_idxt = jnp.pad(jnp.concatenate([w_iq, w_iw], axis=1).T, ((0, 16 - IDX_HEADS), (0, 0))).astype(cd)

    tm = min(1024, T)
    h = _rmsnorm(x2, p["norm1_g"], tm)
    zm = _matmul(h, w_main, tm, 1024, cd)
    qk = _qk_proj(h, w_qk, g_qk, tm)
    vt = _vt_proj(h, w_vt, B, S, min(512, S))
    iqt, ki, ga, wit = _idx_proj(h, w_idx, w_idxt, p["idx_k_ln_g"].reshape(1, -1), p["idx_k_ln_b"].reshape(1, -1),
                                 min(512, T))

    o_gla = _gla(zm, ga, p["gla_w_a2"], p["gla_b_a"].reshape(1, -1), p["gla_norm_g"].reshape(1, -1), B, S, 512)

    QB, KB = 256, min(1024, S)
    mask = _dsa_select(iqt, wit, ki, B, S, QB, KB)
    logit_bound = (DSA_HEAD_DIM * jnp.max(jnp.abs(g_qk[0, :dsa_w])) * jnp.max(jnp.abs(g_qk[0, dsa_w:])))
    o_dsa = _dsa_attn(qk, vt, mask, logit_bound, B, S, 2 * QB, KB)

    w_r = _pad_cols(jnp.concatenate([p["w_router_expert"], p["w_router_group"]], axis=1), LANES)
    b_r = _pad_cols(jnp.concatenate([p["b_router_expert"], p["b_router_group"]]).reshape(1, -1), LANES)
    x1, h2, route = _post(o_gla, o_dsa, zm, x2, p["w_branch_gla"].astype(cd), p["w_branch_dsa"].astype(cd),
                          p["w_out"].astype(cd), p["norm2_g"].reshape(1, -1), w_r, b_r, min(512, T))
    return _moe(h2, route, x1, p["w_exp_gate"].astype(cd), p["w_exp_up"].astype(cd), p["w_exp_down"].astype(cd))


def kernel(x, norm1_g, w_in, gla_w_a2, gla_b_a, gla_norm_g, dsa_q_norm_g, dsa_k_norm_g, idx_k_ln_g, idx_k_ln_b,
           w_branch_gla, w_branch_dsa, w_out, norm2_g, w_router_group, b_router_group, w_router_expert,
           b_router_expert, w_exp_gate, w_exp_up, w_exp_down):
    B, S, D = x.shape
    stacked = dict(norm1_g=norm1_g, w_in=w_in, gla_w_a2=gla_w_a2, gla_b_a=gla_b_a, gla_norm_g=gla_norm_g,
                   dsa_q_norm_g=dsa_q_norm_g, dsa_k_norm_g=dsa_k_norm_g, idx_k_ln_g=idx_k_ln_g,
                   idx_k_ln_b=idx_k_ln_b, w_branch_gla=w_branch_gla, w_branch_dsa=w_branch_dsa, w_out=w_out,
                   norm2_g=norm2_g, w_router_group=w_router_group, b_router_group=b_router_group,
                   w_router_expert=w_router_expert, b_router_expert=b_router_expert, w_exp_gate=w_exp_gate,
                   w_exp_up=w_exp_up, w_exp_down=w_exp_down)
    x2 = x.reshape(B * S, D).astype(F32)
    for l in range(w_in.shape[0]):
        x2 = _layer(x2, B, S, {k: v[l] for k, v in stacked.items()})
    return x2.reshape(B, S, D).astype(x.dtype)
```

```python
import functools
import math

import jax
import jax.numpy as jnp
from jax import lax
from jax.experimental import pallas as pl
from jax.experimental.pallas import tpu as pltpu

F32 = jnp.float32
BF16 = jnp.bfloat16
I32 = jnp.int32
MXU_DTYPE = BF16

CHUNK = 64
EPS = 1e-6
GLA_HEADS, GLA_DK, GLA_DV = 4, 128, 256
GLA_GATE_RANK = 16
GLA_GATE_TEMP = 16.0
DSA_HEADS, DSA_HEAD_DIM = 8, 128
IDX_HEADS, IDX_DIM = 8, 64
TOPK_MAX = 256
N_GROUPS, EXPERTS_PER_GROUP = 4, 8
N_EXPERTS = N_GROUPS * EXPERTS_PER_GROUP
LANES = 128

INT_MIN = -(2 ** 31)
INT_MAX = 2 ** 31 - 1
I16 = jnp.int16
I16_MIN, I16_MAX = -(2 ** 15), 2 ** 15 - 1

VMEM_LIMIT = 56 * 1024 * 1024

_NT = (((1,), (1,)), ((), ()))
_TN = (((0,), (0,)), ((), ()))


def _params(sem):
    return pltpu.CompilerParams(dimension_semantics=sem, vmem_limit_bytes=VMEM_LIMIT)


def _mm(a, b):
    return jnp.dot(a.astype(MXU_DTYPE), b.astype(MXU_DTYPE), preferred_element_type=F32)


def _mm_nt(a, b):
    return lax.dot_general(a.astype(MXU_DTYPE), b.astype(MXU_DTYPE), _NT, preferred_element_type=F32)


def _mm_tn(a, b):
    return lax.dot_general(a.astype(MXU_DTYPE), b.astype(MXU_DTYPE), _TN, preferred_element_type=F32)


def _split(a):
    hi = a.astype(BF16)
    lo = (a - hi.astype(F32)).astype(BF16)
    return hi, lo


def _dot3(a, b):
    ah, al = _split(a)
    bh, bl = _split(b)
    d = lambda u, v: jnp.dot(u, v, preferred_element_type=F32)
    return d(ah, bh) + (d(ah, bl) + d(al, bh))


def _rmsnorm_kernel(x_ref, g_ref, o_ref):
    x = x_ref[...]
    ms = jnp.mean(x * x, axis=-1, keepdims=True)
    o_ref[...] = (x * lax.rsqrt(ms + EPS) * g_ref[...]).astype(o_ref.dtype)


def _rmsnorm(x2, g, tm):
    T, D = x2.shape
    return pl.pallas_call(
        _rmsnorm_kernel,
        out_shape=jax.ShapeDtypeStruct((T, D), MXU_DTYPE),
        grid=(T // tm,),
        in_specs=[pl.BlockSpec((tm, D), lambda i: (i, 0)), pl.BlockSpec((1, D), lambda i: (0, 0))],
        out_specs=pl.BlockSpec((tm, D), lambda i: (i, 0)),
        compiler_params=_params(("parallel",)),
        name="rmsnorm",
    )(x2, g.reshape(1, D))


def _matmul_kernel(h_ref, w_ref, o_ref):
    o_ref[...] = jnp.dot(h_ref[...], w_ref[...], preferred_element_type=F32).astype(o_ref.dtype)


def _matmul(h, w, tm, tn, out_dtype):
    T, D = h.shape
    N = w.shape[1]
    return pl.pallas_call(
        _matmul_kernel,
        out_shape=jax.ShapeDtypeStruct((T, N), out_dtype),
        grid=(T // tm, N // tn),
        in_specs=[pl.BlockSpec((tm, D), lambda i, j: (i, 0)), pl.BlockSpec((D, tn), lambda i, j: (0, j))],
        out_specs=pl.BlockSpec((tm, tn), lambda i, j: (i, j)),
        compiler_params=_params(("parallel", "arbitrary")),
        name="proj_main",
    )(h, w)


def _qk_kernel(h_ref, w_ref, g_ref, o_ref):
    z = jnp.dot(h_ref[...], w_ref[...], preferred_element_type=F32)
    for hh in range(DSA_HEADS):
        sl = slice(hh * DSA_HEAD_DIM, (hh + 1) * DSA_HEAD_DIM)
        zh = z[:, sl]
        ms = jnp.mean(zh * zh, axis=-1, keepdims=True)
        o_ref[:, sl] = (zh * lax.rsqrt(ms + EPS) * g_ref[:, sl]).astype(o_ref.dtype)


def _qk_proj(h, w, g, tm):
    T, D = h.shape
    W = DSA_HEADS * DSA_HEAD_DIM
    return pl.pallas_call(
        _qk_kernel,
        out_shape=jax.ShapeDtypeStruct((T, 2 * W), MXU_DTYPE),
        grid=(T // tm, 2),
        in_specs=[
            pl.BlockSpec((tm, D), lambda i, j: (i, 0)),
            pl.BlockSpec((D, W), lambda i, j: (0, j)),
            pl.BlockSpec((1, W), lambda i, j: (0, j)),
        ],
        out_specs=pl.BlockSpec((tm, W), lambda i, j: (i, j)),
        compiler_params=_params(("parallel", "arbitrary")),
        name="proj_qk",
    )(h, w, g)


VT_ONES = 16
VT_ROWS = DSA_HEAD_DIM + VT_ONES


def _vt_kernel(h_ref, wt_ref, o_ref):
    vt = lax.dot_general(wt_ref[...], h_ref[...], _NT, preferred_element_type=F32).astype(o_ref.dtype)
    ones = jnp.ones((VT_ONES, vt.shape[1]), o_ref.dtype)
    for hh in range(DSA_HEADS):
        o_ref[hh, :DSA_HEAD_DIM, :] = vt[hh * DSA_HEAD_DIM:(hh + 1) * DSA_HEAD_DIM]
        o_ref[hh, DSA_HEAD_DIM:, :] = ones


def _vt_proj(h, wt, B, S, tm):
    T, D = h.shape
    W = wt.shape[0]
    nt = S // tm
    return pl.pallas_call(
        _vt_kernel,
        out_shape=jax.ShapeDtypeStruct((B, DSA_HEADS, VT_ROWS, S), MXU_DTYPE),
        grid=(T // tm,),
        in_specs=[pl.BlockSpec((tm, D), lambda i: (i, 0)), pl.BlockSpec((W, D), lambda i: (0, 0))],
        out_specs=pl.BlockSpec((None, DSA_HEADS, VT_ROWS, tm), lambda i: (i // nt, 0, 0, i % nt)),
        compiler_params=_params(("parallel",)),
        name="proj_vt",
    )(h, wt)


IDX_Q_W = IDX_HEADS * IDX_DIM


def _idx_kernel(h_ref, w_ref, wt_ref, lng_ref, lnb_ref, iqt_ref, ki_ref, ga_ref, wit_ref):
    h = h_ref[...]
    z = jnp.dot(h, w_ref[...], preferred_element_type=F32)
    ik = z[:, :IDX_DIM]
    mu = jnp.mean(ik, axis=-1, keepdims=True)
    var = jnp.mean(jnp.square(ik - mu), axis=-1, keepdims=True)
    ki = (ik - mu) * lax.rsqrt(var + EPS) * lng_ref[...] + lnb_ref[...]
    ki_ref[...] = ki.astype(ki_ref.dtype)
    ga_ref[...] = z[:, LANES:LANES + GLA_GATE_RANK]
    zt = lax.dot_general(wt_ref[...], h, _NT, preferred_element_type=F32)
    iqt_ref[...] = zt[:IDX_Q_W].astype(iqt_ref.dtype)
    wit_ref[...] = zt[IDX_Q_W:IDX_Q_W + IDX_HEADS] * (IDX_HEADS ** -0.5 * IDX_DIM ** -0.5)


def _idx_proj(h, w, wt, lng, lnb, tm):
    T, D = h.shape
    return pl.pallas_call(
        _idx_kernel,
        out_shape=(
            jax.ShapeDtypeStruct((IDX_Q_W, T), MXU_DTYPE),
            jax.ShapeDtypeStruct((T, IDX_DIM), MXU_DTYPE),
            jax.ShapeDtypeStruct((T, GLA_GATE_RANK), F32),
            jax.ShapeDtypeStruct((IDX_HEADS, T), F32),
        ),
        grid=(T // tm,),
        in_specs=[
            pl.BlockSpec((tm, D), lambda i: (i, 0)),
            pl.BlockSpec(w.shape, lambda i: (0, 0)),
            pl.BlockSpec(wt.shape, lambda i: (0, 0)),
            pl.BlockSpec((1, IDX_DIM), lambda i: (0, 0)),
            pl.BlockSpec((1, IDX_DIM), lambda i: (0, 0)),
        ],
        out_specs=(
            pl.BlockSpec((IDX_Q_W, tm), lambda i: (0, i)),
            pl.BlockSpec((tm, IDX_DIM), lambda i: (i, 0)),
            pl.BlockSpec((tm, GLA_GATE_RANK), lambda i: (i, 0)),
            pl.BlockSpec((IDX_HEADS, tm), lambda i: (0, i)),
        ),
        compiler_params=_params(("parallel",)),
        name="proj_idx",
    )(h, w, wt, lng, lnb)


def _gla_kernel(q_ref, k_ref, v_ref, r_ref, ga_ref, wa2_ref, ba_ref, ng_ref, o_ref, st_ref, tot_ref, kd_ref,
                oraw_ref, *, n_chunks):
    tb = q_ref.shape[0]

    @pl.when(pl.program_id(1) == 0)
    def _():
        st_ref[...] = jnp.zeros_like(st_ref)

    x = _dot3(ga_ref[...], wa2_ref[...]) + ba_ref[...]
    la = (jnp.minimum(x, 0.0) - jnp.log(1.0 + jnp.exp(-jnp.abs(x)))) * (1.0 / GLA_GATE_TEMP)
    row = lax.broadcasted_iota(I32, (tb, tb), 0)
    col = lax.broadcasted_iota(I32, (tb, tb), 1)
    same = lax.shift_right_logical(row, 6) == lax.shift_right_logical(col, 6)
    ones_blk = jnp.where(same, 1.0, 0.0).astype(BF16)
    tril_blk = jnp.where(same, jnp.where(row >= col, 1.0, 0.0), 0.0).astype(BF16)
    la_hi, la_lo = _split(la)
    d = lambda u, v: jnp.dot(u, v, preferred_element_type=F32)
    tot = d(ones_blk, la_hi) + d(ones_blk, la_lo)
    cum = d(tril_blk, la_hi) + d(tril_blk, la_lo)
    tot_ref[...] = tot
    kd_ref[...] = (k_ref[...].astype(F32) * jnp.exp(tot - cum)).astype(kd_ref.dtype)

    heads = range(GLA_HEADS)
    ksl = lambda hh: slice(hh * GLA_DK, (hh + 1) * GLA_DK)
    vsl = lambda hh: slice(hh * GLA_DV, (hh + 1) * GLA_DV)

    def chunk(c, carry):
        rows = pl.ds(pl.multiple_of(c * CHUNK, CHUNK), CHUNK)
        first = pl.ds(pl.multiple_of(c * CHUNK, CHUNK), 1)
        upd = [_mm_tn(v_ref[rows, vsl(hh)], kd_ref[rows, ksl(hh)]) for hh in heads]
        st = [st_ref[hh] * jnp.exp(tot_ref[first, ksl(hh)]) + upd[hh] for hh in heads]
        for hh in heads:
            st_ref[hh] = st[hh]
        for hh in heads:
            oraw_ref[rows, vsl(hh)] = _mm_nt(q_ref[rows, ksl(hh)], st[hh])
        return carry

    lax.fori_loop(0, n_chunks, chunk, 0)

    ng = ng_ref[...]
    for hh in heads:
        o = oraw_ref[:, vsl(hh)] * (GLA_DK ** -0.5)
        ms = jnp.mean(o * o, axis=-1, keepdims=True)
        r = r_ref[:, vsl(hh)].astype(F32)
        o_ref[:, vsl(hh)] = (o * lax.rsqrt(ms + EPS) * ng * (r / (1.0 + jnp.exp(-r)))).astype(o_ref.dtype)


def _gla(zm, ga, wa2, ba, ng, B, S, tb):
    T = B * S
    nb = S // tb
    qk_w = GLA_HEADS * GLA_DK
    v_w = GLA_HEADS * GLA_DV
    tok = lambda b, i: b * nb + i
    return pl.pallas_call(
        functools.partial(_gla_kernel, n_chunks=tb // CHUNK),
        out_shape=jax.ShapeDtypeStruct((T, v_w), MXU_DTYPE),
        grid=(B, nb),
        in_specs=[
            pl.BlockSpec((tb, qk_w), lambda b, i: (tok(b, i), 0)),
            pl.BlockSpec((tb, qk_w), lambda b, i: (tok(b, i), 1)),
            pl.BlockSpec((tb, v_w), lambda b, i: (tok(b, i), 1)),
            pl.BlockSpec((tb, v_w), lambda b, i: (tok(b, i), 2)),
            pl.BlockSpec((tb, GLA_GATE_RANK), lambda b, i: (tok(b, i), 0)),
            pl.BlockSpec((GLA_GATE_RANK, qk_w), lambda b, i: (0, 0)),
            pl.BlockSpec((1, qk_w), lambda b, i: (0, 0)),
            pl.BlockSpec((1, GLA_DV), lambda b, i: (0, 0)),
        ],
        out_specs=pl.BlockSpec((tb, v_w), lambda b, i: (tok(b, i), 0)),
        scratch_shapes=[pltpu.VMEM((GLA_HEADS, GLA_DV, GLA_DK), F32), pltpu.VMEM((tb, qk_w), F32),
                        pltpu.VMEM((tb, qk_w), MXU_DTYPE), pltpu.VMEM((tb, v_w), F32)],
        compiler_params=_params(("parallel", "arbitrary")),
        name="gla",
    )(zm, zm, zm, zm, ga, wa2, ba, ng)


def _last_kb(qb, QB, KB):
    return ((qb + 1) * QB - 1) // KB


def _causal_steps(S, QB, KB):
    pairs = [(q, k) for q in range(S // QB) for k in range(_last_kb(q, QB, KB) + 1)]
    qs, ks = zip(*pairs)
    return jnp.asarray(qs, I32), jnp.asarray(ks, I32)


def _select_kernel(qb_ref, kb_ref, iqt_ref, wit_ref, ki_ref, mask_ref, hi_ref, lo_ref, gm_ref, *,
                   QB, KB, S, topk):
    qb = qb_ref[pl.program_id(1)]
    kb = kb_ref[pl.program_id(1)]
    last = _last_kb(qb, QB, KB)

    def rows_of(i):
        return pl.ds(pl.multiple_of(i * KB, KB), KB)

    @pl.when(kb == 0)
    def _():
        gm_ref[...] = jnp.full(gm_ref.shape, INT_MIN, I32)

    rc = 32 * 1024 // QB
    assert topk % rc == 0 and KB % rc == 0
    t_chunk = lax.shift_right_logical(qb * QB + lax.broadcasted_iota(I32, (rc, QB), 1), 6)
    for c in range(KB // rc):
        ki = ki_ref[c * rc:(c + 1) * rc, :]
        sc = jnp.zeros((rc, QB), F32)
        for hh in range(IDX_HEADS):
            lg = _mm(ki, iqt_ref[hh * IDX_DIM:(hh + 1) * IDX_DIM, :])
            sc = sc + jnp.maximum(lg, 0.0) * wit_ref[hh:hh + 1, :]
        s_chunk = lax.shift_right_logical(kb * KB + c * rc + lax.broadcasted_iota(I32, (rc, QB), 0), 6)
        bits = lax.bitcast_convert_type(sc, I32)
        sign = lax.shift_right_arithmetic(bits, 31)
        key = (bits ^ (sign & INT_MAX)) - sign
        key = jnp.where(s_chunk <= t_chunk, key, INT_MIN)
        rows = pl.ds(pl.multiple_of(kb * KB + c * rc, rc), rc)
        hi_ref[rows, :] = lax.shift_right_arithmetic(key, 16).astype(I16)
        lo_ref[rows, :] = ((key & 0xFFFF) + I16_MIN).astype(I16)
        g0 = (c * rc) % topk
        gm_ref[g0:g0 + rc, :] = jnp.maximum(gm_ref[g0:g0 + rc, :], key)

    @pl.when(kb == last)
    def _():
        nblk = last + 1
        slab = 32
        one, zero = jnp.int16(1), jnp.int16(0)

        def fold(m, reduce):
            part = m[0:slab]
            for r in range(1, KB // slab):
                part = reduce(part, m[r * slab:(r + 1) * slab])
            return part

        def count(pred):
            def body(i, acc):
                return acc + fold(pred(i), jnp.add)
            acc = lax.fori_loop(0, nblk, body, jnp.zeros((slab, QB), I16))
            return jnp.sum(acc.astype(I32), axis=0, keepdims=True)

        def count_ge(ref, thr):
            t16 = thr.astype(I16)
            return count(lambda i: jnp.where(ref[rows_of(i), :] >= t16, one, zero))

        def search(ref, target, lo, c_lo, hi, c_hi, rounds):
            def is_open(lo, c_lo, hi):
                return jnp.where(c_lo > target, jnp.where(hi != lo + 1, 1, 0), 0)

            def step(_, st):
                lo, c_lo, hi, c_hi = st
                open_ = is_open(lo, c_lo, hi) > 0
                mid = lax.shift_right_arithmetic(lo + hi, 1)
                c = count_ge(ref, mid)
                up = jnp.logical_and(open_, c >= target)
                dn = jnp.logical_and(open_, c < target)
                return jnp.where(up, mid, lo), jnp.where(up, c, c_lo), jnp.where(dn, mid, hi), jnp.where(dn, c, c_hi)

            def halvings_left(st):
                lo, c_lo, hi, _c_hi = st
                width = jnp.where(is_open(lo, c_lo, hi) > 0, hi - lo, 1)
                return jnp.max(32 - lax.clz(width - 1))

            st = (lo, c_lo, hi, c_hi)
            for most in rounds:
                st = lax.fori_loop(0, jnp.minimum(halvings_left(st), most), step, st)
            lo, c_lo, _hi, c_hi = st
            return lo, c_lo, c_hi

        zeros = jnp.zeros((1, QB), I32)
        gm = gm_ref[...]
        g_lo = jnp.maximum(lax.shift_right_arithmetic(jnp.min(gm, axis=0, keepdims=True), 16), I16_MIN + 1)
        g_hi = lax.shift_right_arithmetic(jnp.max(gm, axis=0, keepdims=True), 16) + 1
        hstar, ch_ge, ch_gt = search(hi_ref, topk, g_lo, count_ge(hi_ref, g_lo), g_hi, zeros, (16,))
        split = ch_ge > topk
        h16 = hstar.astype(I16)

        def low_half():
            def build(i, carry):
                lo_ref[rows_of(i), :] = jnp.where(hi_ref[rows_of(i), :] == h16, lo_ref[rows_of(i), :],
                                                  jnp.int16(I16_MIN))
                return carry
            lax.fori_loop(0, nblk, build, 0)
            target = jnp.where(split, topk - ch_gt, INT_MAX)
            return search(lo_ref, target, jnp.full((1, QB), I16_MIN, I32), ch_ge - ch_gt,
                          jnp.full((1, QB), I16_MAX + 1, I32), zeros, (12, 4))

        any_split = jnp.max(jnp.where(split, 1, 0)) > 0
        lstar, cl_ge, cl_gt = lax.cond(any_split, low_half,
                                       lambda: (jnp.full((1, QB), I16_MIN, I32), zeros, zeros))
        lstar = jnp.where(split, lstar, I16_MIN)
        l16 = lstar.astype(I16)
        c_ge = jnp.where(split, ch_gt + cl_ge, ch_ge)
        c_gt = jnp.where(split, ch_gt + cl_gt, ch_gt)
        excess = c_ge > topk
        need = jnp.where(excess, topk - c_gt, S + 1)
        row16 = lax.broadcasted_iota(I32, (KB, QB), 0).astype(I16)

        def before(i, j):
            return jnp.where(row16 + (i * KB).astype(I16) < j.astype(I16), one, zero)

        def tied(i, then):
            return jnp.where(hi_ref[rows_of(i), :] == h16, jnp.where(lo_ref[rows_of(i), :] == l16, then, zero), zero)

        def tie_cut():
            def step(_, c):
                lo, hi = c
                mid = lax.shift_right_arithmetic(lo + hi, 1)
                ok = count(lambda i: tied(i, before(i, mid))) >= need
                return jnp.where(ok, lo, mid + 1), jnp.where(ok, mid, hi)
            n_steps = max(1, math.ceil(math.log2(S + 1)))
            lo, _hi = lax.fori_loop(0, n_steps, step, (zeros, jnp.full((1, QB), S, I32)))
            return lo

        any_excess = jnp.max(jnp.where(excess, 1, 0)) > 0
        jcut = lax.cond(any_excess, tie_cut, lambda: jnp.full((1, QB), S, I32))

        def write(i, carry):
            hi_t, lo_t = hi_ref[rows_of(i), :], lo_ref[rows_of(i), :]
            in_bucket = jnp.where(lo_t > l16, one, jnp.where(lo_t == l16, before(i, jcut), zero))
            sel = jnp.where(hi_t > h16, one, jnp.where(hi_t == h16, in_bucket, zero))
            mask_ref[rows_of(i), :] = sel.astype(mask_ref.dtype)
            return carry

        lax.fori_loop(0, nblk, write, 0)

        def clear(i, carry):
            mask_ref[rows_of(i), :] = jnp.zeros((KB, QB), mask_ref.dtype)
            return carry

        lax.fori_loop(nblk, S // KB, clear, 0)


def _dsa_select(iqt, wit, ki, B, S, QB, KB):
    nq, nk = S // QB, S // KB
    topk = min(TOPK_MAX, S // 4)
    assert KB % topk == 0
    qs, ks = _causal_steps(S, QB, KB)
    return pl.pallas_call(
        functools.partial(_select_kernel, QB=QB, KB=KB, S=S, topk=topk),
        out_shape=jax.ShapeDtypeStruct((B, S, S), jnp.int8),
        grid_spec=pltpu.PrefetchScalarGridSpec(
            num_scalar_prefetch=2,
            grid=(B, qs.shape[0]),
            in_specs=[
                pl.BlockSpec((IDX_Q_W, QB), lambda b, s, qs, ks: (0, b * nq + qs[s])),
                pl.BlockSpec((IDX_HEADS, QB), lambda b, s, qs, ks: (0, b * nq + qs[s])),
                pl.BlockSpec((KB, IDX_DIM), lambda b, s, qs, ks: (b * nk + ks[s], 0)),
            ],
            out_specs=pl.BlockSpec((None, S, QB), lambda b, s, qs, ks: (b, 0, qs[s])),
            scratch_shapes=[pltpu.VMEM((S, QB), I16), pltpu.VMEM((S, QB), I16), pltpu.VMEM((topk, QB), I32)],
        ),
        compiler_params=_params(("parallel", "arbitrary")),
        name="dsa_select",
    )(qs, ks, iqt, wit, ki)


LOGIT_SAFE = 120.0


def _attn_kernel(qb_ref, kb_ref, small_ref, q_ref, k_ref, vt_ref, m_ref, o_ref, acc_ref, run_ref, *, QB, KB):
    qb = qb_ref[pl.program_id(1)]
    kb = kb_ref[pl.program_id(1)]
    last = _last_kb(qb, QB, KB)

    @pl.when(kb == 0)
    def _():
        acc_ref[...] = jnp.zeros_like(acc_ref)
        run_ref[...] = jnp.full(run_ref.shape, -1e30, F32)

    head = lambda hh: slice(hh * DSA_HEAD_DIM, (hh + 1) * DSA_HEAD_DIM)
    qk_dot = lambda hh: _mm_nt(k_ref[:, head(hh)], q_ref[:, head(hh)])

    @pl.when(small_ref[0] == 1)
    def _():
        mb = m_ref[...].astype(MXU_DTYPE)
        lg = qk_dot(0)
        for hh in range(DSA_HEADS):
            lg_next = qk_dot(hh + 1) if hh + 1 < DSA_HEADS else None
            p = jnp.exp2(lg).astype(MXU_DTYPE) * mb
            acc_ref[hh] += jnp.dot(vt_ref[hh], p, preferred_element_type=F32)
            lg = lg_next

    @pl.when(small_ref[0] != 1)
    def _():
        selected = m_ref[...].astype(F32) > 0.0

        def logits(hh):
            lg = jnp.where(selected, qk_dot(hh), -jnp.inf)
            return lg, jnp.max(lg, axis=0, keepdims=True)

        nxt = logits(0)
        for hh in range(DSA_HEADS):
            lg, top = nxt
            nxt = logits(hh + 1) if hh + 1 < DSA_HEADS else None
            run_old = run_ref[hh]
            run_new = jnp.maximum(run_old, top)
            p = jnp.exp2(lg - run_new).astype(MXU_DTYPE)
            acc_ref[hh] = acc_ref[hh] * jnp.exp2(run_old - run_new) + jnp.dot(vt_ref[hh], p,
                                                                              preferred_element_type=F32)
            run_ref[hh] = run_new

    @pl.when(kb == last)
    def _():
        for hh in range(DSA_HEADS):
            acc = acc_ref[hh]
            o = acc[:DSA_HEAD_DIM] / acc[DSA_HEAD_DIM:DSA_HEAD_DIM + 1]
            o_ref[:, hh * DSA_HEAD_DIM:(hh + 1) * DSA_HEAD_DIM] = o.T.astype(o_ref.dtype)


def _dsa_attn(qk, vt, mask, logit_bound, B, S, QB, KB):
    T = B * S
    nq, nk = S // QB, S // KB
    W = DSA_HEADS * DSA_HEAD_DIM
    qs, ks = _causal_steps(S, QB, KB)
    small = (logit_bound <= LOGIT_SAFE).astype(I32).reshape(1)
    return pl.pallas_call(
        functools.partial(_attn_kernel, QB=QB, KB=KB),
        out_shape=jax.ShapeDtypeStruct((T, W), MXU_DTYPE),
        grid_spec=pltpu.PrefetchScalarGridSpec(
            num_scalar_prefetch=3,
            grid=(B, qs.shape[0]),
            in_specs=[
                pl.BlockSpec((QB, W), lambda b, s, qs, ks, sm: (b * nq + qs[s], 0)),
                pl.BlockSpec((KB, W), lambda b, s, qs, ks, sm: (b * nk + ks[s], 1)),
                pl.BlockSpec((None, DSA_HEADS, VT_ROWS, KB), lambda b, s, qs, ks, sm: (b, 0, 0, ks[s])),
                pl.BlockSpec((None, KB, QB), lambda b, s, qs, ks, sm: (b, ks[s], qs[s])),
            ],
            out_specs=pl.BlockSpec((QB, W), lambda b, s, qs, ks, sm: (b * nq + qs[s], 0)),
            scratch_shapes=[pltpu.VMEM((DSA_HEADS, VT_ROWS, QB), F32), pltpu.VMEM((DSA_HEADS, 1, QB), F32)],
        ),
        compiler_params=_params(("parallel", "arbitrary")),
        name="dsa_attn",
    )(qs, ks, small, qk, qk, vt, mask)


def _post_kernel(og_ref, od_ref, ga_ref, gb_ref, x_ref, pa_ref, pb_ref, wo_ref, g2_ref, wr_ref, br_ref,
                 x1_ref, h2_ref, route_ref):
    sig = lambda v: 1.0 / (1.0 + jnp.exp(-v))
    a = jnp.dot(og_ref[...], pa_ref[...], preferred_element_type=F32)
    b = jnp.dot(od_ref[...], pb_ref[...], preferred_element_type=F32)
    mix = sig(ga_ref[...].astype(F32)) * a + sig(gb_ref[...].astype(F32)) * b
    x1 = x_ref[...] + _mm(mix, wo_ref[...])
    x1_ref[...] = x1
    ms = jnp.mean(x1 * x1, axis=-1, keepdims=True)
    h2 = x1 * lax.rsqrt(ms + EPS) * g2_ref[...]
    h2_ref[...] = h2.astype(h2_ref.dtype)

    logits = _dot3(h2, wr_ref[...]) + br_ref[...]
    tm = logits.shape[0]
    lane = lax.broadcasted_iota(I32, (tm, LANES), 1).astype(F32)
    neg = -jnp.inf
    far = float(2 * LANES)
    rmax = lambda v: jnp.max(v, axis=1, keepdims=True)
    rmin = lambda v: jnp.min(v, axis=1, keepdims=True)
    gl = jnp.where(lane >= N_EXPERTS, jnp.where(lane < N_EXPERTS + N_GROUPS, logits, neg), neg)
    gmax = rmax(gl)
    g_w = 1.0 / jnp.sum(jnp.exp(gl - gmax), axis=1, keepdims=True)
    g_idx = rmin(jnp.where(gl == gmax, lane, far)) - N_EXPERTS
    e_lo = g_idx * EXPERTS_PER_GROUP
    el = jnp.where(lane >= e_lo, jnp.where(lane < e_lo + EXPERTS_PER_GROUP, logits, neg), neg)
    m1 = rmax(el)
    e1 = rmin(jnp.where(el == m1, lane, far))
    el2 = jnp.where(lane == e1, neg, el)
    m2 = rmax(el2)
    e2 = rmin(jnp.where(el2 == m2, lane, far))
    p2 = jnp.exp(m2 - m1)
    w1 = g_w / (1.0 + p2)
    route_ref[...] = jnp.where(lane == 0.0, e1, jnp.where(lane == 1.0, e2, jnp.where(
        lane == 2.0, w1, jnp.where(lane == 3.0, w1 * p2, 0.0))))


def _post(og, od, zm, x2, pa, pb, wo, g2, wr, br, tm):
    T, D = x2.shape
    row = lambda i: (i, 0)
    full = lambda i: (0, 0)
    return pl.pallas_call(
        _post_kernel,
        out_shape=(
            jax.ShapeDtypeStruct((T, D), F32),
            jax.ShapeDtypeStruct((T, D), F32),
            jax.ShapeDtypeStruct((T, LANES), F32),
        ),
        grid=(T // tm,),
        in_specs=[
            pl.BlockSpec((tm, D), row),
            pl.BlockSpec((tm, D), row),
            pl.BlockSpec((tm, D), lambda i: (i, 3)),
            pl.BlockSpec((tm, D), lambda i: (i, 4)),
            pl.BlockSpec((tm, D), row),
            pl.BlockSpec((D, D), full),
            pl.BlockSpec((D, D), full),
            pl.BlockSpec((D, D), full),
            pl.BlockSpec((1, D), full),
            pl.BlockSpec((D, LANES), full),
            pl.BlockSpec((1, LANES), full),
        ],
        out_specs=(pl.BlockSpec((tm, D), row), pl.BlockSpec((tm, D), row), pl.BlockSpec((tm, LANES), row)),
        compiler_params=_params(("parallel",)),
        name="merge_router",
    )(og, od, zm, zm, x2, pa, pb, wo, g2, wr, br)


MOE_TM = 256


def _gather_combine_kernel(idx_ref, nxt_ref, x1_ref, route_ref, src_hbm, o_ref, buf_ref, sem):
    i = pl.program_id(0)
    n = pl.num_programs(0)
    tm = o_ref.shape[0]
    slot = lax.rem(i, 2)

    def fetch(ids_ref, s):
        def start(r, carry):
            for k in range(2):
                pltpu.make_async_copy(src_hbm.at[pl.ds(ids_ref[0, 0, 2 * r + k], 1), :],
                                      buf_ref.at[s, pl.ds(k * tm + r, 1), :], sem.at[s]).start(priority=k)
            return carry
        lax.fori_loop(0, tm, start, 0, unroll=8)

    @pl.when(i == 0)
    def _():
        fetch(idx_ref, 0)

    @pl.when(i + 1 < n)
    def _():
        fetch(nxt_ref, 1 - slot)

    pltpu.make_async_copy(src_hbm.at[pl.ds(0, 2 * tm), :], buf_ref.at[slot], sem.at[slot]).wait()
    r = route_ref[...]
    lane = lax.broadcasted_iota(I32, r.shape, 1)
    w1 = jnp.sum(jnp.where(lane == 2, r, 0.0), axis=1, keepdims=True)
    w2 = jnp.sum(jnp.where(lane == 3, r, 0.0), axis=1, keepdims=True)
    o_ref[...] = x1_ref[...] + (w1 * buf_ref[slot, :tm, :] + w2 * buf_ref[slot, tm:, :])


def _gather_combine(x1, ys, pos, route, tm):
    T, D = x1.shape
    nt = T // tm
    idx3 = pos.reshape(nt, 1, 2 * tm)
    row = lambda i: (i, 0)
    return pl.pallas_call(
        _gather_combine_kernel,
        out_shape=jax.ShapeDtypeStruct((T, D), F32),
        grid=(nt,),
        in_specs=[
            pl.BlockSpec((1, 1, 2 * tm), lambda i: (i, 0, 0), memory_space=pltpu.SMEM),
            pl.BlockSpec((1, 1, 2 * tm), lambda i: (jnp.minimum(i + 1, nt - 1), 0, 0), memory_space=pltpu.SMEM),
            pl.BlockSpec((tm, D), row),
            pl.BlockSpec((tm, LANES), row),
            pl.BlockSpec(memory_space=pl.ANY),
        ],
        out_specs=pl.BlockSpec((tm, D), row),
        scratch_shapes=[pltpu.VMEM((2, 2 * tm, D), ys.dtype), pltpu.SemaphoreType.DMA((2,))],
        compiler_params=_params(("arbitrary",)),
        name="moe_gather_combine",
    )(idx3, idx3, x1, route, ys)


def _scatter_kernel(idx_ref, x_ref, o_hbm, sem, *, fanout):
    tm = x_ref.shape[0]

    def start(r, carry):
        for k in range(fanout):
            dst = idx_ref[0, 0, fanout * r + k]
            pltpu.make_async_copy(x_ref.at[pl.ds(r, 1), :], o_hbm.at[pl.ds(dst, 1), :], sem).start(priority=k)
        return carry

    lax.fori_loop(0, tm, start, 0, unroll=8)
    for k in range(fanout):
        pltpu.make_async_copy(x_ref, o_hbm.at[pl.ds(0, tm), :], sem).wait()


def _row_scatter(x, idx, fanout, tm, name):
    n, D = x.shape
    return pl.pallas_call(
        functools.partial(_scatter_kernel, fanout=fanout),
        out_shape=jax.ShapeDtypeStruct((fanout * n, D), x.dtype),
        grid=(n // tm,),
        in_specs=[
            pl.BlockSpec((1, 1, fanout * tm), lambda i: (i, 0, 0), memory_space=pltpu.SMEM),
            pl.BlockSpec((tm, D), lambda i: (i, 0)),
        ],
        out_specs=pl.BlockSpec(memory_space=pl.ANY),
        scratch_shapes=[pltpu.SemaphoreType.DMA(())],
        compiler_params=_params(("arbitrary",)),
        name=name,
    )(idx.reshape(n // tm, 1, fanout * tm), x)


def _route_plan(route, tm):
    n = 2 * route.shape[0]
    n_tiles = n // tm
    ids = route[:, :2].astype(I32).reshape(n)
    onehot = (ids[:, None] == jnp.arange(N_EXPERTS, dtype=I32)[None, :]).astype(F32).reshape(n_tiles, tm, N_EXPERTS)
    tril = (jnp.arange(tm)[:, None] >= jnp.arange(tm)[None, :]).astype(F32)
    in_tile = jnp.einsum("rc,tce->tre", tril, onehot)
    per_tile = in_tile[:, -1, :]
    before = jnp.cumsum(per_tile, axis=0) - per_tile
    counts = jnp.sum(per_tile, axis=0).astype(I32)
    starts = (jnp.cumsum(counts) - counts).astype(I32)
    rank = jnp.sum(onehot * (in_tile + before[:, None, :] - 1.0), axis=2).reshape(n)
    pos = (jnp.sum(onehot.reshape(n, N_EXPERTS) * starts[None, :].astype(F32), axis=1) + rank).astype(I32)
    bounds = jnp.sort(jnp.concatenate([jnp.arange(n_tiles, dtype=I32) * tm, starts]))
    ends = jnp.concatenate([bounds[1:], jnp.full((1,), n, I32)])
    seg_tile = jnp.minimum(bounds // tm, n_tiles - 1)
    seg_exp = jnp.clip(jnp.sum((starts[None, :] <= bounds[:, None]).astype(I32), axis=1) - 1, 0, N_EXPERTS - 1)
    seg_first = jnp.concatenate([jnp.ones((1,), I32), (seg_tile[1:] != seg_tile[:-1]).astype(I32)])
    return pos, (seg_tile, seg_exp, bounds - seg_tile * tm, ends - seg_tile * tm, seg_first)


def _experts_kernel(tile_ref, exp_ref, lo_ref, hi_ref, first_ref, x_ref, wg_ref, wu_ref, wd_ref, o_ref):
    s = pl.program_id(0)
    lo, hi = lo_ref[s], hi_ref[s]

    @pl.when(first_ref[s] == 1)
    def _():
        o_ref[...] = jnp.zeros_like(o_ref)

    @pl.when(hi > lo)
    def _():
        x = x_ref[...].astype(MXU_DTYPE)
        a = jnp.dot(x, wg_ref[...], preferred_element_type=F32)
        u = jnp.dot(x, wu_ref[...], preferred_element_type=F32)
        y = _mm(a / (1.0 + jnp.exp(-a)) * u, wd_ref[...])
        row = lax.broadcasted_iota(I32, y.shape, 0)
        o_ref[...] += jnp.where(row >= lo, jnp.where(row < hi, y, 0.0), 0.0)


def _experts(xs, segs, wg, wu, wd, tm):
    n, D = xs.shape
    F = wg.shape[2]
    tile_of = lambda s, tile, exp, lo, hi, first: (tile[s], 0)
    w_of = lambda s, tile, exp, lo, hi, first: (exp[s], 0, 0)
    return pl.pallas_call(
        _experts_kernel,
        out_shape=jax.ShapeDtypeStruct((n, D), F32),
        grid_spec=pltpu.PrefetchScalarGridSpec(
            num_scalar_prefetch=5,
            grid=(segs[0].shape[0],),
            in_specs=[
                pl.BlockSpec((tm, D), tile_of),
                pl.BlockSpec((None, D, F), w_of),
                pl.BlockSpec((None, D, F), w_of),
                pl.BlockSpec((None, F, D), w_of),
            ],
            out_specs=pl.BlockSpec((tm, D), tile_of),
        ),
        compiler_params=_params(("arbitrary",)),
        name="experts",
    )(*segs, xs, wg, wu, wd)


def _moe(h2, route, x1, wg, wu, wd):
    T, D = x1.shape
    tm = min(MOE_TM, T)
    pos, segs = _route_plan(route, tm)
    xs = _row_scatter(h2, pos, 2, tm, "moe_scatter")
    ys = _experts(xs, segs, wg, wu, wd, tm)
    return _gather_combine(x1, ys, pos, route, tm)


def _pad_cols(w, n):
    return jnp.pad(w, ((0, 0), (0, n - w.shape[1])))


def _layer(x2, B, S, p):
    T, D = x2.shape
    assert D == 2 * GLA_HEADS * GLA_DK == GLA_HEADS * GLA_DV == DSA_HEADS * DSA_HEAD_DIM
    assert S % 512 == 0
    cd = MXU_DTYPE
    qk_w, v_w, dsa_w = GLA_HEADS * GLA_DK, GLA_HEADS * GLA_DV, DSA_HEADS * DSA_HEAD_DIM
    splits = (qk_w, qk_w, v_w, GLA_GATE_RANK, v_w, dsa_w, dsa_w, dsa_w, IDX_Q_W, IDX_DIM, IDX_HEADS, D, D)
    offs = [0]
    for s in splits:
        offs.append(offs[-1] + s)
    w_in = p["w_in"]
    (w_gq, w_gk, w_gv, w_ga, w_gr, w_dq, w_dk, w_dv, w_iq, w_ik, w_iw, w_ta, w_tb) = [
        w_in[:, offs[i]:offs[i + 1]] for i in range(len(splits))]

    w_main = jnp.concatenate([w_gq, w_gk, w_gv, w_gr, w_ta, w_tb], axis=1).astype(cd)
    w_qk = jnp.concatenate([w_dq, w_dk], axis=1).astype(cd)
    g_qk = jnp.concatenate([jnp.tile(p["dsa_q_norm_g"] * (DSA_HEAD_DIM ** -0.5 * math.log2(math.e)), DSA_HEADS),
                            jnp.tile(p["dsa_k_norm_g"], DSA_HEADS)]).reshape(1, 2 * dsa_w).astype(F32)
    w_vt = w_dv.T.astype(cd)
    w_idx = jnp.concatenate([_pad_cols(w_ik, LANES), _pad_cols(w_ga, LANES)], axis=1).astype(cd)
    w_idxt = jnp.pad(jnp.concatenate([w_iq, w_iw], axis=1).T, ((0, 16 - IDX_HEADS), (0, 0))).astype(cd)

    tm = min(1024, T)
    h = _rmsnorm(x2, p["norm1_g"], tm)
    zm = _matmul(h, w_main, tm, 1024, cd)
    qk = _qk_proj(h, w_qk, g_qk, tm)
    vt = _vt_proj(h, w_vt, B, S, min(512, S))
    iqt, ki, ga, wit = _idx_proj(h, w_idx, w_idxt, p["idx_k_ln_g"].reshape(1, -1), p["idx_k_ln_b"].reshape(1, -1),
                                 min(512, T))

    o_gla = _gla(zm, ga, p["gla_w_a2"], p["gla_b_a"].reshape(1, -1), p["gla_norm_g"].reshape(1, -1), B, S, 512)

    QB, KB = 256, min(1024, S)
    mask = _dsa_select(iqt, wit, ki, B, S, QB, KB)
    logit_bound = (DSA_HEAD_DIM * jnp.max(jnp.abs(g_qk[0, :dsa_w])) * jnp.max(jnp.abs(g_qk[0, dsa_w:])))
    o_dsa = _dsa_attn(qk, vt, mask, logit_bound, B, S, 2 * QB, KB)

    w_r = _pad_cols(jnp.concatenate([p["w_router_expert"], p["w_router_group"]], axis=1), LANES)
    b_r = _pad_cols(jnp.concatenate([p["b_router_expert"], p["b_router_group"]]).reshape(1, -1), LANES)
    x1, h2, route = _post(o_gla, o_dsa, zm, x2, p["w_branch_gla"].astype(cd), p["w_branch_dsa"].astype(cd),
                          p["w_out"].astype(cd), p["norm2_g"].reshape(1, -1), w_r, b_r, min(512, T))
    return _moe(h2, route, x1, p["w_exp_gate"].astype(cd), p["w_exp_up"].astype(cd), p["w_exp_down"].astype(cd))


def kernel(x, norm1_g, w_in, gla_w_a2, gla_b_a, gla_norm_g, dsa_q_norm_g, dsa_k_norm_g, idx_k_ln_g, idx_k_ln_b,
           w_branch_gla, w_branch_dsa, w_out, norm2_g, w_router_group, b_router_group, w_router_expert,
           b_router_expert, w_exp_gate, w_exp_up, w_exp_down):
    B, S, D = x.shape
    stacked = dict(norm1_g=norm1_g, w_in=w_in, gla_w_a2=gla_w_a2, gla_b_a=gla_b_a, gla_norm_g=gla_norm_g,
                   dsa_q_norm_g=dsa_q_norm_g, dsa_k_norm_g=dsa_k_norm_g, idx_k_ln_g=idx_k_ln_g,
                   idx_k_ln_b=idx_k_ln_b, w_branch_gla=w_branch_gla, w_branch_dsa=w_branch_dsa, w_out=w_out,
                   norm2_g=norm2_g, w_router_group=w_router_group, b_router_group=b_router_group,
                   w_router_expert=w_router_expert, b_router_expert=b_router_expert, w_exp_gate=w_exp_gate,
                   w_exp_up=w_exp_up, w_exp_down=w_exp_down)
    x2 = x.reshape(B * S, D).astype(F32)
    for l in range(w_in.shape[0]):
        x2 = _layer(x2, B, S, {k: v[l] for k, v in stacked.items()})
    return x2.reshape(B, S, D).astype(x.dtype)
```

```python
import functools
import math

import jax
import jax.numpy as jnp
from jax import lax
from jax.experimental import pallas as pl
from jax.experimental.pallas import tpu as pltpu

F32 = jnp.float32
BF16 = jnp.bfloat16
I32 = jnp.int32
MXU_DTYPE = BF16

CHUNK = 64
EPS = 1e-6
GLA_HEADS, GLA_DK, GLA_DV = 4, 128, 256
GLA_GATE_RANK = 16
GLA_GATE_TEMP = 16.0
DSA_HEADS, DSA_HEAD_DIM = 8, 128
IDX_HEADS, IDX_DIM = 8, 64
TOPK_MAX = 256
N_GROUPS, EXPERTS_PER_GROUP = 4, 8
N_EXPERTS = N_GROUPS * EXPERTS_PER_GROUP
LANES = 128

INT_MIN = -(2 ** 31)
INT_MAX = 2 ** 31 - 1
I16 = jnp.int16
I16_MIN, I16_MAX = -(2 ** 15), 2 ** 15 - 1

VMEM_LIMIT = 56 * 1024 * 1024

_NT = (((1,), (1,)), ((), ()))
_TN = (((0,), (0,)), ((), ()))


def _params(sem):
    return pltpu.CompilerParams(dimension_semantics=sem, vmem_limit_bytes=VMEM_LIMIT)


def _mm(a, b):
    return jnp.dot(a.astype(MXU_DTYPE), b.astype(MXU_DTYPE), preferred_element_type=F32)


def _mm_nt(a, b):
    return lax.dot_general(a.astype(MXU_DTYPE), b.astype(MXU_DTYPE), _NT, preferred_element_type=F32)


def _mm_tn(a, b):
    return lax.dot_general(a.astype(MXU_DTYPE), b.astype(MXU_DTYPE), _TN, preferred_element_type=F32)


def _split(a):
    hi = a.astype(BF16)
    lo = (a - hi.astype(F32)).astype(BF16)
    return hi, lo


def _dot3(a, b):
    ah, al = _split(a)
    bh, bl = _split(b)
    d = lambda u, v: jnp.dot(u, v, preferred_element_type=F32)
    return d(ah, bh) + (d(ah, bl) + d(al, bh))


def _rmsnorm_kernel(x_ref, g_ref, o_ref):
    x = x_ref[...]
    ms = jnp.mean(x * x, axis=-1, keepdims=True)
    o_ref[...] = (x * lax.rsqrt(ms + EPS) * g_ref[...]).astype(o_ref.dtype)


def _rmsnorm(x2, g, tm):
    T, D = x2.shape
    return pl.pallas_call(
        _rmsnorm_kernel,
        out_shape=jax.ShapeDtypeStruct((T, D), MXU_DTYPE),
        grid=(T // tm,),
        in_specs=[pl.BlockSpec((tm, D), lambda i: (i, 0)), pl.BlockSpec((1, D), lambda i: (0, 0))],
        out_specs=pl.BlockSpec((tm, D), lambda i: (i, 0)),
        compiler_params=_params(("parallel",)),
        name="rmsnorm",
    )(x2, g.reshape(1, D))


def _matmul_kernel(h_ref, w_ref, o_ref):
    o_ref[...] = jnp.dot(h_ref[...], w_ref[...], preferred_element_type=F32).astype(o_ref.dtype)


def _matmul(h, w, tm, tn, out_dtype):
    T, D = h.shape
    N = w.shape[1]
    return pl.pallas_call(
        _matmul_kernel,
        out_shape=jax.ShapeDtypeStruct((T, N), out_dtype),
        grid=(T // tm, N // tn),
        in_specs=[pl.BlockSpec((tm, D), lambda i, j: (i, 0)), pl.BlockSpec((D, tn), lambda i, j: (0, j))],
        out_specs=pl.BlockSpec((tm, tn), lambda i, j: (i, j)),
        compiler_params=_params(("parallel", "arbitrary")),
        name="proj_main",
    )(h, w)


def _qk_kernel(h_ref, w_ref, g_ref, o_ref):
    z = jnp.dot(h_ref[...], w_ref[...], preferred_element_type=F32)
    for hh in range(DSA_HEADS):
        sl = slice(hh * DSA_HEAD_DIM, (hh + 1) * DSA_HEAD_DIM)
        zh = z[:, sl]
        ms = jnp.mean(zh * zh, axis=-1, keepdims=True)
        o_ref[:, sl] = (zh * lax.rsqrt(ms + EPS) * g_ref[:, sl]).astype(o_ref.dtype)


def _qk_proj(h, w, g, tm):
    T, D = h.shape
    W = DSA_HEADS * DSA_HEAD_DIM
    return pl.pallas_call(
        _qk_kernel,
        out_shape=jax.ShapeDtypeStruct((T, 2 * W), MXU_DTYPE),
        grid=(T // tm, 2),
        in_specs=[
            pl.BlockSpec((tm, D), lambda i, j: (i, 0)),
            pl.BlockSpec((D, W), lambda i, j: (0, j)),
            pl.BlockSpec((1, W), lambda i, j: (0, j)),
        ],
        out_specs=pl.BlockSpec((tm, W), lambda i, j: (i, j)),
        compiler_params=_params(("parallel", "arbitrary")),
        name="proj_qk",
    )(h, w, g)


VT_ONES = 16
VT_ROWS = DSA_HEAD_DIM + VT_ONES


def _vt_kernel(h_ref, wt_ref, o_ref):
    vt = lax.dot_general(wt_ref[...], h_ref[...], _NT, preferred_element_type=F32).astype(o_ref.dtype)
    ones = jnp.ones((VT_ONES, vt.shape[1]), o_ref.dtype)
    for hh in range(DSA_HEADS):
        o_ref[hh, :DSA_HEAD_DIM, :] = vt[hh * DSA_HEAD_DIM:(hh + 1) * DSA_HEAD_DIM]
        o_ref[hh, DSA_HEAD_DIM:, :] = ones


def _vt_proj(h, wt, B, S, tm):
    T, D = h.shape
    W = wt.shape[0]
    nt = S // tm
    return pl.pallas_call(
        _vt_kernel,
        out_shape=jax.ShapeDtypeStruct((B, DSA_HEADS, VT_ROWS, S), MXU_DTYPE),
        grid=(T // tm,),
        in_specs=[pl.BlockSpec((tm, D), lambda i: (i, 0)), pl.BlockSpec((W, D), lambda i: (0, 0))],
        out_specs=pl.BlockSpec((None, DSA_HEADS, VT_ROWS, tm), lambda i: (i // nt, 0, 0, i % nt)),
        compiler_params=_params(("parallel",)),
        name="proj_vt",
    )(h, wt)


IDX_Q_W = IDX_HEADS * IDX_DIM


def _idx_kernel(h_ref, w_ref, wt_ref, lng_ref, lnb_ref, iqt_ref, ki_ref, ga_ref, wit_ref):
    h = h_ref[...]
    z = jnp.dot(h, w_ref[...], preferred_element_type=F32)
    ik = z[:, :IDX_DIM]
    mu = jnp.mean(ik, axis=-1, keepdims=True)
    var = jnp.mean(jnp.square(ik - mu), axis=-1, keepdims=True)
    ki = (ik - mu) * lax.rsqrt(var + EPS) * lng_ref[...] + lnb_ref[...]
    ki_ref[...] = ki.astype(ki_ref.dtype)
    ga_ref[...] = z[:, LANES:LANES + GLA_GATE_RANK]
    zt = lax.dot_general(wt_ref[...], h, _NT, preferred_element_type=F32)
    iqt_ref[...] = zt[:IDX_Q_W].astype(iqt_ref.dtype)
    wit_ref[...] = zt[IDX_Q_W:IDX_Q_W + IDX_HEADS] * (IDX_HEADS ** -0.5 * IDX_DIM ** -0.5)


def _idx_proj(h, w, wt, lng, lnb, tm):
    T, D = h.shape
    return pl.pallas_call(
        _idx_kernel,
        out_shape=(
            jax.ShapeDtypeStruct((IDX_Q_W, T), MXU_DTYPE),
            jax.ShapeDtypeStruct((T, IDX_DIM), MXU_DTYPE),
            jax.ShapeDtypeStruct((T, GLA_GATE_RANK), F32),
            jax.ShapeDtypeStruct((IDX_HEADS, T), F32),
        ),
        grid=(T // tm,),
        in_specs=[
            pl.BlockSpec((tm, D), lambda i: (i, 0)),
            pl.BlockSpec(w.shape, lambda i: (0, 0)),
            pl.BlockSpec(wt.shape, lambda i: (0, 0)),
            pl.BlockSpec((1, IDX_DIM), lambda i: (0, 0)),
            pl.BlockSpec((1, IDX_DIM), lambda i: (0, 0)),
        ],
        out_specs=(
            pl.BlockSpec((IDX_Q_W, tm), lambda i: (0, i)),
            pl.BlockSpec((tm, IDX_DIM), lambda i: (i, 0)),
            pl.BlockSpec((tm, GLA_GATE_RANK), lambda i: (i, 0)),
            pl.BlockSpec((IDX_HEADS, tm), lambda i: (0, i)),
        ),
        compiler_params=_params(("parallel",)),
        name="proj_idx",
    )(h, w, wt, lng, lnb)


def _gla_kernel(q_ref, k_ref, v_ref, r_ref, ga_ref, wa2_ref, ba_ref, ng_ref, o_ref, st_ref, tot_ref, kd_ref,
                oraw_ref, *, n_chunks):
    tb = q_ref.shape[0]

    @pl.when(pl.program_id(1) == 0)
    def _():
        st_ref[...] = jnp.zeros_like(st_ref)

    x = _dot3(ga_ref[...], wa2_ref[...]) + ba_ref[...]
    la = (jnp.minimum(x, 0.0) - jnp.log(1.0 + jnp.exp(-jnp.abs(x)))) * (1.0 / GLA_GATE_TEMP)
    row = lax.broadcasted_iota(I32, (tb, tb), 0)
    col = lax.broadcasted_iota(I32, (tb, tb), 1)
    same = lax.shift_right_logical(row, 6) == lax.shift_right_logical(col, 6)
    ones_blk = jnp.where(same, 1.0, 0.0).astype(BF16)
    tril_blk = jnp.where(same, jnp.where(row >= col, 1.0, 0.0), 0.0).astype(BF16)
    la_hi, la_lo = _split(la)
    d = lambda u, v: jnp.dot(u, v, preferred_element_type=F32)
    tot = d(ones_blk, la_hi) + d(ones_blk, la_lo)
    cum = d(tril_blk, la_hi) + d(tril_blk, la_lo)
    tot_ref[...] = tot
    kd_ref[...] = (k_ref[...].astype(F32) * jnp.exp(tot - cum)).astype(kd_ref.dtype)

    heads = range(GLA_HEADS)
    ksl = lambda hh: slice(hh * GLA_DK, (hh + 1) * GLA_DK)
    vsl = lambda hh: slice(hh * GLA_DV, (hh + 1) * GLA_DV)

    def chunk(c, carry):
        rows = pl.ds(pl.multiple_of(c * CHUNK, CHUNK), CHUNK)
        first = pl.ds(pl.multiple_of(c * CHUNK, CHUNK), 1)
        upd = [_mm_tn(v_ref[rows, vsl(hh)], kd_ref[rows, ksl(hh)]) for hh in heads]
        st = [st_ref[hh] * jnp.exp(tot_ref[first, ksl(hh)]) + upd[hh] for hh in heads]
        for hh in heads:
            st_ref[hh] = st[hh]
        for hh in heads:
            oraw_ref[rows, vsl(hh)] = _mm_nt(q_ref[rows, ksl(hh)], st[hh])
        return carry

    lax.fori_loop(0, n_chunks, chunk, 0)

    ng = ng_ref[...]
    for hh in heads:
        o = oraw_ref[:, vsl(hh)] * (GLA_DK ** -0.5)
        ms = jnp.mean(o * o, axis=-1, keepdims=True)
        r = r_ref[:, vsl(hh)].astype(F32)
        o_ref[:, vsl(hh)] = (o * lax.rsqrt(ms + EPS) * ng * (r / (1.0 + jnp.exp(-r)))).astype(o_ref.dtype)


def _gla(zm, ga, wa2, ba, ng, B, S, tb):
    T = B * S
    nb = S // tb
    qk_w = GLA_HEADS * GLA_DK
    v_w = GLA_HEADS * GLA_DV
    tok = lambda b, i: b * nb + i
    return pl.pallas_call(
        functools.partial(_gla_kernel, n_chunks=tb // CHUNK),
        out_shape=jax.ShapeDtypeStruct((T, v_w), MXU_DTYPE),
        grid=(B, nb),
        in_specs=[
            pl.BlockSpec((tb, qk_w), lambda b, i: (tok(b, i), 0)),
            pl.BlockSpec((tb, qk_w), lambda b, i: (tok(b, i), 1)),
            pl.BlockSpec((tb, v_w), lambda b, i: (tok(b, i), 1)),
            pl.BlockSpec((tb, v_w), lambda b, i: (tok(b, i), 2)),
            pl.BlockSpec((tb, GLA_GATE_RANK), lambda b, i: (tok(b, i), 0)),
            pl.BlockSpec((GLA_GATE_RANK, qk_w), lambda b, i: (0, 0)),
            pl.BlockSpec((1, qk_w), lambda b, i: (0, 0)),
            pl.BlockSpec((1, GLA_DV), lambda b, i: (0, 0)),
        ],
        out_specs=pl.BlockSpec((tb, v_w), lambda b, i: (tok(b, i), 0)),
        scratch_shapes=[pltpu.VMEM((GLA_HEADS, GLA_DV, GLA_DK), F32), pltpu.VMEM((tb, qk_w), F32),
                        pltpu.VMEM((tb, qk_w), MXU_DTYPE), pltpu.VMEM((tb, v_w), F32)],
        compiler_params=_params(("parallel", "arbitrary")),
        name="gla",
    )(zm, zm, zm, zm, ga, wa2, ba, ng)


def _last_kb(qb, QB, KB):
    return ((qb + 1) * QB - 1) // KB


def _causal_steps(S, QB, KB):
    pairs = [(q, k) for q in range(S // QB) for k in range(_last_kb(q, QB, KB) + 1)]
    qs, ks = zip(*pairs)
    return jnp.asarray(qs, I32), jnp.asarray(ks, I32)


def _select_kernel(qb_ref, kb_ref, iqt_ref, wit_ref, ki_ref, mask_ref, hi_ref, lo_ref, gm_ref, *,
                   QB, KB, S, topk):
    qb = qb_ref[pl.program_id(1)]
    kb = kb_ref[pl.program_id(1)]
    last = _last_kb(qb, QB, KB)

    def rows_of(i):
        return pl.ds(pl.multiple_of(i * KB, KB), KB)

    @pl.when(kb == 0)
    def _():
        gm_ref[...] = jnp.full(gm_ref.shape, INT_MIN, I32)

    rc = 32 * 1024 // QB
    assert topk % rc == 0 and KB % rc == 0
    t_chunk = lax.shift_right_logical(qb * QB + lax.broadcasted_iota(I32, (rc, QB), 1), 6)
    for c in range(KB // rc):
        ki = ki_ref[c * rc:(c + 1) * rc, :]
        sc = jnp.zeros((rc, QB), F32)
        for hh in range(IDX_HEADS):
            lg = _mm(ki, iqt_ref[hh * IDX_DIM:(hh + 1) * IDX_DIM, :])
            sc = sc + jnp.maximum(lg, 0.0) * wit_ref[hh:hh + 1, :]
        s_chunk = lax.shift_right_logical(kb * KB + c * rc + lax.broadcasted_iota(I32, (rc, QB), 0), 6)
        bits = lax.bitcast_convert_type(sc, I32)
        sign = lax.shift_right_arithmetic(bits, 31)
        key = (bits ^ (sign & INT_MAX)) - sign
        key = jnp.where(s_chunk <= t_chunk, key, INT_MIN)
        rows = pl.ds(pl.multiple_of(kb * KB + c * rc, rc), rc)
        hi_ref[rows, :] = lax.shift_right_arithmetic(key, 16).astype(I16)
        lo_ref[rows, :] = ((key & 0xFFFF) + I16_MIN).astype(I16)
        g0 = (c * rc) % topk
        gm_ref[g0:g0 + rc, :] = jnp.maximum(gm_ref[g0:g0 + rc, :], key)

    @pl.when(kb == last)
    def _():
        nblk = last + 1
        slab = 32
        one, zero = jnp.int16(1), jnp.int16(0)

        def fold(m, reduce):
            part = m[0:slab]
            for r in range(1, KB // slab):
                part = reduce(part, m[r * slab:(r + 1) * slab])
            return part

        def count(pred):
            def body(i, acc):
                return acc + fold(pred(i), jnp.add)
            acc = lax.fori_loop(0, nblk, body, jnp.zeros((slab, QB), I16))
            return jnp.sum(acc.astype(I32), axis=0, keepdims=True)

        def count_ge(ref, thr):
            t16 = thr.astype(I16)
            return count(lambda i: jnp.where(ref[rows_of(i), :] >= t16, one, zero))

        def search(ref, target, lo, c_lo, hi, c_hi, rounds):
            def is_open(lo, c_lo, hi):
                return jnp.where(c_lo > target, jnp.where(hi != lo + 1, 1, 0), 0)

            def step(_, st):
                lo, c_lo, hi, c_hi = st
                open_ = is_open(lo, c_lo, hi) > 0
                mid = lax.shift_right_arithmetic(lo + hi, 1)
                c = count_ge(ref, mid)
                up = jnp.logical_and(open_, c >= target)
                dn = jnp.logical_and(open_, c < target)
                return jnp.where(up, mid, lo), jnp.where(up, c, c_lo), jnp.where(dn, mid, hi), jnp.where(dn, c, c_hi)

            def halvings_left(st):
                lo, c_lo, hi, _c_hi = st
                width = jnp.where(is_open(lo, c_lo, hi) > 0, hi - lo, 1)
                return jnp.max(32 - lax.clz(width - 1))

            st = (lo, c_lo, hi, c_hi)
            for most in rounds:
                st = lax.fori_loop(0, jnp.minimum(halvings_left(st), most), step, st)
            lo, c_lo, _hi, c_hi = st
            return lo, c_lo, c_hi

        zeros = jnp.zeros((1, QB), I32)
        gm = gm_ref[...]
        g_lo = jnp.maximum(lax.shift_right_arithmetic(jnp.min(gm, axis=0, keepdims=True), 16), I16_MIN + 1)
        g_hi = lax.shift_right_arithmetic(jnp.max(gm, axis=0, keepdims=True), 16) + 1
        hstar, ch_ge, ch_gt = search(hi_ref, topk, g_lo, count_ge(hi_ref, g_lo), g_hi, zeros, (16,))
        split = ch_ge > topk
        h16 = hstar.astype(I16)

        def low_half():
            def build(i, carry):
                lo_ref[rows_of(i), :] = jnp.where(hi_ref[rows_of(i), :] == h16, lo_ref[rows_of(i), :],
                                                  jnp.int16(I16_MIN))
                return carry
            lax.fori_loop(0, nblk, build, 0)
            target = jnp.where(split, topk - ch_gt, INT_MAX)
            return search(lo_ref, target, jnp.full((1, QB), I16_MIN, I32), ch_ge - ch_gt,
                          jnp.full((1, QB), I16_MAX + 1, I32), zeros, (12, 4))

        any_split = jnp.max(jnp.where(split, 1, 0)) > 0
        lstar, cl_ge, cl_gt = lax.cond(any_split, low_half,
                                       lambda: (jnp.full((1, QB), I16_MIN, I32), zeros, zeros))
        lstar = jnp.where(split, lstar, I16_MIN)
        l16 = lstar.astype(I16)
        c_ge = jnp.where(split, ch_gt + cl_ge, ch_ge)
        c_gt = jnp.where(split, ch_gt + cl_gt, ch_gt)
        excess = c_ge > topk
        need = jnp.where(excess, topk - c_gt, S + 1)
        row16 = lax.broadcasted_iota(I32, (KB, QB), 0).astype(I16)

        def before(i, j):
            return jnp.where(row16 + (i * KB).astype(I16) < j.astype(I16), one, zero)

        def tied(i, then):
            return jnp.where(hi_ref[rows_of(i), :] == h16, jnp.where(lo_ref[rows_of(i), :] == l16, then, zero), zero)

        def tie_cut():
            def step(_, c):
                lo, hi = c
                mid = lax.shift_right_arithmetic(lo + hi, 1)
                ok = count(lambda i: tied(i, before(i, mid))) >= need
                return jnp.where(ok, lo, mid + 1), jnp.where(ok, mid, hi)
            n_steps = max(1, math.ceil(math.log2(S + 1)))
            lo, _hi = lax.fori_loop(0, n_steps, step, (zeros, jnp.full((1, QB), S, I32)))
            return lo

        any_excess = jnp.max(jnp.where(excess, 1, 0)) > 0
        jcut = lax.cond(any_excess, tie_cut, lambda: jnp.full((1, QB), S, I32))

        def write(i, carry):
            hi_t, lo_t = hi_ref[rows_of(i), :], lo_ref[rows_of(i), :]
            in_bucket = jnp.where(lo_t > l16, one, jnp.where(lo_t == l16, before(i, jcut), zero))
            sel = jnp.where(hi_t > h16, one, jnp.where(hi_t == h16, in_bucket, zero))
            mask_ref[rows_of(i), :] = sel.astype(mask_ref.dtype)
            return carry

        lax.fori_loop(0, nblk, write, 0)

        def clear(i, carry):
            mask_ref[rows_of(i), :] = jnp.zeros((KB, QB), mask_ref.dtype)
            return carry

        lax.fori_loop(nblk, S // KB, clear, 0)


def _dsa_select(iqt, wit, ki, B, S, QB, KB):
    nq, nk = S // QB, S // KB
    topk = min(TOPK_MAX, S // 4)
    assert KB % topk == 0
    qs, ks = _causal_steps(S, QB, KB)
    return pl.pallas_call(
        functools.partial(_select_kernel, QB=QB, KB=KB, S=S, topk=topk),
        out_shape=jax.ShapeDtypeStruct((B, S, S), jnp.int8),
        grid_spec=pltpu.PrefetchScalarGridSpec(
            num_scalar_prefetch=2,
            grid=(B, qs.shape[0]),
            in_specs=[
                pl.BlockSpec((IDX_Q_W, QB), lambda b, s, qs, ks: (0, b * nq + qs[s])),
                pl.BlockSpec((IDX_HEADS, QB), lambda b, s, qs, ks: (0, b * nq + qs[s])),
                pl.BlockSpec((KB, IDX_DIM), lambda b, s, qs, ks: (b * nk + ks[s], 0)),
            ],
            out_specs=pl.BlockSpec((None, S, QB), lambda b, s, qs, ks: (b, 0, qs[s])),
            scratch_shapes=[pltpu.VMEM((S, QB), I16), pltpu.VMEM((S, QB), I16), pltpu.VMEM((topk, QB), I32)],
        ),
        compiler_params=_params(("parallel", "arbitrary")),
        name="dsa_select",
    )(qs, ks, iqt, wit, ki)


LOGIT_SAFE = 120.0


def _attn_kernel(qb_ref, kb_ref, small_ref, q_ref, k_ref, vt_ref, m_ref, o_ref, acc_ref, run_ref, *, QB, KB):
    qb = qb_ref[pl.program_id(1)]
    kb = kb_ref[pl.program_id(1)]
    last = _last_kb(qb, QB, KB)

    @pl.when(kb == 0)
    def _():
        acc_ref[...] = jnp.zeros_like(acc_ref)
        run_ref[...] = jnp.full(run_ref.shape, -1e30, F32)

    head = lambda hh: slice(hh * DSA_HEAD_DIM, (hh + 1) * DSA_HEAD_DIM)
    qk_dot = lambda hh: _mm_nt(k_ref[:, head(hh)], q_ref[:, head(hh)])

    @pl.when(small_ref[0] == 1)
    def _():
        mb = m_ref[...].astype(MXU_DTYPE)
        lg = qk_dot(0)
        for hh in range(DSA_HEADS):
            lg_next = qk_dot(hh + 1) if hh + 1 < DSA_HEADS else None
            p = jnp.exp2(lg).astype(MXU_DTYPE) * mb
            acc_ref[hh] += jnp.dot(vt_ref[hh], p, preferred_element_type=F32)
            lg = lg_next

    @pl.when(small_ref[0] != 1)
    def _():
        selected = m_ref[...].astype(F32) > 0.0

        def logits(hh):
            lg = jnp.where(selected, qk_dot(hh), -jnp.inf)
            return lg, jnp.max(lg, axis=0, keepdims=True)

        nxt = logits(0)
        for hh in range(DSA_HEADS):
            lg, top = nxt
            nxt = logits(hh + 1) if hh + 1 < DSA_HEADS else None
            run_old = run_ref[hh]
            run_new = jnp.maximum(run_old, top)
            p = jnp.exp2(lg - run_new).astype(MXU_DTYPE)
            acc_ref[hh] = acc_ref[hh] * jnp.exp2(run_old - run_new) + jnp.dot(vt_ref[hh], p,
                                                                              preferred_element_type=F32)
            run_ref[hh] = run_new

    @pl.when(kb == last)
    def _():
        for hh in range(DSA_HEADS):
            acc = acc_ref[hh]
            o = acc[:DSA_HEAD_DIM] / acc[DSA_HEAD_DIM:DSA_HEAD_DIM + 1]
            o_ref[:, hh * DSA_HEAD_DIM:(hh + 1) * DSA_HEAD_DIM] = o.T.astype(o_ref.dtype)


def _dsa_attn(qk, vt, mask, logit_bound, B, S, QB, KB):
    T = B * S
    nq, nk = S // QB, S // KB
    W = DSA_HEADS * DSA_HEAD_DIM
    qs, ks = _causal_steps(S, QB, KB)
    small = (logit_bound <= LOGIT_SAFE).astype(I32).reshape(1)
    return pl.pallas_call(
        functools.partial(_attn_kernel, QB=QB, KB=KB),
        out_shape=jax.ShapeDtypeStruct((T, W), MXU_DTYPE),
        grid_spec=pltpu.PrefetchScalarGridSpec(
            num_scalar_prefetch=3,
            grid=(B, qs.shape[0]),
            in_specs=[
                pl.BlockSpec((QB, W), lambda b, s, qs, ks, sm: (b * nq + qs[s], 0)),
                pl.BlockSpec((KB, W), lambda b, s, qs, ks, sm: (b * nk + ks[s], 1)),
                pl.BlockSpec((None, DSA_HEADS, VT_ROWS, KB), lambda b, s, qs, ks, sm: (b, 0, 0, ks[s])),
                pl.BlockSpec((None, KB, QB), lambda b, s, qs, ks, sm: (b, ks[s], qs[s])),
            ],
            out_specs=pl.BlockSpec((QB, W), lambda b, s, qs, ks, sm: (b * nq + qs[s], 0)),
            scratch_shapes=[pltpu.VMEM((DSA_HEADS, VT_ROWS, QB), F32), pltpu.VMEM((DSA_HEADS, 1, QB), F32)],
        ),
        compiler_params=_params(("parallel", "arbitrary")),
        name="dsa_attn",
    )(qs, ks, small, qk, qk, vt, mask)


def _post_kernel(og_ref, od_ref, ga_ref, gb_ref, x_ref, pa_ref, pb_ref, wo_ref, g2_ref, wr_ref, br_ref,
                 x1_ref, h2_ref, route_ref):
    sig = lambda v: 1.0 / (1.0 + jnp.exp(-v))
    a = jnp.dot(og_ref[...], pa_ref[...], preferred_element_type=F32)
    b = jnp.dot(od_ref[...], pb_ref[...], preferred_element_type=F32)
    mix = sig(ga_ref[...].astype(F32)) * a + sig(gb_ref[...].astype(F32)) * b
    x1 = x_ref[...] + _mm(mix, wo_ref[...])
    x1_ref[...] = x1
    ms = jnp.mean(x1 * x1, axis=-1, keepdims=True)
    h2 = x1 * lax.rsqrt(ms + EPS) * g2_ref[...]
    h2_ref[...] = h2.astype(h2_ref.dtype)

    logits = _dot3(h2, wr_ref[...]) + br_ref[...]
    tm = logits.shape[0]
    lane = lax.broadcasted_iota(I32, (tm, LANES), 1).astype(F32)
    neg = -jnp.inf
    far = float(2 * LANES)
    rmax = lambda v: jnp.max(v, axis=1, keepdims=True)
    rmin = lambda v: jnp.min(v, axis=1, keepdims=True)
    gl = jnp.where(lane >= N_EXPERTS, jnp.where(lane < N_EXPERTS + N_GROUPS, logits, neg), neg)
    gmax = rmax(gl)
    g_w = 1.0 / jnp.sum(jnp.exp(gl - gmax), axis=1, keepdims=True)
    g_idx = rmin(jnp.where(gl == gmax, lane, far)) - N_EXPERTS
    e_lo = g_idx * EXPERTS_PER_GROUP
    el = jnp.where(lane >= e_lo, jnp.where(lane < e_lo + EXPERTS_PER_GROUP, logits, neg), neg)
    m1 = rmax(el)
    e1 = rmin(jnp.where(el == m1, lane, far))
    el2 = jnp.where(lane == e1, neg, el)
    m2 = rmax(el2)
    e2 = rmin(jnp.where(el2 == m2, lane, far))
    p2 = jnp.exp(m2 - m1)
    w1 = g_w / (1.0 + p2)
    route_ref[...] = jnp.where(lane == 0.0, e1, jnp.where(lane == 1.0, e2, jnp.where(
        lane == 2.0, w1, jnp.where(lane == 3.0, w1 * p2, 0.0))))


def _post(og, od, zm, x2, pa, pb, wo, g2, wr, br, tm):
    T, D = x2.shape
    row = lambda i: (i, 0)
    full = lambda i: (0, 0)
    return pl.pallas_call(
        _post_kernel,
        out_shape=(
            jax.ShapeDtypeStruct((T, D), F32),
            jax.ShapeDtypeStruct((T, D), F32),
            jax.ShapeDtypeStruct((T, LANES), F32),
        ),
        grid=(T // tm,),
        in_specs=[
            pl.BlockSpec((tm, D), row),
            pl.BlockSpec((tm, D), row),
            pl.BlockSpec((tm, D), lambda i: (i, 3)),
            pl.BlockSpec((tm, D), lambda i: (i, 4)),
            pl.BlockSpec((tm, D), row),
            pl.BlockSpec((D, D), full),
            pl.BlockSpec((D, D), full),
            pl.BlockSpec((D, D), full),
            pl.BlockSpec((1, D), full),
            pl.BlockSpec((D, LANES), full),
            pl.BlockSpec((1, LANES), full),
        ],
        out_specs=(pl.BlockSpec((tm, D), row), pl.BlockSpec((tm, D), row), pl.BlockSpec((tm, LANES), row)),
        compiler_params=_params(("parallel",)),
        name="merge_router",
    )(og, od, zm, zm, x2, pa, pb, wo, g2, wr, br)


MOE_TM = 256


def _gather_combine_kernel(idx_ref, nxt_ref, x1_ref, route_ref, src_hbm, o_ref, buf_ref, sem):
    i = pl.program_id(0)
    n = pl.num_programs(0)
    tm = o_ref.shape[0]
    slot = lax.rem(i, 2)

    def fetch(ids_ref, s):
        def start(r, carry):
            for k in range(2):
                pltpu.make_async_copy(src_hbm.at[pl.ds(ids_ref[0, 0, 2 * r + k], 1), :],
                                      buf_ref.at[s, pl.ds(k * tm + r, 1), :], sem.at[s]).start(priority=k)
            return carry
        lax.fori_loop(0, tm, start, 0, unroll=8)

    @pl.when(i == 0)
    def _():
        fetch(idx_ref, 0)

    @pl.when(i + 1 < n)
    def _():
        fetch(nxt_ref, 1 - slot)

    pltpu.make_async_copy(src_hbm.at[pl.ds(0, 2 * tm), :], buf_ref.at[slot], sem.at[slot]).wait()
    r = route_ref[...]
    lane = lax.broadcasted_iota(I32, r.shape, 1)
    w1 = jnp.sum(jnp.where(lane == 2, r, 0.0), axis=1, keepdims=True)
    w2 = jnp.sum(jnp.where(lane == 3, r, 0.0), axis=1, keepdims=True)
    o_ref[...] = x1_ref[...] + (w1 * buf_ref[slot, :tm, :] + w2 * buf_ref[slot, tm:, :])


def _gather_combine(x1, ys, pos, route, tm):
    T, D = x1.shape
    nt = T // tm
    idx3 = pos.reshape(nt, 1, 2 * tm)
    row = lambda i: (i, 0)
    return pl.pallas_call(
        _gather_combine_kernel,
        out_shape=jax.ShapeDtypeStruct((T, D), F32),
        grid=(nt,),
        in_specs=[
            pl.BlockSpec((1, 1, 2 * tm), lambda i: (i, 0, 0), memory_space=pltpu.SMEM),
            pl.BlockSpec((1, 1, 2 * tm), lambda i: (jnp.minimum(i + 1, nt - 1), 0, 0), memory_space=pltpu.SMEM),
            pl.BlockSpec((tm, D), row),
            pl.BlockSpec((tm, LANES), row),
            pl.BlockSpec(memory_space=pl.ANY),
        ],
        out_specs=pl.BlockSpec((tm, D), row),
        scratch_shapes=[pltpu.VMEM((2, 2 * tm, D), ys.dtype), pltpu.SemaphoreType.DMA((2,))],
        compiler_params=_params(("arbitrary",)),
        name="moe_gather_combine",
    )(idx3, idx3, x1, route, ys)


def _scatter_kernel(idx_ref, x_ref, o_hbm, sem, *, fanout):
    tm = x_ref.shape[0]

    def start(r, carry):
        for k in range(fanout):
            dst = idx_ref[0, 0, fanout * r + k]
            pltpu.make_async_copy(x_ref.at[pl.ds(r, 1), :], o_hbm.at[pl.ds(dst, 1), :], sem).start(priority=k)
        return carry

    lax.fori_loop(0, tm, start, 0, unroll=8)
    for k in range(fanout):
        pltpu.make_async_copy(x_ref, o_hbm.at[pl.ds(0, tm), :], sem).wait()


def _row_scatter(x, idx, fanout, tm, name):
    n, D = x.shape
    return pl.pallas_call(
        functools.partial(_scatter_kernel, fanout=fanout),
        out_shape=jax.ShapeDtypeStruct((fanout * n, D), x.dtype),
        grid=(n // tm,),
        in_specs=[
            pl.BlockSpec((1, 1, fanout * tm), lambda i: (i, 0, 0), memory_space=pltpu.SMEM),
            pl.BlockSpec((tm, D), lambda i: (i, 0)),
        ],
        out_specs=pl.BlockSpec(memory_space=pl.ANY),
        scratch_shapes=[pltpu.SemaphoreType.DMA(())],
        compiler_params=_params(("arbitrary",)),
        name=name,
    )(idx.reshape(n // tm, 1, fanout * tm), x)


def _route_plan(route, tm):
    n = 2 * route.shape[0]
    n_tiles = n // tm
    ids = route[:, :2].astype(I32).reshape(n)
    onehot = (ids[:, None] == jnp.arange(N_EXPERTS, dtype=I32)[None, :]).astype(F32).reshape(n_tiles, tm, N_EXPERTS)
    tril = (jnp.arange(tm)[:, None] >= jnp.arange(tm)[None, :]).astype(F32)
    in_tile = jnp.einsum("rc,tce->tre", tril, onehot)
    per_tile = in_tile[:, -1, :]
    before = jnp.cumsum(per_tile, axis=0) - per_tile
    counts = jnp.sum(per_tile, axis=0).astype(I32)
    starts = (jnp.cumsum(counts) - counts).astype(I32)
    rank = jnp.sum(onehot * (in_tile + before[:, None, :] - 1.0), axis=2).reshape(n)
    pos = (jnp.sum(onehot.reshape(n, N_EXPERTS) * starts[None, :].astype(F32), axis=1) + rank).astype(I32)
    bounds = jnp.sort(jnp.concatenate([jnp.arange(n_tiles, dtype=I32) * tm, starts]))
    ends = jnp.concatenate([bounds[1:], jnp.full((1,), n, I32)])
    seg_tile = jnp.minimum(bounds // tm, n_tiles - 1)
    seg_exp = jnp.clip(jnp.sum((starts[None, :] <= bounds[:, None]).astype(I32), axis=1) - 1, 0, N_EXPERTS - 1)
    seg_first = jnp.concatenate([jnp.ones((1,), I32), (seg_tile[1:] != seg_tile[:-1]).astype(I32)])
    return pos, (seg_tile, seg_exp, bounds - seg_tile * tm, ends - seg_tile * tm, seg_first)


def _experts_kernel(tile_ref, exp_ref, lo_ref, hi_ref, first_ref, x_ref, wg_ref, wu_ref, wd_ref, o_ref):
    s = pl.program_id(0)
    lo, hi = lo_ref[s], hi_ref[s]

    @pl.when(first_ref[s] == 1)
    def _():
        o_ref[...] = jnp.zeros_like(o_ref)

    @pl.when(hi > lo)
    def _():
        x = x_ref[...].astype(MXU_DTYPE)
        a = jnp.dot(x, wg_ref[...], preferred_element_type=F32)
        u = jnp.dot(x, wu_ref[...], preferred_element_type=F32)
        y = _mm(a / (1.0 + jnp.exp(-a)) * u, wd_ref[...])
        row = lax.broadcasted_iota(I32, y.shape, 0)
        o_ref[...] += jnp.where(row >= lo, jnp.where(row < hi, y, 0.0), 0.0)


def _experts(xs, segs, wg, wu, wd, tm):
    n, D = xs.shape
    F = wg.shape[2]
    tile_of = lambda s, tile, exp, lo, hi, first: (tile[s], 0)
    w_of = lambda s, tile, exp, lo, hi, first: (exp[s], 0, 0)
    return pl.pallas_call(
        _experts_kernel,
        out_shape=jax.ShapeDtypeStruct((n, D), F32),
        grid_spec=pltpu.PrefetchScalarGridSpec(
            num_scalar_prefetch=5,
            grid=(segs[0].shape[0],),
            in_specs=[
                pl.BlockSpec((tm, D), tile_of),
                pl.BlockSpec((None, D, F), w_of),
                pl.BlockSpec((None, D, F), w_of),
                pl.BlockSpec((None, F, D), w_of),
            ],
            out_specs=pl.BlockSpec((tm, D), tile_of),
        ),
        compiler_params=_params(("arbitrary",)),
        name="experts",
    )(*segs, xs, wg, wu, wd)


def _moe(h2, route, x1, wg, wu, wd):
    T, D = x1.shape
    tm = min(MOE_TM, T)
    pos, segs = _route_plan(route, tm)
    tok = min(2 * MOE_TM, T)
    xs = _row_scatter(h2, pos, 2, tok, "moe_scatter")
    ys = _experts(xs, segs, wg, wu, wd, tm)
    return _gather_combine(x1, ys, pos, route, tok)


def _pad_cols(w, n):
    return jnp.pad(w, ((0, 0), (0, n - w.shape[1])))


def _layer(x2, B, S, p):
    T, D = x2.shape
    assert D == 2 * GLA_HEADS * GLA_DK == GLA_HEADS * GLA_DV == DSA_HEADS * DSA_HEAD_DIM
    assert S % 512 == 0
    cd = MXU_DTYPE
    qk_w, v_w, dsa_w = GLA_HEADS * GLA_DK, GLA_HEADS * GLA_DV, DSA_HEADS * DSA_HEAD_DIM
    splits = (qk_w, qk_w, v_w, GLA_GATE_RANK, v_w, dsa_w, dsa_w, dsa_w, IDX_Q_W, IDX_DIM, IDX_HEADS, D, D)
    offs = [0]
    for s in splits:
        offs.append(offs[-1] + s)
    w_in = p["w_in"]
    (w_gq, w_gk, w_gv, w_ga, w_gr, w_dq, w_dk, w_dv, w_iq, w_ik, w_iw, w_ta, w_tb) = [
        w_in[:, offs[i]:offs[i + 1]] for i in range(len(splits))]

    w_main = jnp.concatenate([w_gq, w_gk, w_gv, w_gr, w_ta, w_tb], axis=1).astype(cd)
    w_qk = jnp.concatenate([w_dq, w_dk], axis=1).astype(cd)
    g_qk = jnp.concatenate([jnp.tile(p["dsa_q_norm_g"] * (DSA_HEAD_DIM ** -0.5 * math.log2(math.e)), DSA_HEADS),
                            jnp.tile(p["dsa_k_norm_g"], DSA_HEADS)]).reshape(1, 2 * dsa_w).astype(F32)
    w_vt = w_dv.T.astype(cd)
    w_idx = jnp.concatenate([_pad_cols(w_ik, LANES), _pad_cols(w_ga, LANES)], axis=1).astype(cd)
    w_idxt = jnp.pad(jnp.concatenate([w_iq, w_iw], axis=1).T, ((0, 16 - IDX_HEADS), (0, 0))).astype(cd)

    tm = min(1024, T)
    h = _rmsnorm(x2, p["norm1_g"], tm)
    zm = _matmul(h, w_main, tm, 1024, cd)
    qk = _qk_proj(h, w_qk, g_qk, tm)
    vt = _vt_proj(h, w_vt, B, S, min(1024, S))
    iqt, ki, ga, wit = _idx_proj(h, w_idx, w_idxt, p["idx_k_ln_g"].reshape(1, -1), p["idx_k_ln_b"].reshape(1, -1),
                                 tm)

    o_gla = _gla(zm, ga, p["gla_w_a2"], p["gla_b_a"].reshape(1, -1), p["gla_norm_g"].reshape(1, -1), B, S, 256)

    QB, KB = 256, min(1024, S)
    mask = _dsa_select(iqt, wit, ki, B, S, QB, KB)
    logit_bound = (DSA_HEAD_DIM * jnp.max(jnp.abs(g_qk[0, :dsa_w])) * jnp.max(jnp.abs(g_qk[0, dsa_w:])))
    o_dsa = _dsa_attn(qk, vt, mask, logit_bound, B, S, 2 * QB, KB)

    w_r = _pad_cols(jnp.concatenate([p["w_router_expert"], p["w_router_group"]], axis=1), LANES)
    b_r = _pad_cols(jnp.concatenate([p["b_router_expert"], p["b_router_group"]]).reshape(1, -1), LANES)
    x1, h2, route = _post(o_gla, o_dsa, zm, x2, p["w_branch_gla"].astype(cd), p["w_branch_dsa"].astype(cd),
                          p["w_out"].astype(cd), p["norm2_g"].reshape(1, -1), w_r, b_r, min(512, T))
    return _moe(h2, route, x1, p["w_exp_gate"].astype(cd), p["w_exp_up"].astype(cd), p["w_exp_down"].astype(cd))


def kernel(x, norm1_g, w_in, gla_w_a2, gla_b_a, gla_norm_g, dsa_q_norm_g, dsa_k_norm_g, idx_k_ln_g, idx_k_ln_b,
           w_branch_gla, w_branch_dsa, w_out, norm2_g, w_router_group, b_router_group, w_router_expert,
           b_router_expert, w_exp_gate, w_exp_up, w_exp_down):
    B, S, D = x.shape
    stacked = dict(norm1_g=norm1_g, w_in=w_in, gla_w_a2=gla_w_a2, gla_b_a=gla_b_a, gla_norm_g=gla_norm_g,
                   dsa_q_norm_g=dsa_q_norm_g, dsa_k_norm_g=dsa_k_norm_g, idx_k_ln_g=idx_k_ln_g,
                   idx_k_ln_b=idx_k_ln_b, w_branch_gla=w_branch_gla, w_branch_dsa=w_branch_dsa, w_out=w_out,
                   norm2_g=norm2_g, w_router_group=w_router_group, b_router_group=b_router_group,
                   w_router_expert=w_router_expert, b_router_expert=b_router_expert, w_exp_gate=w_exp_gate,
                   w_exp_up=w_exp_up, w_exp_down=w_exp_down)
    x2 = x.reshape(B * S, D).astype(F32)
    for l in range(w_in.shape[0]):
        x2 = _layer(x2, B, S, {k: v[l] for k, v in stacked.items()})
    return x2.reshape(B, S, D).astype(x.dtype)
```

```python
import functools
import math

import jax
import jax.numpy as jnp
from jax import lax
from jax.experimental import pallas as pl
from jax.experimental.pallas import tpu as pltpu

F32 = jnp.float32
BF16 = jnp.bfloat16
I32 = jnp.int32
MXU_DTYPE = BF16

CHUNK = 64
EPS = 1e-6
GLA_HEADS, GLA_DK, GLA_DV = 4, 128, 256
GLA_GATE_RANK = 16
GLA_GATE_TEMP = 16.0
DSA_HEADS, DSA_HEAD_DIM = 8, 128
IDX_HEADS, IDX_DIM = 8, 64
TOPK_MAX = 256
N_GROUPS, EXPERTS_PER_GROUP = 4, 8
N_EXPERTS = N_GROUPS * EXPERTS_PER_GROUP
LANES = 128

INT_MIN = -(2 ** 31)
INT_MAX = 2 ** 31 - 1
I16 = jnp.int16
I16_MIN, I16_MAX = -(2 ** 15), 2 ** 15 - 1

VMEM_LIMIT = 56 * 1024 * 1024

_NT = (((1,), (1,)), ((), ()))
_TN = (((0,), (0,)), ((), ()))


def _params(sem):
    return pltpu.CompilerParams(dimension_semantics=sem, vmem_limit_bytes=VMEM_LIMIT)


def _mm(a, b):
    return jnp.dot(a.astype(MXU_DTYPE), b.astype(MXU_DTYPE), preferred_element_type=F32)


def _mm_nt(a, b):
    return lax.dot_general(a.astype(MXU_DTYPE), b.astype(MXU_DTYPE), _NT, preferred_element_type=F32)


def _mm_tn(a, b):
    return lax.dot_general(a.astype(MXU_DTYPE), b.astype(MXU_DTYPE), _TN, preferred_element_type=F32)


def _split(a):
    hi = a.astype(BF16)
    lo = (a - hi.astype(F32)).astype(BF16)
    return hi, lo


def _dot3(a, b):
    ah, al = _split(a)
    bh, bl = _split(b)
    d = lambda u, v: jnp.dot(u, v, preferred_element_type=F32)
    return d(ah, bh) + (d(ah, bl) + d(al, bh))


def _rmsnorm_kernel(x_ref, g_ref, o_ref):
    x = x_ref[...]
    ms = jnp.mean(x * x, axis=-1, keepdims=True)
    o_ref[...] = (x * lax.rsqrt(ms + EPS) * g_ref[...]).astype(o_ref.dtype)


def _rmsnorm(x2, g, tm):
    T, D = x2.shape
    return pl.pallas_call(
        _rmsnorm_kernel,
        out_shape=jax.ShapeDtypeStruct((T, D), MXU_DTYPE),
        grid=(T // tm,),
        in_specs=[pl.BlockSpec((tm, D), lambda i: (i, 0)), pl.BlockSpec((1, D), lambda i: (0, 0))],
        out_specs=pl.BlockSpec((tm, D), lambda i: (i, 0)),
        compiler_params=_params(("parallel",)),
        name="rmsnorm",
    )(x2, g.reshape(1, D))


def _matmul_kernel(h_ref, w_ref, o_ref):
    o_ref[...] = jnp.dot(h_ref[...], w_ref[...], preferred_element_type=F32).astype(o_ref.dtype)


def _matmul(h, w, tm, tn, out_dtype):
    T, D = h.shape
    N = w.shape[1]
    return pl.pallas_call(
        _matmul_kernel,
        out_shape=jax.ShapeDtypeStruct((T, N), out_dtype),
        grid=(T // tm, N // tn),
        in_specs=[pl.BlockSpec((tm, D), lambda i, j: (i, 0)), pl.BlockSpec((D, tn), lambda i, j: (0, j))],
        out_specs=pl.BlockSpec((tm, tn), lambda i, j: (i, j)),
        compiler_params=_params(("parallel", "arbitrary")),
        name="proj_main",
    )(h, w)


def _qk_kernel(h_ref, w_ref, g_ref, o_ref):
    z = jnp.dot(h_ref[...], w_ref[...], preferred_element_type=F32)
    for hh in range(DSA_HEADS):
        sl = slice(hh * DSA_HEAD_DIM, (hh + 1) * DSA_HEAD_DIM)
        zh = z[:, sl]
        ms = jnp.mean(zh * zh, axis=-1, keepdims=True)
        o_ref[:, sl] = (zh * lax.rsqrt(ms + EPS) * g_ref[:, sl]).astype(o_ref.dtype)


def _qk_proj(h, w, g, tm):
    T, D = h.shape
    W = DSA_HEADS * DSA_HEAD_DIM
    return pl.pallas_call(
        _qk_kernel,
        out_shape=jax.ShapeDtypeStruct((T, 2 * W), MXU_DTYPE),
        grid=(T // tm, 2),
        in_specs=[
            pl.BlockSpec((tm, D), lambda i, j: (i, 0)),
            pl.BlockSpec((D, W), lambda i, j: (0, j)),
            pl.BlockSpec((1, W), lambda i, j: (0, j)),
        ],
        out_specs=pl.BlockSpec((tm, W), lambda i, j: (i, j)),
        compiler_params=_params(("parallel", "arbitrary")),
        name="proj_qk",
    )(h, w, g)


VT_ONES = 16
VT_ROWS = DSA_HEAD_DIM + VT_ONES


def _vt_kernel(h_ref, wt_ref, o_ref):
    vt = lax.dot_general(wt_ref[...], h_ref[...], _NT, preferred_element_type=F32).astype(o_ref.dtype)
    ones = jnp.ones((VT_ONES, vt.shape[1]), o_ref.dtype)
    for hh in range(DSA_HEADS):
        o_ref[hh, :DSA_HEAD_DIM, :] = vt[hh * DSA_HEAD_DIM:(hh + 1) * DSA_HEAD_DIM]
        o_ref[hh, DSA_HEAD_DIM:, :] = ones


def _vt_proj(h, wt, B, S, tm):
    T, D = h.shape
    W = wt.shape[0]
    nt = S // tm
    return pl.pallas_call(
        _vt_kernel,
        out_shape=jax.ShapeDtypeStruct((B, DSA_HEADS, VT_ROWS, S), MXU_DTYPE),
        grid=(T // tm,),
        in_specs=[pl.BlockSpec((tm, D), lambda i: (i, 0)), pl.BlockSpec((W, D), lambda i: (0, 0))],
        out_specs=pl.BlockSpec((None, DSA_HEADS, VT_ROWS, tm), lambda i: (i // nt, 0, 0, i % nt)),
        compiler_params=_params(("parallel",)),
        name="proj_vt",
    )(h, wt)


IDX_Q_W = IDX_HEADS * IDX_DIM


def _idx_kernel(h_ref, w_ref, wt_ref, lng_ref, lnb_ref, iqt_ref, ki_ref, ga_ref, wit_ref):
    h = h_ref[...]
    z = jnp.dot(h, w_ref[...], preferred_element_type=F32)
    ik = z[:, :IDX_DIM]
    mu = jnp.mean(ik, axis=-1, keepdims=True)
    var = jnp.mean(jnp.square(ik - mu), axis=-1, keepdims=True)
    ki = (ik - mu) * lax.rsqrt(var + EPS) * lng_ref[...] + lnb_ref[...]
    ki_ref[...] = ki.astype(ki_ref.dtype)
    ga_ref[...] = z[:, LANES:LANES + GLA_GATE_RANK]
    zt = lax.dot_general(wt_ref[...], h, _NT, preferred_element_type=F32)
    iqt_ref[...] = zt[:IDX_Q_W].astype(iqt_ref.dtype)
    wit_ref[...] = zt[IDX_Q_W:IDX_Q_W + IDX_HEADS] * (IDX_HEADS ** -0.5 * IDX_DIM ** -0.5)


def _idx_proj(h, w, wt, lng, lnb, tm):
    T, D = h.shape
    return pl.pallas_call(
        _idx_kernel,
        out_shape=(
            jax.ShapeDtypeStruct((IDX_Q_W, T), MXU_DTYPE),
            jax.ShapeDtypeStruct((T, IDX_DIM), MXU_DTYPE),
            jax.ShapeDtypeStruct((T, GLA_GATE_RANK), F32),
            jax.ShapeDtypeStruct((IDX_HEADS, T), F32),
        ),
        grid=(T // tm,),
        in_specs=[
            pl.BlockSpec((tm, D), lambda i: (i, 0)),
            pl.BlockSpec(w.shape, lambda i: (0, 0)),
            pl.BlockSpec(wt.shape, lambda i: (0, 0)),
            pl.BlockSpec((1, IDX_DIM), lambda i: (0, 0)),
            pl.BlockSpec((1, IDX_DIM), lambda i: (0, 0)),
        ],
        out_specs=(
            pl.BlockSpec((IDX_Q_W, tm), lambda i: (0, i)),
            pl.BlockSpec((tm, IDX_DIM), lambda i: (i, 0)),
            pl.BlockSpec((tm, GLA_GATE_RANK), lambda i: (i, 0)),
            pl.BlockSpec((IDX_HEADS, tm), lambda i: (0, i)),
        ),
        compiler_params=_params(("parallel",)),
        name="proj_idx",
    )(h, w, wt, lng, lnb)


def _gla_kernel(q_ref, k_ref, v_ref, r_ref, ga_ref, wa2_ref, ba_ref, ng_ref, o_ref, st_ref, tot_ref, kd_ref,
                oraw_ref, *, n_chunks):
    tb = q_ref.shape[0]

    @pl.when(pl.program_id(1) == 0)
    def _():
        st_ref[...] = jnp.zeros_like(st_ref)

    x = _dot3(ga_ref[...], wa2_ref[...]) + ba_ref[...]
    la = (jnp.minimum(x, 0.0) - jnp.log(1.0 + jnp.exp(-jnp.abs(x)))) * (1.0 / GLA_GATE_TEMP)
    row = lax.broadcasted_iota(I32, (tb, tb), 0)
    col = lax.broadcasted_iota(I32, (tb, tb), 1)
    same = lax.shift_right_logical(row, 6) == lax.shift_right_logical(col, 6)
    ones_blk = jnp.where(same, 1.0, 0.0).astype(BF16)
    tril_blk = jnp.where(same, jnp.where(row >= col, 1.0, 0.0), 0.0).astype(BF16)
    la_hi, la_lo = _split(la)
    d = lambda u, v: jnp.dot(u, v, preferred_element_type=F32)
    tot = d(ones_blk, la_hi) + d(ones_blk, la_lo)
    cum = d(tril_blk, la_hi) + d(tril_blk, la_lo)
    tot_ref[...] = tot
    kd_ref[...] = (k_ref[...].astype(F32) * jnp.exp(tot - cum)).astype(kd_ref.dtype)

    heads = range(GLA_HEADS)
    ksl = lambda hh: slice(hh * GLA_DK, (hh + 1) * GLA_DK)
    vsl = lambda hh: slice(hh * GLA_DV, (hh + 1) * GLA_DV)

    def chunk(c, carry):
        rows = pl.ds(pl.multiple_of(c * CHUNK, CHUNK), CHUNK)
        first = pl.ds(pl.multiple_of(c * CHUNK, CHUNK), 1)
        upd = [_mm_tn(v_ref[rows, vsl(hh)], kd_ref[rows, ksl(hh)]) for hh in heads]
        st = [st_ref[hh] * jnp.exp(tot_ref[first, ksl(hh)]) + upd[hh] for hh in heads]
        for hh in heads:
            st_ref[hh] = st[hh]
        for hh in heads:
            oraw_ref[rows, vsl(hh)] = _mm_nt(q_ref[rows, ksl(hh)], st[hh])
        return carry

    lax.fori_loop(0, n_chunks, chunk, 0)

    ng = ng_ref[...]
    for hh in heads:
        o = oraw_ref[:, vsl(hh)] * (GLA_DK ** -0.5)
        ms = jnp.mean(o * o, axis=-1, keepdims=True)
        r = r_ref[:, vsl(hh)].astype(F32)
        o_ref[:, vsl(hh)] = (o * lax.rsqrt(ms + EPS) * ng * (r / (1.0 + jnp.exp(-r)))).astype(o_ref.dtype)


def _gla(zm, ga, wa2, ba, ng, B, S, tb):
    T = B * S
    nb = S // tb
    qk_w = GLA_HEADS * GLA_DK
    v_w = GLA_HEADS * GLA_DV
    tok = lambda b, i: b * nb + i
    return pl.pallas_call(
        functools.partial(_gla_kernel, n_chunks=tb // CHUNK),
        out_shape=jax.ShapeDtypeStruct((T, v_w), MXU_DTYPE),
        grid=(B, nb),
        in_specs=[
            pl.BlockSpec((tb, qk_w), lambda b, i: (tok(b, i), 0)),
            pl.BlockSpec((tb, qk_w), lambda b, i: (tok(b, i), 1)),
            pl.BlockSpec((tb, v_w), lambda b, i: (tok(b, i), 1)),
            pl.BlockSpec((tb, v_w), lambda b, i: (tok(b, i), 2)),
            pl.BlockSpec((tb, GLA_GATE_RANK), lambda b, i: (tok(b, i), 0)),
            pl.BlockSpec((GLA_GATE_RANK, qk_w), lambda b, i: (0, 0)),
            pl.BlockSpec((1, qk_w), lambda b, i: (0, 0)),
            pl.BlockSpec((1, GLA_DV), lambda b, i: (0, 0)),
        ],
        out_specs=pl.BlockSpec((tb, v_w), lambda b, i: (tok(b, i), 0)),
        scratch_shapes=[pltpu.VMEM((GLA_HEADS, GLA_DV, GLA_DK), F32), pltpu.VMEM((tb, qk_w), F32),
                        pltpu.VMEM((tb, qk_w), MXU_DTYPE), pltpu.VMEM((tb, v_w), F32)],
        compiler_params=_params(("parallel", "arbitrary")),
        name="gla",
    )(zm, zm, zm, zm, ga, wa2, ba, ng)


def _last_kb(qb, QB, KB):
    return ((qb + 1) * QB - 1) // KB


def _causal_steps(S, QB, KB):
    pairs = [(q, k) for q in range(S // QB) for k in range(_last_kb(q, QB, KB) + 1)]
    qs, ks = zip(*pairs)
    return jnp.asarray(qs, I32), jnp.asarray(ks, I32)


def _select_kernel(qb_ref, kb_ref, iqt_ref, wit_ref, ki_ref, mask_ref, hi_ref, lo_ref, gm_ref, *,
                   QB, KB, S, topk):
    qb = qb_ref[pl.program_id(1)]
    kb = kb_ref[pl.program_id(1)]
    last = _last_kb(qb, QB, KB)

    def rows_of(i):
        return pl.ds(pl.multiple_of(i * KB, KB), KB)

    @pl.when(kb == 0)
    def _():
        gm_ref[...] = jnp.full(gm_ref.shape, INT_MIN, I32)

    rc = 32 * 1024 // QB
    assert topk % rc == 0 and KB % rc == 0
    t_chunk = lax.shift_right_logical(qb * QB + lax.broadcasted_iota(I32, (rc, QB), 1), 6)
    for c in range(KB // rc):
        ki = ki_ref[c * rc:(c + 1) * rc, :]
        sc = jnp.zeros((rc, QB), F32)
        for hh in range(IDX_HEADS):
            lg = _mm(ki, iqt_ref[hh * IDX_DIM:(hh + 1) * IDX_DIM, :])
            sc = sc + jnp.maximum(lg, 0.0) * wit_ref[hh:hh + 1, :]
        s_chunk = lax.shift_right_logical(kb * KB + c * rc + lax.broadcasted_iota(I32, (rc, QB), 0), 6)
        bits = lax.bitcast_convert_type(sc, I32)
        sign = lax.shift_right_arithmetic(bits, 31)
        key = (bits ^ (sign & INT_MAX)) - sign
        key = jnp.where(s_chunk <= t_chunk, key, INT_MIN)
        rows = pl.ds(pl.multiple_of(kb * KB + c * rc, rc), rc)
        hi_ref[rows, :] = lax.shift_right_arithmetic(key, 16).astype(I16)
        lo_ref[rows, :] = ((key & 0xFFFF) + I16_MIN).astype(I16)
        g0 = (c * rc) % topk
        gm_ref[g0:g0 + rc, :] = jnp.maximum(gm_ref[g0:g0 + rc, :], key)

    @pl.when(kb == last)
    def _():
        nblk = last + 1
        slab = 32
        one, zero = jnp.int16(1), jnp.int16(0)

        def fold(m, reduce):
            part = m[0:slab]
            for r in range(1, KB // slab):
                part = reduce(part, m[r * slab:(r + 1) * slab])
            return part

        def count(pred):
            def body(i, acc):
                return acc + fold(pred(i), jnp.add)
            acc = lax.fori_loop(0, nblk, body, jnp.zeros((slab, QB), I16))
            return jnp.sum(acc.astype(I32), axis=0, keepdims=True)

        def count_ge(ref, thr):
            t16 = thr.astype(I16)
            return count(lambda i: jnp.where(ref[rows_of(i), :] >= t16, one, zero))

        def search(ref, target, lo, c_lo, hi, c_hi, rounds):
            def is_open(lo, c_lo, hi):
                return jnp.where(c_lo > target, jnp.where(hi != lo + 1, 1, 0), 0)

            def step(_, st):
                lo, c_lo, hi, c_hi = st
                open_ = is_open(lo, c_lo, hi) > 0
                mid = lax.shift_right_arithmetic(lo + hi, 1)
                c = count_ge(ref, mid)
                up = jnp.logical_and(open_, c >= target)
                dn = jnp.logical_and(open_, c < target)
                return jnp.where(up, mid, lo), jnp.where(up, c, c_lo), jnp.where(dn, mid, hi), jnp.where(dn, c, c_hi)

            def halvings_left(st):
                lo, c_lo, hi, _c_hi = st
                width = jnp.where(is_open(lo, c_lo, hi) > 0, hi - lo, 1)
                return jnp.max(32 - lax.clz(width - 1))

            st = (lo, c_lo, hi, c_hi)
            for most in rounds:
                st = lax.fori_loop(0, jnp.minimum(halvings_left(st), most), step, st)
            lo, c_lo, _hi, c_hi = st
            return lo, c_lo, c_hi

        zeros = jnp.zeros((1, QB), I32)
        gm = gm_ref[...]
        g_lo = jnp.maximum(lax.shift_right_arithmetic(jnp.min(gm, axis=0, keepdims=True), 16), I16_MIN + 1)
        g_hi = lax.shift_right_arithmetic(jnp.max(gm, axis=0, keepdims=True), 16) + 1
        hstar, ch_ge, ch_gt = search(hi_ref, topk, g_lo, count_ge(hi_ref, g_lo), g_hi, zeros, (16,))
        split = ch_ge > topk
        h16 = hstar.astype(I16)

        def low_half():
            def build(i, carry):
                lo_ref[rows_of(i), :] = jnp.where(hi_ref[rows_of(i), :] == h16, lo_ref[rows_of(i), :],
                                                  jnp.int16(I16_MIN))
                return carry
            lax.fori_loop(0, nblk, build, 0)
            target = jnp.where(split, topk - ch_gt, INT_MAX)
            return search(lo_ref, target, jnp.full((1, QB), I16_MIN, I32), ch_ge - ch_gt,
                          jnp.full((1, QB), I16_MAX + 1, I32), zeros, (12, 4))

        any_split = jnp.max(jnp.where(split, 1, 0)) > 0
        lstar, cl_ge, cl_gt = lax.cond(any_split, low_half,
                                       lambda: (jnp.full((1, QB), I16_MIN, I32), zeros, zeros))
        lstar = jnp.where(split, lstar, I16_MIN)
        l16 = lstar.astype(I16)
        c_ge = jnp.where(split, ch_gt + cl_ge, ch_ge)
        c_gt = jnp.where(split, ch_gt + cl_gt, ch_gt)
        excess = c_ge > topk
        need = jnp.where(excess, topk - c_gt, S + 1)
        row16 = lax.broadcasted_iota(I32, (KB, QB), 0).astype(I16)

        def before(i, j):
            return jnp.where(row16 + (i * KB).astype(I16) < j.astype(I16), one, zero)

        def tied(i, then):
            return jnp.where(hi_ref[rows_of(i), :] == h16, jnp.where(lo_ref[rows_of(i), :] == l16, then, zero), zero)

        def tie_cut():
            def step(_, c):
                lo, hi = c
                mid = lax.shift_right_arithmetic(lo + hi, 1)
                ok = count(lambda i: tied(i, before(i, mid))) >= need
                return jnp.where(ok, lo, mid + 1), jnp.where(ok, mid, hi)
            n_steps = max(1, math.ceil(math.log2(S + 1)))
            lo, _hi = lax.fori_loop(0, n_steps, step, (zeros, jnp.full((1, QB), S, I32)))
            return lo

        any_excess = jnp.max(jnp.where(excess, 1, 0)) > 0
        jcut = lax.cond(any_excess, tie_cut, lambda: jnp.full((1, QB), S, I32))

        def write(i, carry):
            hi_t, lo_t = hi_ref[rows_of(i), :], lo_ref[rows_of(i), :]
            in_bucket = jnp.where(lo_t > l16, one, jnp.where(lo_t == l16, before(i, jcut), zero))
            sel = jnp.where(hi_t > h16, one, jnp.where(hi_t == h16, in_bucket, zero))
            mask_ref[rows_of(i), :] = sel.astype(mask_ref.dtype)
            return carry

        lax.fori_loop(0, nblk, write, 0)

        def clear(i, carry):
            mask_ref[rows_of(i), :] = jnp.zeros((KB, QB), mask_ref.dtype)
            return carry

        lax.fori_loop(nblk, S // KB, clear, 0)


def _dsa_select(iqt, wit, ki, B, S, QB, KB):
    nq, nk = S // QB, S // KB
    topk = min(TOPK_MAX, S // 4)
    assert KB % topk == 0
    assert S <= I16_MAX
    qs, ks = _causal_steps(S, QB, KB)
    return pl.pallas_call(
        functools.partial(_select_kernel, QB=QB, KB=KB, S=S, topk=topk),
        out_shape=jax.ShapeDtypeStruct((B, S, S), jnp.int8),
        grid_spec=pltpu.PrefetchScalarGridSpec(
            num_scalar_prefetch=2,
            grid=(B, qs.shape[0]),
            in_specs=[
                pl.BlockSpec((IDX_Q_W, QB), lambda b, s, qs, ks: (0, b * nq + qs[s])),
                pl.BlockSpec((IDX_HEADS, QB), lambda b, s, qs, ks: (0, b * nq + qs[s])),
                pl.BlockSpec((KB, IDX_DIM), lambda b, s, qs, ks: (b * nk + ks[s], 0)),
            ],
            out_specs=pl.BlockSpec((None, S, QB), lambda b, s, qs, ks: (b, 0, qs[s])),
            scratch_shapes=[pltpu.VMEM((S, QB), I16), pltpu.VMEM((S, QB), I16), pltpu.VMEM((topk, QB), I32)],
        ),
        compiler_params=_params(("parallel", "arbitrary")),
        name="dsa_select",
    )(qs, ks, iqt, wit, ki)


LOGIT_SAFE = 120.0


def _attn_kernel(qb_ref, kb_ref, small_ref, q_ref, k_ref, vt_ref, m_ref, o_ref, acc_ref, run_ref, *, QB, KB):
    qb = qb_ref[pl.program_id(1)]
    kb = kb_ref[pl.program_id(1)]
    last = _last_kb(qb, QB, KB)

    @pl.when(kb == 0)
    def _():
        acc_ref[...] = jnp.zeros_like(acc_ref)
        run_ref[...] = jnp.full(run_ref.shape, -1e30, F32)

    head = lambda hh: slice(hh * DSA_HEAD_DIM, (hh + 1) * DSA_HEAD_DIM)
    qk_dot = lambda hh: _mm_nt(k_ref[:, head(hh)], q_ref[:, head(hh)])

    @pl.when(small_ref[0] == 1)
    def _():
        mb = m_ref[...].astype(MXU_DTYPE)
        lg = qk_dot(0)
        for hh in range(DSA_HEADS):
            lg_next = qk_dot(hh + 1) if hh + 1 < DSA_HEADS else None
            p = jnp.exp2(lg).astype(MXU_DTYPE) * mb
            acc_ref[hh] += jnp.dot(vt_ref[hh], p, preferred_element_type=F32)
            lg = lg_next

    @pl.when(small_ref[0] != 1)
    def _():
        selected = m_ref[...].astype(F32) > 0.0

        def logits(hh):
            lg = jnp.where(selected, qk_dot(hh), -jnp.inf)
            return lg, jnp.max(lg, axis=0, keepdims=True)

        nxt = logits(0)
        for hh in range(DSA_HEADS):
            lg, top = nxt
            nxt = logits(hh + 1) if hh + 1 < DSA_HEADS else None
            run_old = run_ref[hh]
            run_new = jnp.maximum(run_old, top)
            p = jnp.exp2(lg - run_new).astype(MXU_DTYPE)
            acc_ref[hh] = acc_ref[hh] * jnp.exp2(run_old - run_new) + jnp.dot(vt_ref[hh], p,
                                                                              preferred_element_type=F32)
            run_ref[hh] = run_new

    @pl.when(kb == last)
    def _():
        for hh in range(DSA_HEADS):
            acc = acc_ref[hh]
            o = acc[:DSA_HEAD_DIM] / acc[DSA_HEAD_DIM:DSA_HEAD_DIM + 1]
            o_ref[:, hh * DSA_HEAD_DIM:(hh + 1) * DSA_HEAD_DIM] = o.T.astype(o_ref.dtype)


def _dsa_attn(qk, vt, mask, logit_bound, B, S, QB, KB):
    T = B * S
    nq, nk = S // QB, S // KB
    W = DSA_HEADS * DSA_HEAD_DIM
    qs, ks = _causal_steps(S, QB, KB)
    small = (logit_bound <= LOGIT_SAFE).astype(I32).reshape(1)
    return pl.pallas_call(
        functools.partial(_attn_kernel, QB=QB, KB=KB),
        out_shape=jax.ShapeDtypeStruct((T, W), MXU_DTYPE),
        grid_spec=pltpu.PrefetchScalarGridSpec(
            num_scalar_prefetch=3,
            grid=(B, qs.shape[0]),
            in_specs=[
                pl.BlockSpec((QB, W), lambda b, s, qs, ks, sm: (b * nq + qs[s], 0)),
                pl.BlockSpec((KB, W), lambda b, s, qs, ks, sm: (b * nk + ks[s], 1)),
                pl.BlockSpec((None, DSA_HEADS, VT_ROWS, KB), lambda b, s, qs, ks, sm: (b, 0, 0, ks[s])),
                pl.BlockSpec((None, KB, QB), lambda b, s, qs, ks, sm: (b, ks[s], qs[s])),
            ],
            out_specs=pl.BlockSpec((QB, W), lambda b, s, qs, ks, sm: (b * nq + qs[s], 0)),
            scratch_shapes=[pltpu.VMEM((DSA_HEADS, VT_ROWS, QB), F32), pltpu.VMEM((DSA_HEADS, 1, QB), F32)],
        ),
        compiler_params=_params(("parallel", "arbitrary")),
        name="dsa_attn",
    )(qs, ks, small, qk, qk, vt, mask)


def _post_kernel(og_ref, od_ref, ga_ref, gb_ref, x_ref, pa_ref, pb_ref, wo_ref, g2_ref, wr_ref, br_ref,
                 x1_ref, h2_ref, route_ref):
    sig = lambda v: 1.0 / (1.0 + jnp.exp(-v))
    a = jnp.dot(og_ref[...], pa_ref[...], preferred_element_type=F32)
    b = jnp.dot(od_ref[...], pb_ref[...], preferred_element_type=F32)
    mix = sig(ga_ref[...].astype(F32)) * a + sig(gb_ref[...].astype(F32)) * b
    x1 = x_ref[...] + _mm(mix, wo_ref[...])
    x1_ref[...] = x1
    ms = jnp.mean(x1 * x1, axis=-1, keepdims=True)
    h2 = x1 * lax.rsqrt(ms + EPS) * g2_ref[...]
    h2_ref[...] = h2.astype(h2_ref.dtype)

    logits = _dot3(h2, wr_ref[...]) + br_ref[...]
    tm = logits.shape[0]
    lane = lax.broadcasted_iota(I32, (tm, LANES), 1).astype(F32)
    neg = -jnp.inf
    far = float(2 * LANES)
    rmax = lambda v: jnp.max(v, axis=1, keepdims=True)
    rmin = lambda v: jnp.min(v, axis=1, keepdims=True)
    gl = jnp.where(lane >= N_EXPERTS, jnp.where(lane < N_EXPERTS + N_GROUPS, logits, neg), neg)
    gmax = rmax(gl)
    g_w = 1.0 / jnp.sum(jnp.exp(gl - gmax), axis=1, keepdims=True)
    g_idx = rmin(jnp.where(gl == gmax, lane, far)) - N_EXPERTS
    e_lo = g_idx * EXPERTS_PER_GROUP
    el = jnp.where(lane >= e_lo, jnp.where(lane < e_lo + EXPERTS_PER_GROUP, logits, neg), neg)
    m1 = rmax(el)
    e1 = rmin(jnp.where(el == m1, lane, far))
    el2 = jnp.where(lane == e1, neg, el)
    m2 = rmax(el2)
    e2 = rmin(jnp.where(el2 == m2, lane, far))
    p2 = jnp.exp(m2 - m1)
    w1 = g_w / (1.0 + p2)
    route_ref[...] = jnp.where(lane == 0.0, e1, jnp.where(lane == 1.0, e2, jnp.where(
        lane == 2.0, w1, jnp.where(lane == 3.0, w1 * p2, 0.0))))


def _post(og, od, zm, x2, pa, pb, wo, g2, wr, br, tm):
    T, D = x2.shape
    row = lambda i: (i, 0)
    full = lambda i: (0, 0)
    return pl.pallas_call(
        _post_kernel,
        out_shape=(
            jax.ShapeDtypeStruct((T, D), F32),
            jax.ShapeDtypeStruct((T, D), F32),
            jax.ShapeDtypeStruct((T, LANES), F32),
        ),
        grid=(T // tm,),
        in_specs=[
            pl.BlockSpec((tm, D), row),
            pl.BlockSpec((tm, D), row),
            pl.BlockSpec((tm, D), lambda i: (i, 3)),
            pl.BlockSpec((tm, D), lambda i: (i, 4)),
            pl.BlockSpec((tm, D), row),
            pl.BlockSpec((D, D), full),
            pl.BlockSpec((D, D), full),
            pl.BlockSpec((D, D), full),
            pl.BlockSpec((1, D), full),
            pl.BlockSpec((D, LANES), full),
            pl.BlockSpec((1, LANES), full),
        ],
        out_specs=(pl.BlockSpec((tm, D), row), pl.BlockSpec((tm, D), row), pl.BlockSpec((tm, LANES), row)),
        compiler_params=_params(("parallel",)),
        name="merge_router",
    )(og, od, zm, zm, x2, pa, pb, wo, g2, wr, br)


MOE_TM = 256


def _gather_combine_kernel(idx_ref, nxt_ref, x1_ref, route_ref, src_hbm, o_ref, buf_ref, sem):
    i = pl.program_id(0)
    n = pl.num_programs(0)
    tm = o_ref.shape[0]
    slot = lax.rem(i, 2)

    def fetch(ids_ref, s):
        def start(r, carry):
            for k in range(2):
                pltpu.make_async_copy(src_hbm.at[pl.ds(ids_ref[0, 0, 2 * r + k], 1), :],
                                      buf_ref.at[s, pl.ds(k * tm + r, 1), :], sem.at[s]).start(priority=k)
            return carry
        lax.fori_loop(0, tm, start, 0, unroll=8)

    @pl.when(i == 0)
    def _():
        fetch(idx_ref, 0)

    @pl.when(i + 1 < n)
    def _():
        fetch(nxt_ref, 1 - slot)

    pltpu.make_async_copy(src_hbm.at[pl.ds(0, 2 * tm), :], buf_ref.at[slot], sem.at[slot]).wait()
    r = route_ref[...]
    lane = lax.broadcasted_iota(I32, r.shape, 1)
    w1 = jnp.sum(jnp.where(lane == 2, r, 0.0), axis=1, keepdims=True)
    w2 = jnp.sum(jnp.where(lane == 3, r, 0.0), axis=1, keepdims=True)
    o_ref[...] = x1_ref[...] + (w1 * buf_ref[slot, :tm, :] + w2 * buf_ref[slot, tm:, :])


def _gather_combine(x1, ys, pos, route, tm):
    T, D = x1.shape
    nt = T // tm
    idx3 = pos.reshape(nt, 1, 2 * tm)
    row = lambda i: (i, 0)
    return pl.pallas_call(
        _gather_combine_kernel,
        out_shape=jax.ShapeDtypeStruct((T, D), F32),
        grid=(nt,),
        in_specs=[
            pl.BlockSpec((1, 1, 2 * tm), lambda i: (i, 0, 0), memory_space=pltpu.SMEM),
            pl.BlockSpec((1, 1, 2 * tm), lambda i: (jnp.minimum(i + 1, nt - 1), 0, 0), memory_space=pltpu.SMEM),
            pl.BlockSpec((tm, D), row),
            pl.BlockSpec((tm, LANES), row),
            pl.BlockSpec(memory_space=pl.ANY),
        ],
        out_specs=pl.BlockSpec((tm, D), row),
        scratch_shapes=[pltpu.VMEM((2, 2 * tm, D), ys.dtype), pltpu.SemaphoreType.DMA((2,))],
        compiler_params=_params(("arbitrary",)),
        name="moe_gather_combine",
    )(idx3, idx3, x1, route, ys)


def _scatter_kernel(idx_ref, x_ref, o_hbm, sem, *, fanout):
    tm = x_ref.shape[0]

    def start(r, carry):
        for k in range(fanout):
            dst = idx_ref[0, 0, fanout * r + k]
            pltpu.make_async_copy(x_ref.at[pl.ds(r, 1), :], o_hbm.at[pl.ds(dst, 1), :], sem).start(priority=k)
        return carry

    lax.fori_loop(0, tm, start, 0, unroll=8)
    for k in range(fanout):
        pltpu.make_async_copy(x_ref, o_hbm.at[pl.ds(0, tm), :], sem).wait()


def _row_scatter(x, idx, fanout, tm, name):
    n, D = x.shape
    return pl.pallas_call(
        functools.partial(_scatter_kernel, fanout=fanout),
        out_shape=jax.ShapeDtypeStruct((fanout * n, D), x.dtype),
        grid=(n // tm,),
        in_specs=[
            pl.BlockSpec((1, 1, fanout * tm), lambda i: (i, 0, 0), memory_space=pltpu.SMEM),
            pl.BlockSpec((tm, D), lambda i: (i, 0)),
        ],
        out_specs=pl.BlockSpec(memory_space=pl.ANY),
        scratch_shapes=[pltpu.SemaphoreType.DMA(())],
        compiler_params=_params(("arbitrary",)),
        name=name,
    )(idx.reshape(n // tm, 1, fanout * tm), x)


def _route_plan(route, tm):
    n = 2 * route.shape[0]
    n_tiles = n // tm
    ids = route[:, :2].astype(I32).reshape(n)
    onehot = (ids[:, None] == jnp.arange(N_EXPERTS, dtype=I32)[None, :]).astype(F32).reshape(n_tiles, tm, N_EXPERTS)
    tril = (jnp.arange(tm)[:, None] >= jnp.arange(tm)[None, :]).astype(F32)
    in_tile = jnp.einsum("rc,tce->tre", tril, onehot)
    per_tile = in_tile[:, -1, :]
    before = jnp.cumsum(per_tile, axis=0) - per_tile
    counts = jnp.sum(per_tile, axis=0).astype(I32)
    starts = (jnp.cumsum(counts) - counts).astype(I32)
    rank = jnp.sum(onehot * (in_tile + before[:, None, :] - 1.0), axis=2).reshape(n)
    pos = (jnp.sum(onehot.reshape(n, N_EXPERTS) * starts[None, :].astype(F32), axis=1) + rank).astype(I32)
    bounds = jnp.sort(jnp.concatenate([jnp.arange(n_tiles, dtype=I32) * tm, starts]))
    ends = jnp.concatenate([bounds[1:], jnp.full((1,), n, I32)])
    seg_tile = jnp.minimum(bounds // tm, n_tiles - 1)
    seg_exp = jnp.clip(jnp.sum((starts[None, :] <= bounds[:, None]).astype(I32), axis=1) - 1, 0, N_EXPERTS - 1)
    seg_first = jnp.concatenate([jnp.ones((1,), I32), (seg_tile[1:] != seg_tile[:-1]).astype(I32)])
    return pos, (seg_tile, seg_exp, bounds - seg_tile * tm, ends - seg_tile * tm, seg_first)


def _experts_kernel(tile_ref, exp_ref, lo_ref, hi_ref, first_ref, x_ref, wg_ref, wu_ref, wd_ref, o_ref):
    s = pl.program_id(0)
    lo, hi = lo_ref[s], hi_ref[s]

    @pl.when(first_ref[s] == 1)
    def _():
        o_ref[...] = jnp.zeros_like(o_ref)

    @pl.when(hi > lo)
    def _():
        x = x_ref[...].astype(MXU_DTYPE)
        a = jnp.dot(x, wg_ref[...], preferred_element_type=F32)
        u = jnp.dot(x, wu_ref[...], preferred_element_type=F32)
        y = _mm(a / (1.0 + jnp.exp(-a)) * u, wd_ref[...])
        row = lax.broadcasted_iota(I32, y.shape, 0)
        o_ref[...] += jnp.where(row >= lo, jnp.where(row < hi, y, 0.0), 0.0)


def _experts(xs, segs, wg, wu, wd, tm):
    n, D = xs.shape
    F = wg.shape[2]
    tile_of = lambda s, tile, exp, lo, hi, first: (tile[s], 0)
    w_of = lambda s, tile, exp, lo, hi, first: (exp[s], 0, 0)
    return pl.pallas_call(
        _experts_kernel,
        out_shape=jax.ShapeDtypeStruct((n, D), F32),
        grid_spec=pltpu.PrefetchScalarGridSpec(
            num_scalar_prefetch=5,
            grid=(segs[0].shape[0],),
            in_specs=[
                pl.BlockSpec((tm, D), tile_of),
                pl.BlockSpec((None, D, F), w_of),
                pl.BlockSpec((None, D, F), w_of),
                pl.BlockSpec((None, F, D), w_of),
            ],
            out_specs=pl.BlockSpec((tm, D), tile_of),
        ),
        compiler_params=_params(("arbitrary",)),
        name="experts",
    )(*segs, xs, wg, wu, wd)


def _moe(h2, route, x1, wg, wu, wd):
    T, D = x1.shape
    tm = min(MOE_TM, T)
    pos, segs = _route_plan(route, tm)
    tok = min(2 * MOE_TM, T)
    xs = _row_scatter(h2, pos, 2, tok, "moe_scatter")
    ys = _experts(xs, segs, wg, wu, wd, tm)
    return _gather_combine(x1, ys, pos, route, tok)


def _pad_cols(w, n):
    return jnp.pad(w, ((0, 0), (0, n - w.shape[1])))


def _layer(x2, B, S, p):
    T, D = x2.shape
    assert D == 2 * GLA_HEADS * GLA_DK == GLA_HEADS * GLA_DV == DSA_HEADS * DSA_HEAD_DIM
    assert S % 512 == 0
    cd = MXU_DTYPE
    qk_w, v_w, dsa_w = GLA_HEADS * GLA_DK, GLA_HEADS * GLA_DV, DSA_HEADS * DSA_HEAD_DIM
    splits = (qk_w, qk_w, v_w, GLA_GATE_RANK, v_w, dsa_w, dsa_w, dsa_w, IDX_Q_W, IDX_DIM, IDX_HEADS, D, D)
    offs = [0]
    for s in splits:
        offs.append(offs[-1] + s)
    w_in = p["w_in"]
    (w_gq, w_gk, w_gv, w_ga, w_gr, w_dq, w_dk, w_dv, w_iq, w_ik, w_iw, w_ta, w_tb) = [
        w_in[:, offs[i]:offs[i + 1]] for i in range(len(splits))]

    w_main = jnp.concatenate([w_gq, w_gk, w_gv, w_gr, w_ta, w_tb], axis=1).astype(cd)
    w_qk = jnp.concatenate([w_dq, w_dk], axis=1).astype(cd)
    g_qk = jnp.concatenate([jnp.tile(p["dsa_q_norm_g"] * (DSA_HEAD_DIM ** -0.5 * math.log2(math.e)), DSA_HEADS),
                            jnp.tile(p["dsa_k_norm_g"], DSA_HEADS)]).reshape(1, 2 * dsa_w).astype(F32)
    w_vt = w_dv.T.astype(cd)
    w_idx = jnp.concatenate([_pad_cols(w_ik, LANES), _pad_cols(w_ga, LANES)], axis=1).astype(cd)
    w_idxt = jnp.pad(jnp.concatenate([w_iq, w_iw], axis=1).T, ((0, 16 - IDX_HEADS), (0, 0))).astype(cd)

    tm = min(1024, T)
    h = _rmsnorm(x2, p["norm1_g"], tm)
    zm = _matmul(h, w_main, tm, 1024, cd)
    qk = _qk_proj(h, w_qk, g_qk, tm)
    vt = _vt_proj(h, w_vt, B, S, min(1024, S))
    iqt, ki, ga, wit = _idx_proj(h, w_idx, w_idxt, p["idx_k_ln_g"].reshape(1, -1), p["idx_k_ln_b"].reshape(1, -1),
                                 tm)

    o_gla = _gla(zm, ga, p["gla_w_a2"], p["gla_b_a"].reshape(1, -1), p["gla_norm_g"].reshape(1, -1), B, S, 256)

    QB, KB = 256, min(1024, S)
    mask = _dsa_select(iqt, wit, ki, B, S, QB, KB)
    logit_bound = (DSA_HEAD_DIM * jnp.max(jnp.abs(g_qk[0, :dsa_w])) * jnp.max(jnp.abs(g_qk[0, dsa_w:])))
    o_dsa = _dsa_attn(qk, vt, mask, logit_bound, B, S, 2 * QB, KB)

    w_r = _pad_cols(jnp.concatenate([p["w_router_expert"], p["w_router_group"]], axis=1), LANES)
    b_r = _pad_cols(jnp.concatenate([p["b_router_expert"], p["b_router_group"]]).reshape(1, -1), LANES)
    x1, h2, route = _post(o_gla, o_dsa, zm, x2, p["w_branch_gla"].astype(cd), p["w_branch_dsa"].astype(cd),
                          p["w_out"].astype(cd), p["norm2_g"].reshape(1, -1), w_r, b_r, min(512, T))
    return _moe(h2, route, x1, p["w_exp_gate"].astype(cd), p["w_exp_up"].astype(cd), p["w_exp_down"].astype(cd))


def kernel(x, norm1_g, w_in, gla_w_a2, gla_b_a, gla_norm_g, dsa_q_norm_g, dsa_k_norm_g, idx_k_ln_g, idx_k_ln_b,
           w_branch_gla, w_branch_dsa, w_out, norm2_g, w_router_group, b_router_group, w_router_expert,
           b_router_expert, w_exp_gate, w_exp_up, w_exp_down):
    B, S, D = x.shape
    stacked = dict(norm1_g=norm1_g, w_in=w_in, gla_w_a2=gla_w_a2, gla_b_a=gla_b_a, gla_norm_g=gla_norm_g,
                   dsa_q_norm_g=dsa_q_norm_g, dsa_k_norm_g=dsa_k_norm_g, idx_k_ln_g=idx_k_ln_g,
                   idx_k_ln_b=idx_k_ln_b, w_branch_gla=w_branch_gla, w_branch_dsa=w_branch_dsa, w_out=w_out,
                   norm2_g=norm2_g, w_router_group=w_router_group, b_router_group=b_router_group,
                   w_router_expert=w_router_expert, b_router_expert=b_router_expert, w_exp_gate=w_exp_gate,
                   w_exp_up=w_exp_up, w_exp_down=w_exp_down)
    x2 = x.reshape(B * S, D).astype(F32)
    for l in range(w_in.shape[0]):
        x2 = _layer(x2, B, S, {k: v[l] for k, v in stacked.items()})
    return x2.reshape(B, S, D).astype(x.dtype)
```

```python
import functools
import math

import jax
import jax.numpy as jnp
from jax import lax
from jax.experimental import pallas as pl
from jax.experimental.pallas import tpu as pltpu

F32 = jnp.float32
BF16 = jnp.bfloat16
I32 = jnp.int32
MXU_DTYPE = BF16

CHUNK = 64
EPS = 1e-6
GLA_HEADS, GLA_DK, GLA_DV = 4, 128, 256
GLA_GATE_RANK = 16
GLA_GATE_TEMP = 16.0
DSA_HEADS, DSA_HEAD_DIM = 8, 128
IDX_HEADS, IDX_DIM = 8, 64
TOPK_MAX = 256
N_GROUPS, EXPERTS_PER_GROUP = 4, 8
N_EXPERTS = N_GROUPS * EXPERTS_PER_GROUP
LANES = 128

INT_MIN = -(2 ** 31)
INT_MAX = 2 ** 31 - 1
I16 = jnp.int16
I16_MIN, I16_MAX = -(2 ** 15), 2 ** 15 - 1

VMEM_LIMIT = 56 * 1024 * 1024

_NT = (((1,), (1,)), ((), ()))
_TN = (((0,), (0,)), ((), ()))


def _params(sem):
    return pltpu.CompilerParams(dimension_semantics=sem, vmem_limit_bytes=VMEM_LIMIT)


def _mm(a, b):
    return jnp.dot(a.astype(MXU_DTYPE), b.astype(MXU_DTYPE), preferred_element_type=F32)


def _mm_nt(a, b):
    return lax.dot_general(a.astype(MXU_DTYPE), b.astype(MXU_DTYPE), _NT, preferred_element_type=F32)


def _mm_tn(a, b):
    return lax.dot_general(a.astype(MXU_DTYPE), b.astype(MXU_DTYPE), _TN, preferred_element_type=F32)


def _split(a):
    hi = a.astype(BF16)
    lo = (a - hi.astype(F32)).astype(BF16)
    return hi, lo


def _dot3(a, b):
    ah, al = _split(a)
    bh, bl = _split(b)
    d = lambda u, v: jnp.dot(u, v, preferred_element_type=F32)
    return d(ah, bh) + (d(ah, bl) + d(al, bh))


def _rmsnorm_kernel(x_ref, g_ref, o_ref):
    x = x_ref[...]
    ms = jnp.mean(x * x, axis=-1, keepdims=True)
    o_ref[...] = (x * lax.rsqrt(ms + EPS) * g_ref[...]).astype(o_ref.dtype)


def _rmsnorm(x2, g, tm):
    T, D = x2.shape
    return pl.pallas_call(
        _rmsnorm_kernel,
        out_shape=jax.ShapeDtypeStruct((T, D), MXU_DTYPE),
        grid=(T // tm,),
        in_specs=[pl.BlockSpec((tm, D), lambda i: (i, 0)), pl.BlockSpec((1, D), lambda i: (0, 0))],
        out_specs=pl.BlockSpec((tm, D), lambda i: (i, 0)),
        compiler_params=_params(("parallel",)),
        name="rmsnorm",
    )(x2, g.reshape(1, D))


def _matmul_kernel(h_ref, w_ref, o_ref):
    o_ref[...] = jnp.dot(h_ref[...], w_ref[...], preferred_element_type=F32).astype(o_ref.dtype)


def _matmul(h, w, tm, tn, out_dtype):
    T, D = h.shape
    N = w.shape[1]
    return pl.pallas_call(
        _matmul_kernel,
        out_shape=jax.ShapeDtypeStruct((T, N), out_dtype),
        grid=(T // tm, N // tn),
        in_specs=[pl.BlockSpec((tm, D), lambda i, j: (i, 0)), pl.BlockSpec((D, tn), lambda i, j: (0, j))],
        out_specs=pl.BlockSpec((tm, tn), lambda i, j: (i, j)),
        compiler_params=_params(("parallel", "arbitrary")),
        name="proj_main",
    )(h, w)


def _qk_kernel(h_ref, w_ref, g_ref, o_ref):
    z = jnp.dot(h_ref[...], w_ref[...], preferred_element_type=F32)
    for hh in range(DSA_HEADS):
        sl = slice(hh * DSA_HEAD_DIM, (hh + 1) * DSA_HEAD_DIM)
        zh = z[:, sl]
        ms = jnp.mean(zh * zh, axis=-1, keepdims=True)
        o_ref[:, sl] = (zh * lax.rsqrt(ms + EPS) * g_ref[:, sl]).astype(o_ref.dtype)


def _qk_proj(h, w, g, tm):
    T, D = h.shape
    W = DSA_HEADS * DSA_HEAD_DIM
    return pl.pallas_call(
        _qk_kernel,
        out_shape=jax.ShapeDtypeStruct((T, 2 * W), MXU_DTYPE),
        grid=(T // tm, 2),
        in_specs=[
            pl.BlockSpec((tm, D), lambda i, j: (i, 0)),
            pl.BlockSpec((D, W), lambda i, j: (0, j)),
            pl.BlockSpec((1, W), lambda i, j: (0, j)),
        ],
        out_specs=pl.BlockSpec((tm, W), lambda i, j: (i, j)),
        compiler_params=_params(("parallel", "arbitrary")),
        name="proj_qk",
    )(h, w, g)


VT_ONES = 16
VT_ROWS = DSA_HEAD_DIM + VT_ONES


def _vt_kernel(h_ref, wt_ref, o_ref):
    vt = lax.dot_general(wt_ref[...], h_ref[...], _NT, preferred_element_type=F32).astype(o_ref.dtype)
    ones = jnp.ones((VT_ONES, vt.shape[1]), o_ref.dtype)
    for hh in range(DSA_HEADS):
        o_ref[hh, :DSA_HEAD_DIM, :] = vt[hh * DSA_HEAD_DIM:(hh + 1) * DSA_HEAD_DIM]
        o_ref[hh, DSA_HEAD_DIM:, :] = ones


def _vt_proj(h, wt, B, S, tm):
    T, D = h.shape
    W = wt.shape[0]
    nt = S // tm
    return pl.pallas_call(
        _vt_kernel,
        out_shape=jax.ShapeDtypeStruct((B, DSA_HEADS, VT_ROWS, S), MXU_DTYPE),
        grid=(T // tm,),
        in_specs=[pl.BlockSpec((tm, D), lambda i: (i, 0)), pl.BlockSpec((W, D), lambda i: (0, 0))],
        out_specs=pl.BlockSpec((None, DSA_HEADS, VT_ROWS, tm), lambda i: (i // nt, 0, 0, i % nt)),
        compiler_params=_params(("parallel",)),
        name="proj_vt",
    )(h, wt)


IDX_Q_W = IDX_HEADS * IDX_DIM


def _idx_kernel(h_ref, w_ref, wt_ref, lng_ref, lnb_ref, iqt_ref, ki_ref, ga_ref, wit_ref):
    h = h_ref[...]
    z = jnp.dot(h, w_ref[...], preferred_element_type=F32)
    ik = z[:, :IDX_DIM]
    mu = jnp.mean(ik, axis=-1, keepdims=True)
    var = jnp.mean(jnp.square(ik - mu), axis=-1, keepdims=True)
    ki = (ik - mu) * lax.rsqrt(var + EPS) * lng_ref[...] + lnb_ref[...]
    ki_ref[...] = ki.astype(ki_ref.dtype)
    ga_ref[...] = z[:, LANES:LANES + GLA_GATE_RANK]
    zt = lax.dot_general(wt_ref[...], h, _NT, preferred_element_type=F32)
    iqt_ref[...] = zt[:IDX_Q_W].astype(iqt_ref.dtype)
    wit_ref[...] = zt[IDX_Q_W:IDX_Q_W + IDX_HEADS] * (IDX_HEADS ** -0.5 * IDX_DIM ** -0.5)


def _idx_proj(h, w, wt, lng, lnb, tm):
    T, D = h.shape
    return pl.pallas_call(
        _idx_kernel,
        out_shape=(
            jax.ShapeDtypeStruct((IDX_Q_W, T), MXU_DTYPE),
            jax.ShapeDtypeStruct((T, IDX_DIM), MXU_DTYPE),
            jax.ShapeDtypeStruct((T, GLA_GATE_RANK), F32),
            jax.ShapeDtypeStruct((IDX_HEADS, T), F32),
        ),
        grid=(T // tm,),
        in_specs=[
            pl.BlockSpec((tm, D), lambda i: (i, 0)),
            pl.BlockSpec(w.shape, lambda i: (0, 0)),
            pl.BlockSpec(wt.shape, lambda i: (0, 0)),
            pl.BlockSpec((1, IDX_DIM), lambda i: (0, 0)),
            pl.BlockSpec((1, IDX_DIM), lambda i: (0, 0)),
        ],
        out_specs=(
            pl.BlockSpec((IDX_Q_W, tm), lambda i: (0, i)),
            pl.BlockSpec((tm, IDX_DIM), lambda i: (i, 0)),
            pl.BlockSpec((tm, GLA_GATE_RANK), lambda i: (i, 0)),
            pl.BlockSpec((IDX_HEADS, tm), lambda i: (0, i)),
        ),
        compiler_params=_params(("parallel",)),
        name="proj_idx",
    )(h, w, wt, lng, lnb)


def _gla_kernel(q_ref, k_ref, v_ref, r_ref, ga_ref, wa2_ref, ba_ref, ng_ref, o_ref, st_ref, tot_ref, kd_ref,
                oraw_ref, *, n_chunks):
    tb = q_ref.shape[0]

    @pl.when(pl.program_id(1) == 0)
    def _():
        st_ref[...] = jnp.zeros_like(st_ref)

    x = _dot3(ga_ref[...], wa2_ref[...]) + ba_ref[...]
    la = (jnp.minimum(x, 0.0) - jnp.log(1.0 + jnp.exp(-jnp.abs(x)))) * (1.0 / GLA_GATE_TEMP)
    row = lax.broadcasted_iota(I32, (tb, tb), 0)
    col = lax.broadcasted_iota(I32, (tb, tb), 1)
    same = lax.shift_right_logical(row, 6) == lax.shift_right_logical(col, 6)
    ones_blk = jnp.where(same, 1.0, 0.0).astype(BF16)
    tril_blk = jnp.where(same, jnp.where(row >= col, 1.0, 0.0), 0.0).astype(BF16)
    la_hi, la_lo = _split(la)
    d = lambda u, v: jnp.dot(u, v, preferred_element_type=F32)
    tot = d(ones_blk, la_hi) + d(ones_blk, la_lo)
    cum = d(tril_blk, la_hi) + d(tril_blk, la_lo)
    tot_ref[...] = tot
    kd_ref[...] = (k_ref[...].astype(F32) * jnp.exp(tot - cum)).astype(kd_ref.dtype)

    heads = range(GLA_HEADS)
    ksl = lambda hh: slice(hh * GLA_DK, (hh + 1) * GLA_DK)
    vsl = lambda hh: slice(hh * GLA_DV, (hh + 1) * GLA_DV)

    def chunk(c, carry):
        rows = pl.ds(pl.multiple_of(c * CHUNK, CHUNK), CHUNK)
        first = pl.ds(pl.multiple_of(c * CHUNK, CHUNK), 1)
        upd = [_mm_tn(v_ref[rows, vsl(hh)], kd_ref[rows, ksl(hh)]) for hh in heads]
        st = [st_ref[hh] * jnp.exp(tot_ref[first, ksl(hh)]) + upd[hh] for hh in heads]
        for hh in heads:
            st_ref[hh] = st[hh]
        for hh in heads:
            oraw_ref[rows, vsl(hh)] = _mm_nt(q_ref[rows, ksl(hh)], st[hh])
        return carry

    lax.fori_loop(0, n_chunks, chunk, 0)

    ng = ng_ref[...]
    for hh in heads:
        o = oraw_ref[:, vsl(hh)] * (GLA_DK ** -0.5)
        ms = jnp.mean(o * o, axis=-1, keepdims=True)
        r = r_ref[:, vsl(hh)].astype(F32)
        o_ref[:, vsl(hh)] = (o * lax.rsqrt(ms + EPS) * ng * (r / (1.0 + jnp.exp(-r)))).astype(o_ref.dtype)


def _gla(zm, ga, wa2, ba, ng, B, S, tb):
    T = B * S
    nb = S // tb
    qk_w = GLA_HEADS * GLA_DK
    v_w = GLA_HEADS * GLA_DV
    tok = lambda b, i: b * nb + i
    return pl.pallas_call(
        functools.partial(_gla_kernel, n_chunks=tb // CHUNK),
        out_shape=jax.ShapeDtypeStruct((T, v_w), MXU_DTYPE),
        grid=(B, nb),
        in_specs=[
            pl.BlockSpec((tb, qk_w), lambda b, i: (tok(b, i), 0)),
            pl.BlockSpec((tb, qk_w), lambda b, i: (tok(b, i), 1)),
            pl.BlockSpec((tb, v_w), lambda b, i: (tok(b, i), 1)),
            pl.BlockSpec((tb, v_w), lambda b, i: (tok(b, i), 2)),
            pl.BlockSpec((tb, GLA_GATE_RANK), lambda b, i: (tok(b, i), 0)),
            pl.BlockSpec((GLA_GATE_RANK, qk_w), lambda b, i: (0, 0)),
            pl.BlockSpec((1, qk_w), lambda b, i: (0, 0)),
            pl.BlockSpec((1, GLA_DV), lambda b, i: (0, 0)),
        ],
        out_specs=pl.BlockSpec((tb, v_w), lambda b, i: (tok(b, i), 0)),
        scratch_shapes=[pltpu.VMEM((GLA_HEADS, GLA_DV, GLA_DK), F32), pltpu.VMEM((tb, qk_w), F32),
                        pltpu.VMEM((tb, qk_w), MXU_DTYPE), pltpu.VMEM((tb, v_w), F32)],
        compiler_params=_params(("parallel", "arbitrary")),
        name="gla",
    )(zm, zm, zm, zm, ga, wa2, ba, ng)


def _last_kb(qb, QB, KB):
    return ((qb + 1) * QB - 1) // KB


def _causal_steps(S, QB, KB):
    pairs = [(q, k) for q in range(S // QB) for k in range(_last_kb(q, QB, KB) + 1)]
    qs, ks = zip(*pairs)
    return jnp.asarray(qs, I32), jnp.asarray(ks, I32)


def _select_kernel(qb_ref, kb_ref, iqt_ref, wit_ref, ki_ref, mask_ref, hi_ref, lo_ref, gm_ref, *,
                   QB, KB, S, topk):
    qb = qb_ref[pl.program_id(1)]
    kb = kb_ref[pl.program_id(1)]
    last = _last_kb(qb, QB, KB)

    def rows_of(i):
        return pl.ds(pl.multiple_of(i * KB, KB), KB)

    @pl.when(kb == 0)
    def _():
        gm_ref[...] = jnp.full(gm_ref.shape, INT_MIN, I32)

    rc = 32 * 1024 // QB
    assert topk % rc == 0 and KB % rc == 0
    t_chunk = lax.shift_right_logical(qb * QB + lax.broadcasted_iota(I32, (rc, QB), 1), 6)
    for c in range(KB // rc):
        ki = ki_ref[c * rc:(c + 1) * rc, :]
        sc = jnp.zeros((rc, QB), F32)
        for hh in range(IDX_HEADS):
            lg = _mm(ki, iqt_ref[hh * IDX_DIM:(hh + 1) * IDX_DIM, :])
            sc = sc + jnp.maximum(lg, 0.0) * wit_ref[hh:hh + 1, :]
        s_chunk = lax.shift_right_logical(kb * KB + c * rc + lax.broadcasted_iota(I32, (rc, QB), 0), 6)
        bits = lax.bitcast_convert_type(sc, I32)
        sign = lax.shift_right_arithmetic(bits, 31)
        key = (bits ^ (sign & INT_MAX)) - sign
        key = jnp.where(s_chunk <= t_chunk, key, INT_MIN)
        rows = pl.ds(pl.multiple_of(kb * KB + c * rc, rc), rc)
        hi_ref[rows, :] = lax.shift_right_arithmetic(key, 16).astype(I16)
        lo_ref[rows, :] = ((key & 0xFFFF) + I16_MIN).astype(I16)
        g0 = (c * rc) % topk
        gm_ref[g0:g0 + rc, :] = jnp.maximum(gm_ref[g0:g0 + rc, :], key)

    @pl.when(kb == last)
    def _():
        nblk = last + 1
        slab = 32
        one, zero = jnp.int16(1), jnp.int16(0)

        def fold(m, reduce):
            part = m[0:slab]
            for r in range(1, KB // slab):
                part = reduce(part, m[r * slab:(r + 1) * slab])
            return part

        def count(pred):
            def body(i, acc):
                return acc + fold(pred(i), jnp.add)
            acc = lax.fori_loop(0, nblk, body, jnp.zeros((slab, QB), I16))
            return jnp.sum(acc.astype(I32), axis=0, keepdims=True)

        def count_ge(ref, thr):
            t16 = thr.astype(I16)
            return count(lambda i: jnp.where(ref[rows_of(i), :] >= t16, one, zero))

        def search(ref, target, lo, c_lo, hi, c_hi, rounds):
            def is_open(lo, c_lo, hi):
                return jnp.where(c_lo > target, jnp.where(hi != lo + 1, 1, 0), 0)

            def step(_, st):
                lo, c_lo, hi, c_hi = st
                open_ = is_open(lo, c_lo, hi) > 0
                mid = lax.shift_right_arithmetic(lo + hi, 1)
                c = count_ge(ref, mid)
                up = jnp.logical_and(open_, c >= target)
                dn = jnp.logical_and(open_, c < target)
                return jnp.where(up, mid, lo), jnp.where(up, c, c_lo), jnp.where(dn, mid, hi), jnp.where(dn, c, c_hi)

            def halvings_left(st):
                lo, c_lo, hi, _c_hi = st
                width = jnp.where(is_open(lo, c_lo, hi) > 0, hi - lo, 1)
                return jnp.max(32 - lax.clz(width - 1))

            st = (lo, c_lo, hi, c_hi)
            for most in rounds:
                st = lax.fori_loop(0, jnp.minimum(halvings_left(st), most), step, st)
            lo, c_lo, _hi, c_hi = st
            return lo, c_lo, c_hi

        zeros = jnp.zeros((1, QB), I32)
        gm = gm_ref[...]
        g_lo = jnp.maximum(lax.shift_right_arithmetic(jnp.min(gm, axis=0, keepdims=True), 16), I16_MIN + 1)
        g_hi = lax.shift_right_arithmetic(jnp.max(gm, axis=0, keepdims=True), 16) + 1
        hstar, ch_ge, ch_gt = search(hi_ref, topk, g_lo, count_ge(hi_ref, g_lo), g_hi, zeros, (16,))
        split = ch_ge > topk
        h16 = hstar.astype(I16)

        def low_half():
            def build(i, carry):
                lo_ref[rows_of(i), :] = jnp.where(hi_ref[rows_of(i), :] == h16, lo_ref[rows_of(i), :],
                                                  jnp.int16(I16_MIN))
                return carry
            lax.fori_loop(0, nblk, build, 0)
            target = jnp.where(split, topk - ch_gt, INT_MAX)
            return search(lo_ref, target, jnp.full((1, QB), I16_MIN, I32), ch_ge - ch_gt,
                          jnp.full((1, QB), I16_MAX + 1, I32), zeros, (12, 4))

        any_split = jnp.max(jnp.where(split, 1, 0)) > 0
        lstar, cl_ge, cl_gt = lax.cond(any_split, low_half,
                                       lambda: (jnp.full((1, QB), I16_MIN, I32), zeros, zeros))
        lstar = jnp.where(split, lstar, I16_MIN)
        l16 = lstar.astype(I16)
        c_ge = jnp.where(split, ch_gt + cl_ge, ch_ge)
        c_gt = jnp.where(split, ch_gt + cl_gt, ch_gt)
        excess = c_ge > topk
        need = jnp.where(excess, topk - c_gt, S + 1)
        row16 = lax.broadcasted_iota(I32, (KB, QB), 0).astype(I16)

        def before(i, j):
            return jnp.where(row16 + (i * KB).astype(I16) < j.astype(I16), one, zero)

        def tied(i, then):
            return jnp.where(hi_ref[rows_of(i), :] == h16, jnp.where(lo_ref[rows_of(i), :] == l16, then, zero), zero)

        def tie_cut():
            def step(_, c):
                lo, hi = c
                mid = lax.shift_right_arithmetic(lo + hi, 1)
                ok = count(lambda i: tied(i, before(i, mid))) >= need
                return jnp.where(ok, lo, mid + 1), jnp.where(ok, mid, hi)
            n_steps = max(1, math.ceil(math.log2(S + 1)))
            lo, _hi = lax.fori_loop(0, n_steps, step, (zeros, jnp.full((1, QB), S, I32)))
            return lo

        any_excess = jnp.max(jnp.where(excess, 1, 0)) > 0
        jcut = lax.cond(any_excess, tie_cut, lambda: jnp.full((1, QB), S, I32))

        def write(i, carry):
            hi_t, lo_t = hi_ref[rows_of(i), :], lo_ref[rows_of(i), :]
            in_bucket = jnp.where(lo_t > l16, one, jnp.where(lo_t == l16, before(i, jcut), zero))
            sel = jnp.where(hi_t > h16, one, jnp.where(hi_t == h16, in_bucket, zero))
            mask_ref[rows_of(i), :] = sel.astype(mask_ref.dtype)
            return carry

        lax.fori_loop(0, nblk, write, 0)

        def clear(i, carry):
            mask_ref[rows_of(i), :] = jnp.zeros((KB, QB), mask_ref.dtype)
            return carry

        lax.fori_loop(nblk, S // KB, clear, 0)


def _dsa_select(iqt, wit, ki, B, S, QB, KB):
    nq, nk = S // QB, S // KB
    topk = min(TOPK_MAX, S // 4)
    assert KB % topk == 0
    assert S <= I16_MAX
    qs, ks = _causal_steps(S, QB, KB)
    return pl.pallas_call(
        functools.partial(_select_kernel, QB=QB, KB=KB, S=S, topk=topk),
        out_shape=jax.ShapeDtypeStruct((B, S, S), jnp.int8),
        grid_spec=pltpu.PrefetchScalarGridSpec(
            num_scalar_prefetch=2,
            grid=(B, qs.shape[0]),
            in_specs=[
                pl.BlockSpec((IDX_Q_W, QB), lambda b, s, qs, ks: (0, b * nq + qs[s])),
                pl.BlockSpec((IDX_HEADS, QB), lambda b, s, qs, ks: (0, b * nq + qs[s])),
                pl.BlockSpec((KB, IDX_DIM), lambda b, s, qs, ks: (b * nk + ks[s], 0)),
            ],
            out_specs=pl.BlockSpec((None, S, QB), lambda b, s, qs, ks: (b, 0, qs[s])),
            scratch_shapes=[pltpu.VMEM((S, QB), I16), pltpu.VMEM((S, QB), I16), pltpu.VMEM((topk, QB), I32)],
        ),
        compiler_params=_params(("parallel", "arbitrary")),
        name="dsa_select",
    )(qs, ks, iqt, wit, ki)


LOGIT_SAFE = 100.0


def _attn_kernel(qb_ref, kb_ref, small_ref, q_ref, k_ref, vt_ref, m_ref, o_ref, acc_ref, run_ref, *, QB, KB):
    qb = qb_ref[pl.program_id(1)]
    kb = kb_ref[pl.program_id(1)]
    last = _last_kb(qb, QB, KB)

    @pl.when(kb == 0)
    def _():
        acc_ref[...] = jnp.zeros_like(acc_ref)
        run_ref[...] = jnp.full(run_ref.shape, -1e30, F32)

    head = lambda hh: slice(hh * DSA_HEAD_DIM, (hh + 1) * DSA_HEAD_DIM)
    qk_dot = lambda hh: _mm_nt(k_ref[:, head(hh)], q_ref[:, head(hh)])

    @pl.when(small_ref[0] == 1)
    def _():
        mb = m_ref[...].astype(MXU_DTYPE)
        lg = qk_dot(0)
        for hh in range(DSA_HEADS):
            lg_next = qk_dot(hh + 1) if hh + 1 < DSA_HEADS else None
            p = jnp.exp2(lg).astype(MXU_DTYPE) * mb
            acc_ref[hh] += jnp.dot(vt_ref[hh], p, preferred_element_type=F32)
            lg = lg_next

    @pl.when(small_ref[0] != 1)
    def _():
        selected = m_ref[...].astype(F32) > 0.0

        def logits(hh):
            lg = jnp.where(selected, qk_dot(hh), -jnp.inf)
            return lg, jnp.max(lg, axis=0, keepdims=True)

        nxt = logits(0)
        for hh in range(DSA_HEADS):
            lg, top = nxt
            nxt = logits(hh + 1) if hh + 1 < DSA_HEADS else None
            run_old = run_ref[hh]
            run_new = jnp.maximum(run_old, top)
            p = jnp.exp2(lg - run_new).astype(MXU_DTYPE)
            acc_ref[hh] = acc_ref[hh] * jnp.exp2(run_old - run_new) + jnp.dot(vt_ref[hh], p,
                                                                              preferred_element_type=F32)
            run_ref[hh] = run_new

    @pl.when(kb == last)
    def _():
        for hh in range(DSA_HEADS):
            acc = acc_ref[hh]
            o = acc[:DSA_HEAD_DIM] / acc[DSA_HEAD_DIM:DSA_HEAD_DIM + 1]
            o_ref[:, hh * DSA_HEAD_DIM:(hh + 1) * DSA_HEAD_DIM] = o.T.astype(o_ref.dtype)


def _dsa_attn(qk, vt, mask, logit_bound, B, S, QB, KB):
    T = B * S
    nq, nk = S // QB, S // KB
    W = DSA_HEADS * DSA_HEAD_DIM
    qs, ks = _causal_steps(S, QB, KB)
    small = (logit_bound <= LOGIT_SAFE).astype(I32).reshape(1)
    return pl.pallas_call(
        functools.partial(_attn_kernel, QB=QB, KB=KB),
        out_shape=jax.ShapeDtypeStruct((T, W), MXU_DTYPE),
        grid_spec=pltpu.PrefetchScalarGridSpec(
            num_scalar_prefetch=3,
            grid=(B, qs.shape[0]),
            in_specs=[
                pl.BlockSpec((QB, W), lambda b, s, qs, ks, sm: (b * nq + qs[s], 0)),
                pl.BlockSpec((KB, W), lambda b, s, qs, ks, sm: (b * nk + ks[s], 1)),
                pl.BlockSpec((None, DSA_HEADS, VT_ROWS, KB), lambda b, s, qs, ks, sm: (b, 0, 0, ks[s])),
                pl.BlockSpec((None, KB, QB), lambda b, s, qs, ks, sm: (b, ks[s], qs[s])),
            ],
            out_specs=pl.BlockSpec((QB, W), lambda b, s, qs, ks, sm: (b * nq + qs[s], 0)),
            scratch_shapes=[pltpu.VMEM((DSA_HEADS, VT_ROWS, QB), F32), pltpu.VMEM((DSA_HEADS, 1, QB), F32)],
        ),
        compiler_params=_params(("parallel", "arbitrary")),
        name="dsa_attn",
    )(qs, ks, small, qk, qk, vt, mask)


def _post_kernel(og_ref, od_ref, ga_ref, gb_ref, x_ref, pa_ref, pb_ref, wo_ref, g2_ref, wr_ref, br_ref,
                 x1_ref, h2_ref, route_ref):
    sig = lambda v: 1.0 / (1.0 + jnp.exp(-v))
    a = jnp.dot(og_ref[...], pa_ref[...], preferred_element_type=F32)
    b = jnp.dot(od_ref[...], pb_ref[...], preferred_element_type=F32)
    mix = sig(ga_ref[...].astype(F32)) * a + sig(gb_ref[...].astype(F32)) * b
    x1 = x_ref[...] + _mm(mix, wo_ref[...])
    x1_ref[...] = x1
    ms = jnp.mean(x1 * x1, axis=-1, keepdims=True)
    h2 = x1 * lax.rsqrt(ms + EPS) * g2_ref[...]
    h2_ref[...] = h2.astype(h2_ref.dtype)

    logits = _dot3(h2, wr_ref[...]) + br_ref[...]
    tm = logits.shape[0]
    lane = lax.broadcasted_iota(I32, (tm, LANES), 1).astype(F32)
    neg = -jnp.inf
    far = float(2 * LANES)
    rmax = lambda v: jnp.max(v, axis=1, keepdims=True)
    rmin = lambda v: jnp.min(v, axis=1, keepdims=True)
    gl = jnp.where(lane >= N_EXPERTS, jnp.where(lane < N_EXPERTS + N_GROUPS, logits, neg), neg)
    gmax = rmax(gl)
    g_w = 1.0 / jnp.sum(jnp.exp(gl - gmax), axis=1, keepdims=True)
    g_idx = rmin(jnp.where(gl == gmax, lane, far)) - N_EXPERTS
    e_lo = g_idx * EXPERTS_PER_GROUP
    el = jnp.where(lane >= e_lo, jnp.where(lane < e_lo + EXPERTS_PER_GROUP, logits, neg), neg)
    m1 = rmax(el)
    e1 = rmin(jnp.where(el == m1, lane, far))
    el2 = jnp.where(lane == e1, neg, el)
    m2 = rmax(el2)
    e2 = rmin(jnp.where(el2 == m2, lane, far))
    p2 = jnp.exp(m2 - m1)
    w1 = g_w / (1.0 + p2)
    route_ref[...] = jnp.where(lane == 0.0, e1, jnp.where(lane == 1.0, e2, jnp.where(
        lane == 2.0, w1, jnp.where(lane == 3.0, w1 * p2, 0.0))))


def _post(og, od, zm, x2, pa, pb, wo, g2, wr, br, tm):
    T, D = x2.shape
    row = lambda i: (i, 0)
    full = lambda i: (0, 0)
    return pl.pallas_call(
        _post_kernel,
        out_shape=(
            jax.ShapeDtypeStruct((T, D), F32),
            jax.ShapeDtypeStruct((T, D), F32),
            jax.ShapeDtypeStruct((T, LANES), F32),
        ),
        grid=(T // tm,),
        in_specs=[
            pl.BlockSpec((tm, D), row),
            pl.BlockSpec((tm, D), row),
            pl.BlockSpec((tm, D), lambda i: (i, 3)),
            pl.BlockSpec((tm, D), lambda i: (i, 4)),
            pl.BlockSpec((tm, D), row),
            pl.BlockSpec((D, D), full),
            pl.BlockSpec((D, D), full),
            pl.BlockSpec((D, D), full),
            pl.BlockSpec((1, D), full),
            pl.BlockSpec((D, LANES), full),
            pl.BlockSpec((1, LANES), full),
        ],
        out_specs=(pl.BlockSpec((tm, D), row), pl.BlockSpec((tm, D), row), pl.BlockSpec((tm, LANES), row)),
        compiler_params=_params(("parallel",)),
        name="merge_router",
    )(og, od, zm, zm, x2, pa, pb, wo, g2, wr, br)


MOE_TM = 256


def _gather_combine_kernel(idx_ref, nxt_ref, x1_ref, route_ref, src_hbm, o_ref, buf_ref, sem):
    i = pl.program_id(0)
    n = pl.num_programs(0)
    tm = o_ref.shape[0]
    slot = lax.rem(i, 2)

    def fetch(ids_ref, s):
        def start(r, carry):
            for k in range(2):
                pltpu.make_async_copy(src_hbm.at[pl.ds(ids_ref[0, 0, 2 * r + k], 1), :],
                                      buf_ref.at[s, pl.ds(k * tm + r, 1), :], sem.at[s]).start(priority=k)
            return carry
        lax.fori_loop(0, tm, start, 0, unroll=8)

    @pl.when(i == 0)
    def _():
        fetch(idx_ref, 0)

    @pl.when(i + 1 < n)
    def _():
        fetch(nxt_ref, 1 - slot)

    pltpu.make_async_copy(src_hbm.at[pl.ds(0, 2 * tm), :], buf_ref.at[slot], sem.at[slot]).wait()
    r = route_ref[...]
    lane = lax.broadcasted_iota(I32, r.shape, 1)
    w1 = jnp.sum(jnp.where(lane == 2, r, 0.0), axis=1, keepdims=True)
    w2 = jnp.sum(jnp.where(lane == 3, r, 0.0), axis=1, keepdims=True)
    o_ref[...] = x1_ref[...] + (w1 * buf_ref[slot, :tm, :] + w2 * buf_ref[slot, tm:, :])


def _gather_combine(x1, ys, pos, route, tm):
    T, D = x1.shape
    nt = T // tm
    idx3 = pos.reshape(nt, 1, 2 * tm)
    row = lambda i: (i, 0)
    return pl.pallas_call(
        _gather_combine_kernel,
        out_shape=jax.ShapeDtypeStruct((T, D), F32),
        grid=(nt,),
        in_specs=[
            pl.BlockSpec((1, 1, 2 * tm), lambda i: (i, 0, 0), memory_space=pltpu.SMEM),
            pl.BlockSpec((1, 1, 2 * tm), lambda i: (jnp.minimum(i + 1, nt - 1), 0, 0), memory_space=pltpu.SMEM),
            pl.BlockSpec((tm, D), row),
            pl.BlockSpec((tm, LANES), row),
            pl.BlockSpec(memory_space=pl.ANY),
        ],
        out_specs=pl.BlockSpec((tm, D), row),
        scratch_shapes=[pltpu.VMEM((2, 2 * tm, D), ys.dtype), pltpu.SemaphoreType.DMA((2,))],
        compiler_params=_params(("arbitrary",)),
        name="moe_gather_combine",
    )(idx3, idx3, x1, route, ys)


def _scatter_kernel(idx_ref, x_ref, o_hbm, sem, *, fanout):
    tm = x_ref.shape[0]

    def start(r, carry):
        for k in range(fanout):
            dst = idx_ref[0, 0, fanout * r + k]
            pltpu.make_async_copy(x_ref.at[pl.ds(r, 1), :], o_hbm.at[pl.ds(dst, 1), :], sem).start(priority=k)
        return carry

    lax.fori_loop(0, tm, start, 0, unroll=8)
    for k in range(fanout):
        pltpu.make_async_copy(x_ref, o_hbm.at[pl.ds(0, tm), :], sem).wait()


def _row_scatter(x, idx, fanout, tm, name):
    n, D = x.shape
    return pl.pallas_call(
        functools.partial(_scatter_kernel, fanout=fanout),
        out_shape=jax.ShapeDtypeStruct((fanout * n, D), x.dtype),
        grid=(n // tm,),
        in_specs=[
            pl.BlockSpec((1, 1, fanout * tm), lambda i: (i, 0, 0), memory_space=pltpu.SMEM),
            pl.BlockSpec((tm, D), lambda i: (i, 0)),
        ],
        out_specs=pl.BlockSpec(memory_space=pl.ANY),
        scratch_shapes=[pltpu.SemaphoreType.DMA(())],
        compiler_params=_params(("arbitrary",)),
        name=name,
    )(idx.reshape(n // tm, 1, fanout * tm), x)


def _route_plan(route, tm):
    n = 2 * route.shape[0]
    n_tiles = n // tm
    ids = route[:, :2].astype(I32).reshape(n)
    onehot = (ids[:, None] == jnp.arange(N_EXPERTS, dtype=I32)[None, :]).astype(F32).reshape(n_tiles, tm, N_EXPERTS)
    tril = (jnp.arange(tm)[:, None] >= jnp.arange(tm)[None, :]).astype(F32)
    in_tile = jnp.einsum("rc,tce->tre", tril, onehot)
    per_tile = in_tile[:, -1, :]
    before = jnp.cumsum(per_tile, axis=0) - per_tile
    counts = jnp.sum(per_tile, axis=0).astype(I32)
    starts = (jnp.cumsum(counts) - counts).astype(I32)
    rank = jnp.sum(onehot * (in_tile + before[:, None, :] - 1.0), axis=2).reshape(n)
    pos = (jnp.sum(onehot.reshape(n, N_EXPERTS) * starts[None, :].astype(F32), axis=1) + rank).astype(I32)
    bounds = jnp.sort(jnp.concatenate([jnp.arange(n_tiles, dtype=I32) * tm, starts]))
    ends = jnp.concatenate([bounds[1:], jnp.full((1,), n, I32)])
    seg_tile = jnp.minimum(bounds // tm, n_tiles - 1)
    seg_exp = jnp.clip(jnp.sum((starts[None, :] <= bounds[:, None]).astype(I32), axis=1) - 1, 0, N_EXPERTS - 1)
    seg_first = jnp.concatenate([jnp.ones((1,), I32), (seg_tile[1:] != seg_tile[:-1]).astype(I32)])
    return pos, (seg_tile, seg_exp, bounds - seg_tile * tm, ends - seg_tile * tm, seg_first)


def _experts_kernel(tile_ref, exp_ref, lo_ref, hi_ref, first_ref, x_ref, wg_ref, wu_ref, wd_ref, o_ref):
    s = pl.program_id(0)
    lo, hi = lo_ref[s], hi_ref[s]

    @pl.when(first_ref[s] == 1)
    def _():
        o_ref[...] = jnp.zeros_like(o_ref)

    @pl.when(hi > lo)
    def _():
        x = x_ref[...].astype(MXU_DTYPE)
        a = jnp.dot(x, wg_ref[...], preferred_element_type=F32)
        u = jnp.dot(x, wu_ref[...], preferred_element_type=F32)
        y = _mm(a / (1.0 + jnp.exp(-a)) * u, wd_ref[...])
        row = lax.broadcasted_iota(I32, y.shape, 0)
        o_ref[...] += jnp.where(row >= lo, jnp.where(row < hi, y, 0.0), 0.0)


def _experts(xs, segs, wg, wu, wd, tm):
    n, D = xs.shape
    F = wg.shape[2]
    tile_of = lambda s, tile, exp, lo, hi, first: (tile[s], 0)
    w_of = lambda s, tile, exp, lo, hi, first: (exp[s], 0, 0)
    return pl.pallas_call(
        _experts_kernel,
        out_shape=jax.ShapeDtypeStruct((n, D), F32),
        grid_spec=pltpu.PrefetchScalarGridSpec(
            num_scalar_prefetch=5,
            grid=(segs[0].shape[0],),
            in_specs=[
                pl.BlockSpec((tm, D), tile_of),
                pl.BlockSpec((None, D, F), w_of),
                pl.BlockSpec((None, D, F), w_of),
                pl.BlockSpec((None, F, D), w_of),
            ],
            out_specs=pl.BlockSpec((tm, D), tile_of),
        ),
        compiler_params=_params(("arbitrary",)),
        name="experts",
    )(*segs, xs, wg, wu, wd)


def _moe(h2, route, x1, wg, wu, wd):
    T, D = x1.shape
    tm = min(MOE_TM, T)
    pos, segs = _route_plan(route, tm)
    tok = min(2 * MOE_TM, T)
    xs = _row_scatter(h2, pos, 2, tok, "moe_scatter")
    ys = _experts(xs, segs, wg, wu, wd, tm)
    return _gather_combine(x1, ys, pos, route, tok)


def _pad_cols(w, n):
    return jnp.pad(w, ((0, 0), (0, n - w.shape[1])))


def _layer(x2, B, S, p):
    T, D = x2.shape
    assert D == 2 * GLA_HEADS * GLA_DK == GLA_HEADS * GLA_DV == DSA_HEADS * DSA_HEAD_DIM
    assert S % 512 == 0
    cd = MXU_DTYPE
    qk_w, v_w, dsa_w = GLA_HEADS * GLA_DK, GLA_HEADS * GLA_DV, DSA_HEADS * DSA_HEAD_DIM
    splits = (qk_w, qk_w, v_w, GLA_GATE_RANK, v_w, dsa_w, dsa_w, dsa_w, IDX_Q_W, IDX_DIM, IDX_HEADS, D, D)
    offs = [0]
    for s in splits:
        offs.append(offs[-1] + s)
    w_in = p["w_in"]
    (w_gq, w_gk, w_gv, w_ga, w_gr, w_dq, w_dk, w_dv, w_iq, w_ik, w_iw, w_ta, w_tb) = [
        w_in[:, offs[i]:offs[i + 1]] for i in range(len(splits))]

    w_main = jnp.concatenate([w_gq, w_gk, w_gv, w_gr, w_ta, w_tb], axis=1).astype(cd)
    w_qk = jnp.concatenate([w_dq, w_dk], axis=1).astype(cd)
    g_qk = jnp.concatenate([jnp.tile(p["dsa_q_norm_g"] * (DSA_HEAD_DIM ** -0.5 * math.log2(math.e)), DSA_HEADS),
                            jnp.tile(p["dsa_k_norm_g"], DSA_HEADS)]).reshape(1, 2 * dsa_w).astype(F32)
    w_vt = w_dv.T.astype(cd)
    w_idx = jnp.concatenate([_pad_cols(w_ik, LANES), _pad_cols(w_ga, LANES)], axis=1).astype(cd)
    w_idxt = jnp.pad(jnp.concatenate([w_iq, w_iw], axis=1).T, ((0, 16 - IDX_HEADS), (0, 0))).astype(cd)

    tm = min(1024, T)
    h = _rmsnorm(x2, p["norm1_g"], tm)
    zm = _matmul(h, w_main, tm, 1024, cd)
    qk = _qk_proj(h, w_qk, g_qk, tm)
    vt = _vt_proj(h, w_vt, B, S, min(1024, S))
    iqt, ki, ga, wit = _idx_proj(h, w_idx, w_idxt, p["idx_k_ln_g"].reshape(1, -1), p["idx_k_ln_b"].reshape(1, -1),
                                 tm)

    o_gla = _gla(zm, ga, p["gla_w_a2"], p["gla_b_a"].reshape(1, -1), p["gla_norm_g"].reshape(1, -1), B, S, 256)

    QB, KB = 256, min(1024, S)
    mask = _dsa_select(iqt, wit, ki, B, S, QB, KB)
    logit_bound = (DSA_HEAD_DIM * jnp.max(jnp.abs(g_qk[0, :dsa_w])) * jnp.max(jnp.abs(g_qk[0, dsa_w:])))
    o_dsa = _dsa_attn(qk, vt, mask, logit_bound, B, S, 2 * QB, KB)

    w_r = _pad_cols(jnp.concatenate([p["w_router_expert"], p["w_router_group"]], axis=1), LANES)
    b_r = _pad_cols(jnp.concatenate([p["b_router_expert"], p["b_router_group"]]).reshape(1, -1), LANES)
    x1, h2, route = _post(o_gla, o_dsa, zm, x2, p["w_branch_gla"].astype(cd), p["w_branch_dsa"].astype(cd),
                          p["w_out"].astype(cd), p["norm2_g"].reshape(1, -1), w_r, b_r, min(512, T))
    return _moe(h2, route, x1, p["w_exp_gate"].astype(cd), p["w_exp_up"].astype(cd), p["w_exp_down"].astype(cd))


def kernel(x, norm1_g, w_in, gla_w_a2, gla_b_a, gla_norm_g, dsa_q_norm_g, dsa_k_norm_g, idx_k_ln_g, idx_k_ln_b,
           w_branch_gla, w_branch_dsa, w_out, norm2_g, w_router_group, b_router_group, w_router_expert,
           b_router_expert, w_exp_gate, w_exp_up, w_exp_down):
    B, S, D = x.shape
    stacked = dict(norm1_g=norm1_g, w_in=w_in, gla_w_a2=gla_w_a2, gla_b_a=gla_b_a, gla_norm_g=gla_norm_g,
                   dsa_q_norm_g=dsa_q_norm_g, dsa_k_norm_g=dsa_k_norm_g, idx_k_ln_g=idx_k_ln_g,
                   idx_k_ln_b=idx_k_ln_b, w_branch_gla=w_branch_gla, w_branch_dsa=w_branch_dsa, w_out=w_out,
                   norm2_g=norm2_g, w_router_group=w_router_group, b_router_group=b_router_group,
                   w_router_expert=w_router_expert, b_router_expert=b_router_expert, w_exp_gate=w_exp_gate,
                   w_exp_up=w_exp_up, w_exp_down=w_exp_down)
    x2 = x.reshape(B * S, D).astype(F32)
    for l in range(w_in.shape[0]):
        x2 = _layer(x2, B, S, {k: v[l] for k, v in stacked.items()})
    return x2.reshape(B, S, D).astype(x.dtype)
```

```python
import functools
import math

import jax
import jax.numpy as jnp
from jax import lax
from jax.experimental import pallas as pl
from jax.experimental.pallas import tpu as pltpu

F32 = jnp.float32
BF16 = jnp.bfloat16
I32 = jnp.int32
MXU_DTYPE = BF16

CHUNK = 64
EPS = 1e-6
GLA_HEADS, GLA_DK, GLA_DV = 4, 128, 256
GLA_GATE_RANK = 16
GLA_GATE_TEMP = 16.0
DSA_HEADS, DSA_HEAD_DIM = 8, 128
IDX_HEADS, IDX_DIM = 8, 64
TOPK_MAX = 256
N_GROUPS, EXPERTS_PER_GROUP = 4, 8
N_EXPERTS = N_GROUPS * EXPERTS_PER_GROUP
LANES = 128

INT_MIN = -(2 ** 31)
INT_MAX = 2 ** 31 - 1
I16 = jnp.int16
I16_MIN, I16_MAX = -(2 ** 15), 2 ** 15 - 1

VMEM_LIMIT = 56 * 1024 * 1024

_NT = (((1,), (1,)), ((), ()))
_TN = (((0,), (0,)), ((), ()))


def _params(sem):
    return pltpu.CompilerParams(dimension_semantics=sem, vmem_limit_bytes=VMEM_LIMIT)


def _mm(a, b):
    return jnp.dot(a.astype(MXU_DTYPE), b.astype(MXU_DTYPE), preferred_element_type=F32)


def _mm_nt(a, b):
    return lax.dot_general(a.astype(MXU_DTYPE), b.astype(MXU_DTYPE), _NT, preferred_element_type=F32)


def _mm_tn(a, b):
    return lax.dot_general(a.astype(MXU_DTYPE), b.astype(MXU_DTYPE), _TN, preferred_element_type=F32)


def _split(a):
    hi = a.astype(BF16)
    lo = (a - hi.astype(F32)).astype(BF16)
    return hi, lo


def _dot3(a, b):
    ah, al = _split(a)
    bh, bl = _split(b)
    d = lambda u, v: jnp.dot(u, v, preferred_element_type=F32)
    return d(ah, bh) + (d(ah, bl) + d(al, bh))


def _rmsnorm_kernel(x_ref, g_ref, o_ref):
    x = x_ref[...]
    ms = jnp.mean(x * x, axis=-1, keepdims=True)
    o_ref[...] = (x * lax.rsqrt(ms + EPS) * g_ref[...]).astype(o_ref.dtype)


def _rmsnorm(x2, g, tm):
    T, D = x2.shape
    return pl.pallas_call(
        _rmsnorm_kernel,
        out_shape=jax.ShapeDtypeStruct((T, D), MXU_DTYPE),
        grid=(T // tm,),
        in_specs=[pl.BlockSpec((tm, D), lambda i: (i, 0)), pl.BlockSpec((1, D), lambda i: (0, 0))],
        out_specs=pl.BlockSpec((tm, D), lambda i: (i, 0)),
        compiler_params=_params(("parallel",)),
        name="rmsnorm",
    )(x2, g.reshape(1, D))


def _matmul_kernel(h_ref, w_ref, o_ref):
    o_ref[...] = jnp.dot(h_ref[...], w_ref[...], preferred_element_type=F32).astype(o_ref.dtype)


def _matmul(h, w, tm, tn, out_dtype):
    T, D = h.shape
    N = w.shape[1]
    return pl.pallas_call(
        _matmul_kernel,
        out_shape=jax.ShapeDtypeStruct((T, N), out_dtype),
        grid=(T // tm, N // tn),
        in_specs=[pl.BlockSpec((tm, D), lambda i, j: (i, 0)), pl.BlockSpec((D, tn), lambda i, j: (0, j))],
        out_specs=pl.BlockSpec((tm, tn), lambda i, j: (i, j)),
        compiler_params=_params(("parallel", "arbitrary")),
        name="proj_main",
    )(h, w)


def _qk_kernel(h_ref, w_ref, g_ref, o_ref):
    z = jnp.dot(h_ref[...], w_ref[...], preferred_element_type=F32)
    for hh in range(DSA_HEADS):
        sl = slice(hh * DSA_HEAD_DIM, (hh + 1) * DSA_HEAD_DIM)
        zh = z[:, sl]
        ms = jnp.mean(zh * zh, axis=-1, keepdims=True)
        o_ref[:, sl] = (zh * lax.rsqrt(ms + EPS) * g_ref[:, sl]).astype(o_ref.dtype)


def _qk_proj(h, w, g, tm):
    T, D = h.shape
    W = DSA_HEADS * DSA_HEAD_DIM
    return pl.pallas_call(
        _qk_kernel,
        out_shape=jax.ShapeDtypeStruct((T, 2 * W), MXU_DTYPE),
        grid=(T // tm, 2),
        in_specs=[
            pl.BlockSpec((tm, D), lambda i, j: (i, 0)),
            pl.BlockSpec((D, W), lambda i, j: (0, j)),
            pl.BlockSpec((1, W), lambda i, j: (0, j)),
        ],
        out_specs=pl.BlockSpec((tm, W), lambda i, j: (i, j)),
        compiler_params=_params(("parallel", "arbitrary")),
        name="proj_qk",
    )(h, w, g)


VT_ONES = 16
VT_ROWS = DSA_HEAD_DIM + VT_ONES


def _vt_kernel(h_ref, wt_ref, o_ref):
    vt = lax.dot_general(wt_ref[...], h_ref[...], _NT, preferred_element_type=F32).astype(o_ref.dtype)
    ones = jnp.ones((VT_ONES, vt.shape[1]), o_ref.dtype)
    for hh in range(DSA_HEADS):
        o_ref[hh, :DSA_HEAD_DIM, :] = vt[hh * DSA_HEAD_DIM:(hh + 1) * DSA_HEAD_DIM]
        o_ref[hh, DSA_HEAD_DIM:, :] = ones


def _vt_proj(h, wt, B, S, tm):
    T, D = h.shape
    W = wt.shape[0]
    nt = S // tm
    return pl.pallas_call(
        _vt_kernel,
        out_shape=jax.ShapeDtypeStruct((B, DSA_HEADS, VT_ROWS, S), MXU_DTYPE),
        grid=(T // tm,),
        in_specs=[pl.BlockSpec((tm, D), lambda i: (i, 0)), pl.BlockSpec((W, D), lambda i: (0, 0))],
        out_specs=pl.BlockSpec((None, DSA_HEADS, VT_ROWS, tm), lambda i: (i // nt, 0, 0, i % nt)),
        compiler_params=_params(("parallel",)),
        name="proj_vt",
    )(h, wt)


IDX_Q_W = IDX_HEADS * IDX_DIM


def _idx_kernel(h_ref, w_ref, wt_ref, lng_ref, lnb_ref, iqt_ref, ki_ref, ga_ref, wit_ref):
    h = h_ref[...]
    z = jnp.dot(h, w_ref[...], preferred_element_type=F32)
    ik = z[:, :IDX_DIM]
    mu = jnp.mean(ik, axis=-1, keepdims=True)
    var = jnp.mean(jnp.square(ik - mu), axis=-1, keepdims=True)
    ki = (ik - mu) * lax.rsqrt(var + EPS) * lng_ref[...] + lnb_ref[...]
    ki_ref[...] = ki.astype(ki_ref.dtype)
    ga_ref[...] = z[:, LANES:LANES + GLA_GATE_RANK]
    zt = lax.dot_general(wt_ref[...], h, _NT, preferred_element_type=F32)
    iqt_ref[...] = zt[:IDX_Q_W].astype(iqt_ref.dtype)
    wit_ref[...] = zt[IDX_Q_W:IDX_Q_W + IDX_HEADS] * (IDX_HEADS ** -0.5 * IDX_DIM ** -0.5)


def _idx_proj(h, w, wt, lng, lnb, tm):
    T, D = h.shape
    return pl.pallas_call(
        _idx_kernel,
        out_shape=(
            jax.ShapeDtypeStruct((IDX_Q_W, T), MXU_DTYPE),
            jax.ShapeDtypeStruct((T, IDX_DIM), MXU_DTYPE),
            jax.ShapeDtypeStruct((T, GLA_GATE_RANK), F32),
            jax.ShapeDtypeStruct((IDX_HEADS, T), F32),
        ),
        grid=(T // tm,),
        in_specs=[
            pl.BlockSpec((tm, D), lambda i: (i, 0)),
            pl.BlockSpec(w.shape, lambda i: (0, 0)),
            pl.BlockSpec(wt.shape, lambda i: (0, 0)),
            pl.BlockSpec((1, IDX_DIM), lambda i: (0, 0)),
            pl.BlockSpec((1, IDX_DIM), lambda i: (0, 0)),
        ],
        out_specs=(
            pl.BlockSpec((IDX_Q_W, tm), lambda i: (0, i)),
            pl.BlockSpec((tm, IDX_DIM), lambda i: (i, 0)),
            pl.BlockSpec((tm, GLA_GATE_RANK), lambda i: (i, 0)),
            pl.BlockSpec((IDX_HEADS, tm), lambda i: (0, i)),
        ),
        compiler_params=_params(("parallel",)),
        name="proj_idx",
    )(h, w, wt, lng, lnb)


def _gla_kernel(q_ref, k_ref, v_ref, r_ref, ga_ref, wa2_ref, ba_ref, ng_ref, o_ref, st_ref, tot_ref, kd_ref,
                oraw_ref, *, n_chunks):
    tb = q_ref.shape[0]

    @pl.when(pl.program_id(1) == 0)
    def _():
        st_ref[...] = jnp.zeros_like(st_ref)

    x = _dot3(ga_ref[...], wa2_ref[...]) + ba_ref[...]
    la = (jnp.minimum(x, 0.0) - jnp.log(1.0 + jnp.exp(-jnp.abs(x)))) * (1.0 / GLA_GATE_TEMP)
    row = lax.broadcasted_iota(I32, (tb, tb), 0)
    col = lax.broadcasted_iota(I32, (tb, tb), 1)
    same = lax.shift_right_logical(row, 6) == lax.shift_right_logical(col, 6)
    ones_blk = jnp.where(same, 1.0, 0.0).astype(BF16)
    tril_blk = jnp.where(same, jnp.where(row >= col, 1.0, 0.0), 0.0).astype(BF16)
    la_hi, la_lo = _split(la)
    d = lambda u, v: jnp.dot(u, v, preferred_element_type=F32)
    tot = d(ones_blk, la_hi) + d(ones_blk, la_lo)
    cum = d(tril_blk, la_hi) + d(tril_blk, la_lo)
    tot_ref[...] = tot
    kd_ref[...] = (k_ref[...].astype(F32) * jnp.exp(tot - cum)).astype(kd_ref.dtype)

    heads = range(GLA_HEADS)
    ksl = lambda hh: slice(hh * GLA_DK, (hh + 1) * GLA_DK)
    vsl = lambda hh: slice(hh * GLA_DV, (hh + 1) * GLA_DV)

    def chunk(c, carry):
        rows = pl.ds(pl.multiple_of(c * CHUNK, CHUNK), CHUNK)
        first = pl.ds(pl.multiple_of(c * CHUNK, CHUNK), 1)
        upd = [_mm_tn(v_ref[rows, vsl(hh)], kd_ref[rows, ksl(hh)]) for hh in heads]
        st = [st_ref[hh] * jnp.exp(tot_ref[first, ksl(hh)]) + upd[hh] for hh in heads]
        for hh in heads:
            st_ref[hh] = st[hh]
        for hh in heads:
            oraw_ref[rows, vsl(hh)] = _mm_nt(q_ref[rows, ksl(hh)], st[hh])
        return carry

    lax.fori_loop(0, n_chunks, chunk, 0)

    ng = ng_ref[...]
    for hh in heads:
        o = oraw_ref[:, vsl(hh)] * (GLA_DK ** -0.5)
        ms = jnp.mean(o * o, axis=-1, keepdims=True)
        r = r_ref[:, vsl(hh)].astype(F32)
        o_ref[:, vsl(hh)] = (o * lax.rsqrt(ms + EPS) * ng * (r / (1.0 + jnp.exp(-r)))).astype(o_ref.dtype)


def _gla(zm, ga, wa2, ba, ng, B, S, tb):
    T = B * S
    nb = S // tb
    qk_w = GLA_HEADS * GLA_DK
    v_w = GLA_HEADS * GLA_DV
    tok = lambda b, i: b * nb + i
    return pl.pallas_call(
        functools.partial(_gla_kernel, n_chunks=tb // CHUNK),
        out_shape=jax.ShapeDtypeStruct((T, v_w), MXU_DTYPE),
        grid=(B, nb),
        in_specs=[
            pl.BlockSpec((tb, qk_w), lambda b, i: (tok(b, i), 0)),
            pl.BlockSpec((tb, qk_w), lambda b, i: (tok(b, i), 1)),
            pl.BlockSpec((tb, v_w), lambda b, i: (tok(b, i), 1)),
            pl.BlockSpec((tb, v_w), lambda b, i: (tok(b, i), 2)),
            pl.BlockSpec((tb, GLA_GATE_RANK), lambda b, i: (tok(b, i), 0)),
            pl.BlockSpec((GLA_GATE_RANK, qk_w), lambda b, i: (0, 0)),
            pl.BlockSpec((1, qk_w), lambda b, i: (0, 0)),
            pl.BlockSpec((1, GLA_DV), lambda b, i: (0, 0)),
        ],
        out_specs=pl.BlockSpec((tb, v_w), lambda b, i: (tok(b, i), 0)),
        scratch_shapes=[pltpu.VMEM((GLA_HEADS, GLA_DV, GLA_DK), F32), pltpu.VMEM((tb, qk_w), F32),
                        pltpu.VMEM((tb, qk_w), MXU_DTYPE), pltpu.VMEM((tb, v_w), F32)],
        compiler_params=_params(("parallel", "arbitrary")),
        name="gla",
    )(zm, zm, zm, zm, ga, wa2, ba, ng)


def _last_kb(qb, QB, KB):
    return ((qb + 1) * QB - 1) // KB


def _causal_steps(S, QB, KB):
    pairs = [(q, k) for q in range(S // QB) for k in range(_last_kb(q, QB, KB) + 1)]
    qs, ks = zip(*pairs)
    return jnp.asarray(qs, I32), jnp.asarray(ks, I32)


def _select_kernel(qb_ref, kb_ref, iqt_ref, wit_ref, ki_ref, mask_ref, hi_ref, lo_ref, gm_ref, *,
                   QB, KB, S, topk):
    qb = qb_ref[pl.program_id(1)]
    kb = kb_ref[pl.program_id(1)]
    last = _last_kb(qb, QB, KB)

    def rows_of(i):
        return pl.ds(pl.multiple_of(i * KB, KB), KB)

    @pl.when(kb == 0)
    def _():
        gm_ref[...] = jnp.full(gm_ref.shape, INT_MIN, I32)

    rc = 32 * 1024 // QB
    assert topk % rc == 0 and KB % rc == 0
    t_chunk = lax.shift_right_logical(qb * QB + lax.broadcasted_iota(I32, (rc, QB), 1), 6)
    for c in range(KB // rc):
        ki = ki_ref[c * rc:(c + 1) * rc, :]
        sc = jnp.zeros((rc, QB), F32)
        for hh in range(IDX_HEADS):
            lg = _mm(ki, iqt_ref[hh * IDX_DIM:(hh + 1) * IDX_DIM, :])
            sc = sc + jnp.maximum(lg, 0.0) * wit_ref[hh:hh + 1, :]
        s_chunk = lax.shift_right_logical(kb * KB + c * rc + lax.broadcasted_iota(I32, (rc, QB), 0), 6)
        bits = lax.bitcast_convert_type(sc, I32)
        sign = lax.shift_right_arithmetic(bits, 31)
        key = (bits ^ (sign & INT_MAX)) - sign
        key = jnp.where(s_chunk <= t_chunk, key, INT_MIN)
        rows = pl.ds(pl.multiple_of(kb * KB + c * rc, rc), rc)
        hi_ref[rows, :] = lax.shift_right_arithmetic(key, 16).astype(I16)
        lo_ref[rows, :] = ((key & 0xFFFF) + I16_MIN).astype(I16)
        g0 = (c * rc) % topk
        gm_ref[g0:g0 + rc, :] = jnp.maximum(gm_ref[g0:g0 + rc, :], key)

    @pl.when(kb == last)
    def _():
        nblk = last + 1
        slab = 32
        one, zero = jnp.int16(1), jnp.int16(0)

        def fold(m, reduce):
            part = m[0:slab]
            for r in range(1, KB // slab):
                part = reduce(part, m[r * slab:(r + 1) * slab])
            return part

        def count(pred):
            def body(i, acc):
                return acc + fold(pred(i), jnp.add)
            acc = lax.fori_loop(0, nblk, body, jnp.zeros((slab, QB), I16))
            return jnp.sum(acc.astype(I32), axis=0, keepdims=True)

        def count_ge(ref, thr):
            t16 = thr.astype(I16)
            return count(lambda i: jnp.where(ref[rows_of(i), :] >= t16, one, zero))

        def search(ref, target, lo, c_lo, hi, c_hi, rounds):
            def is_open(lo, c_lo, hi):
                return jnp.where(c_lo > target, jnp.where(hi != lo + 1, 1, 0), 0)

            def step(_, st):
                lo, c_lo, hi, c_hi = st
                open_ = is_open(lo, c_lo, hi) > 0
                mid = lax.shift_right_arithmetic(lo + hi, 1)
                c = count_ge(ref, mid)
                up = jnp.logical_and(open_, c >= target)
                dn = jnp.logical_and(open_, c < target)
                return jnp.where(up, mid, lo), jnp.where(up, c, c_lo), jnp.where(dn, mid, hi), jnp.where(dn, c, c_hi)

            def halvings_left(st):
                lo, c_lo, hi, _c_hi = st
                width = jnp.where(is_open(lo, c_lo, hi) > 0, hi - lo, 1)
                return jnp.max(32 - lax.clz(width - 1))

            st = (lo, c_lo, hi, c_hi)
            for most in rounds:
                st = lax.fori_loop(0, jnp.minimum(halvings_left(st), most), step, st)
            lo, c_lo, _hi, c_hi = st
            return lo, c_lo, c_hi

        zeros = jnp.zeros((1, QB), I32)
        gm = gm_ref[...]
        g_lo = jnp.maximum(lax.shift_right_arithmetic(jnp.min(gm, axis=0, keepdims=True), 16), I16_MIN + 1)
        g_hi = lax.shift_right_arithmetic(jnp.max(gm, axis=0, keepdims=True), 16) + 1
        hstar, ch_ge, ch_gt = search(hi_ref, topk, g_lo, count_ge(hi_ref, g_lo), g_hi, zeros, (16,))
        split = ch_ge > topk
        h16 = hstar.astype(I16)

        def low_half():
            def build(i, carry):
                lo_ref[rows_of(i), :] = jnp.where(hi_ref[rows_of(i), :] == h16, lo_ref[rows_of(i), :],
                                                  jnp.int16(I16_MIN))
                return carry
            lax.fori_loop(0, nblk, build, 0)
            target = jnp.where(split, topk - ch_gt, INT_MAX)
            return search(lo_ref, target, jnp.full((1, QB), I16_MIN, I32), ch_ge - ch_gt,
                          jnp.full((1, QB), I16_MAX + 1, I32), zeros, (12, 2, 2))

        any_split = jnp.max(jnp.where(split, 1, 0)) > 0
        lstar, cl_ge, cl_gt = lax.cond(any_split, low_half,
                                       lambda: (jnp.full((1, QB), I16_MIN, I32), zeros, zeros))
        lstar = jnp.where(split, lstar, I16_MIN)
        l16 = lstar.astype(I16)
        c_ge = jnp.where(split, ch_gt + cl_ge, ch_ge)
        c_gt = jnp.where(split, ch_gt + cl_gt, ch_gt)
        excess = c_ge > topk
        need = jnp.where(excess, topk - c_gt, S + 1)
        row16 = lax.broadcasted_iota(I32, (KB, QB), 0).astype(I16)

        def before(i, j):
            return jnp.where(row16 + (i * KB).astype(I16) < j.astype(I16), one, zero)

        def tied(i, then):
            return jnp.where(hi_ref[rows_of(i), :] == h16, jnp.where(lo_ref[rows_of(i), :] == l16, then, zero), zero)

        def tie_cut():
            def step(_, c):
                lo, hi = c
                mid = lax.shift_right_arithmetic(lo + hi, 1)
                ok = count(lambda i: tied(i, before(i, mid))) >= need
                return jnp.where(ok, lo, mid + 1), jnp.where(ok, mid, hi)
            n_steps = max(1, math.ceil(math.log2(S + 1)))
            lo, _hi = lax.fori_loop(0, n_steps, step, (zeros, jnp.full((1, QB), S, I32)))
            return lo

        any_excess = jnp.max(jnp.where(excess, 1, 0)) > 0
        jcut = lax.cond(any_excess, tie_cut, lambda: jnp.full((1, QB), S, I32))

        def write(i, carry):
            hi_t, lo_t = hi_ref[rows_of(i), :], lo_ref[rows_of(i), :]
            in_bucket = jnp.where(lo_t > l16, one, jnp.where(lo_t == l16, before(i, jcut), zero))
            sel = jnp.where(hi_t > h16, one, jnp.where(hi_t == h16, in_bucket, zero))
            mask_ref[rows_of(i), :] = sel.astype(mask_ref.dtype)
            return carry

        lax.fori_loop(0, nblk, write, 0)

        def clear(i, carry):
            mask_ref[rows_of(i), :] = jnp.zeros((KB, QB), mask_ref.dtype)
            return carry

        lax.fori_loop(nblk, S // KB, clear, 0)


def _dsa_select(iqt, wit, ki, B, S, QB, KB):
    nq, nk = S // QB, S // KB
    topk = min(TOPK_MAX, S // 4)
    assert KB % topk == 0
    assert S <= I16_MAX
    qs, ks = _causal_steps(S, QB, KB)
    return pl.pallas_call(
        functools.partial(_select_kernel, QB=QB, KB=KB, S=S, topk=topk),
        out_shape=jax.ShapeDtypeStruct((B, S, S), jnp.int8),
        grid_spec=pltpu.PrefetchScalarGridSpec(
            num_scalar_prefetch=2,
            grid=(B, qs.shape[0]),
            in_specs=[
                pl.BlockSpec((IDX_Q_W, QB), lambda b, s, qs, ks: (0, b * nq + qs[s])),
                pl.BlockSpec((IDX_HEADS, QB), lambda b, s, qs, ks: (0, b * nq + qs[s])),
                pl.BlockSpec((KB, IDX_DIM), lambda b, s, qs, ks: (b * nk + ks[s], 0)),
            ],
            out_specs=pl.BlockSpec((None, S, QB), lambda b, s, qs, ks: (b, 0, qs[s])),
            scratch_shapes=[pltpu.VMEM((S, QB), I16), pltpu.VMEM((S, QB), I16), pltpu.VMEM((topk, QB), I32)],
        ),
        compiler_params=_params(("parallel", "arbitrary")),
        name="dsa_select",
    )(qs, ks, iqt, wit, ki)


LOGIT_SAFE = 100.0


def _attn_kernel(qb_ref, kb_ref, small_ref, q_ref, k_ref, vt_ref, m_ref, o_ref, acc_ref, run_ref, *, QB, KB):
    qb = qb_ref[pl.program_id(1)]
    kb = kb_ref[pl.program_id(1)]
    last = _last_kb(qb, QB, KB)

    @pl.when(kb == 0)
    def _():
        acc_ref[...] = jnp.zeros_like(acc_ref)
        run_ref[...] = jnp.full(run_ref.shape, -1e30, F32)

    head = lambda hh: slice(hh * DSA_HEAD_DIM, (hh + 1) * DSA_HEAD_DIM)
    qk_dot = lambda hh: _mm_nt(k_ref[:, head(hh)], q_ref[:, head(hh)])

    @pl.when(small_ref[0] == 1)
    def _():
        mb = m_ref[...].astype(MXU_DTYPE)
        lg = qk_dot(0)
        for hh in range(DSA_HEADS):
            lg_next = qk_dot(hh + 1) if hh + 1 < DSA_HEADS else None
            p = jnp.exp2(lg).astype(MXU_DTYPE) * mb
            acc_ref[hh] += jnp.dot(vt_ref[hh], p, preferred_element_type=F32)
            lg = lg_next

    @pl.when(small_ref[0] != 1)
    def _():
        selected = m_ref[...].astype(F32) > 0.0

        def logits(hh):
            lg = jnp.where(selected, qk_dot(hh), -jnp.inf)
            return lg, jnp.max(lg, axis=0, keepdims=True)

        nxt = logits(0)
        for hh in range(DSA_HEADS):
            lg, top = nxt
            nxt = logits(hh + 1) if hh + 1 < DSA_HEADS else None
            run_old = run_ref[hh]
            run_new = jnp.maximum(run_old, top)
            p = jnp.exp2(lg - run_new).astype(MXU_DTYPE)
            acc_ref[hh] = acc_ref[hh] * jnp.exp2(run_old - run_new) + jnp.dot(vt_ref[hh], p,
                                                                              preferred_element_type=F32)
            run_ref[hh] = run_new

    @pl.when(kb == last)
    def _():
        for hh in range(DSA_HEADS):
            acc = acc_ref[hh]
            o = acc[:DSA_HEAD_DIM] / acc[DSA_HEAD_DIM:DSA_HEAD_DIM + 1]
            o_ref[:, hh * DSA_HEAD_DIM:(hh + 1) * DSA_HEAD_DIM] = o.T.astype(o_ref.dtype)


def _dsa_attn(qk, vt, mask, logit_bound, B, S, QB, KB):
    T = B * S
    nq, nk = S // QB, S // KB
    W = DSA_HEADS * DSA_HEAD_DIM
    qs, ks = _causal_steps(S, QB, KB)
    small = (logit_bound <= LOGIT_SAFE).astype(I32).reshape(1)
    return pl.pallas_call(
        functools.partial(_attn_kernel, QB=QB, KB=KB),
        out_shape=jax.ShapeDtypeStruct((T, W), MXU_DTYPE),
        grid_spec=pltpu.PrefetchScalarGridSpec(
            num_scalar_prefetch=3,
            grid=(B, qs.shape[0]),
            in_specs=[
                pl.BlockSpec((QB, W), lambda b, s, qs, ks, sm: (b * nq + qs[s], 0)),
                pl.BlockSpec((KB, W), lambda b, s, qs, ks, sm: (b * nk + ks[s], 1)),
                pl.BlockSpec((None, DSA_HEADS, VT_ROWS, KB), lambda b, s, qs, ks, sm: (b, 0, 0, ks[s])),
                pl.BlockSpec((None, KB, QB), lambda b, s, qs, ks, sm: (b, ks[s], qs[s])),
            ],
            out_specs=pl.BlockSpec((QB, W), lambda b, s, qs, ks, sm: (b * nq + qs[s], 0)),
            scratch_shapes=[pltpu.VMEM((DSA_HEADS, VT_ROWS, QB), F32), pltpu.VMEM((DSA_HEADS, 1, QB), F32)],
        ),
        compiler_params=_params(("parallel", "arbitrary")),
        name="dsa_attn",
    )(qs, ks, small, qk, qk, vt, mask)


def _post_kernel(og_ref, od_ref, ga_ref, gb_ref, x_ref, pa_ref, pb_ref, wo_ref, g2_ref, wr_ref, br_ref,
                 x1_ref, h2_ref, route_ref):
    sig = lambda v: 1.0 / (1.0 + jnp.exp(-v))
    a = jnp.dot(og_ref[...], pa_ref[...], preferred_element_type=F32)
    b = jnp.dot(od_ref[...], pb_ref[...], preferred_element_type=F32)
    mix = sig(ga_ref[...].astype(F32)) * a + sig(gb_ref[...].astype(F32)) * b
    x1 = x_ref[...] + _mm(mix, wo_ref[...])
    x1_ref[...] = x1
    ms = jnp.mean(x1 * x1, axis=-1, keepdims=True)
    h2 = x1 * lax.rsqrt(ms + EPS) * g2_ref[...]
    h2_ref[...] = h2.astype(h2_ref.dtype)

    logits = _dot3(h2, wr_ref[...]) + br_ref[...]
    tm = logits.shape[0]
    lane = lax.broadcasted_iota(I32, (tm, LANES), 1).astype(F32)
    neg = -jnp.inf
    far = float(2 * LANES)
    rmax = lambda v: jnp.max(v, axis=1, keepdims=True)
    rmin = lambda v: jnp.min(v, axis=1, keepdims=True)
    gl = jnp.where(lane >= N_EXPERTS, jnp.where(lane < N_EXPERTS + N_GROUPS, logits, neg), neg)
    gmax = rmax(gl)
    g_w = 1.0 / jnp.sum(jnp.exp(gl - gmax), axis=1, keepdims=True)
    g_idx = rmin(jnp.where(gl == gmax, lane, far)) - N_EXPERTS
    e_lo = g_idx * EXPERTS_PER_GROUP
    el = jnp.where(lane >= e_lo, jnp.where(lane < e_lo + EXPERTS_PER_GROUP, logits, neg), neg)
    m1 = rmax(el)
    e1 = rmin(jnp.where(el == m1, lane, far))
    el2 = jnp.where(lane == e1, neg, el)
    m2 = rmax(el2)
    e2 = rmin(jnp.where(el2 == m2, lane, far))
    p2 = jnp.exp(m2 - m1)
    w1 = g_w / (1.0 + p2)
    route_ref[...] = jnp.where(lane == 0.0, e1, jnp.where(lane == 1.0, e2, jnp.where(
        lane == 2.0, w1, jnp.where(lane == 3.0, w1 * p2, 0.0))))


def _post(og, od, zm, x2, pa, pb, wo, g2, wr, br, tm):
    T, D = x2.shape
    row = lambda i: (i, 0)
    full = lambda i: (0, 0)
    return pl.pallas_call(
        _post_kernel,
        out_shape=(
            jax.ShapeDtypeStruct((T, D), F32),
            jax.ShapeDtypeStruct((T, D), F32),
            jax.ShapeDtypeStruct((T, LANES), F32),
        ),
        grid=(T // tm,),
        in_specs=[
            pl.BlockSpec((tm, D), row),
            pl.BlockSpec((tm, D), row),
            pl.BlockSpec((tm, D), lambda i: (i, 3)),
            pl.BlockSpec((tm, D), lambda i: (i, 4)),
            pl.BlockSpec((tm, D), row),
            pl.BlockSpec((D, D), full),
            pl.BlockSpec((D, D), full),
            pl.BlockSpec((D, D), full),
            pl.BlockSpec((1, D), full),
            pl.BlockSpec((D, LANES), full),
            pl.BlockSpec((1, LANES), full),
        ],
        out_specs=(pl.BlockSpec((tm, D), row), pl.BlockSpec((tm, D), row), pl.BlockSpec((tm, LANES), row)),
        compiler_params=_params(("parallel",)),
        name="merge_router",
    )(og, od, zm, zm, x2, pa, pb, wo, g2, wr, br)


MOE_TM = 256


def _gather_combine_kernel(idx_ref, nxt_ref, x1_ref, route_ref, src_hbm, o_ref, buf_ref, sem):
    i = pl.program_id(0)
    n = pl.num_programs(0)
    tm = o_ref.shape[0]
    slot = lax.rem(i, 2)

    def fetch(ids_ref, s):
        def start(r, carry):
            for k in range(2):
                pltpu.make_async_copy(src_hbm.at[pl.ds(ids_ref[0, 0, 2 * r + k], 1), :],
                                      buf_ref.at[s, pl.ds(k * tm + r, 1), :], sem.at[s]).start(priority=k)
            return carry
        lax.fori_loop(0, tm, start, 0, unroll=8)

    @pl.when(i == 0)
    def _():
        fetch(idx_ref, 0)

    @pl.when(i + 1 < n)
    def _():
        fetch(nxt_ref, 1 - slot)

    pltpu.make_async_copy(src_hbm.at[pl.ds(0, 2 * tm), :], buf_ref.at[slot], sem.at[slot]).wait()
    r = route_ref[...]
    lane = lax.broadcasted_iota(I32, r.shape, 1)
    w1 = jnp.sum(jnp.where(lane == 2, r, 0.0), axis=1, keepdims=True)
    w2 = jnp.sum(jnp.where(lane == 3, r, 0.0), axis=1, keepdims=True)
    o_ref[...] = x1_ref[...] + (w1 * buf_ref[slot, :tm, :] + w2 * buf_ref[slot, tm:, :])


def _gather_combine(x1, ys, pos, route, tm):
    T, D = x1.shape
    nt = T // tm
    idx3 = pos.reshape(nt, 1, 2 * tm)
    row = lambda i: (i, 0)
    return pl.pallas_call(
        _gather_combine_kernel,
        out_shape=jax.ShapeDtypeStruct((T, D), F32),
        grid=(nt,),
        in_specs=[
            pl.BlockSpec((1, 1, 2 * tm), lambda i: (i, 0, 0), memory_space=pltpu.SMEM),
            pl.BlockSpec((1, 1, 2 * tm), lambda i: (jnp.minimum(i + 1, nt - 1), 0, 0), memory_space=pltpu.SMEM),
            pl.BlockSpec((tm, D), row),
            pl.BlockSpec((tm, LANES), row),
            pl.BlockSpec(memory_space=pl.ANY),
        ],
        out_specs=pl.BlockSpec((tm, D), row),
        scratch_shapes=[pltpu.VMEM((2, 2 * tm, D), ys.dtype), pltpu.SemaphoreType.DMA((2,))],
        compiler_params=_params(("arbitrary",)),
        name="moe_gather_combine",
    )(idx3, idx3, x1, route, ys)


def _scatter_kernel(idx_ref, x_ref, o_hbm, sem, *, fanout):
    tm = x_ref.shape[0]

    def start(r, carry):
        for k in range(fanout):
            dst = idx_ref[0, 0, fanout * r + k]
            pltpu.make_async_copy(x_ref.at[pl.ds(r, 1), :], o_hbm.at[pl.ds(dst, 1), :], sem).start(priority=k)
        return carry

    lax.fori_loop(0, tm, start, 0, unroll=8)
    for k in range(fanout):
        pltpu.make_async_copy(x_ref, o_hbm.at[pl.ds(0, tm), :], sem).wait()


def _row_scatter(x, idx, fanout, tm, name):
    n, D = x.shape
    return pl.pallas_call(
        functools.partial(_scatter_kernel, fanout=fanout),
        out_shape=jax.ShapeDtypeStruct((fanout * n, D), x.dtype),
        grid=(n // tm,),
        in_specs=[
            pl.BlockSpec((1, 1, fanout * tm), lambda i: (i, 0, 0), memory_space=pltpu.SMEM),
            pl.BlockSpec((tm, D), lambda i: (i, 0)),
        ],
        out_specs=pl.BlockSpec(memory_space=pl.ANY),
        scratch_shapes=[pltpu.SemaphoreType.DMA(())],
        compiler_params=_params(("arbitrary",)),
        name=name,
    )(idx.reshape(n // tm, 1, fanout * tm), x)


def _route_plan(route, tm):
    n = 2 * route.shape[0]
    n_tiles = n // tm
    ids = route[:, :2].astype(I32).reshape(n)
    onehot = (ids[:, None] == jnp.arange(N_EXPERTS, dtype=I32)[None, :]).astype(F32).reshape(n_tiles, tm, N_EXPERTS)
    tril = (jnp.arange(tm)[:, None] >= jnp.arange(tm)[None, :]).astype(F32)
    in_tile = jnp.einsum("rc,tce->tre", tril, onehot)
    per_tile = in_tile[:, -1, :]
    before = jnp.cumsum(per_tile, axis=0) - per_tile
    counts = jnp.sum(per_tile, axis=0).astype(I32)
    starts = (jnp.cumsum(counts) - counts).astype(I32)
    rank = jnp.sum(onehot * (in_tile + before[:, None, :] - 1.0), axis=2).reshape(n)
    pos = (jnp.sum(onehot.reshape(n, N_EXPERTS) * starts[None, :].astype(F32), axis=1) + rank).astype(I32)
    bounds = jnp.sort(jnp.concatenate([jnp.arange(n_tiles, dtype=I32) * tm, starts]))
    ends = jnp.concatenate([bounds[1:], jnp.full((1,), n, I32)])
    seg_tile = jnp.minimum(bounds // tm, n_tiles - 1)
    seg_exp = jnp.clip(jnp.sum((starts[None, :] <= bounds[:, None]).astype(I32), axis=1) - 1, 0, N_EXPERTS - 1)
    seg_first = jnp.concatenate([jnp.ones((1,), I32), (seg_tile[1:] != seg_tile[:-1]).astype(I32)])
    return pos, (seg_tile, seg_exp, bounds - seg_tile * tm, ends - seg_tile * tm, seg_first)


def _experts_kernel(tile_ref, exp_ref, lo_ref, hi_ref, first_ref, x_ref, wg_ref, wu_ref, wd_ref, o_ref):
    s = pl.program_id(0)
    lo, hi = lo_ref[s], hi_ref[s]

    @pl.when(first_ref[s] == 1)
    def _():
        o_ref[...] = jnp.zeros_like(o_ref)

    @pl.when(hi > lo)
    def _():
        x = x_ref[...].astype(MXU_DTYPE)
        a = jnp.dot(x, wg_ref[...], preferred_element_type=F32)
        u = jnp.dot(x, wu_ref[...], preferred_element_type=F32)
        y = _mm(a / (1.0 + jnp.exp(-a)) * u, wd_ref[...])
        row = lax.broadcasted_iota(I32, y.shape, 0)
        o_ref[...] += jnp.where(row >= lo, jnp.where(row < hi, y, 0.0), 0.0)


def _experts(xs, segs, wg, wu, wd, tm):
    n, D = xs.shape
    F = wg.shape[2]
    tile_of = lambda s, tile, exp, lo, hi, first: (tile[s], 0)
    w_of = lambda s, tile, exp, lo, hi, first: (exp[s], 0, 0)
    return pl.pallas_call(
        _experts_kernel,
        out_shape=jax.ShapeDtypeStruct((n, D), F32),
        grid_spec=pltpu.PrefetchScalarGridSpec(
            num_scalar_prefetch=5,
            grid=(segs[0].shape[0],),
            in_specs=[
                pl.BlockSpec((tm, D), tile_of),
                pl.BlockSpec((None, D, F), w_of),
                pl.BlockSpec((None, D, F), w_of),
                pl.BlockSpec((None, F, D), w_of),
            ],
            out_specs=pl.BlockSpec((tm, D), tile_of),
        ),
        compiler_params=_params(("arbitrary",)),
        name="experts",
    )(*segs, xs, wg, wu, wd)


def _moe(h2, route, x1, wg, wu, wd):
    T, D = x1.shape
    tm = min(MOE_TM, T)
    pos, segs = _route_plan(route, tm)
    tok = min(2 * MOE_TM, T)
    xs = _row_scatter(h2, pos, 2, tok, "moe_scatter")
    ys = _experts(xs, segs, wg, wu, wd, tm)
    return _gather_combine(x1, ys, pos, route, tok)


def _pad_cols(w, n):
    return jnp.pad(w, ((0, 0), (0, n - w.shape[1])))


def _layer(x2, B, S, p):
    T, D = x2.shape
    assert D == 2 * GLA_HEADS * GLA_DK == GLA_HEADS * GLA_DV == DSA_HEADS * DSA_HEAD_DIM
    assert S % 512 == 0
    cd = MXU_DTYPE
    qk_w, v_w, dsa_w = GLA_HEADS * GLA_DK, GLA_HEADS * GLA_DV, DSA_HEADS * DSA_HEAD_DIM
    splits = (qk_w, qk_w, v_w, GLA_GATE_RANK, v_w, dsa_w, dsa_w, dsa_w, IDX_Q_W, IDX_DIM, IDX_HEADS, D, D)
    offs = [0]
    for s in splits:
        offs.append(offs[-1] + s)
    w_in = p["w_in"]
    (w_gq, w_gk, w_gv, w_ga, w_gr, w_dq, w_dk, w_dv, w_iq, w_ik, w_iw, w_ta, w_tb) = [
        w_in[:, offs[i]:offs[i + 1]] for i in range(len(splits))]

    w_main = jnp.concatenate([w_gq, w_gk, w_gv, w_gr, w_ta, w_tb], axis=1).astype(cd)
    w_qk = jnp.concatenate([w_dq, w_dk], axis=1).astype(cd)
    g_qk = jnp.concatenate([jnp.tile(p["dsa_q_norm_g"] * (DSA_HEAD_DIM ** -0.5 * math.log2(math.e)), DSA_HEADS),
                            jnp.tile(p["dsa_k_norm_g"], DSA_HEADS)]).reshape(1, 2 * dsa_w).astype(F32)
    w_vt = w_dv.T.astype(cd)
    w_idx = jnp.concatenate([_pad_cols(w_ik, LANES), _pad_cols(w_ga, LANES)], axis=1).astype(cd)
    w_idxt = jnp.pad(jnp.concatenate([w_iq, w_iw], axis=1).T, ((0, 16 - IDX_HEADS), (0, 0))).astype(cd)

    tm = min(1024, T)
    h = _rmsnorm(x2, p["norm1_g"], tm)
    zm = _matmul(h, w_main, tm, 2560, cd)
    qk = _qk_proj(h, w_qk, g_qk, tm)
    vt = _vt_proj(h, w_vt, B, S, min(1024, S))
    iqt, ki, ga, wit = _idx_proj(h, w_idx, w_idxt, p["idx_k_ln_g"].reshape(1, -1), p["idx_k_ln_b"].reshape(1, -1),
                                 tm)

    o_gla = _gla(zm, ga, p["gla_w_a2"], p["gla_b_a"].reshape(1, -1), p["gla_norm_g"].reshape(1, -1), B, S, 256)

    QB, KB = 256, min(1024, S)
    mask = _dsa_select(iqt, wit, ki, B, S, QB, KB)
    logit_bound = (DSA_HEAD_DIM * jnp.max(jnp.abs(g_qk[0, :dsa_w])) * jnp.max(jnp.abs(g_qk[0, dsa_w:])))
    o_dsa = _dsa_attn(qk, vt, mask, logit_bound, B, S, 2 * QB, KB)

    w_r = _pad_cols(jnp.concatenate([p["w_router_expert"], p["w_router_group"]], axis=1), LANES)
    b_r = _pad_cols(jnp.concatenate([p["b_router_expert"], p["b_router_group"]]).reshape(1, -1), LANES)
    x1, h2, route = _post(o_gla, o_dsa, zm, x2, p["w_branch_gla"].astype(cd), p["w_branch_dsa"].astype(cd),
                          p["w_out"].astype(cd), p["norm2_g"].reshape(1, -1), w_r, b_r, min(512, T))
    return _moe(h2, route, x1, p["w_exp_gate"].astype(cd), p["w_exp_up"].astype(cd), p["w_exp_down"].astype(cd))


def kernel(x, norm1_g, w_in, gla_w_a2, gla_b_a, gla_norm_g, dsa_q_norm_g, dsa_k_norm_g, idx_k_ln_g, idx_k_ln_b,
           w_branch_gla, w_branch_dsa, w_out, norm2_g, w_router_group, b_router_group, w_router_expert,
           b_router_expert, w_exp_gate, w_exp_up, w_exp_down):
    B, S, D = x.shape
    stacked = dict(norm1_g=norm1_g, w_in=w_in, gla_w_a2=gla_w_a2, gla_b_a=gla_b_a, gla_norm_g=gla_norm_g,
                   dsa_q_norm_g=dsa_q_norm_g, dsa_k_norm_g=dsa_k_norm_g, idx_k_ln_g=idx_k_ln_g,
                   idx_k_ln_b=idx_k_ln_b, w_branch_gla=w_branch_gla, w_branch_dsa=w_branch_dsa, w_out=w_out,
                   norm2_g=norm2_g, w_router_group=w_router_group, b_router_group=b_router_group,
                   w_router_expert=w_router_expert, b_router_expert=b_router_expert, w_exp_gate=w_exp_gate,
                   w_exp_up=w_exp_up, w_exp_down=w_exp_down)
    x2 = x.reshape(B * S, D).astype(F32)
    for l in range(w_in.shape[0]):
        x2 = _layer(x2, B, S, {k: v[l] for k, v in stacked.items()})
    return x2.reshape(B, S, D).astype(x.dtype)
```

```python
import functools
import math

import jax
import jax.numpy as jnp
from jax import lax
from jax.experimental import pallas as pl
from jax.experimental.pallas import tpu as pltpu

F32 = jnp.float32
BF16 = jnp.bfloat16
I32 = jnp.int32
MXU_DTYPE = BF16

CHUNK = 64
EPS = 1e-6
GLA_HEADS, GLA_DK, GLA_DV = 4, 128, 256
GLA_GATE_RANK = 16
GLA_GATE_TEMP = 16.0
DSA_HEADS, DSA_HEAD_DIM = 8, 128
IDX_HEADS, IDX_DIM = 8, 64
TOPK_MAX = 256
N_GROUPS, EXPERTS_PER_GROUP = 4, 8
N_EXPERTS = N_GROUPS * EXPERTS_PER_GROUP
LANES = 128

INT_MIN = -(2 ** 31)
INT_MAX = 2 ** 31 - 1
I16 = jnp.int16
I16_MIN, I16_MAX = -(2 ** 15), 2 ** 15 - 1

VMEM_LIMIT = 56 * 1024 * 1024

_NT = (((1,), (1,)), ((), ()))
_TN = (((0,), (0,)), ((), ()))


def _params(sem):
    return pltpu.CompilerParams(dimension_semantics=sem, vmem_limit_bytes=VMEM_LIMIT)


def _mm(a, b):
    return jnp.dot(a.astype(MXU_DTYPE), b.astype(MXU_DTYPE), preferred_element_type=F32)


def _mm_nt(a, b):
    return lax.dot_general(a.astype(MXU_DTYPE), b.astype(MXU_DTYPE), _NT, preferred_element_type=F32)


def _mm_tn(a, b):
    return lax.dot_general(a.astype(MXU_DTYPE), b.astype(MXU_DTYPE), _TN, preferred_element_type=F32)


def _split(a):
    hi = a.astype(BF16)
    lo = (a - hi.astype(F32)).astype(BF16)
    return hi, lo


def _dot3(a, b):
    ah, al = _split(a)
    bh, bl = _split(b)
    d = lambda u, v: jnp.dot(u, v, preferred_element_type=F32)
    return d(ah, bh) + (d(ah, bl) + d(al, bh))


def _norm_matmul_kernel(x_ref, g_ref, w_ref, h_ref, o_ref):
    @pl.when(pl.program_id(1) == 0)
    def _():
        x = x_ref[...]
        ms = jnp.mean(x * x, axis=-1, keepdims=True)
        h_ref[...] = (x * lax.rsqrt(ms + EPS) * g_ref[...]).astype(h_ref.dtype)

    o_ref[...] = jnp.dot(h_ref[...], w_ref[...], preferred_element_type=F32).astype(o_ref.dtype)


def _norm_matmul(x2, g, w, tm, tn, out_dtype):
    T, D = x2.shape
    N = w.shape[1]
    return pl.pallas_call(
        _norm_matmul_kernel,
        out_shape=(jax.ShapeDtypeStruct((T, D), MXU_DTYPE), jax.ShapeDtypeStruct((T, N), out_dtype)),
        grid=(T // tm, N // tn),
        in_specs=[pl.BlockSpec((tm, D), lambda i, j: (i, 0)), pl.BlockSpec((1, D), lambda i, j: (0, 0)),
                  pl.BlockSpec((D, tn), lambda i, j: (0, j))],
        out_specs=(pl.BlockSpec((tm, D), lambda i, j: (i, 0)), pl.BlockSpec((tm, tn), lambda i, j: (i, j))),
        compiler_params=_params(("parallel", "arbitrary")),
        name="norm_proj_main",
    )(x2, g.reshape(1, D), w)


def _qk_kernel(h_ref, w_ref, g_ref, o_ref):
    z = jnp.dot(h_ref[...], w_ref[...], preferred_element_type=F32)
    for hh in range(DSA_HEADS):
        sl = slice(hh * DSA_HEAD_DIM, (hh + 1) * DSA_HEAD_DIM)
        zh = z[:, sl]
        ms = jnp.mean(zh * zh, axis=-1, keepdims=True)
        o_ref[:, sl] = (zh * lax.rsqrt(ms + EPS) * g_ref[:, sl]).astype(o_ref.dtype)


def _qk_proj(h, w, g, tm):
    T, D = h.shape
    W = DSA_HEADS * DSA_HEAD_DIM
    return pl.pallas_call(
        _qk_kernel,
        out_shape=jax.ShapeDtypeStruct((T, 2 * W), MXU_DTYPE),
        grid=(T // tm, 2),
        in_specs=[
            pl.BlockSpec((tm, D), lambda i, j: (i, 0)),
            pl.BlockSpec((D, W), lambda i, j: (0, j)),
            pl.BlockSpec((1, W), lambda i, j: (0, j)),
        ],
        out_specs=pl.BlockSpec((tm, W), lambda i, j: (i, j)),
        compiler_params=_params(("parallel", "arbitrary")),
        name="proj_qk",
    )(h, w, g)


VT_ONES = 16
VT_ROWS = DSA_HEAD_DIM + VT_ONES


def _vt_kernel(h_ref, wt_ref, o_ref):
    vt = lax.dot_general(wt_ref[...], h_ref[...], _NT, preferred_element_type=F32).astype(o_ref.dtype)
    ones = jnp.ones((VT_ONES, vt.shape[1]), o_ref.dtype)
    for hh in range(DSA_HEADS):
        o_ref[hh, :DSA_HEAD_DIM, :] = vt[hh * DSA_HEAD_DIM:(hh + 1) * DSA_HEAD_DIM]
        o_ref[hh, DSA_HEAD_DIM:, :] = ones


def _vt_proj(h, wt, B, S, tm):
    T, D = h.shape
    W = wt.shape[0]
    nt = S // tm
    return pl.pallas_call(
        _vt_kernel,
        out_shape=jax.ShapeDtypeStruct((B, DSA_HEADS, VT_ROWS, S), MXU_DTYPE),
        grid=(T // tm,),
        in_specs=[pl.BlockSpec((tm, D), lambda i: (i, 0)), pl.BlockSpec((W, D), lambda i: (0, 0))],
        out_specs=pl.BlockSpec((None, DSA_HEADS, VT_ROWS, tm), lambda i: (i // nt, 0, 0, i % nt)),
        compiler_params=_params(("parallel",)),
        name="proj_vt",
    )(h, wt)


IDX_Q_W = IDX_HEADS * IDX_DIM


def _idx_kernel(h_ref, w_ref, wt_ref, lng_ref, lnb_ref, iqt_ref, ki_ref, ga_ref, wit_ref):
    h = h_ref[...]
    z = jnp.dot(h, w_ref[...], preferred_element_type=F32)
    ik = z[:, :IDX_DIM]
    mu = jnp.mean(ik, axis=-1, keepdims=True)
    var = jnp.mean(jnp.square(ik - mu), axis=-1, keepdims=True)
    ki = (ik - mu) * lax.rsqrt(var + EPS) * lng_ref[...] + lnb_ref[...]
    ki_ref[...] = ki.astype(ki_ref.dtype)
    ga_ref[...] = z[:, LANES:LANES + GLA_GATE_RANK]
    zt = lax.dot_general(wt_ref[...], h, _NT, preferred_element_type=F32)
    iqt_ref[...] = zt[:IDX_Q_W].astype(iqt_ref.dtype)
    wit_ref[...] = zt[IDX_Q_W:IDX_Q_W + IDX_HEADS] * (IDX_HEADS ** -0.5 * IDX_DIM ** -0.5)


def _idx_proj(h, w, wt, lng, lnb, tm):
    T, D = h.shape
    return pl.pallas_call(
        _idx_kernel,
        out_shape=(
            jax.ShapeDtypeStruct((IDX_Q_W, T), MXU_DTYPE),
            jax.ShapeDtypeStruct((T, IDX_DIM), MXU_DTYPE),
            jax.ShapeDtypeStruct((T, GLA_GATE_RANK), F32),
            jax.ShapeDtypeStruct((IDX_HEADS, T), F32),
        ),
        grid=(T // tm,),
        in_specs=[
            pl.BlockSpec((tm, D), lambda i: (i, 0)),
            pl.BlockSpec(w.shape, lambda i: (0, 0)),
            pl.BlockSpec(wt.shape, lambda i: (0, 0)),
            pl.BlockSpec((1, IDX_DIM), lambda i: (0, 0)),
            pl.BlockSpec((1, IDX_DIM), lambda i: (0, 0)),
        ],
        out_specs=(
            pl.BlockSpec((IDX_Q_W, tm), lambda i: (0, i)),
            pl.BlockSpec((tm, IDX_DIM), lambda i: (i, 0)),
            pl.BlockSpec((tm, GLA_GATE_RANK), lambda i: (i, 0)),
            pl.BlockSpec((IDX_HEADS, tm), lambda i: (0, i)),
        ),
        compiler_params=_params(("parallel",)),
        name="proj_idx",
    )(h, w, wt, lng, lnb)


def _gla_kernel(q_ref, k_ref, v_ref, r_ref, ga_ref, wa2_ref, ba_ref, ng_ref, o_ref, st_ref, tot_ref, kd_ref,
                oraw_ref, *, n_chunks):
    tb = q_ref.shape[0]

    @pl.when(pl.program_id(1) == 0)
    def _():
        st_ref[...] = jnp.zeros_like(st_ref)

    x = _dot3(ga_ref[...], wa2_ref[...]) + ba_ref[...]
    la = (jnp.minimum(x, 0.0) - jnp.log(1.0 + jnp.exp(-jnp.abs(x)))) * (1.0 / GLA_GATE_TEMP)
    row = lax.broadcasted_iota(I32, (tb, tb), 0)
    col = lax.broadcasted_iota(I32, (tb, tb), 1)
    same = lax.shift_right_logical(row, 6) == lax.shift_right_logical(col, 6)
    ones_blk = jnp.where(same, 1.0, 0.0).astype(BF16)
    tril_blk = jnp.where(same, jnp.where(row >= col, 1.0, 0.0), 0.0).astype(BF16)
    la_hi, la_lo = _split(la)
    d = lambda u, v: jnp.dot(u, v, preferred_element_type=F32)
    tot = d(ones_blk, la_hi) + d(ones_blk, la_lo)
    cum = d(tril_blk, la_hi) + d(tril_blk, la_lo)
    tot_ref[...] = tot
    kd_ref[...] = (k_ref[...].astype(F32) * jnp.exp(tot - cum)).astype(kd_ref.dtype)

    heads = range(GLA_HEADS)
    ksl = lambda hh: slice(hh * GLA_DK, (hh + 1) * GLA_DK)
    vsl = lambda hh: slice(hh * GLA_DV, (hh + 1) * GLA_DV)

    def chunk(c, carry):
        rows = pl.ds(pl.multiple_of(c * CHUNK, CHUNK), CHUNK)
        first = pl.ds(pl.multiple_of(c * CHUNK, CHUNK), 1)
        upd = [_mm_tn(v_ref[rows, vsl(hh)], kd_ref[rows, ksl(hh)]) for hh in heads]
        st = [st_ref[hh] * jnp.exp(tot_ref[first, ksl(hh)]) + upd[hh] for hh in heads]
        for hh in heads:
            st_ref[hh] = st[hh]
        for hh in heads:
            oraw_ref[rows, vsl(hh)] = _mm_nt(q_ref[rows, ksl(hh)], st[hh])
        return carry

    lax.fori_loop(0, n_chunks, chunk, 0)

    ng = ng_ref[...]
    for hh in heads:
        o = oraw_ref[:, vsl(hh)] * (GLA_DK ** -0.5)
        ms = jnp.mean(o * o, axis=-1, keepdims=True)
        r = r_ref[:, vsl(hh)].astype(F32)
        o_ref[:, vsl(hh)] = (o * lax.rsqrt(ms + EPS) * ng * (r / (1.0 + jnp.exp(-r)))).astype(o_ref.dtype)


def _gla(zm, ga, wa2, ba, ng, B, S, tb):
    T = B * S
    nb = S // tb
    qk_w = GLA_HEADS * GLA_DK
    v_w = GLA_HEADS * GLA_DV
    tok = lambda b, i: b * nb + i
    return pl.pallas_call(
        functools.partial(_gla_kernel, n_chunks=tb // CHUNK),
        out_shape=jax.ShapeDtypeStruct((T, v_w), MXU_DTYPE),
        grid=(B, nb),
        in_specs=[
            pl.BlockSpec((tb, qk_w), lambda b, i: (tok(b, i), 0)),
            pl.BlockSpec((tb, qk_w), lambda b, i: (tok(b, i), 1)),
            pl.BlockSpec((tb, v_w), lambda b, i: (tok(b, i), 1)),
            pl.BlockSpec((tb, v_w), lambda b, i: (tok(b, i), 2)),
            pl.BlockSpec((tb, GLA_GATE_RANK), lambda b, i: (tok(b, i), 0)),
            pl.BlockSpec((GLA_GATE_RANK, qk_w), lambda b, i: (0, 0)),
            pl.BlockSpec((1, qk_w), lambda b, i: (0, 0)),
            pl.BlockSpec((1, GLA_DV), lambda b, i: (0, 0)),
        ],
        out_specs=pl.BlockSpec((tb, v_w), lambda b, i: (tok(b, i), 0)),
        scratch_shapes=[pltpu.VMEM((GLA_HEADS, GLA_DV, GLA_DK), F32), pltpu.VMEM((tb, qk_w), F32),
                        pltpu.VMEM((tb, qk_w), MXU_DTYPE), pltpu.VMEM((tb, v_w), F32)],
        compiler_params=_params(("parallel", "arbitrary")),
        name="gla",
    )(zm, zm, zm, zm, ga, wa2, ba, ng)


def _last_kb(qb, QB, KB):
    return ((qb + 1) * QB - 1) // KB


def _causal_steps(S, QB, KB):
    pairs = [(q, k) for q in range(S // QB) for k in range(_last_kb(q, QB, KB) + 1)]
    qs, ks = zip(*pairs)
    return jnp.asarray(qs, I32), jnp.asarray(ks, I32)


def _select_kernel(qb_ref, kb_ref, iqt_ref, wit_ref, ki_ref, mask_ref, hi_ref, lo_ref, gm_ref, *,
                   QB, KB, S, topk):
    qb = qb_ref[pl.program_id(1)]
    kb = kb_ref[pl.program_id(1)]
    last = _last_kb(qb, QB, KB)

    def rows_of(i):
        return pl.ds(pl.multiple_of(i * KB, KB), KB)

    @pl.when(kb == 0)
    def _():
        gm_ref[...] = jnp.full(gm_ref.shape, INT_MIN, I32)

    rc = 32 * 1024 // QB
    assert topk % rc == 0 and KB % rc == 0
    t_chunk = lax.shift_right_logical(qb * QB + lax.broadcasted_iota(I32, (rc, QB), 1), 6)
    for c in range(KB // rc):
        ki = ki_ref[c * rc:(c + 1) * rc, :]
        sc = jnp.zeros((rc, QB), F32)
        for hh in range(IDX_HEADS):
            lg = _mm(ki, iqt_ref[hh * IDX_DIM:(hh + 1) * IDX_DIM, :])
            sc = sc + jnp.maximum(lg, 0.0) * wit_ref[hh:hh + 1, :]
        s_chunk = lax.shift_right_logical(kb * KB + c * rc + lax.broadcasted_iota(I32, (rc, QB), 0), 6)
        bits = lax.bitcast_convert_type(sc, I32)
        sign = lax.shift_right_arithmetic(bits, 31)
        key = (bits ^ (sign & INT_MAX)) - sign
        key = jnp.where(s_chunk <= t_chunk, key, INT_MIN)
        rows = pl.ds(pl.multiple_of(kb * KB + c * rc, rc), rc)
        hi_ref[rows, :] = lax.shift_right_arithmetic(key, 16).astype(I16)
        lo_ref[rows, :] = ((key & 0xFFFF) + I16_MIN).astype(I16)
        g0 = (c * rc) % topk
        gm_ref[g0:g0 + rc, :] = jnp.maximum(gm_ref[g0:g0 + rc, :], key)

    @pl.when(kb == last)
    def _():
        nblk = last + 1
        slab = 32
        one, zero = jnp.int16(1), jnp.int16(0)

        def fold(m, reduce):
            part = m[0:slab]
            for r in range(1, KB // slab):
                part = reduce(part, m[r * slab:(r + 1) * slab])
            return part

        def count(pred):
            def body(i, acc):
                return acc + fold(pred(i), jnp.add)
            acc = lax.fori_loop(0, nblk, body, jnp.zeros((slab, QB), I16))
            return jnp.sum(acc.astype(I32), axis=0, keepdims=True)

        def count_ge(ref, thr):
            t16 = thr.astype(I16)
            return count(lambda i: jnp.where(ref[rows_of(i), :] >= t16, one, zero))

        def search(ref, target, lo, c_lo, hi, c_hi, rounds):
            def is_open(lo, c_lo, hi):
                return jnp.where(c_lo > target, jnp.where(hi != lo + 1, 1, 0), 0)

            def step(_, st):
                lo, c_lo, hi, c_hi = st
                open_ = is_open(lo, c_lo, hi) > 0
                mid = lax.shift_right_arithmetic(lo + hi, 1)
                c = count_ge(ref, mid)
                up = jnp.logical_and(open_, c >= target)
                dn = jnp.logical_and(open_, c < target)
                return jnp.where(up, mid, lo), jnp.where(up, c, c_lo), jnp.where(dn, mid, hi), jnp.where(dn, c, c_hi)

            def halvings_left(st):
                lo, c_lo, hi, _c_hi = st
                width = jnp.where(is_open(lo, c_lo, hi) > 0, hi - lo, 1)
                return jnp.max(32 - lax.clz(width - 1))

            st = (lo, c_lo, hi, c_hi)
            for most in rounds:
                st = lax.fori_loop(0, jnp.minimum(halvings_left(st), most), step, st)
            lo, c_lo, _hi, c_hi = st
            return lo, c_lo, c_hi

        zeros = jnp.zeros((1, QB), I32)
        gm = gm_ref[...]
        g_lo = jnp.maximum(lax.shift_right_arithmetic(jnp.min(gm, axis=0, keepdims=True), 16), I16_MIN + 1)
        g_hi = lax.shift_right_arithmetic(jnp.max(gm, axis=0, keepdims=True), 16) + 1
        hstar, ch_ge, ch_gt = search(hi_ref, topk, g_lo, count_ge(hi_ref, g_lo), g_hi, zeros, (16,))
        split = ch_ge > topk
        h16 = hstar.astype(I16)

        def low_half():
            def build(i, carry):
                lo_ref[rows_of(i), :] = jnp.where(hi_ref[rows_of(i), :] == h16, lo_ref[rows_of(i), :],
                                                  jnp.int16(I16_MIN))
                return carry
            lax.fori_loop(0, nblk, build, 0)
            target = jnp.where(split, topk - ch_gt, INT_MAX)
            return search(lo_ref, target, jnp.full((1, QB), I16_MIN, I32), ch_ge - ch_gt,
                          jnp.full((1, QB), I16_MAX + 1, I32), zeros, (12, 2, 2))

        any_split = jnp.max(jnp.where(split, 1, 0)) > 0
        lstar, cl_ge, cl_gt = lax.cond(any_split, low_half,
                                       lambda: (jnp.full((1, QB), I16_MIN, I32), zeros, zeros))
        lstar = jnp.where(split, lstar, I16_MIN)
        l16 = lstar.astype(I16)
        c_ge = jnp.where(split, ch_gt + cl_ge, ch_ge)
        c_gt = jnp.where(split, ch_gt + cl_gt, ch_gt)
        excess = c_ge > topk
        need = jnp.where(excess, topk - c_gt, S + 1)
        row16 = lax.broadcasted_iota(I32, (KB, QB), 0).astype(I16)

        def before(i, j):
            return jnp.where(row16 + (i * KB).astype(I16) < j.astype(I16), one, zero)

        def tied(i, then):
            return jnp.where(hi_ref[rows_of(i), :] == h16, jnp.where(lo_ref[rows_of(i), :] == l16, then, zero), zero)

        def tie_cut():
            def step(_, c):
                lo, hi = c
                mid = lax.shift_right_arithmetic(lo + hi, 1)
                ok = count(lambda i: tied(i, before(i, mid))) >= need
                return jnp.where(ok, lo, mid + 1), jnp.where(ok, mid, hi)
            n_steps = max(1, math.ceil(math.log2(S + 1)))
            lo, _hi = lax.fori_loop(0, n_steps, step, (zeros, jnp.full((1, QB), S, I32)))
            return lo

        any_excess = jnp.max(jnp.where(excess, 1, 0)) > 0
        jcut = lax.cond(any_excess, tie_cut, lambda: jnp.full((1, QB), S, I32))

        def write(i, carry):
            hi_t, lo_t = hi_ref[rows_of(i), :], lo_ref[rows_of(i), :]
            in_bucket = jnp.where(lo_t > l16, one, jnp.where(lo_t == l16, before(i, jcut), zero))
            sel = jnp.where(hi_t > h16, one, jnp.where(hi_t == h16, in_bucket, zero))
            mask_ref[rows_of(i), :] = sel.astype(mask_ref.dtype)
            return carry

        lax.fori_loop(0, nblk, write, 0)

        def clear(i, carry):
            mask_ref[rows_of(i), :] = jnp.zeros((KB, QB), mask_ref.dtype)
            return carry

        lax.fori_loop(nblk, S // KB, clear, 0)


def _dsa_select(iqt, wit, ki, B, S, QB, KB):
    nq, nk = S // QB, S // KB
    topk = min(TOPK_MAX, S // 4)
    assert KB % topk == 0
    assert S <= I16_MAX
    qs, ks = _causal_steps(S, QB, KB)
    return pl.pallas_call(
        functools.partial(_select_kernel, QB=QB, KB=KB, S=S, topk=topk),
        out_shape=jax.ShapeDtypeStruct((B, S, S), jnp.int8),
        grid_spec=pltpu.PrefetchScalarGridSpec(
            num_scalar_prefetch=2,
            grid=(B, qs.shape[0]),
            in_specs=[
                pl.BlockSpec((IDX_Q_W, QB), lambda b, s, qs, ks: (0, b * nq + qs[s])),
                pl.BlockSpec((IDX_HEADS, QB), lambda b, s, qs, ks: (0, b * nq + qs[s])),
                pl.BlockSpec((KB, IDX_DIM), lambda b, s, qs, ks: (b * nk + ks[s], 0)),
            ],
            out_specs=pl.BlockSpec((None, S, QB), lambda b, s, qs, ks: (b, 0, qs[s])),
            scratch_shapes=[pltpu.VMEM((S, QB), I16), pltpu.VMEM((S, QB), I16), pltpu.VMEM((topk, QB), I32)],
        ),
        compiler_params=_params(("parallel", "arbitrary")),
        name="dsa_select",
    )(qs, ks, iqt, wit, ki)


LOGIT_SAFE = 100.0


def _attn_kernel(qb_ref, kb_ref, small_ref, q_ref, k_ref, vt_ref, m_ref, o_ref, acc_ref, run_ref, *, QB, KB):
    qb = qb_ref[pl.program_id(1)]
    kb = kb_ref[pl.program_id(1)]
    last = _last_kb(qb, QB, KB)

    @pl.when(kb == 0)
    def _():
        acc_ref[...] = jnp.zeros_like(acc_ref)
        run_ref[...] = jnp.full(run_ref.shape, -1e30, F32)

    head = lambda hh: slice(hh * DSA_HEAD_DIM, (hh + 1) * DSA_HEAD_DIM)
    qk_dot = lambda hh: _mm_nt(k_ref[:, head(hh)], q_ref[:, head(hh)])

    @pl.when(small_ref[0] == 1)
    def _():
        mb = m_ref[...].astype(MXU_DTYPE)
        lg = qk_dot(0)
        for hh in range(DSA_HEADS):
            lg_next = qk_dot(hh + 1) if hh + 1 < DSA_HEADS else None
            p = jnp.exp2(lg).astype(MXU_DTYPE) * mb
            acc_ref[hh] += jnp.dot(vt_ref[hh], p, preferred_element_type=F32)
            lg = lg_next

    @pl.when(small_ref[0] != 1)
    def _():
        selected = m_ref[...].astype(F32) > 0.0

        def logits(hh):
            lg = jnp.where(selected, qk_dot(hh), -jnp.inf)
            return lg, jnp.max(lg, axis=0, keepdims=True)

        nxt = logits(0)
        for hh in range(DSA_HEADS):
            lg, top = nxt
            nxt = logits(hh + 1) if hh + 1 < DSA_HEADS else None
            run_old = run_ref[hh]
            run_new = jnp.maximum(run_old, top)
            p = jnp.exp2(lg - run_new).astype(MXU_DTYPE)
            acc_ref[hh] = acc_ref[hh] * jnp.exp2(run_old - run_new) + jnp.dot(vt_ref[hh], p,
                                                                              preferred_element_type=F32)
            run_ref[hh] = run_new

    @pl.when(kb == last)
    def _():
        for hh in range(DSA_HEADS):
            acc = acc_ref[hh]
            o = acc[:DSA_HEAD_DIM] / acc[DSA_HEAD_DIM:DSA_HEAD_DIM + 1]
            o_ref[:, hh * DSA_HEAD_DIM:(hh + 1) * DSA_HEAD_DIM] = o.T.astype(o_ref.dtype)


def _dsa_attn(qk, vt, mask, logit_bound, B, S, QB, KB):
    T = B * S
    nq, nk = S // QB, S // KB
    W = DSA_HEADS * DSA_HEAD_DIM
    qs, ks = _causal_steps(S, QB, KB)
    small = (logit_bound <= LOGIT_SAFE).astype(I32).reshape(1)
    return pl.pallas_call(
        functools.partial(_attn_kernel, QB=QB, KB=KB),
        out_shape=jax.ShapeDtypeStruct((T, W), MXU_DTYPE),
        grid_spec=pltpu.PrefetchScalarGridSpec(
            num_scalar_prefetch=3,
            grid=(B, qs.shape[0]),
            in_specs=[
                pl.BlockSpec((QB, W), lambda b, s, qs, ks, sm: (b * nq + qs[s], 0)),
                pl.BlockSpec((KB, W), lambda b, s, qs, ks, sm: (b * nk + ks[s], 1)),
                pl.BlockSpec((None, DSA_HEADS, VT_ROWS, KB), lambda b, s, qs, ks, sm: (b, 0, 0, ks[s])),
                pl.BlockSpec((None, KB, QB), lambda b, s, qs, ks, sm: (b, ks[s], qs[s])),
            ],
            out_specs=pl.BlockSpec((QB, W), lambda b, s, qs, ks, sm: (b * nq + qs[s], 0)),
            scratch_shapes=[pltpu.VMEM((DSA_HEADS, VT_ROWS, QB), F32), pltpu.VMEM((DSA_HEADS, 1, QB), F32)],
        ),
        compiler_params=_params(("parallel", "arbitrary")),
        name="dsa_attn",
    )(qs, ks, small, qk, qk, vt, mask)


def _post_kernel(og_ref, od_ref, ga_ref, gb_ref, x_ref, pa_ref, pb_ref, wo_ref, g2_ref, wr_ref, br_ref,
                 x1_ref, h2_ref, route_ref):
    sig = lambda v: 1.0 / (1.0 + jnp.exp(-v))
    a = jnp.dot(og_ref[...], pa_ref[...], preferred_element_type=F32)
    b = jnp.dot(od_ref[...], pb_ref[...], preferred_element_type=F32)
    mix = sig(ga_ref[...].astype(F32)) * a + sig(gb_ref[...].astype(F32)) * b
    x1 = x_ref[...] + _mm(mix, wo_ref[...])
    x1_ref[...] = x1
    ms = jnp.mean(x1 * x1, axis=-1, keepdims=True)
    h2 = x1 * lax.rsqrt(ms + EPS) * g2_ref[...]
    h2_ref[...] = h2.astype(h2_ref.dtype)

    logits = _dot3(h2, wr_ref[...]) + br_ref[...]
    tm = logits.shape[0]
    lane = lax.broadcasted_iota(I32, (tm, LANES), 1).astype(F32)
    neg = -jnp.inf
    far = float(2 * LANES)
    rmax = lambda v: jnp.max(v, axis=1, keepdims=True)
    rmin = lambda v: jnp.min(v, axis=1, keepdims=True)
    gl = jnp.where(lane >= N_EXPERTS, jnp.where(lane < N_EXPERTS + N_GROUPS, logits, neg), neg)
    gmax = rmax(gl)
    g_w = 1.0 / jnp.sum(jnp.exp(gl - gmax), axis=1, keepdims=True)
    g_idx = rmin(jnp.where(gl == gmax, lane, far)) - N_EXPERTS
    e_lo = g_idx * EXPERTS_PER_GROUP
    el = jnp.where(lane >= e_lo, jnp.where(lane < e_lo + EXPERTS_PER_GROUP, logits, neg), neg)
    m1 = rmax(el)
    e1 = rmin(jnp.where(el == m1, lane, far))
    el2 = jnp.where(lane == e1, neg, el)
    m2 = rmax(el2)
    e2 = rmin(jnp.where(el2 == m2, lane, far))
    p2 = jnp.exp(m2 - m1)
    w1 = g_w / (1.0 + p2)
    route_ref[...] = jnp.where(lane == 0.0, e1, jnp.where(lane == 1.0, e2, jnp.where(
        lane == 2.0, w1, jnp.where(lane == 3.0, w1 * p2, 0.0))))


def _post(og, od, zm, x2, pa, pb, wo, g2, wr, br, tm):
    T, D = x2.shape
    row = lambda i: (i, 0)
    full = lambda i: (0, 0)
    return pl.pallas_call(
        _post_kernel,
        out_shape=(
            jax.ShapeDtypeStruct((T, D), F32),
            jax.ShapeDtypeStruct((T, D), F32),
            jax.ShapeDtypeStruct((T, LANES), F32),
        ),
        grid=(T // tm,),
        in_specs=[
            pl.BlockSpec((tm, D), row),
            pl.BlockSpec((tm, D), row),
            pl.BlockSpec((tm, D), lambda i: (i, 3)),
            pl.BlockSpec((tm, D), lambda i: (i, 4)),
            pl.BlockSpec((tm, D), row),
            pl.BlockSpec((D, D), full),
            pl.BlockSpec((D, D), full),
            pl.BlockSpec((D, D), full),
            pl.BlockSpec((1, D), full),
            pl.BlockSpec((D, LANES), full),
            pl.BlockSpec((1, LANES), full),
        ],
        out_specs=(pl.BlockSpec((tm, D), row), pl.BlockSpec((tm, D), row), pl.BlockSpec((tm, LANES), row)),
        compiler_params=_params(("parallel",)),
        name="merge_router",
    )(og, od, zm, zm, x2, pa, pb, wo, g2, wr, br)


MOE_TM = 256


def _gather_combine_kernel(idx_ref, nxt_ref, x1_ref, route_ref, src_hbm, o_ref, buf_ref, sem):
    i = pl.program_id(0)
    n = pl.num_programs(0)
    tm = o_ref.shape[0]
    slot = lax.rem(i, 2)

    def fetch(ids_ref, s):
        def start(r, carry):
            for k in range(2):
                pltpu.make_async_copy(src_hbm.at[pl.ds(ids_ref[0, 0, 2 * r + k], 1), :],
                                      buf_ref.at[s, pl.ds(k * tm + r, 1), :], sem.at[s]).start(priority=k)
            return carry
        lax.fori_loop(0, tm, start, 0, unroll=8)

    @pl.when(i == 0)
    def _():
        fetch(idx_ref, 0)

    @pl.when(i + 1 < n)
    def _():
        fetch(nxt_ref, 1 - slot)

    pltpu.make_async_copy(src_hbm.at[pl.ds(0, 2 * tm), :], buf_ref.at[slot], sem.at[slot]).wait()
    r = route_ref[...]
    lane = lax.broadcasted_iota(I32, r.shape, 1)
    w1 = jnp.sum(jnp.where(lane == 2, r, 0.0), axis=1, keepdims=True)
    w2 = jnp.sum(jnp.where(lane == 3, r, 0.0), axis=1, keepdims=True)
    o_ref[...] = x1_ref[...] + (w1 * buf_ref[slot, :tm, :] + w2 * buf_ref[slot, tm:, :])


def _gather_combine(x1, ys, pos, route, tm):
    T, D = x1.shape
    nt = T // tm
    idx3 = pos.reshape(nt, 1, 2 * tm)
    row = lambda i: (i, 0)
    return pl.pallas_call(
        _gather_combine_kernel,
        out_shape=jax.ShapeDtypeStruct((T, D), F32),
        grid=(nt,),
        in_specs=[
            pl.BlockSpec((1, 1, 2 * tm), lambda i: (i, 0, 0), memory_space=pltpu.SMEM),
            pl.BlockSpec((1, 1, 2 * tm), lambda i: (jnp.minimum(i + 1, nt - 1), 0, 0), memory_space=pltpu.SMEM),
            pl.BlockSpec((tm, D), row),
            pl.BlockSpec((tm, LANES), row),
            pl.BlockSpec(memory_space=pl.ANY),
        ],
        out_specs=pl.BlockSpec((tm, D), row),
        scratch_shapes=[pltpu.VMEM((2, 2 * tm, D), ys.dtype), pltpu.SemaphoreType.DMA((2,))],
        compiler_params=_params(("arbitrary",)),
        name="moe_gather_combine",
    )(idx3, idx3, x1, route, ys)


def _scatter_kernel(idx_ref, x_ref, o_hbm, sem, *, fanout):
    tm = x_ref.shape[0]

    def start(r, carry):
        for k in range(fanout):
            dst = idx_ref[0, 0, fanout * r + k]
            pltpu.make_async_copy(x_ref.at[pl.ds(r, 1), :], o_hbm.at[pl.ds(dst, 1), :], sem).start(priority=k)
        return carry

    lax.fori_loop(0, tm, start, 0, unroll=8)
    for k in range(fanout):
        pltpu.make_async_copy(x_ref, o_hbm.at[pl.ds(0, tm), :], sem).wait()


def _row_scatter(x, idx, fanout, tm, name):
    n, D = x.shape
    return pl.pallas_call(
        functools.partial(_scatter_kernel, fanout=fanout),
        out_shape=jax.ShapeDtypeStruct((fanout * n, D), x.dtype),
        grid=(n // tm,),
        in_specs=[
            pl.BlockSpec((1, 1, fanout * tm), lambda i: (i, 0, 0), memory_space=pltpu.SMEM),
            pl.BlockSpec((tm, D), lambda i: (i, 0)),
        ],
        out_specs=pl.BlockSpec(memory_space=pl.ANY),
        scratch_shapes=[pltpu.SemaphoreType.DMA(())],
        compiler_params=_params(("arbitrary",)),
        name=name,
    )(idx.reshape(n // tm, 1, fanout * tm), x)


def _route_plan(route, tm):
    n = 2 * route.shape[0]
    n_tiles = n // tm
    ids = route[:, :2].astype(I32).reshape(n)
    onehot = (ids[:, None] == jnp.arange(N_EXPERTS, dtype=I32)[None, :]).astype(F32).reshape(n_tiles, tm, N_EXPERTS)
    tril = (jnp.arange(tm)[:, None] >= jnp.arange(tm)[None, :]).astype(F32)
    in_tile = jnp.einsum("rc,tce->tre", tril, onehot)
    per_tile = in_tile[:, -1, :]
    before = jnp.cumsum(per_tile, axis=0) - per_tile
    counts = jnp.sum(per_tile, axis=0).astype(I32)
    starts = (jnp.cumsum(counts) - counts).astype(I32)
    rank = jnp.sum(onehot * (in_tile + before[:, None, :] - 1.0), axis=2).reshape(n)
    pos = (jnp.sum(onehot.reshape(n, N_EXPERTS) * starts[None, :].astype(F32), axis=1) + rank).astype(I32)
    bounds = jnp.sort(jnp.concatenate([jnp.arange(n_tiles, dtype=I32) * tm, starts]))
    ends = jnp.concatenate([bounds[1:], jnp.full((1,), n, I32)])
    seg_tile = jnp.minimum(bounds // tm, n_tiles - 1)
    seg_exp = jnp.clip(jnp.sum((starts[None, :] <= bounds[:, None]).astype(I32), axis=1) - 1, 0, N_EXPERTS - 1)
    seg_first = jnp.concatenate([jnp.ones((1,), I32), (seg_tile[1:] != seg_tile[:-1]).astype(I32)])
    return pos, (seg_tile, seg_exp, bounds - seg_tile * tm, ends - seg_tile * tm, seg_first)


def _experts_kernel(tile_ref, exp_ref, lo_ref, hi_ref, first_ref, x_ref, wg_ref, wu_ref, wd_ref, o_ref):
    s = pl.program_id(0)
    lo, hi = lo_ref[s], hi_ref[s]

    @pl.when(first_ref[s] == 1)
    def _():
        o_ref[...] = jnp.zeros_like(o_ref)

    @pl.when(hi > lo)
    def _():
        x = x_ref[...].astype(MXU_DTYPE)
        a = jnp.dot(x, wg_ref[...], preferred_element_type=F32)
        u = jnp.dot(x, wu_ref[...], preferred_element_type=F32)
        y = _mm(a / (1.0 + jnp.exp(-a)) * u, wd_ref[...])
        row = lax.broadcasted_iota(I32, y.shape, 0)
        o_ref[...] += jnp.where(row >= lo, jnp.where(row < hi, y, 0.0), 0.0)


def _experts(xs, segs, wg, wu, wd, tm):
    n, D = xs.shape
    F = wg.shape[2]
    tile_of = lambda s, tile, exp, lo, hi, first: (tile[s], 0)
    w_of = lambda s, tile, exp, lo, hi, first: (exp[s], 0, 0)
    return pl.pallas_call(
        _experts_kernel,
        out_shape=jax.ShapeDtypeStruct((n, D), F32),
        grid_spec=pltpu.PrefetchScalarGridSpec(
            num_scalar_prefetch=5,
            grid=(segs[0].shape[0],),
            in_specs=[
                pl.BlockSpec((tm, D), tile_of),
                pl.BlockSpec((None, D, F), w_of),
                pl.BlockSpec((None, D, F), w_of),
                pl.BlockSpec((None, F, D), w_of),
            ],
            out_specs=pl.BlockSpec((tm, D), tile_of),
        ),
        compiler_params=_params(("arbitrary",)),
        name="experts",
    )(*segs, xs, wg, wu, wd)


def _moe(h2, route, x1, wg, wu, wd):
    T, D = x1.shape
    tm = min(MOE_TM, T)
    pos, segs = _route_plan(route, tm)
    tok = min(2 * MOE_TM, T)
    xs = _row_scatter(h2, pos, 2, tok, "moe_scatter")
    ys = _experts(xs, segs, wg, wu, wd, tm)
    return _gather_combine(x1, ys, pos, route, tok)


def _pad_cols(w, n):
    return jnp.pad(w, ((0, 0), (0, n - w.shape[1])))


def _layer(x2, B, S, p):
    T, D = x2.shape
    assert D == 2 * GLA_HEADS * GLA_DK == GLA_HEADS * GLA_DV == DSA_HEADS * DSA_HEAD_DIM
    assert S % 512 == 0
    cd = MXU_DTYPE
    qk_w, v_w, dsa_w = GLA_HEADS * GLA_DK, GLA_HEADS * GLA_DV, DSA_HEADS * DSA_HEAD_DIM
    splits = (qk_w, qk_w, v_w, GLA_GATE_RANK, v_w, dsa_w, dsa_w, dsa_w, IDX_Q_W, IDX_DIM, IDX_HEADS, D, D)
    offs = [0]
    for s in splits:
        offs.append(offs[-1] + s)
    w_in = p["w_in"]
    (w_gq, w_gk, w_gv, w_ga, w_gr, w_dq, w_dk, w_dv, w_iq, w_ik, w_iw, w_ta, w_tb) = [
        w_in[:, offs[i]:offs[i + 1]] for i in range(len(splits))]

    w_main = jnp.concatenate([w_gq, w_gk, w_gv, w_gr, w_ta, w_tb], axis=1).astype(cd)
    w_qk = jnp.concatenate([w_dq, w_dk], axis=1).astype(cd)
    g_qk = jnp.concatenate([jnp.tile(p["dsa_q_norm_g"] * (DSA_HEAD_DIM ** -0.5 * math.log2(math.e)), DSA_HEADS),
                            jnp.tile(p["dsa_k_norm_g"], DSA_HEADS)]).reshape(1, 2 * dsa_w).astype(F32)
    w_vt = w_dv.T.astype(cd)
    w_idx = jnp.concatenate([_pad_cols(w_ik, LANES), _pad_cols(w_ga, LANES)], axis=1).astype(cd)
    w_idxt = jnp.pad(jnp.concatenate([w_iq, w_iw], axis=1).T, ((0, 16 - IDX_HEADS), (0, 0))).astype(cd)

    tm = min(1024, T)
    h, zm = _norm_matmul(x2, p["norm1_g"], w_main, tm, 2560, cd)
    qk = _qk_proj(h, w_qk, g_qk, tm)
    vt = _vt_proj(h, w_vt, B, S, min(1024, S))
    iqt, ki, ga, wit = _idx_proj(h, w_idx, w_idxt, p["idx_k_ln_g"].reshape(1, -1), p["idx_k_ln_b"].reshape(1, -1),
                                 tm)

    o_gla = _gla(zm, ga, p["gla_w_a2"], p["gla_b_a"].reshape(1, -1), p["gla_norm_g"].reshape(1, -1), B, S, 256)

    QB, KB = 256, min(1024, S)
    mask = _dsa_select(iqt, wit, ki, B, S, QB, KB)
    logit_bound = (DSA_HEAD_DIM * jnp.max(jnp.abs(g_qk[0, :dsa_w])) * jnp.max(jnp.abs(g_qk[0, dsa_w:])))
    o_dsa = _dsa_attn(qk, vt, mask, logit_bound, B, S, 2 * QB, KB)

    w_r = _pad_cols(jnp.concatenate([p["w_router_expert"], p["w_router_group"]], axis=1), LANES)
    b_r = _pad_cols(jnp.concatenate([p["b_router_expert"], p["b_router_group"]]).reshape(1, -1), LANES)
    x1, h2, route = _post(o_gla, o_dsa, zm, x2, p["w_branch_gla"].astype(cd), p["w_branch_dsa"].astype(cd),
                          p["w_out"].astype(cd), p["norm2_g"].reshape(1, -1), w_r, b_r, min(512, T))
    return _moe(h2, route, x1, p["w_exp_gate"].astype(cd), p["w_exp_up"].astype(cd), p["w_exp_down"].astype(cd))


def kernel(x, norm1_g, w_in, gla_w_a2, gla_b_a, gla_norm_g, dsa_q_norm_g, dsa_k_norm_g, idx_k_ln_g, idx_k_ln_b,
           w_branch_gla, w_branch_dsa, w_out, norm2_g, w_router_group, b_router_group, w_router_expert,
           b_router_expert, w_exp_gate, w_exp_up, w_exp_down):
    B, S, D = x.shape
    stacked = dict(norm1_g=norm1_g, w_in=w_in, gla_w_a2=gla_w_a2, gla_b_a=gla_b_a, gla_norm_g=gla_norm_g,
                   dsa_q_norm_g=dsa_q_norm_g, dsa_k_norm_g=dsa_k_norm_g, idx_k_ln_g=idx_k_ln_g,
                   idx_k_ln_b=idx_k_ln_b, w_branch_gla=w_branch_gla, w_branch_dsa=w_branch_dsa, w_out=w_out,
                   norm2_g=norm2_g, w_router_group=w_router_group, b_router_group=b_router_group,
                   w_router_expert=w_router_expert, b_router_expert=b_router_expert, w_exp_gate=w_exp_gate,
                   w_exp_up=w_exp_up, w_exp_down=w_exp_down)
    x2 = x.reshape(B * S, D).astype(F32)
    for l in range(w_in.shape[0]):
        x2 = _layer(x2, B, S, {k: v[l] for k, v in stacked.items()})
    return x2.reshape(B, S, D).astype(x.dtype)
```
